```python
import math
import jax
import jax.numpy as jnp
from jax import lax
import numpy as np


D_MODEL = 1024
BATCH = 4
SEQ = 4096
DEPTH = 4

CHUNK = 64
Q_BLOCK = 128
N_A_LAYERS = DEPTH // 2
N_B_LAYERS = DEPTH - N_A_LAYERS
DIFF_HEAD_DIM = 64
DIFF_V_DIM = 2 * DIFF_HEAD_DIM
DIFF_HEADS = D_MODEL // DIFF_V_DIM
DIFF_WIDTH = DIFF_HEADS * DIFF_V_DIM
MLA_NOPE_DIM = 128
MLA_ROPE_DIM = 64
MLA_V_DIM = 128
MLA_HEADS = D_MODEL // MLA_V_DIM
MLA_Q_RANK = 384
MLA_KV_RANK = 256
ROPE_BASE = 10000.0
N_EXPERTS = 32
N_GROUPS = 8
TOP_K = 2
EXPERT_FF = 256
NORM_EPS = 1e-5
DEEPNORM_ALPHA = (2 * DEPTH) ** 0.25
DEEPNORM_BETA = (8 * DEPTH) ** -0.25

kernel_name = 'hybrid_diffattn_mla_moe_deepnorm'


def _layernorm(x, g, b):
    xf = x.astype(jnp.float32)
    mu = jnp.mean(xf, axis=-1, keepdims=True)
    var = jnp.mean(jnp.square(xf - mu), axis=-1, keepdims=True)
    return ((xf - mu) * lax.rsqrt(var + NORM_EPS) * g.astype(jnp.float32) + b.astype(jnp.float32)).astype(x.dtype)


def _rmsnorm(x, g):
    xf = x.astype(jnp.float32)
    inv = lax.rsqrt(jnp.mean(jnp.square(xf), axis=-1, keepdims=True) + NORM_EPS)
    return (xf * inv * g.astype(jnp.float32)).astype(x.dtype)


def _rope_tables(seq):
    inv_freq = ROPE_BASE ** (-jnp.arange(0, MLA_ROPE_DIM, 2, dtype=jnp.float32) / MLA_ROPE_DIM)
    ang = jnp.arange(seq, dtype=jnp.float32)[:, None] * inv_freq[None, :]
    return jnp.cos(ang), jnp.sin(ang)


def _rope(x, cos, sin):
    xf = x.astype(jnp.float32)
    x1, x2 = jnp.split(xf, 2, axis=-1)
    return jnp.concatenate([x1 * cos - x2 * sin, x2 * cos + x1 * sin], axis=-1).astype(x.dtype)


def _alibi_slopes(n_heads):
    return 2.0 ** (-8.0 * jnp.arange(1, n_heads + 1, dtype=jnp.float32) / n_heads)


def _chunk_causal_mask(q0, q1):
    q_chunk = jnp.arange(q0, q1) // CHUNK
    k_chunk = jnp.arange(q1) // CHUNK
    return k_chunk[None, :] <= q_chunk[:, None]


def _sweep_query_blocks(seq, block_fn):
    return jnp.concatenate([block_fn(q0, q0 + Q_BLOCK) for q0 in range(0, seq, Q_BLOCK)], axis=1)


def _lambda_init(layer):
    return 0.8 - 0.6 * math.exp(-0.3 * layer)


def _diff_attention(x, w_qkv, lam, subln_g, w_o, lambda_init):
    b, s, _ = x.shape
    q, k, v = jnp.split(x @ w_qkv, 3, axis=-1)
    q = q.reshape(b, s, DIFF_HEADS, 2, DIFF_HEAD_DIM) * (DIFF_HEAD_DIM ** -0.5)
    k = k.reshape(b, s, DIFF_HEADS, 2, DIFF_HEAD_DIM)
    v = v.reshape(b, s, DIFF_HEADS, DIFF_V_DIM)
    lf = lam.astype(jnp.float32)
    lam_full = jnp.exp(jnp.sum(lf[0] * lf[1])) - jnp.exp(jnp.sum(lf[2] * lf[3])) + lambda_init
    slopes = _alibi_slopes(DIFF_HEADS)[:, None, None, None]

    def block(q0, q1):
        logits = jnp.einsum('bqhcd,bkhcd->bhcqk', q[:, q0:q1], k[:, :q1]).astype(jnp.float32)
        dist = jnp.abs(jnp.arange(q0, q1)[:, None] - jnp.arange(q1)[None, :]).astype(jnp.float32)
        logits = jnp.where(_chunk_causal_mask(q0, q1), logits - slopes * dist, -jnp.inf)
        p = jax.nn.softmax(logits, axis=-1)
        a = (p[:, :, 0] - lam_full * p[:, :, 1]).astype(v.dtype)
        return jnp.einsum('bhqk,bkhd->bqhd', a, v[:, :q1])

    o = _sweep_query_blocks(s, block)
    o = _rmsnorm(o, subln_g) * (1.0 - lambda_init)
    return o.reshape(b, s, DIFF_WIDTH) @ w_o


def _mla_shared_kv(x, w_kv_a, kv_norm_g, w_kv_b, cos, sin):
    b, s, _ = x.shape
    kv_a = x @ w_kv_a
    c_kv = _rmsnorm(kv_a[..., :MLA_KV_RANK], kv_norm_g)
    k_rope = _rope(kv_a[..., MLA_KV_RANK:], cos, sin)
    kv = (c_kv @ w_kv_b).reshape(b, s, MLA_HEADS, MLA_NOPE_DIM + MLA_V_DIM)
    return kv[..., :MLA_NOPE_DIM], kv[..., MLA_NOPE_DIM:], k_rope


def _mla_attention(x, w_q_a, q_norm_g, w_q_b, w_o, k_nope, v, k_rope, cos, sin):
    b, s, _ = x.shape
    q = (_rmsnorm(x @ w_q_a, q_norm_g) @ w_q_b).reshape(b, s, MLA_HEADS, MLA_NOPE_DIM + MLA_ROPE_DIM)
    scale = (MLA_NOPE_DIM + MLA_ROPE_DIM) ** -0.5
    q_nope = q[..., :MLA_NOPE_DIM] * scale
    q_rope = _rope(q[..., MLA_NOPE_DIM:], cos[:, None], sin[:, None]) * scale

    def block(q0, q1):
        logits = (jnp.einsum('bqhd,bkhd->bhqk', q_nope[:, q0:q1], k_nope[:, :q1]).astype(jnp.float32)
                  + jnp.einsum('bqhr,bkr->bhqk', q_rope[:, q0:q1], k_rope[:, :q1]).astype(jnp.float32))
        logits = jnp.where(_chunk_causal_mask(q0, q1), logits, -jnp.inf)
        p = jax.nn.softmax(logits, axis=-1).astype(v.dtype)
        return jnp.einsum('bhqk,bkhd->bqhd', p, v[:, :q1])

    o = _sweep_query_blocks(s, block)
    return o.reshape(b, s, MLA_HEADS * MLA_V_DIM) @ w_o


def _grouped_moe(x, router_w, router_b, w_gate, w_up, w_down):
    b, s, d = x.shape
    t = x.reshape(b * s, d)
    scores = jax.nn.sigmoid((t @ router_w).astype(jnp.float32))
    sel = scores + router_b.astype(jnp.float32)
    grouped = sel.reshape(-1, N_GROUPS, N_EXPERTS // N_GROUPS)
    group_score = jnp.sum(lax.top_k(grouped, 2)[0], axis=-1)
    group_mask = jnp.argmax(group_score, axis=-1)[:, None] == jnp.arange(N_GROUPS)[None, :]
    in_group = jnp.where(group_mask[:, :, None], grouped, -jnp.inf).reshape(-1, N_EXPERTS)
    _, idx = lax.top_k(in_group, TOP_K)
    gate = jnp.take_along_axis(scores, idx, axis=-1)
    gate = gate / jnp.sum(gate, axis=-1, keepdims=True)
    combine = jnp.sum(jax.nn.one_hot(idx, N_EXPERTS, dtype=jnp.float32) * gate[..., None], axis=1).astype(t.dtype)
    y = jnp.zeros_like(t)
    for e in range(N_EXPERTS):
        h = jax.nn.silu(t @ w_gate[e]) * (t @ w_up[e])
        y = y + combine[:, e:e + 1] * (h @ w_down[e])
    return y.reshape(b, s, d)


def setup_inputs(seed: int = 0) -> dict:
    key = jax.random.key(seed)
    ks = jax.random.split(key, 19)
    d = D_MODEL

    def nrm(k, shape, scale):
        return jax.random.normal(k, shape, dtype=jnp.float32) * scale

    return {
        'x': nrm(ks[0], (BATCH, SEQ, d), 1.0),
        'diff_w_qkv': nrm(ks[1], (N_A_LAYERS, d, 3 * DIFF_WIDTH), d ** -0.5),
        'diff_lambda': nrm(ks[2], (N_A_LAYERS, 4, DIFF_HEAD_DIM), 0.1),
        'diff_subln_g': 1.0 + nrm(ks[3], (N_A_LAYERS, DIFF_V_DIM), 0.01),
        'diff_w_o': nrm(ks[4], (N_A_LAYERS, DIFF_WIDTH, d), DEEPNORM_BETA * DIFF_WIDTH ** -0.5),
        'mla_w_kv_a': nrm(ks[5], (d, MLA_KV_RANK + MLA_ROPE_DIM), d ** -0.5),
        'mla_kv_norm_g': 1.0 + nrm(ks[6], (MLA_KV_RANK,), 0.01),
        'mla_w_kv_b': nrm(ks[7], (MLA_KV_RANK, MLA_HEADS * (MLA_NOPE_DIM + MLA_V_DIM)), MLA_KV_RANK ** -0.5),
        'mla_w_q_a': nrm(ks[8], (N_B_LAYERS, d, MLA_Q_RANK), d ** -0.5),
        'mla_q_norm_g': 1.0 + nrm(ks[9], (N_B_LAYERS, MLA_Q_RANK), 0.01),
        'mla_w_q_b': nrm(ks[10], (N_B_LAYERS, MLA_Q_RANK, MLA_HEADS * (MLA_NOPE_DIM + MLA_ROPE_DIM)), MLA_Q_RANK ** -0.5),
        'mla_w_o': nrm(ks[11], (N_B_LAYERS, MLA_HEADS * MLA_V_DIM, d), DEEPNORM_BETA * (MLA_HEADS * MLA_V_DIM) ** -0.5),
        'router_w': nrm(ks[12], (d, N_EXPERTS), d ** -0.5),
        'router_b': nrm(ks[13], (N_EXPERTS,), 0.01),
        'moe_w_gate': nrm(ks[14], (DEPTH, N_EXPERTS, d, EXPERT_FF), d ** -0.5),
        'moe_w_up': nrm(ks[15], (DEPTH, N_EXPERTS, d, EXPERT_FF), d ** -0.5),
        'moe_w_down': nrm(ks[16], (DEPTH, N_EXPERTS, EXPERT_FF, d), DEEPNORM_BETA * EXPERT_FF ** -0.5),
        'ln_g': 1.0 + nrm(ks[17], (DEPTH, 2, d), 0.01),
        'ln_b': nrm(ks[18], (DEPTH, 2, d), 0.01),
    }


def reference(x, diff_w_qkv, diff_lambda, diff_subln_g, diff_w_o, mla_w_kv_a, mla_kv_norm_g, mla_w_kv_b,
              mla_w_q_a, mla_q_norm_g, mla_w_q_b, mla_w_o, router_w, router_b, moe_w_gate, moe_w_up,
              moe_w_down, ln_g, ln_b):
    cos, sin = _rope_tables(x.shape[1])
    for layer in range(DEPTH):
        if layer < N_A_LAYERS:
            h = _diff_attention(x, diff_w_qkv[layer], diff_lambda[layer], diff_subln_g[layer],
                                diff_w_o[layer], _lambda_init(layer))
        else:
            if layer == N_A_LAYERS:
                k_nope, v_shared, k_rope = _mla_shared_kv(x, mla_w_kv_a, mla_kv_norm_g, mla_w_kv_b, cos, sin)
            j = layer - N_A_LAYERS
            h = _mla_attention(x, mla_w_q_a[j], mla_q_norm_g[j], mla_w_q_b[j], mla_w_o[j],
                               k_nope, v_shared, k_rope, cos, sin)
        x = _layernorm(DEEPNORM_ALPHA * x + h, ln_g[layer, 0], ln_b[layer, 0])
        f = _grouped_moe(x, router_w, router_b, moe_w_gate[layer], moe_w_up[layer], moe_w_down[layer])
        x = _layernorm(DEEPNORM_ALPHA * x + f, ln_g[layer, 1], ln_b[layer, 1])
    return x
```

```python
import functools
import math

import jax
import jax.numpy as jnp
from jax import lax
from jax.experimental import pallas as pl
from jax.experimental.pallas import tpu as pltpu

D_MODEL = 1024
DEPTH = 4
CHUNK = 64
N_A_LAYERS = DEPTH // 2
DIFF_HEAD_DIM = 64
DIFF_V_DIM = 2 * DIFF_HEAD_DIM
DIFF_HEADS = D_MODEL // DIFF_V_DIM
MLA_NOPE_DIM = 128
MLA_ROPE_DIM = 64
MLA_V_DIM = 128
MLA_HEADS = D_MODEL // MLA_V_DIM
MLA_Q_RANK = 384
MLA_KV_RANK = 256
MLA_QK_PAD = 256
ROPE_BASE = 10000.0
N_EXPERTS = 32
N_GROUPS = 8
GROUP_SIZE = N_EXPERTS // N_GROUPS
EXPERT_FF = 256
NORM_EPS = 1e-5
DEEPNORM_ALPHA = (2 * DEPTH) ** 0.25

LANES = 128
VMEM_LIMIT = 48 * 1024 * 1024

F32 = jnp.float32
BF16 = jnp.bfloat16


def _lambda_init(layer):
    return 0.8 - 0.6 * math.exp(-0.3 * layer)


def _cparams(sem):
    return pltpu.CompilerParams(dimension_semantics=sem, vmem_limit_bytes=VMEM_LIMIT)


def _dot(a, b):
    return jnp.dot(a, b, preferred_element_type=F32)


def _dot_nt(a, b):
    return lax.dot_general(a, b, (((1,), (1,)), ((), ())), preferred_element_type=F32)


def _rms(x, g):
    return x * lax.rsqrt(jnp.mean(x * x, axis=-1, keepdims=True) + NORM_EPS) * g


def _ln(x, g, b):
    mu = jnp.mean(x, axis=-1, keepdims=True)
    xc = x - mu
    var = jnp.mean(xc * xc, axis=-1, keepdims=True)
    return xc * lax.rsqrt(var + NORM_EPS) * g + b


def _rope_halves(a, cs, sn):
    return a * cs + pltpu.roll(a, MLA_ROPE_DIM, axis=1) * sn


def _qkv_kernel(x_ref, w_ref, o_ref):
    o_ref[...] = _dot(x_ref[...].astype(BF16), w_ref[...]).astype(o_ref.dtype)


def _qkv_proj(x, w, tm=512):
    t, d = x.shape
    n = w.shape[1]
    return pl.pallas_call(
        _qkv_kernel,
        out_shape=jax.ShapeDtypeStruct((t, n), BF16),
        grid=(t // tm,),
        in_specs=[pl.BlockSpec((tm, d), lambda i: (i, 0)),
                  pl.BlockSpec((d, n), lambda i: (0, 0))],
        out_specs=pl.BlockSpec((tm, n), lambda i: (i, 0)),
        compiler_params=_cparams(("parallel",)),
        name="qkv_proj",
    )(x, w)


def _mla_kv_kernel(x_ref, wa_ref, g_ref, wb_ref, cs_ref, sn_ref, kcat_ref, v_ref):
    kva = _dot(x_ref[...].astype(BF16), wa_ref[...])
    c_kv = _rms(kva[:, :MLA_KV_RANK], g_ref[...]).astype(BF16)
    kr = _rope_halves(kva[:, MLA_KV_RANK:], cs_ref[...], sn_ref[...]).astype(BF16)
    kv = _dot(c_kv, wb_ref[...])
    for h in range(MLA_HEADS):
        base = h * MLA_QK_PAD
        kcat_ref[:, base:base + MLA_NOPE_DIM] = kv[:, base:base + MLA_NOPE_DIM].astype(BF16)
        kcat_ref[:, base + MLA_NOPE_DIM:base + MLA_QK_PAD] = kr
        v_ref[:, h * MLA_V_DIM:(h + 1) * MLA_V_DIM] = kv[:, base + MLA_NOPE_DIM:base + MLA_QK_PAD].astype(BF16)


def _mla_kv_proj(x, wa, g, wb, cs, sn, seq, tm=256):
    t, d = x.shape
    ns = seq // tm
    return pl.pallas_call(
        _mla_kv_kernel,
        out_shape=(jax.ShapeDtypeStruct((t, MLA_HEADS * MLA_QK_PAD), BF16),
                   jax.ShapeDtypeStruct((t, MLA_HEADS * MLA_V_DIM), BF16)),
        grid=(t // tm,),
        in_specs=[pl.BlockSpec((tm, d), lambda i: (i, 0)),
                  pl.BlockSpec(wa.shape, lambda i: (0, 0)),
                  pl.BlockSpec(g.shape, lambda i: (0, 0)),
                  pl.BlockSpec(wb.shape, lambda i: (0, 0)),
                  pl.BlockSpec((tm, LANES), lambda i: (i % ns, 0)),
                  pl.BlockSpec((tm, LANES), lambda i: (i % ns, 0))],
        out_specs=(pl.BlockSpec((tm, MLA_HEADS * MLA_QK_PAD), lambda i: (i, 0)),
                   pl.BlockSpec((tm, MLA_HEADS * MLA_V_DIM), lambda i: (i, 0))),
        compiler_params=_cparams(("parallel",)),
        name="mla_kv_proj",
    )(x, wa, g, wb, cs, sn)


def _mla_q_kernel(x_ref, wa_ref, g_ref, wb_ref, cs_ref, sn_ref, q_ref, *, scale):
    qa = _dot(x_ref[...].astype(BF16), wa_ref[...])
    qn = _rms(qa, g_ref[...]).astype(BF16)
    qf = _dot(qn, wb_ref[...]) * scale
    cs = cs_ref[...]
    sn = sn_ref[...]
    for h in range(MLA_HEADS):
        base = h * MLA_QK_PAD
        q_ref[:, base:base + MLA_NOPE_DIM] = qf[:, base:base + MLA_NOPE_DIM].astype(BF16)
        q_ref[:, base + MLA_NOPE_DIM:base + MLA_QK_PAD] = _rope_halves(
            qf[:, base + MLA_NOPE_DIM:base + MLA_QK_PAD], cs, sn).astype(BF16)


def _mla_q_proj(x, wa, g, wb, cs, sn, seq, tm=256):
    t, d = x.shape
    ns = seq // tm
    scale = (MLA_NOPE_DIM + MLA_ROPE_DIM) ** -0.5
    return pl.pallas_call(
        functools.partial(_mla_q_kernel, scale=scale),
        out_shape=jax.ShapeDtypeStruct((t, MLA_HEADS * MLA_QK_PAD), BF16),
        grid=(t // tm,),
        in_specs=[pl.BlockSpec((tm, d), lambda i: (i, 0)),
                  pl.BlockSpec(wa.shape, lambda i: (0, 0)),
                  pl.BlockSpec(g.shape, lambda i: (0, 0)),
                  pl.BlockSpec(wb.shape, lambda i: (0, 0)),
                  pl.BlockSpec((tm, LANES), lambda i: (i % ns, 0)),
                  pl.BlockSpec((tm, LANES), lambda i: (i % ns, 0))],
        out_specs=pl.BlockSpec((tm, MLA_HEADS * MLA_QK_PAD), lambda i: (i, 0)),
        compiler_params=_cparams(("parallel",)),
        name="mla_q_proj",
    )(x, wa, g, wb, cs, sn)


def _softmax_step(s, vb, c, m_ref, l_ref, acc_ref):
    m_prev = m_ref[c]
    m_new = jnp.maximum(m_prev, jnp.max(s, axis=1, keepdims=True))
    alpha = jnp.exp(m_prev - m_new)
    p = jnp.exp(s - m_new)
    l_ref[c] = alpha * l_ref[c] + jnp.sum(p, axis=1, keepdims=True)
    acc_ref[c] = alpha * acc_ref[c] + _dot(p.astype(BF16), vb)
    m_ref[c] = m_new


def _chunk_mask(t):
    r = lax.broadcasted_iota(jnp.int32, (t, t), 0)
    c = lax.broadcasted_iota(jnp.int32, (t, t), 1)
    return r, c, (c // CHUNK) <= (r // CHUNK)


def _diff_attn_kernel(slopes_ref, lam_ref, g_ref, q_ref, k_ref, v_ref, o_ref,
                      m_ref, l_ref, acc_ref, *, t, lambda_init):
    h = pl.program_id(1)
    i = pl.program_id(2)
    slope = slopes_ref[h]
    lane = lax.broadcasted_iota(jnp.int32, (t, LANES), 1)
    qf = q_ref[...].astype(F32)
    qs = (jnp.where(lane < DIFF_HEAD_DIM, qf, 0.0).astype(BF16),
          jnp.where(lane >= DIFF_HEAD_DIM, qf, 0.0).astype(BF16))
    m_ref[...] = jnp.full(m_ref.shape, -jnp.inf, F32)
    l_ref[...] = jnp.zeros(l_ref.shape, F32)
    acc_ref[...] = jnp.zeros(acc_ref.shape, F32)
    col = lax.broadcasted_iota(jnp.int32, (1, t), 1).astype(F32)

    def past_tile(j, carry):
        off = pl.multiple_of(j * t, t)
        kb = k_ref[pl.ds(off, t), :]
        vb = v_ref[pl.ds(off, t), :]
        bias = slope * (col + ((j - i) * t).astype(F32))
        for c in range(2):
            _softmax_step(_dot_nt(qs[c], kb) + bias, vb, c, m_ref, l_ref, acc_ref)
        return carry

    lax.fori_loop(0, i, past_tile, 0)

    off = pl.multiple_of(i * t, t)
    kb = k_ref[pl.ds(off, t), :]
    vb = v_ref[pl.ds(off, t), :]
    r, c_io, visible = _chunk_mask(t)
    bias = slope * (r - jnp.abs(r - c_io)).astype(F32)
    for c in range(2):
        s = jnp.where(visible, _dot_nt(qs[c], kb) + bias, -jnp.inf)
        _softmax_step(s, vb, c, m_ref, l_ref, acc_ref)

    lam = lam_ref[...]
    lam_full = (jnp.exp(jnp.sum(lam[0:1] * lam[1:2], axis=1, keepdims=True))
                - jnp.exp(jnp.sum(lam[2:3] * lam[3:4], axis=1, keepdims=True)) + lambda_init)
    o = acc_ref[0] / l_ref[0] - lam_full * (acc_ref[1] / l_ref[1])
    o_ref[...] = (_rms(o, g_ref[...]) * (1.0 - lambda_init)).astype(o_ref.dtype)


def _diff_attention(qkv, slopes, lam, g, batch, seq, lambda_init, t=256):
    nq = seq // t
    hh = DIFF_HEADS
    kern = functools.partial(_diff_attn_kernel, t=t, lambda_init=lambda_init)
    return pl.pallas_call(
        kern,
        out_shape=jax.ShapeDtypeStruct((batch * seq, hh * DIFF_V_DIM), BF16),
        grid=(batch, hh, nq),
        in_specs=[pl.BlockSpec(memory_space=pltpu.SMEM),
                  pl.BlockSpec(lam.shape, lambda b, h, i: (0, 0)),
                  pl.BlockSpec(g.shape, lambda b, h, i: (0, 0)),
                  pl.BlockSpec((t, LANES), lambda b, h, i: (b * nq + i, h)),
                  pl.BlockSpec((seq, LANES), lambda b, h, i: (b, hh + h)),
                  pl.BlockSpec((seq, LANES), lambda b, h, i: (b, 2 * hh + h))],
        out_specs=pl.BlockSpec((t, LANES), lambda b, h, i: (b * nq + i, h)),
        scratch_shapes=[pltpu.VMEM((2, t, 1), F32), pltpu.VMEM((2, t, 1), F32),
                        pltpu.VMEM((2, t, DIFF_V_DIM), F32)],
        compiler_params=_cparams(("parallel", "parallel", "arbitrary")),
        name="diff_attention",
    )(slopes, lam, g, qkv, qkv, qkv)


def _mla_attn_kernel(q_ref, k_ref, v_ref, o_ref, m_ref, l_ref, acc_ref, *, t):
    i = pl.program_id(2)
    q = q_ref[...]
    m_ref[...] = jnp.full(m_ref.shape, -jnp.inf, F32)
    l_ref[...] = jnp.zeros(l_ref.shape, F32)
    acc_ref[...] = jnp.zeros(acc_ref.shape, F32)

    def past_tile(j, carry):
        off = pl.multiple_of(j * t, t)
        _softmax_step(_dot_nt(q, k_ref[pl.ds(off, t), :]), v_ref[pl.ds(off, t), :], 0, m_ref, l_ref, acc_ref)
        return carry

    lax.fori_loop(0, i, past_tile, 0)

    off = pl.multiple_of(i * t, t)
    _, _, visible = _chunk_mask(t)
    s = jnp.where(visible, _dot_nt(q, k_ref[pl.ds(off, t), :]), -jnp.inf)
    _softmax_step(s, v_ref[pl.ds(off, t), :], 0, m_ref, l_ref, acc_ref)
    o_ref[...] = (acc_ref[0] / l_ref[0]).astype(o_ref.dtype)


def _mla_attention(qcat, kcat, v, batch, seq, t=256):
    nq = seq // t
    return pl.pallas_call(
        functools.partial(_mla_attn_kernel, t=t),
        out_shape=jax.ShapeDtypeStruct((batch * seq, MLA_HEADS * MLA_V_DIM), BF16),
        grid=(batch, MLA_HEADS, nq),
        in_specs=[pl.BlockSpec((t, MLA_QK_PAD), lambda b, h, i: (b * nq + i, h)),
                  pl.BlockSpec((seq, MLA_QK_PAD), lambda b, h, i: (b, h)),
                  pl.BlockSpec((seq, MLA_V_DIM), lambda b, h, i: (b, h))],
        out_specs=pl.BlockSpec((t, MLA_V_DIM), lambda b, h, i: (b * nq + i, h)),
        scratch_shapes=[pltpu.VMEM((1, t, 1), F32), pltpu.VMEM((1, t, 1), F32),
                        pltpu.VMEM((1, t, MLA_V_DIM), F32)],
        compiler_params=_cparams(("parallel", "parallel", "arbitrary")),
        name="mla_attention",
    )(qcat, kcat, v)


def _split_bf16(x):
    hi = x.astype(BF16)
    lo = (x - hi.astype(F32)).astype(BF16)
    return hi, lo


def _route(logits_t, bias_t):
    score = 1.0 / (1.0 + jnp.exp(-logits_t))
    sel = score + bias_t
    a = [sel[N_GROUPS * j:N_GROUPS * (j + 1)] for j in range(GROUP_SIZE)]
    sc = [score[N_GROUPS * j:N_GROUPS * (j + 1)] for j in range(GROUP_SIZE)]
    hi1, lo1 = jnp.maximum(a[0], a[1]), jnp.minimum(a[0], a[1])
    hi2, lo2 = jnp.maximum(a[2], a[3]), jnp.minimum(a[2], a[3])
    group_score = jnp.maximum(hi1, hi2) + jnp.maximum(jnp.minimum(hi1, hi2), jnp.maximum(lo1, lo2))
    gidx = lax.broadcasted_iota(jnp.int32, group_score.shape, 0)
    best = jnp.max(group_score, axis=0, keepdims=True)
    first_best = jnp.min(jnp.where(group_score == best, gidx, N_GROUPS), axis=0, keepdims=True)
    in_group = gidx == first_best
    picked = []
    for j in range(GROUP_SIZE):
        beaten_by = jnp.zeros(a[j].shape, jnp.int32)
        for i in range(GROUP_SIZE):
            if i == j:
                continue
            ahead = (a[i] > a[j]) | ((a[i] == a[j]) if i < j else False)
            beaten_by = beaten_by + ahead.astype(jnp.int32)
        picked.append(jnp.where(beaten_by < 2, sc[j], 0.0))
    denom = (picked[0] + picked[1]) + (picked[2] + picked[3])
    return [jnp.where(in_group, p / denom, 0.0) for p in picked]


def _out_norm_route_kernel(a_ref, wo_ref, x_ref, g_ref, b_ref, rw_ref, rb_ref, x1_ref, cw_ref):
    y = _dot(a_ref[...], wo_ref[...])
    x1 = _ln(DEEPNORM_ALPHA * x_ref[...] + y, g_ref[...], b_ref[...])
    x1_ref[...] = x1
    xh, xl = _split_bf16(x1)
    wh, wl = _split_bf16(rw_ref[...])
    logits_t = _dot_nt(wh, xh) + (_dot_nt(wh, xl) + _dot_nt(wl, xh))
    cw = _route(logits_t, rb_ref[...])
    for j in range(GROUP_SIZE):
        cw_ref[N_GROUPS * j:N_GROUPS * (j + 1), :] = cw[j]


def _out_norm_route(a, wo, x, g, b, rw_t, rb_t, tm=256):
    t, d = x.shape
    return pl.pallas_call(
        _out_norm_route_kernel,
        out_shape=(jax.ShapeDtypeStruct((t, d), F32),
                   jax.ShapeDtypeStruct((N_EXPERTS, t), F32)),
        grid=(t // tm,),
        in_specs=[pl.BlockSpec((tm, d), lambda i: (i, 0)),
                  pl.BlockSpec((d, d), lambda i: (0, 0)),
                  pl.BlockSpec((tm, d), lambda i: (i, 0)),
                  pl.BlockSpec((1, d), lambda i: (0, 0)),
                  pl.BlockSpec((1, d), lambda i: (0, 0)),
                  pl.BlockSpec((N_EXPERTS, d), lambda i: (0, 0)),
                  pl.BlockSpec((N_EXPERTS, 1), lambda i: (0, 0))],
        out_specs=(pl.BlockSpec((tm, d), lambda i: (i, 0)),
                   pl.BlockSpec((N_EXPERTS, tm), lambda i: (0, i))),
        compiler_params=_cparams(("parallel",)),
        name="out_norm_route",
    )(a, wo, x, g, b, rw_t, rb_t)


def _moe_kernel(x_ref, cw_ref, wg_ref, wu_ref, wd_ref, g_ref, b_ref, o_ref, xb_ref, acc_ref):
    grp = pl.program_id(1)

    @pl.when(grp == 0)
    def _():
        xb_ref[...] = x_ref[...].astype(BF16)
        acc_ref[...] = jnp.zeros(acc_ref.shape, F32)

    xb = xb_ref[...]
    y = acc_ref[...]
    for j in range(GROUP_SIZE):
        hg = _dot(xb, wg_ref[j])
        hu = _dot(xb, wu_ref[j])
        hid = (hg / (1.0 + jnp.exp(-hg))) * hu * cw_ref[:, j:j + 1]
        y = y + _dot(hid.astype(BF16), wd_ref[j])
    acc_ref[...] = y

    @pl.when(grp == N_GROUPS - 1)
    def _():
        o_ref[...] = _ln(DEEPNORM_ALPHA * x_ref[...] + y, g_ref[...], b_ref[...])


def _moe(x1, cw, wg, wu, wd, g, b, layer, tm=512):
    t, d = x1.shape
    gs, ff = GROUP_SIZE, EXPERT_FF
    return pl.pallas_call(
        _moe_kernel,
        out_shape=jax.ShapeDtypeStruct((t, d), F32),
        grid=(t // tm, N_GROUPS),
        in_specs=[pl.BlockSpec((tm, d), lambda i, e: (i, 0)),
                  pl.BlockSpec((None, tm, gs), lambda i, e: (e, i, 0)),
                  pl.BlockSpec((None, gs, d, ff), lambda i, e: (layer, e, 0, 0)),
                  pl.BlockSpec((None, gs, d, ff), lambda i, e: (layer, e, 0, 0)),
                  pl.BlockSpec((None, gs, ff, d), lambda i, e: (layer, e, 0, 0)),
                  pl.BlockSpec((1, d), lambda i, e: (0, 0)),
                  pl.BlockSpec((1, d), lambda i, e: (0, 0))],
        out_specs=pl.BlockSpec((tm, d), lambda i, e: (i, 0)),
        scratch_shapes=[pltpu.VMEM((tm, d), BF16), pltpu.VMEM((tm, d), F32)],
        compiler_params=_cparams(("parallel", "arbitrary")),
        name="grouped_moe",
    )(x1, cw, wg, wu, wd, g, b)


def _rope_tables(seq):
    inv_freq = ROPE_BASE ** (-jnp.arange(0, MLA_ROPE_DIM, 2, dtype=F32) / MLA_ROPE_DIM)
    ang = jnp.arange(seq, dtype=F32)[:, None] * inv_freq[None, :]
    cos, sin = jnp.cos(ang), jnp.sin(ang)
    zero = jnp.zeros((seq, LANES - MLA_ROPE_DIM), F32)
    return (jnp.concatenate([cos, cos, zero], axis=1),
            jnp.concatenate([-sin, sin, zero], axis=1))


def _swap_halves(w):
    half = MLA_ROPE_DIM // 2
    return jnp.concatenate([w[..., half:], w[..., :half]], axis=-1)


def kernel(x, diff_w_qkv, diff_lambda, diff_subln_g, diff_w_o, mla_w_kv_a, mla_kv_norm_g, mla_w_kv_b,
           mla_w_q_a, mla_q_norm_g, mla_w_q_b, mla_w_o, router_w, router_b, moe_w_gate, moe_w_up,
           moe_w_down, ln_g, ln_b):
    batch, seq, d = x.shape
    t = batch * seq
    xs = x.reshape(t, d)

    q_scale = jnp.concatenate([jnp.full((DIFF_HEADS * DIFF_V_DIM,), DIFF_HEAD_DIM ** -0.5, F32),
                               jnp.ones((2 * DIFF_HEADS * DIFF_V_DIM,), F32)])
    w_qkv = (diff_w_qkv * q_scale).astype(BF16)
    w_o_a = diff_w_o.astype(BF16)
    w_o_b = mla_w_o.astype(BF16)
    wg = moe_w_gate.astype(BF16)
    wu = moe_w_up.astype(BF16)
    wd = moe_w_down.astype(BF16)
    slopes = 2.0 ** (-8.0 * jnp.arange(1, DIFF_HEADS + 1, dtype=F32) / DIFF_HEADS)
    rw_t = router_w.T.reshape(N_GROUPS, GROUP_SIZE, d).transpose(1, 0, 2).reshape(N_EXPERTS, d)
    rb_t = router_b.reshape(N_GROUPS, GROUP_SIZE).T.reshape(N_EXPERTS, 1).astype(F32)
    cs, sn = _rope_tables(seq)
    kv_rope_w = mla_w_kv_a[:, MLA_KV_RANK:]
    w_kv_a = jnp.concatenate([mla_w_kv_a, _swap_halves(kv_rope_w)], axis=1).astype(BF16)
    w_kv_b = mla_w_kv_b.astype(BF16)
    wqb = mla_w_q_b.reshape(-1, MLA_Q_RANK, MLA_HEADS, MLA_NOPE_DIM + MLA_ROPE_DIM)
    w_q_b = jnp.concatenate([wqb, _swap_halves(wqb[..., MLA_NOPE_DIM:])], axis=-1).reshape(
        -1, MLA_Q_RANK, MLA_HEADS * MLA_QK_PAD).astype(BF16)
    w_q_a = mla_w_q_a.astype(BF16)

    kcat = v_shared = None
    for layer in range(DEPTH):
        if layer < N_A_LAYERS:
            qkv = _qkv_proj(xs, w_qkv[layer])
            attn = _diff_attention(qkv, slopes, diff_lambda[layer], diff_subln_g[layer].reshape(1, -1),
                                   batch, seq, _lambda_init(layer))
            w_o = w_o_a[layer]
        else:
            j = layer - N_A_LAYERS
            if j == 0:
                kcat, v_shared = _mla_kv_proj(xs, w_kv_a, mla_kv_norm_g.reshape(1, -1), w_kv_b, cs, sn, seq)
            qcat = _mla_q_proj(xs, w_q_a[j], mla_q_norm_g[j].reshape(1, -1), w_q_b[j], cs, sn, seq)
            attn = _mla_attention(qcat, kcat, v_shared, batch, seq)
            w_o = w_o_b[j]
        x1, cw_t = _out_norm_route(attn, w_o, xs, ln_g[layer, 0].reshape(1, -1), ln_b[layer, 0].reshape(1, -1),
                                   rw_t, rb_t)
        cw = cw_t.reshape(GROUP_SIZE, N_GROUPS, t).transpose(1, 2, 0)
        xs = _moe(x1, cw, wg, wu, wd, ln_g[layer, 1].reshape(1, -1), ln_b[layer, 1].reshape(1, -1), layer)
    return xs.reshape(batch, seq, d)
```

```python
import functools
import math

import jax
import jax.numpy as jnp
from jax import lax
from jax.experimental import pallas as pl
from jax.experimental.pallas import tpu as pltpu

D_MODEL = 1024
DEPTH = 4
CHUNK = 64
N_A_LAYERS = DEPTH // 2
DIFF_HEAD_DIM = 64
DIFF_V_DIM = 2 * DIFF_HEAD_DIM
DIFF_HEADS = D_MODEL // DIFF_V_DIM
MLA_NOPE_DIM = 128
MLA_ROPE_DIM = 64
MLA_V_DIM = 128
MLA_HEADS = D_MODEL // MLA_V_DIM
MLA_Q_RANK = 384
MLA_KV_RANK = 256
MLA_QK_PAD = 256
ROPE_BASE = 10000.0
N_EXPERTS = 32
N_GROUPS = 8
GROUP_SIZE = N_EXPERTS // N_GROUPS
EXPERT_FF = 256
NORM_EPS = 1e-5
DEEPNORM_ALPHA = (2 * DEPTH) ** 0.25
LOG2E = math.log2(math.e)

LANES = 128
T_TILE = 256
KV_UNROLL = 4
VMEM_LIMIT = 48 * 1024 * 1024

F32 = jnp.float32
BF16 = jnp.bfloat16


def _lambda_init(layer):
    return 0.8 - 0.6 * math.exp(-0.3 * layer)


def _cparams(sem):
    return pltpu.CompilerParams(dimension_semantics=sem, vmem_limit_bytes=VMEM_LIMIT)


def _dot(a, b):
    return jnp.dot(a, b, preferred_element_type=F32)


def _dot_nt(a, b):
    return lax.dot_general(a, b, (((1,), (1,)), ((), ())), preferred_element_type=F32)


def _rms(x, g):
    return x * lax.rsqrt(jnp.mean(x * x, axis=-1, keepdims=True) + NORM_EPS) * g


def _ln(x, g, b):
    mu = jnp.mean(x, axis=-1, keepdims=True)
    xc = x - mu
    var = jnp.mean(xc * xc, axis=-1, keepdims=True)
    return xc * lax.rsqrt(var + NORM_EPS) * g + b


def _rope_halves(a, cs, sn):
    return a * cs + pltpu.roll(a, MLA_ROPE_DIM, axis=1) * sn


def _qkv_kernel(x_ref, wk_ref, wqv_ref, k_ref, qt_ref, vt_ref):
    xb = x_ref[...].astype(BF16)
    k_ref[...] = _dot(xb, wk_ref[...]).astype(BF16)
    qv = _dot_nt(wqv_ref[...], xb)
    for h in range(DIFF_HEADS):
        qt_ref[0, h] = qv[h * LANES:(h + 1) * LANES].astype(BF16)
        vt_ref[0, h] = qv[D_MODEL + h * LANES:D_MODEL + (h + 1) * LANES].astype(BF16)


def _qkv_proj(x, wk, wqv):
    t, d = x.shape
    nt = t // T_TILE
    head_major = jax.ShapeDtypeStruct((nt, DIFF_HEADS, LANES, T_TILE), BF16)
    head_block = pl.BlockSpec((1, DIFF_HEADS, LANES, T_TILE), lambda i: (i, 0, 0, 0))
    return pl.pallas_call(
        _qkv_kernel,
        out_shape=(jax.ShapeDtypeStruct((t, d), BF16), head_major, head_major),
        grid=(nt,),
        in_specs=[pl.BlockSpec((T_TILE, d), lambda i: (i, 0)),
                  pl.BlockSpec(wk.shape, lambda i: (0, 0)),
                  pl.BlockSpec(wqv.shape, lambda i: (0, 0))],
        out_specs=(pl.BlockSpec((T_TILE, d), lambda i: (i, 0)), head_block, head_block),
        compiler_params=_cparams(("parallel",)),
        name="qkv_proj",
    )(x, wk, wqv)


def _mla_kv_kernel(x_ref, wa_ref, g_ref, wk_ref, wv_ref, cs_ref, sn_ref, kcat_ref, vt_ref):
    kva = _dot(x_ref[...].astype(BF16), wa_ref[...])
    c_kv = _rms(kva[:, :MLA_KV_RANK], g_ref[...]).astype(BF16)
    kr = _rope_halves(kva[:, MLA_KV_RANK:], cs_ref[...], sn_ref[...]).astype(BF16)
    kn = _dot(c_kv, wk_ref[...])
    vt = _dot_nt(wv_ref[...], c_kv)
    for h in range(MLA_HEADS):
        base = h * MLA_QK_PAD
        kcat_ref[:, base:base + MLA_NOPE_DIM] = kn[:, h * MLA_NOPE_DIM:(h + 1) * MLA_NOPE_DIM].astype(BF16)
        kcat_ref[:, base + MLA_NOPE_DIM:base + MLA_QK_PAD] = kr
        vt_ref[0, h] = vt[h * MLA_V_DIM:(h + 1) * MLA_V_DIM].astype(BF16)


def _mla_kv_proj(x, wa, g, wk, wv, cs, sn, seq):
    t, d = x.shape
    nt = t // T_TILE
    ns = seq // T_TILE
    return pl.pallas_call(
        _mla_kv_kernel,
        out_shape=(jax.ShapeDtypeStruct((t, MLA_HEADS * MLA_QK_PAD), BF16),
                   jax.ShapeDtypeStruct((nt, MLA_HEADS, MLA_V_DIM, T_TILE), BF16)),
        grid=(nt,),
        in_specs=[pl.BlockSpec((T_TILE, d), lambda i: (i, 0)),
                  pl.BlockSpec(wa.shape, lambda i: (0, 0)),
                  pl.BlockSpec(g.shape, lambda i: (0, 0)),
                  pl.BlockSpec(wk.shape, lambda i: (0, 0)),
                  pl.BlockSpec(wv.shape, lambda i: (0, 0)),
                  pl.BlockSpec((T_TILE, LANES), lambda i: (i % ns, 0)),
                  pl.BlockSpec((T_TILE, LANES), lambda i: (i % ns, 0))],
        out_specs=(pl.BlockSpec((T_TILE, MLA_HEADS * MLA_QK_PAD), lambda i: (i, 0)),
                   pl.BlockSpec((1, MLA_HEADS, MLA_V_DIM, T_TILE), lambda i: (i, 0, 0, 0))),
        compiler_params=_cparams(("parallel",)),
        name="mla_kv_proj",
    )(x, wa, g, wk, wv, cs, sn)


def _mla_q_kernel(x_ref, wa_ref, g_ref, wb_ref, cst_ref, snt_ref, qt_ref, *, scale):
    qa = _dot(x_ref[...].astype(BF16), wa_ref[...])
    qn = _rms(qa, g_ref[...]).astype(BF16)
    qt = _dot_nt(wb_ref[...], qn) * scale
    cst = cst_ref[...]
    snt = snt_ref[...]
    zero = jnp.zeros((MLA_ROPE_DIM, T_TILE), BF16)
    for h in range(MLA_HEADS):
        base = h * MLA_QK_PAD
        r0 = base + MLA_NOPE_DIM
        rope = qt[r0:r0 + MLA_ROPE_DIM] * cst + qt[r0 + MLA_ROPE_DIM:base + MLA_QK_PAD] * snt
        qt_ref[0, h, 0:MLA_NOPE_DIM] = qt[base:r0].astype(BF16)
        qt_ref[0, h, MLA_NOPE_DIM:MLA_NOPE_DIM + MLA_ROPE_DIM] = rope.astype(BF16)
        qt_ref[0, h, MLA_NOPE_DIM + MLA_ROPE_DIM:MLA_QK_PAD] = zero


def _mla_q_proj(x, wa, g, wb_t, cst, snt, seq):
    t, d = x.shape
    nt = t // T_TILE
    ns = seq // T_TILE
    scale = LOG2E * (MLA_NOPE_DIM + MLA_ROPE_DIM) ** -0.5
    return pl.pallas_call(
        functools.partial(_mla_q_kernel, scale=scale),
        out_shape=jax.ShapeDtypeStruct((nt, MLA_HEADS, MLA_QK_PAD, T_TILE), BF16),
        grid=(nt,),
        in_specs=[pl.BlockSpec((T_TILE, d), lambda i: (i, 0)),
                  pl.BlockSpec(wa.shape, lambda i: (0, 0)),
                  pl.BlockSpec(g.shape, lambda i: (0, 0)),
                  pl.BlockSpec(wb_t.shape, lambda i: (0, 0)),
                  pl.BlockSpec((MLA_ROPE_DIM, T_TILE), lambda i: (0, i % ns)),
                  pl.BlockSpec((MLA_ROPE_DIM, T_TILE), lambda i: (0, i % ns))],
        out_specs=pl.BlockSpec((1, MLA_HEADS, MLA_QK_PAD, T_TILE), lambda i: (i, 0, 0, 0)),
        compiler_params=_cparams(("parallel",)),
        name="mla_q_proj",
    )(x, wa, g, wb_t, cst, snt)


def _tile_softmax(tiles):
    probs = []
    for s_t, shift, vt in tiles:
        m_t = jnp.max(s_t, axis=0, keepdims=True)
        p_t = jnp.exp2(s_t - m_t)
        probs.append((m_t + shift, jnp.sum(p_t, axis=0, keepdims=True), p_t.astype(BF16), vt))
    return [(m_t, l_t, _dot(vt, p_t)) for m_t, l_t, p_t, vt in probs]


def _merge_tiles(parts, c, m_ref, l_ref, acc_ref):
    m_prev = m_ref[c]
    m_new = m_prev
    for m_t, _, _ in parts:
        m_new = jnp.maximum(m_new, m_t)
    alpha = jnp.exp2(m_prev - m_new)
    l = alpha * l_ref[c]
    acc = alpha * acc_ref[c]
    for m_t, l_t, pv in parts:
        w = jnp.exp2(m_t - m_new)
        l = l + w * l_t
        acc = acc + w * pv
    m_ref[c] = m_new
    l_ref[c] = l
    acc_ref[c] = acc


def _for_key_tiles(n_past, body):
    def group(g, carry):
        body(g * KV_UNROLL, KV_UNROLL, False)
        return carry

    n_groups = n_past // KV_UNROLL
    lax.fori_loop(0, n_groups, group, 0)
    left = n_past - n_groups * KV_UNROLL
    for r in range(KV_UNROLL):
        @pl.when(left == r)
        def _(r=r):
            body(n_groups * KV_UNROLL, r, True)


def _init_softmax(m_ref, l_ref, acc_ref):
    m_ref[...] = jnp.full(m_ref.shape, -jnp.inf, F32)
    l_ref[...] = jnp.zeros(l_ref.shape, F32)
    acc_ref[...] = jnp.zeros(acc_ref.shape, F32)


def _key_query_iota():
    rk = lax.broadcasted_iota(jnp.int32, (T_TILE, T_TILE), 0)
    rq = lax.broadcasted_iota(jnp.int32, (T_TILE, T_TILE), 1)
    return rk, rq, (rk // CHUNK) <= (rq // CHUNK)


def _diff_attn_kernel(slh_ref, sll_ref, lam_ref, g_ref, qt_ref, k_ref, vt_ref, o_ref,
                      m_ref, l_ref, acc_ref, *, lambda_init):
    h = pl.program_id(1)
    i = pl.program_id(2)
    sl_hi = slh_ref[h]
    sl_lo = sll_ref[h]
    slope = sl_hi + sl_lo
    row = lax.broadcasted_iota(jnp.int32, (LANES, T_TILE), 0)
    qf = qt_ref[...].astype(F32)
    qs = (jnp.where(row < DIFF_HEAD_DIM, qf, 0.0).astype(BF16),
          jnp.where(row >= DIFF_HEAD_DIM, qf, 0.0).astype(BF16))
    slope_rows = jnp.where(row == 0, sl_hi, jnp.where(row == 1, sl_lo, 0.0)).astype(BF16)
    qs_aug = [jnp.concatenate([q, slope_rows], axis=0) for q in qs]
    key_off = lax.broadcasted_iota(jnp.int32, (T_TILE, LANES), 0).astype(F32)
    lane = lax.broadcasted_iota(jnp.int32, (T_TILE, LANES), 1)
    key_cols = jnp.where(lane < 2, key_off, 0.0).astype(BF16)
    _init_softmax(m_ref, l_ref, acc_ref)

    def key_tiles(j0, n_past, with_diagonal):
        tiles = []
        for u in range(n_past):
            j = j0 + u
            off = pl.multiple_of(j * T_TILE, T_TILE)
            k_aug = jnp.concatenate([k_ref[pl.ds(off, T_TILE), :], key_cols], axis=1)
            shift = slope * ((j - i) * T_TILE).astype(F32)
            tiles += [(_dot(k_aug, qs_aug[c]), shift, vt_ref[j]) for c in range(2)]
        if with_diagonal:
            off = pl.multiple_of(i * T_TILE, T_TILE)
            kb = k_ref[pl.ds(off, T_TILE), :]
            rk, rq, visible = _key_query_iota()
            bias = slope * (rq - jnp.abs(rq - rk)).astype(F32)
            tiles += [(jnp.where(visible, _dot(kb, qs[c]) + bias, -jnp.inf), 0.0, vt_ref[i]) for c in range(2)]
        parts = _tile_softmax(tiles)
        for c in range(2):
            _merge_tiles(parts[c::2], c, m_ref, l_ref, acc_ref)

    _for_key_tiles(i, key_tiles)

    lam = lam_ref[...]
    lam_full = (jnp.exp(jnp.sum(lam[0:1] * lam[1:2], axis=1, keepdims=True))
                - jnp.exp(jnp.sum(lam[2:3] * lam[3:4], axis=1, keepdims=True)) + lambda_init)
    o_t = acc_ref[0] / l_ref[0] - lam_full * (acc_ref[1] / l_ref[1])
    inv = lax.rsqrt(jnp.mean(o_t * o_t, axis=0, keepdims=True) + NORM_EPS)
    o_t = o_t * inv * (g_ref[...] * (1.0 - lambda_init))
    o_ref[...] = o_t.T.astype(o_ref.dtype)


def _diff_attention(qt, k, vt, slope_hi, slope_lo, lam, g_col, batch, seq, lambda_init):
    nq = seq // T_TILE
    hh = DIFF_HEADS
    return pl.pallas_call(
        functools.partial(_diff_attn_kernel, lambda_init=lambda_init),
        out_shape=jax.ShapeDtypeStruct((batch * seq, hh * DIFF_V_DIM), BF16),
        grid=(batch, hh, nq),
        in_specs=[pl.BlockSpec(memory_space=pltpu.SMEM),
                  pl.BlockSpec(memory_space=pltpu.SMEM),
                  pl.BlockSpec(lam.shape, lambda b, h, i: (0, 0)),
                  pl.BlockSpec(g_col.shape, lambda b, h, i: (0, 0)),
                  pl.BlockSpec((None, None, LANES, T_TILE), lambda b, h, i: (b * nq + i, h, 0, 0)),
                  pl.BlockSpec((seq, LANES), lambda b, h, i: (b, h)),
                  pl.BlockSpec((nq, None, DIFF_V_DIM, T_TILE), lambda b, h, i: (b, h, 0, 0))],
        out_specs=pl.BlockSpec((T_TILE, LANES), lambda b, h, i: (b * nq + i, h)),
        scratch_shapes=[pltpu.VMEM((2, 1, T_TILE), F32), pltpu.VMEM((2, 1, T_TILE), F32),
                        pltpu.VMEM((2, DIFF_V_DIM, T_TILE), F32)],
        compiler_params=_cparams(("parallel", "parallel", "arbitrary")),
        name="diff_attention",
    )(slope_hi, slope_lo, lam, g_col, qt, k, vt)


def _mla_attn_kernel(qt_ref, k_ref, vt_ref, o_ref, m_ref, l_ref, acc_ref):
    i = pl.program_id(2)
    qt = qt_ref[...]
    _init_softmax(m_ref, l_ref, acc_ref)

    def key_tiles(j0, n_past, with_diagonal):
        tiles = []
        for u in range(n_past):
            off = pl.multiple_of((j0 + u) * T_TILE, T_TILE)
            tiles.append((_dot(k_ref[pl.ds(off, T_TILE), :], qt), 0.0, vt_ref[j0 + u]))
        if with_diagonal:
            off = pl.multiple_of(i * T_TILE, T_TILE)
            _, _, visible = _key_query_iota()
            s_t = jnp.where(visible, _dot(k_ref[pl.ds(off, T_TILE), :], qt), -jnp.inf)
            tiles.append((s_t, 0.0, vt_ref[i]))
        _merge_tiles(_tile_softmax(tiles), 0, m_ref, l_ref, acc_ref)

    _for_key_tiles(i, key_tiles)
    o_ref[...] = (acc_ref[0] / l_ref[0]).T.astype(o_ref.dtype)


def _mla_attention(qt, kcat, vt, batch, seq):
    nq = seq // T_TILE
    return pl.pallas_call(
        _mla_attn_kernel,
        out_shape=jax.ShapeDtypeStruct((batch * seq, MLA_HEADS * MLA_V_DIM), BF16),
        grid=(batch, MLA_HEADS, nq),
        in_specs=[pl.BlockSpec((None, None, MLA_QK_PAD, T_TILE), lambda b, h, i: (b * nq + i, h, 0, 0)),
                  pl.BlockSpec((seq, MLA_QK_PAD), lambda b, h, i: (b, h)),
                  pl.BlockSpec((nq, None, MLA_V_DIM, T_TILE), lambda b, h, i: (b, h, 0, 0))],
        out_specs=pl.BlockSpec((T_TILE, MLA_V_DIM), lambda b, h, i: (b * nq + i, h)),
        scratch_shapes=[pltpu.VMEM((1, 1, T_TILE), F32), pltpu.VMEM((1, 1, T_TILE), F32),
                        pltpu.VMEM((1, MLA_V_DIM, T_TILE), F32)],
        compiler_params=_cparams(("parallel", "parallel", "arbitrary")),
        name="mla_attention",
    )(qt, kcat, vt)


def _split_bf16(x):
    hi = x.astype(BF16)
    lo = (x - hi.astype(F32)).astype(BF16)
    return hi, lo


def _route(logits_t, bias_t):
    score = 1.0 / (1.0 + jnp.exp(-logits_t))
    sel = score + bias_t
    a = [sel[N_GROUPS * j:N_GROUPS * (j + 1)] for j in range(GROUP_SIZE)]
    sc = [score[N_GROUPS * j:N_GROUPS * (j + 1)] for j in range(GROUP_SIZE)]
    hi1, lo1 = jnp.maximum(a[0], a[1]), jnp.minimum(a[0], a[1])
    hi2, lo2 = jnp.maximum(a[2], a[3]), jnp.minimum(a[2], a[3])
    group_score = jnp.maximum(hi1, hi2) + jnp.maximum(jnp.minimum(hi1, hi2), jnp.maximum(lo1, lo2))
    gidx = lax.broadcasted_iota(jnp.int32, group_score.shape, 0)
    best = jnp.max(group_score, axis=0, keepdims=True)
    first_best = jnp.min(jnp.where(group_score == best, gidx, N_GROUPS), axis=0, keepdims=True)
    in_group = gidx == first_best
    picked = []
    for j in range(GROUP_SIZE):
        beaten_by = jnp.zeros(a[j].shape, jnp.int32)
        for i in range(GROUP_SIZE):
            if i == j:
                continue
            ahead = (a[i] > a[j]) | ((a[i] == a[j]) if i < j else False)
            beaten_by = beaten_by + ahead.astype(jnp.int32)
        picked.append(jnp.where(beaten_by < 2, sc[j], 0.0))
    denom = (picked[0] + picked[1]) + (picked[2] + picked[3])
    return [jnp.where(in_group, p / denom, 0.0) for p in picked]


def _out_norm_route_kernel(a_ref, wo_ref, x_ref, g_ref, b_ref, rw_ref, rb_ref, x1_ref, cw_ref):
    y = _dot(a_ref[...], wo_ref[...])
    x1 = _ln(DEEPNORM_ALPHA * x_ref[...] + y, g_ref[...], b_ref[...])
    x1_ref[...] = x1
    xh, xl = _split_bf16(x1)
    wh, wl = _split_bf16(rw_ref[...])
    logits_t = _dot_nt(wh, xh) + (_dot_nt(wh, xl) + _dot_nt(wl, xh))
    cw = _route(logits_t, rb_ref[...])
    for j in range(GROUP_SIZE):
        cw_ref[N_GROUPS * j:N_GROUPS * (j + 1), :] = cw[j]


def _out_norm_route(a, wo, x, g, b, rw_t, rb_t, tm=256):
    t, d = x.shape
    return pl.pallas_call(
        _out_norm_route_kernel,
        out_shape=(jax.ShapeDtypeStruct((t, d), F32),
                   jax.ShapeDtypeStruct((N_EXPERTS, t), F32)),
        grid=(t // tm,),
        in_specs=[pl.BlockSpec((tm, d), lambda i: (i, 0)),
                  pl.BlockSpec((d, d), lambda i: (0, 0)),
                  pl.BlockSpec((tm, d), lambda i: (i, 0)),
                  pl.BlockSpec((1, d), lambda i: (0, 0)),
                  pl.BlockSpec((1, d), lambda i: (0, 0)),
                  pl.BlockSpec((N_EXPERTS, d), lambda i: (0, 0)),
                  pl.BlockSpec((N_EXPERTS, 1), lambda i: (0, 0))],
        out_specs=(pl.BlockSpec((tm, d), lambda i: (i, 0)),
                   pl.BlockSpec((N_EXPERTS, tm), lambda i: (0, i))),
        compiler_params=_cparams(("parallel",)),
        name="out_norm_route",
    )(a, wo, x, g, b, rw_t, rb_t)


def _moe_kernel(x_ref, cw_ref, wg_ref, wu_ref, wd_ref, g_ref, b_ref, o_ref, xb_ref, acc_ref):
    grp = pl.program_id(1)

    @pl.when(grp == 0)
    def _():
        xb_ref[...] = x_ref[...].astype(BF16)
        acc_ref[...] = jnp.zeros(acc_ref.shape, F32)

    xb = xb_ref[...]
    y = acc_ref[...]
    for j in range(GROUP_SIZE):
        hg = _dot(xb, wg_ref[j])
        hu = _dot(xb, wu_ref[j])
        hid = (hg / (1.0 + jnp.exp(-hg))) * hu * cw_ref[:, j:j + 1]
        y = y + _dot(hid.astype(BF16), wd_ref[j])
    acc_ref[...] = y

    @pl.when(grp == N_GROUPS - 1)
    def _():
        o_ref[...] = _ln(DEEPNORM_ALPHA * x_ref[...] + y, g_ref[...], b_ref[...])


def _moe(x1, cw, wg, wu, wd, g, b, layer, tm=512):
    t, d = x1.shape
    gs, ff = GROUP_SIZE, EXPERT_FF
    return pl.pallas_call(
        _moe_kernel,
        out_shape=jax.ShapeDtypeStruct((t, d), F32),
        grid=(t // tm, N_GROUPS),
        in_specs=[pl.BlockSpec((tm, d), lambda i, e: (i, 0)),
                  pl.BlockSpec((None, tm, gs), lambda i, e: (e, i, 0)),
                  pl.BlockSpec((None, gs, d, ff), lambda i, e: (layer, e, 0, 0)),
                  pl.BlockSpec((None, gs, d, ff), lambda i, e: (layer, e, 0, 0)),
                  pl.BlockSpec((None, gs, ff, d), lambda i, e: (layer, e, 0, 0)),
                  pl.BlockSpec((1, d), lambda i, e: (0, 0)),
                  pl.BlockSpec((1, d), lambda i, e: (0, 0))],
        out_specs=pl.BlockSpec((tm, d), lambda i, e: (i, 0)),
        scratch_shapes=[pltpu.VMEM((tm, d), BF16), pltpu.VMEM((tm, d), F32)],
        compiler_params=_cparams(("parallel", "arbitrary")),
        name="grouped_moe",
    )(x1, cw, wg, wu, wd, g, b)


def _rope_tables(seq):
    inv_freq = ROPE_BASE ** (-jnp.arange(0, MLA_ROPE_DIM, 2, dtype=F32) / MLA_ROPE_DIM)
    ang = jnp.arange(seq, dtype=F32)[:, None] * inv_freq[None, :]
    cos, sin = jnp.cos(ang), jnp.sin(ang)
    cos2 = jnp.concatenate([cos, cos], axis=1)
    sin2 = jnp.concatenate([-sin, sin], axis=1)
    zero = jnp.zeros((seq, LANES - MLA_ROPE_DIM), F32)
    return (jnp.concatenate([cos2, zero], axis=1), jnp.concatenate([sin2, zero], axis=1), cos2.T, sin2.T)


def _swap_halves(w):
    half = MLA_ROPE_DIM // 2
    return jnp.concatenate([w[..., half:], w[..., :half]], axis=-1)


def kernel(x, diff_w_qkv, diff_lambda, diff_subln_g, diff_w_o, mla_w_kv_a, mla_kv_norm_g, mla_w_kv_b,
           mla_w_q_a, mla_q_norm_g, mla_w_q_b, mla_w_o, router_w, router_b, moe_w_gate, moe_w_up,
           moe_w_down, ln_g, ln_b):
    batch, seq, d = x.shape
    t = batch * seq
    xs = x.reshape(t, d)

    wq, wk, wv = jnp.split(diff_w_qkv, 3, axis=-1)
    w_k = wk.astype(BF16)
    w_qv_t = jnp.concatenate([wq * (LOG2E * DIFF_HEAD_DIM ** -0.5), wv], axis=-1).transpose(0, 2, 1).astype(BF16)
    w_o_a = diff_w_o.astype(BF16)
    w_o_b = mla_w_o.astype(BF16)
    wg = moe_w_gate.astype(BF16)
    wu = moe_w_up.astype(BF16)
    wd = moe_w_down.astype(BF16)
    slopes = LOG2E * 2.0 ** (-8.0 * jnp.arange(1, DIFF_HEADS + 1, dtype=F32) / DIFF_HEADS)
    slope_hi = slopes.astype(BF16).astype(F32)
    slope_lo = (slopes - slope_hi).astype(BF16).astype(F32)
    rw_t = router_w.T.reshape(N_GROUPS, GROUP_SIZE, d).transpose(1, 0, 2).reshape(N_EXPERTS, d)
    rb_t = router_b.reshape(N_GROUPS, GROUP_SIZE).T.reshape(N_EXPERTS, 1).astype(F32)
    cs, sn, cs_t, sn_t = _rope_tables(seq)
    kv_rope_w = mla_w_kv_a[:, MLA_KV_RANK:]
    w_kv_a = jnp.concatenate([mla_w_kv_a, _swap_halves(kv_rope_w)], axis=1).astype(BF16)
    w_kv_b = mla_w_kv_b.reshape(MLA_KV_RANK, MLA_HEADS, MLA_NOPE_DIM + MLA_V_DIM)
    w_kn = w_kv_b[..., :MLA_NOPE_DIM].reshape(MLA_KV_RANK, -1).astype(BF16)
    w_v_t = w_kv_b[..., MLA_NOPE_DIM:].reshape(MLA_KV_RANK, -1).T.astype(BF16)
    wqb = mla_w_q_b.reshape(-1, MLA_Q_RANK, MLA_HEADS, MLA_NOPE_DIM + MLA_ROPE_DIM)
    w_q_b_t = jnp.concatenate([wqb, _swap_halves(wqb[..., MLA_NOPE_DIM:])], axis=-1).reshape(
        -1, MLA_Q_RANK, MLA_HEADS * MLA_QK_PAD).transpose(0, 2, 1).astype(BF16)
    w_q_a = mla_w_q_a.astype(BF16)

    kcat = vt_shared = None
    for layer in range(DEPTH):
        if layer < N_A_LAYERS:
            k, qt, vt = _qkv_proj(xs, w_k[layer], w_qv_t[layer])
            attn = _diff_attention(qt, k, vt, slope_hi, slope_lo, diff_lambda[layer],
                                   diff_subln_g[layer].reshape(-1, 1), batch, seq, _lambda_init(layer))
            w_o = w_o_a[layer]
        else:
            j = layer - N_A_LAYERS
            if j == 0:
                kcat, vt_shared = _mla_kv_proj(xs, w_kv_a, mla_kv_norm_g.reshape(1, -1), w_kn, w_v_t, cs, sn, seq)
            qt = _mla_q_proj(xs, w_q_a[j], mla_q_norm_g[j].reshape(1, -1), w_q_b_t[j], cs_t, sn_t, seq)
            attn = _mla_attention(qt, kcat, vt_shared, batch, seq)
            w_o = w_o_b[j]
        x1, cw_t = _out_norm_route(attn, w_o, xs, ln_g[layer, 0].reshape(1, -1), ln_b[layer, 0].reshape(1, -1),
                                   rw_t, rb_t)
        cw = cw_t.reshape(GROUP_SIZE, N_GROUPS, t).transpose(1, 2, 0)
        xs = _moe(x1, cw, wg, wu, wd, ln_g[layer, 1].reshape(1, -1), ln_b[layer, 1].reshape(1, -1), layer)
    return xs.reshape(batch, seq, d)
```

```python
import functools
import math

import jax
import jax.numpy as jnp
from jax import lax
from jax.experimental import pallas as pl
from jax.experimental.pallas import tpu as pltpu

D_MODEL = 1024
DEPTH = 4
CHUNK = 64
N_A_LAYERS = DEPTH // 2
DIFF_HEAD_DIM = 64
DIFF_V_DIM = 2 * DIFF_HEAD_DIM
DIFF_HEADS = D_MODEL // DIFF_V_DIM
MLA_NOPE_DIM = 128
MLA_ROPE_DIM = 64
MLA_V_DIM = 128
MLA_HEADS = D_MODEL // MLA_V_DIM
MLA_Q_RANK = 384
MLA_KV_RANK = 256
MLA_QK_PAD = 256
ROPE_BASE = 10000.0
N_EXPERTS = 32
N_GROUPS = 8
GROUP_SIZE = N_EXPERTS // N_GROUPS
EXPERT_FF = 256
NORM_EPS = 1e-5
DEEPNORM_ALPHA = (2 * DEPTH) ** 0.25
LOG2E = math.log2(math.e)

LANES = 128
T_TILE = 256
KV_UNROLL = 4
ROUTE_TILE = 256
MOE_TILE = 256
PERM_ROWS = 256
ROW_WIDE = D_MODEL + LANES
VMEM_LIMIT = 48 * 1024 * 1024

F32 = jnp.float32
BF16 = jnp.bfloat16


def _lambda_init(layer):
    return 0.8 - 0.6 * math.exp(-0.3 * layer)


def _cparams(sem):
    return pltpu.CompilerParams(dimension_semantics=sem, vmem_limit_bytes=VMEM_LIMIT)


def _dot(a, b):
    return jnp.dot(a, b, preferred_element_type=F32)


def _dot_nt(a, b):
    return lax.dot_general(a, b, (((1,), (1,)), ((), ())), preferred_element_type=F32)


def _rms(x, g):
    return x * lax.rsqrt(jnp.mean(x * x, axis=-1, keepdims=True) + NORM_EPS) * g


def _ln(x, g, b):
    mu = jnp.mean(x, axis=-1, keepdims=True)
    xc = x - mu
    var = jnp.mean(xc * xc, axis=-1, keepdims=True)
    return xc * lax.rsqrt(var + NORM_EPS) * g + b


def _rope_halves(a, cs, sn):
    return a * cs + pltpu.roll(a, MLA_ROPE_DIM, axis=1) * sn


def _qkv_kernel(x_ref, wk_ref, wqv_ref, k_ref, qt_ref, vt_ref):
    xb = x_ref[...].astype(BF16)
    k_ref[...] = _dot(xb, wk_ref[...]).astype(BF16)
    qv = _dot_nt(wqv_ref[...], xb)
    for h in range(DIFF_HEADS):
        qt_ref[0, h] = qv[h * LANES:(h + 1) * LANES].astype(BF16)
        vt_ref[0, h] = qv[D_MODEL + h * LANES:D_MODEL + (h + 1) * LANES].astype(BF16)


def _qkv_proj(x, wk, wqv):
    t, d = x.shape
    nt = t // T_TILE
    head_major = jax.ShapeDtypeStruct((nt, DIFF_HEADS, LANES, T_TILE), BF16)
    head_block = pl.BlockSpec((1, DIFF_HEADS, LANES, T_TILE), lambda i: (i, 0, 0, 0))
    return pl.pallas_call(
        _qkv_kernel,
        out_shape=(jax.ShapeDtypeStruct((t, d), BF16), head_major, head_major),
        grid=(nt,),
        in_specs=[pl.BlockSpec((T_TILE, d), lambda i: (i, 0)),
                  pl.BlockSpec(wk.shape, lambda i: (0, 0)),
                  pl.BlockSpec(wqv.shape, lambda i: (0, 0))],
        out_specs=(pl.BlockSpec((T_TILE, d), lambda i: (i, 0)), head_block, head_block),
        compiler_params=_cparams(("parallel",)),
        name="qkv_proj",
    )(x, wk, wqv)


def _mla_kv_kernel(x_ref, wa_ref, g_ref, wk_ref, wv_ref, cs_ref, sn_ref, kcat_ref, vt_ref):
    kva = _dot(x_ref[...].astype(BF16), wa_ref[...])
    c_kv = _rms(kva[:, :MLA_KV_RANK], g_ref[...]).astype(BF16)
    kr = _rope_halves(kva[:, MLA_KV_RANK:], cs_ref[...], sn_ref[...]).astype(BF16)
    kn = _dot(c_kv, wk_ref[...])
    vt = _dot_nt(wv_ref[...], c_kv)
    for h in range(MLA_HEADS):
        base = h * MLA_QK_PAD
        kcat_ref[:, base:base + MLA_NOPE_DIM] = kn[:, h * MLA_NOPE_DIM:(h + 1) * MLA_NOPE_DIM].astype(BF16)
        kcat_ref[:, base + MLA_NOPE_DIM:base + MLA_QK_PAD] = kr
        vt_ref[0, h] = vt[h * MLA_V_DIM:(h + 1) * MLA_V_DIM].astype(BF16)


def _mla_kv_proj(x, wa, g, wk, wv, cs, sn, seq):
    t, d = x.shape
    nt = t // T_TILE
    ns = seq // T_TILE
    return pl.pallas_call(
        _mla_kv_kernel,
        out_shape=(jax.ShapeDtypeStruct((t, MLA_HEADS * MLA_QK_PAD), BF16),
                   jax.ShapeDtypeStruct((nt, MLA_HEADS, MLA_V_DIM, T_TILE), BF16)),
        grid=(nt,),
        in_specs=[pl.BlockSpec((T_TILE, d), lambda i: (i, 0)),
                  pl.BlockSpec(wa.shape, lambda i: (0, 0)),
                  pl.BlockSpec(g.shape, lambda i: (0, 0)),
                  pl.BlockSpec(wk.shape, lambda i: (0, 0)),
                  pl.BlockSpec(wv.shape, lambda i: (0, 0)),
                  pl.BlockSpec((T_TILE, LANES), lambda i: (i % ns, 0)),
                  pl.BlockSpec((T_TILE, LANES), lambda i: (i % ns, 0))],
        out_specs=(pl.BlockSpec((T_TILE, MLA_HEADS * MLA_QK_PAD), lambda i: (i, 0)),
                   pl.BlockSpec((1, MLA_HEADS, MLA_V_DIM, T_TILE), lambda i: (i, 0, 0, 0))),
        compiler_params=_cparams(("parallel",)),
        name="mla_kv_proj",
    )(x, wa, g, wk, wv, cs, sn)


def _mla_q_kernel(x_ref, wa_ref, g_ref, wb_ref, cst_ref, snt_ref, qt_ref, *, scale):
    qa = _dot(x_ref[...].astype(BF16), wa_ref[...])
    qn = _rms(qa, g_ref[...]).astype(BF16)
    qt = _dot_nt(wb_ref[...], qn) * scale
    cst = cst_ref[...]
    snt = snt_ref[...]
    zero = jnp.zeros((MLA_ROPE_DIM, T_TILE), BF16)
    for h in range(MLA_HEADS):
        base = h * MLA_QK_PAD
        r0 = base + MLA_NOPE_DIM
        rope = qt[r0:r0 + MLA_ROPE_DIM] * cst + qt[r0 + MLA_ROPE_DIM:base + MLA_QK_PAD] * snt
        qt_ref[0, h, 0:MLA_NOPE_DIM] = qt[base:r0].astype(BF16)
        qt_ref[0, h, MLA_NOPE_DIM:MLA_NOPE_DIM + MLA_ROPE_DIM] = rope.astype(BF16)
        qt_ref[0, h, MLA_NOPE_DIM + MLA_ROPE_DIM:MLA_QK_PAD] = zero


def _mla_q_proj(x, wa, g, wb_t, cst, snt, seq):
    t, d = x.shape
    nt = t // T_TILE
    ns = seq // T_TILE
    scale = LOG2E * (MLA_NOPE_DIM + MLA_ROPE_DIM) ** -0.5
    return pl.pallas_call(
        functools.partial(_mla_q_kernel, scale=scale),
        out_shape=jax.ShapeDtypeStruct((nt, MLA_HEADS, MLA_QK_PAD, T_TILE), BF16),
        grid=(nt,),
        in_specs=[pl.BlockSpec((T_TILE, d), lambda i: (i, 0)),
                  pl.BlockSpec(wa.shape, lambda i: (0, 0)),
                  pl.BlockSpec(g.shape, lambda i: (0, 0)),
                  pl.BlockSpec(wb_t.shape, lambda i: (0, 0)),
                  pl.BlockSpec((MLA_ROPE_DIM, T_TILE), lambda i: (0, i % ns)),
                  pl.BlockSpec((MLA_ROPE_DIM, T_TILE), lambda i: (0, i % ns))],
        out_specs=pl.BlockSpec((1, MLA_HEADS, MLA_QK_PAD, T_TILE), lambda i: (i, 0, 0, 0)),
        compiler_params=_cparams(("parallel",)),
        name="mla_q_proj",
    )(x, wa, g, wb_t, cst, snt)


def _tile_softmax(tiles):
    probs = []
    for s_t, shift, vt in tiles:
        m_t = jnp.max(s_t, axis=0, keepdims=True)
        p_t = jnp.exp2(s_t - m_t)
        probs.append((m_t + shift, jnp.sum(p_t, axis=0, keepdims=True), p_t.astype(BF16), vt))
    return [(m_t, l_t, _dot(vt, p_t)) for m_t, l_t, p_t, vt in probs]


def _merge_tiles(parts, c, m_ref, l_ref, acc_ref):
    m_prev = m_ref[c]
    m_new = m_prev
    for m_t, _, _ in parts:
        m_new = jnp.maximum(m_new, m_t)
    alpha = jnp.exp2(m_prev - m_new)
    l = alpha * l_ref[c]
    acc = alpha * acc_ref[c]
    for m_t, l_t, pv in parts:
        w = jnp.exp2(m_t - m_new)
        l = l + w * l_t
        acc = acc + w * pv
    m_ref[c] = m_new
    l_ref[c] = l
    acc_ref[c] = acc


def _for_key_tiles(n_past, body):
    def group(g, carry):
        body(g * KV_UNROLL, KV_UNROLL, False)
        return carry

    n_groups = n_past // KV_UNROLL
    lax.fori_loop(0, n_groups, group, 0)
    left = n_past - n_groups * KV_UNROLL
    for r in range(KV_UNROLL):
        @pl.when(left == r)
        def _(r=r):
            body(n_groups * KV_UNROLL, r, True)


def _init_softmax(m_ref, l_ref, acc_ref):
    m_ref[...] = jnp.full(m_ref.shape, -jnp.inf, F32)
    l_ref[...] = jnp.zeros(l_ref.shape, F32)
    acc_ref[...] = jnp.zeros(acc_ref.shape, F32)


def _key_query_iota():
    rk = lax.broadcasted_iota(jnp.int32, (T_TILE, T_TILE), 0)
    rq = lax.broadcasted_iota(jnp.int32, (T_TILE, T_TILE), 1)
    return rk, rq, (rk // CHUNK) <= (rq // CHUNK)


def _diff_attn_kernel(slh_ref, sll_ref, lam_ref, g_ref, qt_ref, k_ref, vt_ref, o_ref,
                      m_ref, l_ref, acc_ref, *, lambda_init):
    h = pl.program_id(1)
    i = pl.program_id(2)
    sl_hi = slh_ref[h]
    sl_lo = sll_ref[h]
    slope = sl_hi + sl_lo
    row = lax.broadcasted_iota(jnp.int32, (LANES, T_TILE), 0)
    qf = qt_ref[...].astype(F32)
    qs = (jnp.where(row < DIFF_HEAD_DIM, qf, 0.0).astype(BF16),
          jnp.where(row >= DIFF_HEAD_DIM, qf, 0.0).astype(BF16))
    slope_rows = jnp.where(row == 0, sl_hi, jnp.where(row == 1, sl_lo, 0.0)).astype(BF16)
    qs_aug = [jnp.concatenate([q, slope_rows], axis=0) for q in qs]
    key_off = lax.broadcasted_iota(jnp.int32, (T_TILE, LANES), 0).astype(F32)
    lane = lax.broadcasted_iota(jnp.int32, (T_TILE, LANES), 1)
    key_cols = jnp.where(lane < 2, key_off, 0.0).astype(BF16)
    _init_softmax(m_ref, l_ref, acc_ref)

    def key_tiles(j0, n_past, with_diagonal):
        tiles = []
        for u in range(n_past):
            j = j0 + u
            off = pl.multiple_of(j * T_TILE, T_TILE)
            k_aug = jnp.concatenate([k_ref[pl.ds(off, T_TILE), :], key_cols], axis=1)
            shift = slope * ((j - i) * T_TILE).astype(F32)
            tiles += [(_dot(k_aug, qs_aug[c]), shift, vt_ref[j]) for c in range(2)]
        if with_diagonal:
            off = pl.multiple_of(i * T_TILE, T_TILE)
            kb = k_ref[pl.ds(off, T_TILE), :]
            rk, rq, visible = _key_query_iota()
            bias = slope * (rq - jnp.abs(rq - rk)).astype(F32)
            tiles += [(jnp.where(visible, _dot(kb, qs[c]) + bias, -jnp.inf), 0.0, vt_ref[i]) for c in range(2)]
        parts = _tile_softmax(tiles)
        for c in range(2):
            _merge_tiles(parts[c::2], c, m_ref, l_ref, acc_ref)

    _for_key_tiles(i, key_tiles)

    lam = lam_ref[...]
    lam_full = (jnp.exp(jnp.sum(lam[0:1] * lam[1:2], axis=1, keepdims=True))
                - jnp.exp(jnp.sum(lam[2:3] * lam[3:4], axis=1, keepdims=True)) + lambda_init)
    o_t = acc_ref[0] / l_ref[0] - lam_full * (acc_ref[1] / l_ref[1])
    inv = lax.rsqrt(jnp.mean(o_t * o_t, axis=0, keepdims=True) + NORM_EPS)
    o_t = o_t * inv * (g_ref[...] * (1.0 - lambda_init))
    o_ref[...] = o_t.T.astype(o_ref.dtype)


def _diff_attention(qt, k, vt, slope_hi, slope_lo, lam, g_col, batch, seq, lambda_init):
    nq = seq // T_TILE
    hh = DIFF_HEADS
    return pl.pallas_call(
        functools.partial(_diff_attn_kernel, lambda_init=lambda_init),
        out_shape=jax.ShapeDtypeStruct((batch * seq, hh * DIFF_V_DIM), BF16),
        grid=(batch, hh, nq),
        in_specs=[pl.BlockSpec(memory_space=pltpu.SMEM),
                  pl.BlockSpec(memory_space=pltpu.SMEM),
                  pl.BlockSpec(lam.shape, lambda b, h, i: (0, 0)),
                  pl.BlockSpec(g_col.shape, lambda b, h, i: (0, 0)),
                  pl.BlockSpec((None, None, LANES, T_TILE), lambda b, h, i: (b * nq + i, h, 0, 0)),
                  pl.BlockSpec((seq, LANES), lambda b, h, i: (b, h)),
                  pl.BlockSpec((nq, None, DIFF_V_DIM, T_TILE), lambda b, h, i: (b, h, 0, 0))],
        out_specs=pl.BlockSpec((T_TILE, LANES), lambda b, h, i: (b * nq + i, h)),
        scratch_shapes=[pltpu.VMEM((2, 1, T_TILE), F32), pltpu.VMEM((2, 1, T_TILE), F32),
                        pltpu.VMEM((2, DIFF_V_DIM, T_TILE), F32)],
        compiler_params=_cparams(("parallel", "parallel", "arbitrary")),
        name="diff_attention",
    )(slope_hi, slope_lo, lam, g_col, qt, k, vt)


def _mla_attn_kernel(qt_ref, k_ref, vt_ref, o_ref, m_ref, l_ref, acc_ref):
    i = pl.program_id(2)
    qt = qt_ref[...]
    _init_softmax(m_ref, l_ref, acc_ref)

    def key_tiles(j0, n_past, with_diagonal):
        tiles = []
        for u in range(n_past):
            off = pl.multiple_of((j0 + u) * T_TILE, T_TILE)
            tiles.append((_dot(k_ref[pl.ds(off, T_TILE), :], qt), 0.0, vt_ref[j0 + u]))
        if with_diagonal:
            off = pl.multiple_of(i * T_TILE, T_TILE)
            _, _, visible = _key_query_iota()
            s_t = jnp.where(visible, _dot(k_ref[pl.ds(off, T_TILE), :], qt), -jnp.inf)
            tiles.append((s_t, 0.0, vt_ref[i]))
        _merge_tiles(_tile_softmax(tiles), 0, m_ref, l_ref, acc_ref)

    _for_key_tiles(i, key_tiles)
    o_ref[...] = (acc_ref[0] / l_ref[0]).T.astype(o_ref.dtype)


def _mla_attention(qt, kcat, vt, batch, seq):
    nq = seq // T_TILE
    return pl.pallas_call(
        _mla_attn_kernel,
        out_shape=jax.ShapeDtypeStruct((batch * seq, MLA_HEADS * MLA_V_DIM), BF16),
        grid=(batch, MLA_HEADS, nq),
        in_specs=[pl.BlockSpec((None, None, MLA_QK_PAD, T_TILE), lambda b, h, i: (b * nq + i, h, 0, 0)),
                  pl.BlockSpec((seq, MLA_QK_PAD), lambda b, h, i: (b, h)),
                  pl.BlockSpec((nq, None, MLA_V_DIM, T_TILE), lambda b, h, i: (b, h, 0, 0))],
        out_specs=pl.BlockSpec((T_TILE, MLA_V_DIM), lambda b, h, i: (b * nq + i, h)),
        scratch_shapes=[pltpu.VMEM((1, 1, T_TILE), F32), pltpu.VMEM((1, 1, T_TILE), F32),
                        pltpu.VMEM((1, MLA_V_DIM, T_TILE), F32)],
        compiler_params=_cparams(("parallel", "parallel", "arbitrary")),
        name="mla_attention",
    )(qt, kcat, vt)


def _split_bf16(x):
    hi = x.astype(BF16)
    lo = (x - hi.astype(F32)).astype(BF16)
    return hi, lo


def _route(logits_t, bias_t):
    score = 1.0 / (1.0 + jnp.exp(-logits_t))
    sel = score + bias_t
    a = [sel[N_GROUPS * j:N_GROUPS * (j + 1)] for j in range(GROUP_SIZE)]
    sc = [score[N_GROUPS * j:N_GROUPS * (j + 1)] for j in range(GROUP_SIZE)]
    hi1, lo1 = jnp.maximum(a[0], a[1]), jnp.minimum(a[0], a[1])
    hi2, lo2 = jnp.maximum(a[2], a[3]), jnp.minimum(a[2], a[3])
    group_score = jnp.maximum(hi1, hi2) + jnp.maximum(jnp.minimum(hi1, hi2), jnp.maximum(lo1, lo2))
    gidx = lax.broadcasted_iota(jnp.int32, group_score.shape, 0)
    best = jnp.max(group_score, axis=0, keepdims=True)
    first_best = jnp.min(jnp.where(group_score == best, gidx, N_GROUPS), axis=0, keepdims=True)
    in_group = gidx == first_best
    picked = []
    for j in range(GROUP_SIZE):
        beaten_by = jnp.zeros(a[j].shape, jnp.int32)
        for i in range(GROUP_SIZE):
            if i == j:
                continue
            ahead = (a[i] > a[j]) | ((a[i] == a[j]) if i < j else False)
            beaten_by = beaten_by + ahead.astype(jnp.int32)
        picked.append(jnp.where(beaten_by < 2, sc[j], 0.0))
    denom = (picked[0] + picked[1]) + (picked[2] + picked[3])
    member_w = [jnp.sum(jnp.where(in_group, p / denom, 0.0), axis=0, keepdims=True) for p in picked]
    return in_group, member_w


def _out_norm_route_kernel(a_ref, wo_ref, x_ref, g_ref, b_ref, rw_ref, rb_ref, tri_ref,
                           x1_ref, gid_ref, rank_ref, cnt_ref, seen_ref):
    @pl.when(pl.program_id(0) == 0)
    def _():
        seen_ref[...] = jnp.zeros(seen_ref.shape, F32)

    y = _dot(a_ref[...], wo_ref[...])
    x1 = _ln(DEEPNORM_ALPHA * x_ref[...] + y, g_ref[...], b_ref[...])
    x1_ref[:, :D_MODEL] = x1
    xh, xl = _split_bf16(x1)
    wh, wl = _split_bf16(rw_ref[...])
    logits_t = _dot_nt(wh, xh) + (_dot_nt(wh, xl) + _dot_nt(wl, xh))
    in_group, member_w = _route(logits_t, rb_ref[...])
    tm = logits_t.shape[1]

    row = lax.broadcasted_iota(jnp.int32, (N_GROUPS, tm), 0)
    w_rows = jnp.zeros((N_GROUPS, tm), F32)
    for j in range(GROUP_SIZE):
        w_rows = jnp.where(row == j, member_w[j], w_rows)
    w_rows = jnp.concatenate([w_rows, jnp.zeros((LANES - N_GROUPS, tm), F32)], axis=0)
    x1_ref[:, D_MODEL:] = w_rows.T

    onehot = in_group.astype(F32)
    earlier = _dot(onehot.astype(BF16), tri_ref[...])
    seen = seen_ref[:, 0:1]
    gidx = lax.broadcasted_iota(jnp.int32, onehot.shape, 0)
    gid_ref[...] = jnp.sum(jnp.where(in_group, gidx, 0), axis=0, keepdims=True)
    rank_ref[...] = jnp.sum(onehot * (earlier + seen), axis=0, keepdims=True).astype(jnp.int32)
    seen_new = seen_ref[...] + jnp.sum(onehot, axis=1, keepdims=True)
    seen_ref[...] = seen_new
    cnt_ref[...] = seen_new.astype(jnp.int32)


def _out_norm_route(a, wo, x, g, b, rw_t, rb_t, tri, tm=ROUTE_TILE):
    t, d = x.shape
    return pl.pallas_call(
        _out_norm_route_kernel,
        out_shape=(jax.ShapeDtypeStruct((t, ROW_WIDE), F32),
                   jax.ShapeDtypeStruct((1, t), jnp.int32),
                   jax.ShapeDtypeStruct((1, t), jnp.int32),
                   jax.ShapeDtypeStruct((N_GROUPS, LANES), jnp.int32)),
        grid=(t // tm,),
        in_specs=[pl.BlockSpec((tm, d), lambda i: (i, 0)),
                  pl.BlockSpec((d, d), lambda i: (0, 0)),
                  pl.BlockSpec((tm, d), lambda i: (i, 0)),
                  pl.BlockSpec((1, d), lambda i: (0, 0)),
                  pl.BlockSpec((1, d), lambda i: (0, 0)),
                  pl.BlockSpec((N_EXPERTS, d), lambda i: (0, 0)),
                  pl.BlockSpec((N_EXPERTS, 1), lambda i: (0, 0)),
                  pl.BlockSpec((tm, tm), lambda i: (0, 0))],
        out_specs=(pl.BlockSpec((tm, ROW_WIDE), lambda i: (i, 0)),
                   pl.BlockSpec((1, tm), lambda i: (0, i)),
                   pl.BlockSpec((1, tm), lambda i: (0, i)),
                   pl.BlockSpec((N_GROUPS, LANES), lambda i: (0, 0))),
        scratch_shapes=[pltpu.VMEM((N_GROUPS, LANES), F32)],
        compiler_params=_cparams(("arbitrary",)),
        name="out_norm_route",
    )(a, wo, x, g, b, rw_t, rb_t, tri)


def _row_copy(src_hbm, dst_hbm, src_row, dst_row, sem):
    return pltpu.make_async_copy(src_hbm.at[pl.ds(src_row, 1)], dst_hbm.at[pl.ds(dst_row, 1)], sem)


def _scatter_rows_kernel(dest_ref, src_hbm, base_hbm, out_hbm, sem):
    del base_hbm
    first = pl.program_id(0) * PERM_ROWS

    def start(r, carry):
        _row_copy(src_hbm, out_hbm, first + r, dest_ref[0, r], sem).start()
        return carry

    def wait(r, carry):
        _row_copy(src_hbm, out_hbm, 0, 0, sem).wait()
        return carry

    lax.fori_loop(0, PERM_ROWS, start, 0, unroll=8)
    lax.fori_loop(0, PERM_ROWS, wait, 0, unroll=8)


def _scatter_rows(src, dest, base):
    t = src.shape[0]
    return pl.pallas_call(
        _scatter_rows_kernel,
        out_shape=jax.ShapeDtypeStruct(base.shape, base.dtype),
        grid=(t // PERM_ROWS,),
        in_specs=[pl.BlockSpec((None, 1, PERM_ROWS), lambda i: (i, 0, 0), memory_space=pltpu.SMEM),
                  pl.BlockSpec(memory_space=pl.ANY),
                  pl.BlockSpec(memory_space=pl.ANY)],
        out_specs=pl.BlockSpec(memory_space=pl.ANY),
        scratch_shapes=[pltpu.SemaphoreType.DMA],
        input_output_aliases={2: 0},
        compiler_params=_cparams(("arbitrary",)),
        name="scatter_rows",
    )(dest.reshape(t // PERM_ROWS, 1, PERM_ROWS), src, base)


def _gather_rows_kernel(dest_ref, src_hbm, out_hbm, sem):
    first = pl.program_id(0) * PERM_ROWS

    def start(r, carry):
        _row_copy(src_hbm, out_hbm, dest_ref[0, r], first + r, sem).start()
        return carry

    def wait(r, carry):
        _row_copy(src_hbm, out_hbm, 0, 0, sem).wait()
        return carry

    lax.fori_loop(0, PERM_ROWS, start, 0, unroll=8)
    lax.fori_loop(0, PERM_ROWS, wait, 0, unroll=8)


def _gather_rows(src, dest):
    t = dest.shape[0]
    return pl.pallas_call(
        _gather_rows_kernel,
        out_shape=jax.ShapeDtypeStruct((t, src.shape[1]), src.dtype),
        grid=(t // PERM_ROWS,),
        in_specs=[pl.BlockSpec((None, 1, PERM_ROWS), lambda i: (i, 0, 0), memory_space=pltpu.SMEM),
                  pl.BlockSpec(memory_space=pl.ANY)],
        out_specs=pl.BlockSpec(memory_space=pl.ANY),
        scratch_shapes=[pltpu.SemaphoreType.DMA],
        compiler_params=_cparams(("arbitrary",)),
        name="gather_rows",
    )(dest.reshape(t // PERM_ROWS, 1, PERM_ROWS), src)


def _moe_kernel(tile_group_ref, n_tiles_ref, xs_ref, wg_ref, wu_ref, wd_ref, g_ref, b_ref, o_ref):
    del tile_group_ref
    k = pl.program_id(0)

    @pl.when(k < n_tiles_ref[0])
    def _():
        x = xs_ref[:, :D_MODEL]
        cw = xs_ref[:, D_MODEL:]
        xb = x.astype(BF16)
        y = jnp.zeros(x.shape, F32)
        for j in range(GROUP_SIZE):
            hg = _dot(xb, wg_ref[j])
            hu = _dot(xb, wu_ref[j])
            hid = (hg / (1.0 + jnp.exp(-hg))) * hu * cw[:, j:j + 1]
            y = y + _dot(hid.astype(BF16), wd_ref[j])
        o_ref[...] = _ln(DEEPNORM_ALPHA * x + y, g_ref[...], b_ref[...])

    @pl.when(k >= n_tiles_ref[0])
    def _():
        o_ref[...] = jnp.zeros(o_ref.shape, F32)


def _moe(xs, tile_group, n_tiles, wg, wu, wd, g, b, layer):
    tp = xs.shape[0]
    d, gs, ff = D_MODEL, GROUP_SIZE, EXPERT_FF
    grid_spec = pltpu.PrefetchScalarGridSpec(
        num_scalar_prefetch=2,
        grid=(tp // MOE_TILE,),
        in_specs=[pl.BlockSpec((MOE_TILE, ROW_WIDE), lambda k, tg, nt: (k, 0)),
                  pl.BlockSpec((None, gs, d, ff), lambda k, tg, nt: (layer, tg[k], 0, 0)),
                  pl.BlockSpec((None, gs, d, ff), lambda k, tg, nt: (layer, tg[k], 0, 0)),
                  pl.BlockSpec((None, gs, ff, d), lambda k, tg, nt: (layer, tg[k], 0, 0)),
                  pl.BlockSpec((1, d), lambda k, tg, nt: (0, 0)),
                  pl.BlockSpec((1, d), lambda k, tg, nt: (0, 0))],
        out_specs=pl.BlockSpec((MOE_TILE, d), lambda k, tg, nt: (k, 0)))
    return pl.pallas_call(
        _moe_kernel,
        out_shape=jax.ShapeDtypeStruct((tp, d), F32),
        grid_spec=grid_spec,
        compiler_params=_cparams(("arbitrary",)),
        name="grouped_moe",
    )(tile_group, n_tiles, xs, wg, wu, wd, g, b)


def _sort_plan(gid, rank, counts, n_tiles_max):
    cnt = counts[:, 0]
    padded = ((cnt + MOE_TILE - 1) // MOE_TILE) * MOE_TILE
    ends = jnp.cumsum(padded)
    starts = ends - padded
    dest = starts[gid[0]] + rank[0]
    n_tiles = ends[-1] // MOE_TILE
    tile_row = jnp.arange(n_tiles_max, dtype=jnp.int32) * MOE_TILE
    tile_group = jnp.sum((tile_row[:, None] >= ends[None, :]).astype(jnp.int32), axis=1)
    last_group = jnp.sum((ends[-1] - 1 >= ends).astype(jnp.int32))
    tile_group = jnp.minimum(tile_group, last_group)
    return dest.astype(jnp.int32), tile_group.astype(jnp.int32), n_tiles.reshape(1).astype(jnp.int32)


def _rope_tables(seq):
    inv_freq = ROPE_BASE ** (-jnp.arange(0, MLA_ROPE_DIM, 2, dtype=F32) / MLA_ROPE_DIM)
    ang = jnp.arange(seq, dtype=F32)[:, None] * inv_freq[None, :]
    cos, sin = jnp.cos(ang), jnp.sin(ang)
    cos2 = jnp.concatenate([cos, cos], axis=1)
    sin2 = jnp.concatenate([-sin, sin], axis=1)
    zero = jnp.zeros((seq, LANES - MLA_ROPE_DIM), F32)
    return (jnp.concatenate([cos2, zero], axis=1), jnp.concatenate([sin2, zero], axis=1), cos2.T, sin2.T)


def _swap_halves(w):
    half = MLA_ROPE_DIM // 2
    return jnp.concatenate([w[..., half:], w[..., :half]], axis=-1)


def kernel(x, diff_w_qkv, diff_lambda, diff_subln_g, diff_w_o, mla_w_kv_a, mla_kv_norm_g, mla_w_kv_b,
           mla_w_q_a, mla_q_norm_g, mla_w_q_b, mla_w_o, router_w, router_b, moe_w_gate, moe_w_up,
           moe_w_down, ln_g, ln_b):
    batch, seq, d = x.shape
    t = batch * seq
    xs = x.reshape(t, d)

    wq, wk, wv = jnp.split(diff_w_qkv, 3, axis=-1)
    w_k = wk.astype(BF16)
    w_qv_t = jnp.concatenate([wq * (LOG2E * DIFF_HEAD_DIM ** -0.5), wv], axis=-1).transpose(0, 2, 1).astype(BF16)
    w_o_a = diff_w_o.astype(BF16)
    w_o_b = mla_w_o.astype(BF16)
    wg = moe_w_gate.astype(BF16)
    wu = moe_w_up.astype(BF16)
    wd = moe_w_down.astype(BF16)
    slopes = LOG2E * 2.0 ** (-8.0 * jnp.arange(1, DIFF_HEADS + 1, dtype=F32) / DIFF_HEADS)
    slope_hi = slopes.astype(BF16).astype(F32)
    slope_lo = (slopes - slope_hi).astype(BF16).astype(F32)
    rw_t = router_w.T.reshape(N_GROUPS, GROUP_SIZE, d).transpose(1, 0, 2).reshape(N_EXPERTS, d)
    rb_t = router_b.reshape(N_GROUPS, GROUP_SIZE).T.reshape(N_EXPERTS, 1).astype(F32)
    cs, sn, cs_t, sn_t = _rope_tables(seq)
    kv_rope_w = mla_w_kv_a[:, MLA_KV_RANK:]
    w_kv_a = jnp.concatenate([mla_w_kv_a, _swap_halves(kv_rope_w)], axis=1).astype(BF16)
    w_kv_b = mla_w_kv_b.reshape(MLA_KV_RANK, MLA_HEADS, MLA_NOPE_DIM + MLA_V_DIM)
    w_kn = w_kv_b[..., :MLA_NOPE_DIM].reshape(MLA_KV_RANK, -1).astype(BF16)
    w_v_t = w_kv_b[..., MLA_NOPE_DIM:].reshape(MLA_KV_RANK, -1).T.astype(BF16)
    wqb = mla_w_q_b.reshape(-1, MLA_Q_RANK, MLA_HEADS, MLA_NOPE_DIM + MLA_ROPE_DIM)
    w_q_b_t = jnp.concatenate([wqb, _swap_halves(wqb[..., MLA_NOPE_DIM:])], axis=-1).reshape(
        -1, MLA_Q_RANK, MLA_HEADS * MLA_QK_PAD).transpose(0, 2, 1).astype(BF16)
    w_q_a = mla_w_q_a.astype(BF16)

    tri = jnp.triu(jnp.ones((ROUTE_TILE, ROUTE_TILE), BF16), 1)
    n_tiles_max = t // MOE_TILE + N_GROUPS
    sorted_rows = jnp.zeros((n_tiles_max * MOE_TILE, ROW_WIDE), F32)

    kcat = vt_shared = None
    for layer in range(DEPTH):
        if layer < N_A_LAYERS:
            k, qt, vt = _qkv_proj(xs, w_k[layer], w_qv_t[layer])
            attn = _diff_attention(qt, k, vt, slope_hi, slope_lo, diff_lambda[layer],
                                   diff_subln_g[layer].reshape(-1, 1), batch, seq, _lambda_init(layer))
            w_o = w_o_a[layer]
        else:
            j = layer - N_A_LAYERS
            if j == 0:
                kcat, vt_shared = _mla_kv_proj(xs, w_kv_a, mla_kv_norm_g.reshape(1, -1), w_kn, w_v_t, cs, sn, seq)
            qt = _mla_q_proj(xs, w_q_a[j], mla_q_norm_g[j].reshape(1, -1), w_q_b_t[j], cs_t, sn_t, seq)
            attn = _mla_attention(qt, kcat, vt_shared, batch, seq)
            w_o = w_o_b[j]
        x1, gid, rank, counts = _out_norm_route(attn, w_o, xs, ln_g[layer, 0].reshape(1, -1),
                                                ln_b[layer, 0].reshape(1, -1), rw_t, rb_t, tri)
        dest, tile_group, n_tiles = _sort_plan(gid, rank, counts, n_tiles_max)
        sorted_rows = _scatter_rows(x1, dest, sorted_rows)
        ys = _moe(sorted_rows, tile_group, n_tiles, wg, wu, wd, ln_g[layer, 1].reshape(1, -1),
                  ln_b[layer, 1].reshape(1, -1), layer)
        xs = _gather_rows(ys, dest)
    return xs.reshape(batch, seq, d)
```

```python
import functools
import math

import jax
import jax.numpy as jnp
from jax import lax
from jax.experimental import pallas as pl
from jax.experimental.pallas import tpu as pltpu

D_MODEL = 1024
DEPTH = 4
CHUNK = 64
N_A_LAYERS = DEPTH // 2
DIFF_HEAD_DIM = 64
DIFF_V_DIM = 2 * DIFF_HEAD_DIM
DIFF_HEADS = D_MODEL // DIFF_V_DIM
MLA_NOPE_DIM = 128
MLA_ROPE_DIM = 64
MLA_V_DIM = 128
MLA_HEADS = D_MODEL // MLA_V_DIM
MLA_Q_RANK = 384
MLA_KV_RANK = 256
MLA_QK_PAD = 256
ROPE_BASE = 10000.0
N_EXPERTS = 32
N_GROUPS = 8
GROUP_SIZE = N_EXPERTS // N_GROUPS
EXPERT_FF = 256
NORM_EPS = 1e-5
DEEPNORM_ALPHA = (2 * DEPTH) ** 0.25
LOG2E = math.log2(math.e)

LANES = 128
T_TILE = 256
KV_UNROLL = 4
ROUTE_TILE = 256
MOE_TILE = 256
PERM_ROWS = 256
ROW_WIDE = D_MODEL + LANES
VMEM_LIMIT = 48 * 1024 * 1024

F32 = jnp.float32
BF16 = jnp.bfloat16


def _lambda_init(layer):
    return 0.8 - 0.6 * math.exp(-0.3 * layer)


def _cparams(sem):
    return pltpu.CompilerParams(dimension_semantics=sem, vmem_limit_bytes=VMEM_LIMIT)


def _dot(a, b):
    return jnp.dot(a, b, preferred_element_type=F32)


def _dot_nt(a, b):
    return lax.dot_general(a, b, (((1,), (1,)), ((), ())), preferred_element_type=F32)


def _rms(x, g):
    return x * lax.rsqrt(jnp.mean(x * x, axis=-1, keepdims=True) + NORM_EPS) * g


def _ln(x, g, b):
    mu = jnp.mean(x, axis=-1, keepdims=True)
    xc = x - mu
    var = jnp.mean(xc * xc, axis=-1, keepdims=True)
    return xc * lax.rsqrt(var + NORM_EPS) * g + b


def _rope_halves(a, cs, sn):
    return a * cs + pltpu.roll(a, MLA_ROPE_DIM, axis=1) * sn


def _qkv_kernel(x_ref, wk_ref, wqv_ref, k_ref, qt_ref, vt_ref):
    xb = x_ref[...].astype(BF16)
    k_ref[...] = _dot(xb, wk_ref[...]).astype(BF16)
    qv = _dot_nt(wqv_ref[...], xb)
    for h in range(DIFF_HEADS):
        qt_ref[0, h] = qv[h * LANES:(h + 1) * LANES].astype(BF16)
        vt_ref[0, h] = qv[D_MODEL + h * LANES:D_MODEL + (h + 1) * LANES].astype(BF16)


def _qkv_proj(x, wk, wqv):
    t, d = x.shape
    nt = t // T_TILE
    head_major = jax.ShapeDtypeStruct((nt, DIFF_HEADS, LANES, T_TILE), BF16)
    head_block = pl.BlockSpec((1, DIFF_HEADS, LANES, T_TILE), lambda i: (i, 0, 0, 0))
    return pl.pallas_call(
        _qkv_kernel,
        out_shape=(jax.ShapeDtypeStruct((t, d), BF16), head_major, head_major),
        grid=(nt,),
        in_specs=[pl.BlockSpec((T_TILE, d), lambda i: (i, 0)),
                  pl.BlockSpec(wk.shape, lambda i: (0, 0)),
                  pl.BlockSpec(wqv.shape, lambda i: (0, 0))],
        out_specs=(pl.BlockSpec((T_TILE, d), lambda i: (i, 0)), head_block, head_block),
        compiler_params=_cparams(("parallel",)),
        name="qkv_proj",
    )(x, wk, wqv)


def _mla_kv_kernel(x_ref, wa_ref, g_ref, wk_ref, wv_ref, cs_ref, sn_ref, kcat_ref, vt_ref):
    kva = _dot(x_ref[...].astype(BF16), wa_ref[...])
    c_kv = _rms(kva[:, :MLA_KV_RANK], g_ref[...]).astype(BF16)
    kr = _rope_halves(kva[:, MLA_KV_RANK:], cs_ref[...], sn_ref[...]).astype(BF16)
    kn = _dot(c_kv, wk_ref[...])
    vt = _dot_nt(wv_ref[...], c_kv)
    for h in range(MLA_HEADS):
        base = h * MLA_QK_PAD
        kcat_ref[:, base:base + MLA_NOPE_DIM] = kn[:, h * MLA_NOPE_DIM:(h + 1) * MLA_NOPE_DIM].astype(BF16)
        kcat_ref[:, base + MLA_NOPE_DIM:base + MLA_QK_PAD] = kr
        vt_ref[0, h] = vt[h * MLA_V_DIM:(h + 1) * MLA_V_DIM].astype(BF16)


def _mla_kv_proj(x, wa, g, wk, wv, cs, sn, seq):
    t, d = x.shape
    nt = t // T_TILE
    ns = seq // T_TILE
    return pl.pallas_call(
        _mla_kv_kernel,
        out_shape=(jax.ShapeDtypeStruct((t, MLA_HEADS * MLA_QK_PAD), BF16),
                   jax.ShapeDtypeStruct((nt, MLA_HEADS, MLA_V_DIM, T_TILE), BF16)),
        grid=(nt,),
        in_specs=[pl.BlockSpec((T_TILE, d), lambda i: (i, 0)),
                  pl.BlockSpec(wa.shape, lambda i: (0, 0)),
                  pl.BlockSpec(g.shape, lambda i: (0, 0)),
                  pl.BlockSpec(wk.shape, lambda i: (0, 0)),
                  pl.BlockSpec(wv.shape, lambda i: (0, 0)),
                  pl.BlockSpec((T_TILE, LANES), lambda i: (i % ns, 0)),
                  pl.BlockSpec((T_TILE, LANES), lambda i: (i % ns, 0))],
        out_specs=(pl.BlockSpec((T_TILE, MLA_HEADS * MLA_QK_PAD), lambda i: (i, 0)),
                   pl.BlockSpec((1, MLA_HEADS, MLA_V_DIM, T_TILE), lambda i: (i, 0, 0, 0))),
        compiler_params=_cparams(("parallel",)),
        name="mla_kv_proj",
    )(x, wa, g, wk, wv, cs, sn)


def _mla_q_kernel(x_ref, wa_ref, g_ref, wb_ref, cst_ref, snt_ref, qt_ref, *, scale):
    qa = _dot(x_ref[...].astype(BF16), wa_ref[...])
    qn = _rms(qa, g_ref[...]).astype(BF16)
    qt = _dot_nt(wb_ref[...], qn) * scale
    cst = cst_ref[...]
    snt = snt_ref[...]
    zero = jnp.zeros((MLA_ROPE_DIM, T_TILE), BF16)
    for h in range(MLA_HEADS):
        base = h * MLA_QK_PAD
        r0 = base + MLA_NOPE_DIM
        rope = qt[r0:r0 + MLA_ROPE_DIM] * cst + qt[r0 + MLA_ROPE_DIM:base + MLA_QK_PAD] * snt
        qt_ref[0, h, 0:MLA_NOPE_DIM] = qt[base:r0].astype(BF16)
        qt_ref[0, h, MLA_NOPE_DIM:MLA_NOPE_DIM + MLA_ROPE_DIM] = rope.astype(BF16)
        qt_ref[0, h, MLA_NOPE_DIM + MLA_ROPE_DIM:MLA_QK_PAD] = zero


def _mla_q_proj(x, wa, g, wb_t, cst, snt, seq):
    t, d = x.shape
    nt = t // T_TILE
    ns = seq // T_TILE
    scale = LOG2E * (MLA_NOPE_DIM + MLA_ROPE_DIM) ** -0.5
    return pl.pallas_call(
        functools.partial(_mla_q_kernel, scale=scale),
        out_shape=jax.ShapeDtypeStruct((nt, MLA_HEADS, MLA_QK_PAD, T_TILE), BF16),
        grid=(nt,),
        in_specs=[pl.BlockSpec((T_TILE, d), lambda i: (i, 0)),
                  pl.BlockSpec(wa.shape, lambda i: (0, 0)),
                  pl.BlockSpec(g.shape, lambda i: (0, 0)),
                  pl.BlockSpec(wb_t.shape, lambda i: (0, 0)),
                  pl.BlockSpec((MLA_ROPE_DIM, T_TILE), lambda i: (0, i % ns)),
                  pl.BlockSpec((MLA_ROPE_DIM, T_TILE), lambda i: (0, i % ns))],
        out_specs=pl.BlockSpec((1, MLA_HEADS, MLA_QK_PAD, T_TILE), lambda i: (i, 0, 0, 0)),
        compiler_params=_cparams(("parallel",)),
        name="mla_q_proj",
    )(x, wa, g, wb_t, cst, snt)


def _tile_softmax(tiles):
    probs = []
    for s_t, shift, vt in tiles:
        m_t = jnp.max(s_t, axis=0, keepdims=True)
        p_t = jnp.exp2(s_t - m_t)
        probs.append((m_t + shift, jnp.sum(p_t, axis=0, keepdims=True), p_t.astype(BF16), vt))
    return [(m_t, l_t, _dot(vt, p_t)) for m_t, l_t, p_t, vt in probs]


def _merge_tiles(parts, c, m_ref, l_ref, acc_ref):
    m_prev = m_ref[c]
    m_new = m_prev
    for m_t, _, _ in parts:
        m_new = jnp.maximum(m_new, m_t)
    alpha = jnp.exp2(m_prev - m_new)
    l = alpha * l_ref[c]
    acc = alpha * acc_ref[c]
    for m_t, l_t, pv in parts:
        w = jnp.exp2(m_t - m_new)
        l = l + w * l_t
        acc = acc + w * pv
    m_ref[c] = m_new
    l_ref[c] = l
    acc_ref[c] = acc


def _for_key_tiles(n_past, body):
    def group(g, carry):
        body(g * KV_UNROLL, KV_UNROLL, False)
        return carry

    n_groups = n_past // KV_UNROLL
    lax.fori_loop(0, n_groups, group, 0)
    left = n_past - n_groups * KV_UNROLL
    for r in range(KV_UNROLL):
        @pl.when(left == r)
        def _(r=r):
            body(n_groups * KV_UNROLL, r, True)


def _init_softmax(m_ref, l_ref, acc_ref):
    m_ref[...] = jnp.full(m_ref.shape, -jnp.inf, F32)
    l_ref[...] = jnp.zeros(l_ref.shape, F32)
    acc_ref[...] = jnp.zeros(acc_ref.shape, F32)


def _key_query_iota():
    rk = lax.broadcasted_iota(jnp.int32, (T_TILE, T_TILE), 0)
    rq = lax.broadcasted_iota(jnp.int32, (T_TILE, T_TILE), 1)
    return rk, rq, (rk // CHUNK) <= (rq // CHUNK)


def _diff_attn_kernel(slh_ref, sll_ref, lam_ref, g_ref, qt_ref, k_ref, vt_ref, o_ref,
                      m_ref, l_ref, acc_ref, *, lambda_init):
    h = pl.program_id(1)
    i = pl.program_id(2)
    sl_hi = slh_ref[h]
    sl_lo = sll_ref[h]
    slope = sl_hi + sl_lo
    row = lax.broadcasted_iota(jnp.int32, (LANES, T_TILE), 0)
    qf = qt_ref[...].astype(F32)
    qs = (jnp.where(row < DIFF_HEAD_DIM, qf, 0.0).astype(BF16),
          jnp.where(row >= DIFF_HEAD_DIM, qf, 0.0).astype(BF16))
    slope_rows = jnp.where(row == 0, sl_hi, jnp.where(row == 1, sl_lo, 0.0)).astype(BF16)
    qs_aug = [jnp.concatenate([q, slope_rows], axis=0) for q in qs]
    key_off = lax.broadcasted_iota(jnp.int32, (T_TILE, LANES), 0).astype(F32)
    lane = lax.broadcasted_iota(jnp.int32, (T_TILE, LANES), 1)
    key_cols = jnp.where(lane < 2, key_off, 0.0).astype(BF16)
    _init_softmax(m_ref, l_ref, acc_ref)

    def key_tiles(j0, n_past, with_diagonal):
        tiles = []
        for u in range(n_past):
            j = j0 + u
            off = pl.multiple_of(j * T_TILE, T_TILE)
            k_aug = jnp.concatenate([k_ref[pl.ds(off, T_TILE), :], key_cols], axis=1)
            shift = slope * ((j - i) * T_TILE).astype(F32)
            tiles += [(_dot(k_aug, qs_aug[c]), shift, vt_ref[j]) for c in range(2)]
        if with_diagonal:
            off = pl.multiple_of(i * T_TILE, T_TILE)
            kb = k_ref[pl.ds(off, T_TILE), :]
            rk, rq, visible = _key_query_iota()
            bias = slope * (rq - jnp.abs(rq - rk)).astype(F32)
            tiles += [(jnp.where(visible, _dot(kb, qs[c]) + bias, -jnp.inf), 0.0, vt_ref[i]) for c in range(2)]
        parts = _tile_softmax(tiles)
        for c in range(2):
            _merge_tiles(parts[c::2], c, m_ref, l_ref, acc_ref)

    _for_key_tiles(i, key_tiles)

    lam = lam_ref[...]
    lam_full = (jnp.exp(jnp.sum(lam[0:1] * lam[1:2], axis=1, keepdims=True))
                - jnp.exp(jnp.sum(lam[2:3] * lam[3:4], axis=1, keepdims=True)) + lambda_init)
    o_t = acc_ref[0] / l_ref[0] - lam_full * (acc_ref[1] / l_ref[1])
    inv = lax.rsqrt(jnp.mean(o_t * o_t, axis=0, keepdims=True) + NORM_EPS)
    o_t = o_t * inv * (g_ref[...] * (1.0 - lambda_init))
    o_ref[...] = o_t.T.astype(o_ref.dtype)


def _diff_attention(qt, k, vt, slope_hi, slope_lo, lam, g_col, batch, seq, lambda_init):
    nq = seq // T_TILE
    hh = DIFF_HEADS
    return pl.pallas_call(
        functools.partial(_diff_attn_kernel, lambda_init=lambda_init),
        out_shape=jax.ShapeDtypeStruct((batch * seq, hh * DIFF_V_DIM), BF16),
        grid=(batch, hh, nq),
        in_specs=[pl.BlockSpec(memory_space=pltpu.SMEM),
                  pl.BlockSpec(memory_space=pltpu.SMEM),
                  pl.BlockSpec(lam.shape, lambda b, h, i: (0, 0)),
                  pl.BlockSpec(g_col.shape, lambda b, h, i: (0, 0)),
                  pl.BlockSpec((None, None, LANES, T_TILE), lambda b, h, i: (b * nq + i, h, 0, 0)),
                  pl.BlockSpec((seq, LANES), lambda b, h, i: (b, h)),
                  pl.BlockSpec((nq, None, DIFF_V_DIM, T_TILE), lambda b, h, i: (b, h, 0, 0))],
        out_specs=pl.BlockSpec((T_TILE, LANES), lambda b, h, i: (b * nq + i, h)),
        scratch_shapes=[pltpu.VMEM((2, 1, T_TILE), F32), pltpu.VMEM((2, 1, T_TILE), F32),
                        pltpu.VMEM((2, DIFF_V_DIM, T_TILE), F32)],
        compiler_params=_cparams(("parallel", "parallel", "arbitrary")),
        name="diff_attention",
    )(slope_hi, slope_lo, lam, g_col, qt, k, vt)


def _mla_attn_kernel(qt_ref, k_ref, vt_ref, o_ref, m_ref, l_ref, acc_ref):
    i = pl.program_id(2)
    qt = qt_ref[...]
    _init_softmax(m_ref, l_ref, acc_ref)

    def key_tiles(j0, n_past, with_diagonal):
        tiles = []
        for u in range(n_past):
            off = pl.multiple_of((j0 + u) * T_TILE, T_TILE)
            tiles.append((_dot(k_ref[pl.ds(off, T_TILE), :], qt), 0.0, vt_ref[j0 + u]))
        if with_diagonal:
            off = pl.multiple_of(i * T_TILE, T_TILE)
            _, _, visible = _key_query_iota()
            s_t = jnp.where(visible, _dot(k_ref[pl.ds(off, T_TILE), :], qt), -jnp.inf)
            tiles.append((s_t, 0.0, vt_ref[i]))
        _merge_tiles(_tile_softmax(tiles), 0, m_ref, l_ref, acc_ref)

    _for_key_tiles(i, key_tiles)
    o_ref[...] = (acc_ref[0] / l_ref[0]).T.astype(o_ref.dtype)


def _mla_attention(qt, kcat, vt, batch, seq):
    nq = seq // T_TILE
    return pl.pallas_call(
        _mla_attn_kernel,
        out_shape=jax.ShapeDtypeStruct((batch * seq, MLA_HEADS * MLA_V_DIM), BF16),
        grid=(batch, MLA_HEADS, nq),
        in_specs=[pl.BlockSpec((None, None, MLA_QK_PAD, T_TILE), lambda b, h, i: (b * nq + i, h, 0, 0)),
                  pl.BlockSpec((seq, MLA_QK_PAD), lambda b, h, i: (b, h)),
                  pl.BlockSpec((nq, None, MLA_V_DIM, T_TILE), lambda b, h, i: (b, h, 0, 0))],
        out_specs=pl.BlockSpec((T_TILE, MLA_V_DIM), lambda b, h, i: (b * nq + i, h)),
        scratch_shapes=[pltpu.VMEM((1, 1, T_TILE), F32), pltpu.VMEM((1, 1, T_TILE), F32),
                        pltpu.VMEM((1, MLA_V_DIM, T_TILE), F32)],
        compiler_params=_cparams(("parallel", "parallel", "arbitrary")),
        name="mla_attention",
    )(qt, kcat, vt)


def _split_bf16(x):
    hi = x.astype(BF16)
    lo = (x - hi.astype(F32)).astype(BF16)
    return hi, lo


def _route(logits_t, bias_t):
    score = 1.0 / (1.0 + jnp.exp(-logits_t))
    sel = score + bias_t
    a = [sel[N_GROUPS * j:N_GROUPS * (j + 1)] for j in range(GROUP_SIZE)]
    sc = [score[N_GROUPS * j:N_GROUPS * (j + 1)] for j in range(GROUP_SIZE)]
    hi1, lo1 = jnp.maximum(a[0], a[1]), jnp.minimum(a[0], a[1])
    hi2, lo2 = jnp.maximum(a[2], a[3]), jnp.minimum(a[2], a[3])
    group_score = jnp.maximum(hi1, hi2) + jnp.maximum(jnp.minimum(hi1, hi2), jnp.maximum(lo1, lo2))
    gidx = lax.broadcasted_iota(jnp.int32, group_score.shape, 0)
    best = jnp.max(group_score, axis=0, keepdims=True)
    first_best = jnp.min(jnp.where(group_score == best, gidx, N_GROUPS), axis=0, keepdims=True)
    in_group = gidx == first_best
    picked = []
    for j in range(GROUP_SIZE):
        beaten_by = jnp.zeros(a[j].shape, jnp.int32)
        for i in range(GROUP_SIZE):
            if i == j:
                continue
            ahead = (a[i] > a[j]) | ((a[i] == a[j]) if i < j else False)
            beaten_by = beaten_by + ahead.astype(jnp.int32)
        picked.append(jnp.where(beaten_by < 2, sc[j], 0.0))
    denom = (picked[0] + picked[1]) + (picked[2] + picked[3])
    member_w = [jnp.sum(jnp.where(in_group, p / denom, 0.0), axis=0, keepdims=True) for p in picked]
    return in_group, member_w


def _out_norm_route_kernel(a_ref, wo_ref, x_ref, g_ref, b_ref, rw_ref, rb_ref, tri_ref,
                           x1_ref, gid_ref, rank_ref, cnt_ref, seen_ref):
    @pl.when(pl.program_id(0) == 0)
    def _():
        seen_ref[...] = jnp.zeros(seen_ref.shape, F32)

    y = _dot(a_ref[...], wo_ref[...])
    x1 = _ln(DEEPNORM_ALPHA * x_ref[...] + y, g_ref[...], b_ref[...])
    x1_ref[:, :D_MODEL] = x1
    xh, xl = _split_bf16(x1)
    wh, wl = _split_bf16(rw_ref[...])
    logits_t = _dot_nt(wh, xh) + (_dot_nt(wh, xl) + _dot_nt(wl, xh))
    in_group, member_w = _route(logits_t, rb_ref[...])
    tm = logits_t.shape[1]

    row = lax.broadcasted_iota(jnp.int32, (N_GROUPS, tm), 0)
    w_rows = jnp.zeros((N_GROUPS, tm), F32)
    for j in range(GROUP_SIZE):
        w_rows = jnp.where(row == j, member_w[j], w_rows)
    w_rows = jnp.concatenate([w_rows, jnp.zeros((LANES - N_GROUPS, tm), F32)], axis=0)
    x1_ref[:, D_MODEL:] = w_rows.T

    onehot = in_group.astype(F32)
    earlier = _dot(onehot.astype(BF16), tri_ref[...])
    seen = seen_ref[:, 0:1]
    gidx = lax.broadcasted_iota(jnp.int32, onehot.shape, 0)
    gid_ref[...] = jnp.sum(jnp.where(in_group, gidx, 0), axis=0, keepdims=True)
    rank_ref[...] = jnp.sum(onehot * (earlier + seen), axis=0, keepdims=True).astype(jnp.int32)
    seen_new = seen_ref[...] + jnp.sum(onehot, axis=1, keepdims=True)
    seen_ref[...] = seen_new
    cnt_ref[...] = seen_new.astype(jnp.int32)


def _out_norm_route(a, wo, x, g, b, rw_t, rb_t, tri, tm=ROUTE_TILE):
    t, d = x.shape
    return pl.pallas_call(
        _out_norm_route_kernel,
        out_shape=(jax.ShapeDtypeStruct((t, ROW_WIDE), F32),
                   jax.ShapeDtypeStruct((1, t), jnp.int32),
                   jax.ShapeDtypeStruct((1, t), jnp.int32),
                   jax.ShapeDtypeStruct((N_GROUPS, LANES), jnp.int32)),
        grid=(t // tm,),
        in_specs=[pl.BlockSpec((tm, d), lambda i: (i, 0)),
                  pl.BlockSpec((d, d), lambda i: (0, 0)),
                  pl.BlockSpec((tm, d), lambda i: (i, 0)),
                  pl.BlockSpec((1, d), lambda i: (0, 0)),
                  pl.BlockSpec((1, d), lambda i: (0, 0)),
                  pl.BlockSpec((N_EXPERTS, d), lambda i: (0, 0)),
                  pl.BlockSpec((N_EXPERTS, 1), lambda i: (0, 0)),
                  pl.BlockSpec((tm, tm), lambda i: (0, 0))],
        out_specs=(pl.BlockSpec((tm, ROW_WIDE), lambda i: (i, 0)),
                   pl.BlockSpec((1, tm), lambda i: (0, i)),
                   pl.BlockSpec((1, tm), lambda i: (0, i)),
                   pl.BlockSpec((N_GROUPS, LANES), lambda i: (0, 0))),
        scratch_shapes=[pltpu.VMEM((N_GROUPS, LANES), F32)],
        compiler_params=_cparams(("arbitrary",)),
        name="out_norm_route",
    )(a, wo, x, g, b, rw_t, rb_t, tri)


def _row_copy(src, dst, src_row, dst_row, sem):
    return pltpu.make_async_copy(src.at[pl.ds(src_row, 1)], dst.at[pl.ds(dst_row, 1)], sem)


def _copy_rows(src, dst, src_row_of, dst_row_of, sem):
    def start(r, carry):
        _row_copy(src, dst, src_row_of(r), dst_row_of(r), sem).start()
        return carry

    def wait(r, carry):
        _row_copy(src, dst, 0, 0, sem).wait()
        return carry

    lax.fori_loop(0, PERM_ROWS, start, 0, unroll=8)
    lax.fori_loop(0, PERM_ROWS, wait, 0, unroll=8)


def _scatter_rows_kernel(dest_ref, src_ref, base_hbm, out_hbm, sem):
    del base_hbm
    _copy_rows(src_ref, out_hbm, lambda r: r, lambda r: dest_ref[0, r], sem)


def _scatter_rows(src, dest, base):
    t, width = src.shape
    return pl.pallas_call(
        _scatter_rows_kernel,
        out_shape=jax.ShapeDtypeStruct(base.shape, base.dtype),
        grid=(t // PERM_ROWS,),
        in_specs=[pl.BlockSpec((None, 1, PERM_ROWS), lambda i: (i, 0, 0), memory_space=pltpu.SMEM),
                  pl.BlockSpec((PERM_ROWS, width), lambda i: (i, 0)),
                  pl.BlockSpec(memory_space=pl.ANY)],
        out_specs=pl.BlockSpec(memory_space=pl.ANY),
        scratch_shapes=[pltpu.SemaphoreType.DMA],
        input_output_aliases={2: 0},
        compiler_params=_cparams(("arbitrary",)),
        name="scatter_rows",
    )(dest.reshape(t // PERM_ROWS, 1, PERM_ROWS), src, base)


def _gather_rows_kernel(dest_ref, src_hbm, out_ref, sem):
    _copy_rows(src_hbm, out_ref, lambda r: dest_ref[0, r], lambda r: r, sem)


def _gather_rows(src, dest):
    t = dest.shape[0]
    width = src.shape[1]
    return pl.pallas_call(
        _gather_rows_kernel,
        out_shape=jax.ShapeDtypeStruct((t, width), src.dtype),
        grid=(t // PERM_ROWS,),
        in_specs=[pl.BlockSpec((None, 1, PERM_ROWS), lambda i: (i, 0, 0), memory_space=pltpu.SMEM),
                  pl.BlockSpec(memory_space=pl.ANY)],
        out_specs=pl.BlockSpec((PERM_ROWS, width), lambda i: (i, 0)),
        scratch_shapes=[pltpu.SemaphoreType.DMA],
        compiler_params=_cparams(("arbitrary",)),
        name="gather_rows",
    )(dest.reshape(t // PERM_ROWS, 1, PERM_ROWS), src)


def _moe_kernel(tile_group_ref, n_tiles_ref, xs_ref, wg_ref, wu_ref, wd_ref, g_ref, b_ref, o_ref):
    del tile_group_ref
    k = pl.program_id(0)

    @pl.when(k < n_tiles_ref[0])
    def _():
        x = xs_ref[:, :D_MODEL]
        cw = xs_ref[:, D_MODEL:]
        xb = x.astype(BF16)
        y = jnp.zeros(x.shape, F32)
        for j in range(GROUP_SIZE):
            hg = _dot(xb, wg_ref[j])
            hu = _dot(xb, wu_ref[j])
            hid = (hg / (1.0 + jnp.exp(-hg))) * hu * cw[:, j:j + 1]
            y = y + _dot(hid.astype(BF16), wd_ref[j])
        o_ref[...] = _ln(DEEPNORM_ALPHA * x + y, g_ref[...], b_ref[...])

    @pl.when(k >= n_tiles_ref[0])
    def _():
        o_ref[...] = jnp.zeros(o_ref.shape, F32)


def _moe(xs, tile_group, n_tiles, wg, wu, wd, g, b, layer):
    tp = xs.shape[0]
    d, gs, ff = D_MODEL, GROUP_SIZE, EXPERT_FF
    grid_spec = pltpu.PrefetchScalarGridSpec(
        num_scalar_prefetch=2,
        grid=(tp // MOE_TILE,),
        in_specs=[pl.BlockSpec((MOE_TILE, ROW_WIDE), lambda k, tg, nt: (k, 0)),
                  pl.BlockSpec((None, gs, d, ff), lambda k, tg, nt: (layer, tg[k], 0, 0)),
                  pl.BlockSpec((None, gs, d, ff), lambda k, tg, nt: (layer, tg[k], 0, 0)),
                  pl.BlockSpec((None, gs, ff, d), lambda k, tg, nt: (layer, tg[k], 0, 0)),
                  pl.BlockSpec((1, d), lambda k, tg, nt: (0, 0)),
                  pl.BlockSpec((1, d), lambda k, tg, nt: (0, 0))],
        out_specs=pl.BlockSpec((MOE_TILE, d), lambda k, tg, nt: (k, 0)))
    return pl.pallas_call(
        _moe_kernel,
        out_shape=jax.ShapeDtypeStruct((tp, d), F32),
        grid_spec=grid_spec,
        compiler_params=_cparams(("arbitrary",)),
        name="grouped_moe",
    )(tile_group, n_tiles, xs, wg, wu, wd, g, b)


def _sort_plan(gid, rank, counts, n_tiles_max):
    cnt = counts[:, 0]
    padded = ((cnt + MOE_TILE - 1) // MOE_TILE) * MOE_TILE
    ends = jnp.cumsum(padded)
    starts = ends - padded
    dest = starts[gid[0]] + rank[0]
    n_tiles = ends[-1] // MOE_TILE
    tile_row = jnp.arange(n_tiles_max, dtype=jnp.int32) * MOE_TILE
    tile_group = jnp.sum((tile_row[:, None] >= ends[None, :]).astype(jnp.int32), axis=1)
    last_group = jnp.sum((ends[-1] - 1 >= ends).astype(jnp.int32))
    tile_group = jnp.minimum(tile_group, last_group)
    return dest.astype(jnp.int32), tile_group.astype(jnp.int32), n_tiles.reshape(1).astype(jnp.int32)


def _rope_tables(seq):
    inv_freq = ROPE_BASE ** (-jnp.arange(0, MLA_ROPE_DIM, 2, dtype=F32) / MLA_ROPE_DIM)
    ang = jnp.arange(seq, dtype=F32)[:, None] * inv_freq[None, :]
    cos, sin = jnp.cos(ang), jnp.sin(ang)
    cos2 = jnp.concatenate([cos, cos], axis=1)
    sin2 = jnp.concatenate([-sin, sin], axis=1)
    zero = jnp.zeros((seq, LANES - MLA_ROPE_DIM), F32)
    return (jnp.concatenate([cos2, zero], axis=1), jnp.concatenate([sin2, zero], axis=1), cos2.T, sin2.T)


def _swap_halves(w):
    half = MLA_ROPE_DIM // 2
    return jnp.concatenate([w[..., half:], w[..., :half]], axis=-1)


def kernel(x, diff_w_qkv, diff_lambda, diff_subln_g, diff_w_o, mla_w_kv_a, mla_kv_norm_g, mla_w_kv_b,
           mla_w_q_a, mla_q_norm_g, mla_w_q_b, mla_w_o, router_w, router_b, moe_w_gate, moe_w_up,
           moe_w_down, ln_g, ln_b):
    batch, seq, d = x.shape
    t = batch * seq
    xs = x.reshape(t, d)

    wq, wk, wv = jnp.split(diff_w_qkv, 3, axis=-1)
    w_k = wk.astype(BF16)
    w_qv_t = jnp.concatenate([wq * (LOG2E * DIFF_HEAD_DIM ** -0.5), wv], axis=-1).transpose(0, 2, 1).astype(BF16)
    w_o_a = diff_w_o.astype(BF16)
    w_o_b = mla_w_o.astype(BF16)
    wg = moe_w_gate.astype(BF16)
    wu = moe_w_up.astype(BF16)
    wd = moe_w_down.astype(BF16)
    slopes = LOG2E * 2.0 ** (-8.0 * jnp.arange(1, DIFF_HEADS + 1, dtype=F32) / DIFF_HEADS)
    slope_hi = slopes.astype(BF16).astype(F32)
    slope_lo = (slopes - slope_hi).astype(BF16).astype(F32)
    rw_t = router_w.T.reshape(N_GROUPS, GROUP_SIZE, d).transpose(1, 0, 2).reshape(N_EXPERTS, d)
    rb_t = router_b.reshape(N_GROUPS, GROUP_SIZE).T.reshape(N_EXPERTS, 1).astype(F32)
    cs, sn, cs_t, sn_t = _rope_tables(seq)
    kv_rope_w = mla_w_kv_a[:, MLA_KV_RANK:]
    w_kv_a = jnp.concatenate([mla_w_kv_a, _swap_halves(kv_rope_w)], axis=1).astype(BF16)
    w_kv_b = mla_w_kv_b.reshape(MLA_KV_RANK, MLA_HEADS, MLA_NOPE_DIM + MLA_V_DIM)
    w_kn = w_kv_b[..., :MLA_NOPE_DIM].reshape(MLA_KV_RANK, -1).astype(BF16)
    w_v_t = w_kv_b[..., MLA_NOPE_DIM:].reshape(MLA_KV_RANK, -1).T.astype(BF16)
    wqb = mla_w_q_b.reshape(-1, MLA_Q_RANK, MLA_HEADS, MLA_NOPE_DIM + MLA_ROPE_DIM)
    w_q_b_t = jnp.concatenate([wqb, _swap_halves(wqb[..., MLA_NOPE_DIM:])], axis=-1).reshape(
        -1, MLA_Q_RANK, MLA_HEADS * MLA_QK_PAD).transpose(0, 2, 1).astype(BF16)
    w_q_a = mla_w_q_a.astype(BF16)

    tri = jnp.triu(jnp.ones((ROUTE_TILE, ROUTE_TILE), BF16), 1)
    n_tiles_max = t // MOE_TILE + N_GROUPS
    sorted_rows = jnp.zeros((n_tiles_max * MOE_TILE, ROW_WIDE), F32)

    kcat = vt_shared = None
    for layer in range(DEPTH):
        if layer < N_A_LAYERS:
            k, qt, vt = _qkv_proj(xs, w_k[layer], w_qv_t[layer])
            attn = _diff_attention(qt, k, vt, slope_hi, slope_lo, diff_lambda[layer],
                                   diff_subln_g[layer].reshape(-1, 1), batch, seq, _lambda_init(layer))
            w_o = w_o_a[layer]
        else:
            j = layer - N_A_LAYERS
            if j == 0:
                kcat, vt_shared = _mla_kv_proj(xs, w_kv_a, mla_kv_norm_g.reshape(1, -1), w_kn, w_v_t, cs, sn, seq)
            qt = _mla_q_proj(xs, w_q_a[j], mla_q_norm_g[j].reshape(1, -1), w_q_b_t[j], cs_t, sn_t, seq)
            attn = _mla_attention(qt, kcat, vt_shared, batch, seq)
            w_o = w_o_b[j]
        x1, gid, rank, counts = _out_norm_route(attn, w_o, xs, ln_g[layer, 0].reshape(1, -1),
                                                ln_b[layer, 0].reshape(1, -1), rw_t, rb_t, tri)
        dest, tile_group, n_tiles = _sort_plan(gid, rank, counts, n_tiles_max)
        sorted_rows = _scatter_rows(x1, dest, sorted_rows)
        ys = _moe(sorted_rows, tile_group, n_tiles, wg, wu, wd, ln_g[layer, 1].reshape(1, -1),
                  ln_b[layer, 1].reshape(1, -1), layer)
        xs = _gather_rows(ys, dest)
    return xs.reshape(batch, seq, d)
```

```python
import functools
import math

import jax
import jax.numpy as jnp
from jax import lax
from jax.experimental import pallas as pl
from jax.experimental.pallas import tpu as pltpu

D_MODEL = 1024
DEPTH = 4
CHUNK = 64
N_A_LAYERS = DEPTH // 2
DIFF_HEAD_DIM = 64
DIFF_V_DIM = 2 * DIFF_HEAD_DIM
DIFF_HEADS = D_MODEL // DIFF_V_DIM
MLA_NOPE_DIM = 128
MLA_ROPE_DIM = 64
MLA_V_DIM = 128
MLA_HEADS = D_MODEL // MLA_V_DIM
MLA_Q_RANK = 384
MLA_KV_RANK = 256
MLA_QK_PAD = 256
ROPE_BASE = 10000.0
N_EXPERTS = 32
N_GROUPS = 8
GROUP_SIZE = N_EXPERTS // N_GROUPS
EXPERT_FF = 256
NORM_EPS = 1e-5
DEEPNORM_ALPHA = (2 * DEPTH) ** 0.25
LOG2E = math.log2(math.e)

LANES = 128
T_TILE = 256
KV_UNROLL = 4
GROUP_TILES = 2
Q_TILE = GROUP_TILES * T_TILE
MASKED = 1e30
ROUTE_TILE = 256
MOE_TILE = 256
PERM_ROWS = 256
ROW_WIDE = D_MODEL + LANES
VMEM_LIMIT = 48 * 1024 * 1024

F32 = jnp.float32
BF16 = jnp.bfloat16


def _lambda_init(layer):
    return 0.8 - 0.6 * math.exp(-0.3 * layer)


def _cparams(sem):
    return pltpu.CompilerParams(dimension_semantics=sem, vmem_limit_bytes=VMEM_LIMIT)


def _dot(a, b):
    return jnp.dot(a, b, preferred_element_type=F32)


def _dot_nt(a, b):
    return lax.dot_general(a, b, (((1,), (1,)), ((), ())), preferred_element_type=F32)


def _rms(x, g):
    return x * lax.rsqrt(jnp.mean(x * x, axis=-1, keepdims=True) + NORM_EPS) * g


def _ln(x, g, b):
    mu = jnp.mean(x, axis=-1, keepdims=True)
    xc = x - mu
    var = jnp.mean(xc * xc, axis=-1, keepdims=True)
    return xc * lax.rsqrt(var + NORM_EPS) * g + b


def _rope_halves(a, cs, sn):
    return a * cs + pltpu.roll(a, MLA_ROPE_DIM, axis=1) * sn


def _qkv_kernel(x_ref, wk_ref, wqv_ref, k_ref, qt_ref, vt_ref):
    xb = x_ref[...].astype(BF16)
    k_ref[...] = _dot(xb, wk_ref[...]).astype(BF16)
    qv = _dot_nt(wqv_ref[...], xb)
    for h in range(DIFF_HEADS):
        qt_ref[0, h] = qv[h * LANES:(h + 1) * LANES].astype(BF16)
        vt_ref[0, h] = qv[D_MODEL + h * LANES:D_MODEL + (h + 1) * LANES].astype(BF16)


def _qkv_proj(x, wk, wqv):
    t, d = x.shape
    nt = t // T_TILE
    head_major = jax.ShapeDtypeStruct((nt, DIFF_HEADS, LANES, T_TILE), BF16)
    head_block = pl.BlockSpec((1, DIFF_HEADS, LANES, T_TILE), lambda i: (i, 0, 0, 0))
    return pl.pallas_call(
        _qkv_kernel,
        out_shape=(jax.ShapeDtypeStruct((t, d), BF16), head_major, head_major),
        grid=(nt,),
        in_specs=[pl.BlockSpec((T_TILE, d), lambda i: (i, 0)),
                  pl.BlockSpec(wk.shape, lambda i: (0, 0)),
                  pl.BlockSpec(wqv.shape, lambda i: (0, 0))],
        out_specs=(pl.BlockSpec((T_TILE, d), lambda i: (i, 0)), head_block, head_block),
        compiler_params=_cparams(("parallel",)),
        name="qkv_proj",
    )(x, wk, wqv)


def _mla_kv_kernel(x_ref, wa_ref, g_ref, wk_ref, wv_ref, cs_ref, sn_ref, kcat_ref, vt_ref):
    kva = _dot(x_ref[...].astype(BF16), wa_ref[...])
    c_kv = _rms(kva[:, :MLA_KV_RANK], g_ref[...]).astype(BF16)
    kr = _rope_halves(kva[:, MLA_KV_RANK:], cs_ref[...], sn_ref[...]).astype(BF16)
    kn = _dot(c_kv, wk_ref[...])
    vt = _dot_nt(wv_ref[...], c_kv)
    for h in range(MLA_HEADS):
        base = h * MLA_QK_PAD
        kcat_ref[:, base:base + MLA_NOPE_DIM] = kn[:, h * MLA_NOPE_DIM:(h + 1) * MLA_NOPE_DIM].astype(BF16)
        kcat_ref[:, base + MLA_NOPE_DIM:base + MLA_QK_PAD] = kr
        vt_ref[0, h] = vt[h * MLA_V_DIM:(h + 1) * MLA_V_DIM].astype(BF16)


def _mla_kv_proj(x, wa, g, wk, wv, cs, sn, seq):
    t, d = x.shape
    nt = t // T_TILE
    ns = seq // T_TILE
    return pl.pallas_call(
        _mla_kv_kernel,
        out_shape=(jax.ShapeDtypeStruct((t, MLA_HEADS * MLA_QK_PAD), BF16),
                   jax.ShapeDtypeStruct((nt, MLA_HEADS, MLA_V_DIM, T_TILE), BF16)),
        grid=(nt,),
        in_specs=[pl.BlockSpec((T_TILE, d), lambda i: (i, 0)),
                  pl.BlockSpec(wa.shape, lambda i: (0, 0)),
                  pl.BlockSpec(g.shape, lambda i: (0, 0)),
                  pl.BlockSpec(wk.shape, lambda i: (0, 0)),
                  pl.BlockSpec(wv.shape, lambda i: (0, 0)),
                  pl.BlockSpec((T_TILE, LANES), lambda i: (i % ns, 0)),
                  pl.BlockSpec((T_TILE, LANES), lambda i: (i % ns, 0))],
        out_specs=(pl.BlockSpec((T_TILE, MLA_HEADS * MLA_QK_PAD), lambda i: (i, 0)),
                   pl.BlockSpec((1, MLA_HEADS, MLA_V_DIM, T_TILE), lambda i: (i, 0, 0, 0))),
        compiler_params=_cparams(("parallel",)),
        name="mla_kv_proj",
    )(x, wa, g, wk, wv, cs, sn)


def _mla_q_kernel(x_ref, wa_ref, g_ref, wb_ref, cst_ref, snt_ref, qt_ref, *, scale):
    qa = _dot(x_ref[...].astype(BF16), wa_ref[...])
    qn = _rms(qa, g_ref[...]).astype(BF16)
    qt = _dot_nt(wb_ref[...], qn) * scale
    cst = cst_ref[...]
    snt = snt_ref[...]
    zero = jnp.zeros((MLA_ROPE_DIM, T_TILE), BF16)
    for h in range(MLA_HEADS):
        base = h * MLA_QK_PAD
        r0 = base + MLA_NOPE_DIM
        rope = qt[r0:r0 + MLA_ROPE_DIM] * cst + qt[r0 + MLA_ROPE_DIM:base + MLA_QK_PAD] * snt
        qt_ref[0, h, 0:MLA_NOPE_DIM] = qt[base:r0].astype(BF16)
        qt_ref[0, h, MLA_NOPE_DIM:MLA_NOPE_DIM + MLA_ROPE_DIM] = rope.astype(BF16)
        qt_ref[0, h, MLA_NOPE_DIM + MLA_ROPE_DIM:MLA_QK_PAD] = zero


def _mla_q_proj(x, wa, g, wb_t, cst, snt, seq):
    t, d = x.shape
    nt = t // T_TILE
    ns = seq // T_TILE
    scale = LOG2E * (MLA_NOPE_DIM + MLA_ROPE_DIM) ** -0.5
    return pl.pallas_call(
        functools.partial(_mla_q_kernel, scale=scale),
        out_shape=jax.ShapeDtypeStruct((nt, MLA_HEADS, MLA_QK_PAD, T_TILE), BF16),
        grid=(nt,),
        in_specs=[pl.BlockSpec((T_TILE, d), lambda i: (i, 0)),
                  pl.BlockSpec(wa.shape, lambda i: (0, 0)),
                  pl.BlockSpec(g.shape, lambda i: (0, 0)),
                  pl.BlockSpec(wb_t.shape, lambda i: (0, 0)),
                  pl.BlockSpec((MLA_ROPE_DIM, T_TILE), lambda i: (0, i % ns)),
                  pl.BlockSpec((MLA_ROPE_DIM, T_TILE), lambda i: (0, i % ns))],
        out_specs=pl.BlockSpec((1, MLA_HEADS, MLA_QK_PAD, T_TILE), lambda i: (i, 0, 0, 0)),
        compiler_params=_cparams(("parallel",)),
        name="mla_q_proj",
    )(x, wa, g, wb_t, cst, snt)


def _tile_softmax(tiles):
    probs = []
    for s_t, shift, vt in tiles:
        m_t = jnp.max(s_t, axis=0, keepdims=True)
        p_t = jnp.exp2(s_t - m_t)
        probs.append((m_t + shift, jnp.sum(p_t, axis=0, keepdims=True), p_t.astype(BF16), vt))
    return [(m_t, l_t, _dot(vt, p_t)) for m_t, l_t, p_t, vt in probs]


def _merge_tiles(parts, c, m_ref, l_ref, acc_ref):
    m_prev = m_ref[c]
    m_new = m_prev
    for m_t, _, _ in parts:
        m_new = jnp.maximum(m_new, m_t)
    alpha = jnp.exp2(m_prev - m_new)
    l = alpha * l_ref[c]
    acc = alpha * acc_ref[c]
    for m_t, l_t, pv in parts:
        w = jnp.exp2(m_t - m_new)
        l = l + w * l_t
        acc = acc + w * pv
    m_ref[c] = m_new
    l_ref[c] = l
    acc_ref[c] = acc


def _for_key_tiles(n_past, body):
    def group(g, carry):
        body(g * KV_UNROLL, KV_UNROLL, False)
        return carry

    n_groups = n_past // KV_UNROLL
    lax.fori_loop(0, n_groups, group, 0)
    left = n_past - n_groups * KV_UNROLL
    for r in range(KV_UNROLL):
        @pl.when(left == r)
        def _(r=r):
            body(n_groups * KV_UNROLL, r, True)


def _init_softmax(m_ref, l_ref, acc_ref):
    m_ref[...] = jnp.full(m_ref.shape, -jnp.inf, F32)
    l_ref[...] = jnp.zeros(l_ref.shape, F32)
    acc_ref[...] = jnp.zeros(acc_ref.shape, F32)


def _key_query_iota():
    rk = lax.broadcasted_iota(jnp.int32, (T_TILE, T_TILE), 0)
    rq = lax.broadcasted_iota(jnp.int32, (T_TILE, T_TILE), 1)
    return rk, rq, (rk // CHUNK) <= (rq // CHUNK)


def _diff_attn_kernel(slh_ref, sll_ref, lam_ref, g_ref, qt_ref, k_ref, vt_ref, o_ref,
                      m_ref, l_ref, acc_ref, *, lambda_init):
    h = pl.program_id(1)
    i = pl.program_id(2)
    sl_hi = slh_ref[h]
    sl_lo = sll_ref[h]
    slope = sl_hi + sl_lo
    row = lax.broadcasted_iota(jnp.int32, (LANES, T_TILE), 0)
    qf = qt_ref[...].astype(F32)
    qs = (jnp.where(row < DIFF_HEAD_DIM, qf, 0.0).astype(BF16),
          jnp.where(row >= DIFF_HEAD_DIM, qf, 0.0).astype(BF16))
    slope_rows = jnp.where(row == 0, sl_hi, jnp.where(row == 1, sl_lo, 0.0)).astype(BF16)
    qs_aug = [jnp.concatenate([q, slope_rows], axis=0) for q in qs]
    key_off = lax.broadcasted_iota(jnp.int32, (T_TILE, LANES), 0).astype(F32)
    lane = lax.broadcasted_iota(jnp.int32, (T_TILE, LANES), 1)
    key_cols = jnp.where(lane < 2, key_off, 0.0).astype(BF16)
    _init_softmax(m_ref, l_ref, acc_ref)

    def key_tiles(j0, n_past, with_diagonal):
        tiles = []
        for u in range(n_past):
            j = j0 + u
            off = pl.multiple_of(j * T_TILE, T_TILE)
            k_aug = jnp.concatenate([k_ref[pl.ds(off, T_TILE), :], key_cols], axis=1)
            shift = slope * ((j - i) * T_TILE).astype(F32)
            tiles += [(_dot(k_aug, qs_aug[c]), shift, vt_ref[j]) for c in range(2)]
        if with_diagonal:
            off = pl.multiple_of(i * T_TILE, T_TILE)
            kb = k_ref[pl.ds(off, T_TILE), :]
            rk, rq, visible = _key_query_iota()
            bias = slope * (rq - jnp.abs(rq - rk)).astype(F32)
            tiles += [(jnp.where(visible, _dot(kb, qs[c]) + bias, -jnp.inf), 0.0, vt_ref[i]) for c in range(2)]
        parts = _tile_softmax(tiles)
        for c in range(2):
            _merge_tiles(parts[c::2], c, m_ref, l_ref, acc_ref)

    _for_key_tiles(i, key_tiles)

    lam = lam_ref[...]
    lam_full = (jnp.exp(jnp.sum(lam[0:1] * lam[1:2], axis=1, keepdims=True))
                - jnp.exp(jnp.sum(lam[2:3] * lam[3:4], axis=1, keepdims=True)) + lambda_init)
    o_t = acc_ref[0] / l_ref[0] - lam_full * (acc_ref[1] / l_ref[1])
    inv = lax.rsqrt(jnp.mean(o_t * o_t, axis=0, keepdims=True) + NORM_EPS)
    o_t = o_t * inv * (g_ref[...] * (1.0 - lambda_init))
    o_ref[...] = o_t.T.astype(o_ref.dtype)


def _diff_attention(qt, k, vt, slope_hi, slope_lo, lam, g_col, batch, seq, lambda_init):
    nq = seq // T_TILE
    hh = DIFF_HEADS
    return pl.pallas_call(
        functools.partial(_diff_attn_kernel, lambda_init=lambda_init),
        out_shape=jax.ShapeDtypeStruct((batch * seq, hh * DIFF_V_DIM), BF16),
        grid=(batch, hh, nq),
        in_specs=[pl.BlockSpec(memory_space=pltpu.SMEM),
                  pl.BlockSpec(memory_space=pltpu.SMEM),
                  pl.BlockSpec(lam.shape, lambda b, h, i: (0, 0)),
                  pl.BlockSpec(g_col.shape, lambda b, h, i: (0, 0)),
                  pl.BlockSpec((None, None, LANES, T_TILE), lambda b, h, i: (b * nq + i, h, 0, 0)),
                  pl.BlockSpec((seq, LANES), lambda b, h, i: (b, h)),
                  pl.BlockSpec((nq, None, DIFF_V_DIM, T_TILE), lambda b, h, i: (b, h, 0, 0))],
        out_specs=pl.BlockSpec((T_TILE, LANES), lambda b, h, i: (b * nq + i, h)),
        scratch_shapes=[pltpu.VMEM((2, 1, T_TILE), F32), pltpu.VMEM((2, 1, T_TILE), F32),
                        pltpu.VMEM((2, DIFF_V_DIM, T_TILE), F32)],
        compiler_params=_cparams(("parallel", "parallel", "arbitrary")),
        name="diff_attention",
    )(slope_hi, slope_lo, lam, g_col, qt, k, vt)


def _mla_attn_kernel(qt_ref, k_ref, vt_ref, o_ref, m_ref, l_ref, acc_ref):
    i = pl.program_id(2)
    qt = qt_ref[...]
    _init_softmax(m_ref, l_ref, acc_ref)

    def key_tiles(j0, n_past, with_diagonal):
        tiles = []
        for u in range(n_past):
            off = pl.multiple_of((j0 + u) * T_TILE, T_TILE)
            tiles.append((_dot(k_ref[pl.ds(off, T_TILE), :], qt), 0.0, vt_ref[j0 + u]))
        if with_diagonal:
            off = pl.multiple_of(i * T_TILE, T_TILE)
            _, _, visible = _key_query_iota()
            s_t = jnp.where(visible, _dot(k_ref[pl.ds(off, T_TILE), :], qt), -jnp.inf)
            tiles.append((s_t, 0.0, vt_ref[i]))
        _merge_tiles(_tile_softmax(tiles), 0, m_ref, l_ref, acc_ref)

    _for_key_tiles(i, key_tiles)
    o_ref[...] = (acc_ref[0] / l_ref[0]).T.astype(o_ref.dtype)


def _mla_attention(qt, kcat, vt, batch, seq):
    nq = seq // T_TILE
    return pl.pallas_call(
        _mla_attn_kernel,
        out_shape=jax.ShapeDtypeStruct((batch * seq, MLA_HEADS * MLA_V_DIM), BF16),
        grid=(batch, MLA_HEADS, nq),
        in_specs=[pl.BlockSpec((None, None, MLA_QK_PAD, T_TILE), lambda b, h, i: (b * nq + i, h, 0, 0)),
                  pl.BlockSpec((seq, MLA_QK_PAD), lambda b, h, i: (b, h)),
                  pl.BlockSpec((nq, None, MLA_V_DIM, T_TILE), lambda b, h, i: (b, h, 0, 0))],
        out_specs=pl.BlockSpec((T_TILE, MLA_V_DIM), lambda b, h, i: (b * nq + i, h)),
        scratch_shapes=[pltpu.VMEM((1, 1, T_TILE), F32), pltpu.VMEM((1, 1, T_TILE), F32),
                        pltpu.VMEM((1, MLA_V_DIM, T_TILE), F32)],
        compiler_params=_cparams(("parallel", "parallel", "arbitrary")),
        name="mla_attention",
    )(qt, kcat, vt)


def _consume_group(tiles, n_comp, m_ref, l_ref, acc_ref):
    parts = []
    for s_t, shift, vt in tiles:
        m_t = jnp.max(s_t, axis=0, keepdims=True)
        p_t = jnp.exp2(s_t - m_t)
        parts.append((m_t + shift, jnp.sum(p_t, axis=0, keepdims=True), _dot(vt, p_t.astype(BF16))))
    for c in range(n_comp):
        _merge_tiles(parts[c::n_comp], c, m_ref, l_ref, acc_ref)


def _attention_pipeline(n_prev_groups, issue_scores, consume, buf_a, buf_b):
    issue_scores(0, buf_a)

    def pair(p, carry):
        issue_scores(2 * p + 1, buf_b)
        consume(2 * p, buf_a, False)
        issue_scores(2 * p + 2, buf_a)
        consume(2 * p + 1, buf_b, False)
        return carry

    lax.fori_loop(0, n_prev_groups // 2, pair, 0)
    odd = n_prev_groups % 2

    @pl.when(odd == 1)
    def _():
        issue_scores(n_prev_groups, buf_b)
        consume(n_prev_groups - 1, buf_a, False)
        consume(n_prev_groups, buf_b, True)

    @pl.when(odd == 0)
    def _():
        consume(n_prev_groups, buf_a, True)


def _diff_attn2_kernel(slh_ref, sll_ref, lam_ref, g_ref, dm_ref, qt_ref, k_ref, vt_ref, o_ref,
                       buf_a, buf_b, m_ref, l_ref, acc_ref, *, lambda_init):
    h = pl.program_id(1)
    i = pl.program_id(2)
    sl_hi = slh_ref[h]
    sl_lo = sll_ref[h]
    slope = sl_hi + sl_lo
    row = lax.broadcasted_iota(jnp.int32, (LANES, Q_TILE), 0)
    qf = jnp.concatenate([qt_ref[u] for u in range(GROUP_TILES)], axis=1).astype(F32)
    slope_rows = jnp.where(row == 0, sl_hi, jnp.where(row == 1, sl_lo, 0.0)).astype(BF16)
    qs_aug = [jnp.concatenate([jnp.where(keep, qf, 0.0).astype(BF16), slope_rows], axis=0)
              for keep in (row < DIFF_HEAD_DIM, row >= DIFF_HEAD_DIM)]
    key_off = lax.broadcasted_iota(jnp.int32, (T_TILE, LANES), 0).astype(F32)
    lane = lax.broadcasted_iota(jnp.int32, (T_TILE, LANES), 1)
    key_cols = jnp.where(lane < 2, key_off, 0.0).astype(BF16)
    _init_softmax(m_ref, l_ref, acc_ref)

    def issue_scores(g, buf):
        for u in range(GROUP_TILES):
            off = pl.multiple_of((g * GROUP_TILES + u) * T_TILE, T_TILE)
            k_aug = jnp.concatenate([k_ref[pl.ds(off, T_TILE), :], key_cols], axis=1)
            for c in range(2):
                buf[2 * u + c] = _dot(k_aug, qs_aug[c])

    def consume(g, buf, diagonal):
        tiles = []
        for u in range(GROUP_TILES):
            j = g * GROUP_TILES + u
            shift = slope * (j * T_TILE - i * Q_TILE).astype(F32)
            for c in range(2):
                s_t = buf[2 * u + c]
                if diagonal:
                    s_t = s_t + slope * dm_ref[u * T_TILE:(u + 1) * T_TILE, :]
                tiles.append((s_t, shift, vt_ref[j]))
        _consume_group(tiles, 2, m_ref, l_ref, acc_ref)

    _attention_pipeline(i, issue_scores, consume, buf_a, buf_b)

    lam = lam_ref[...]
    lam_full = (jnp.exp(jnp.sum(lam[0:1] * lam[1:2], axis=1, keepdims=True))
                - jnp.exp(jnp.sum(lam[2:3] * lam[3:4], axis=1, keepdims=True)) + lambda_init)
    o_t = acc_ref[0] / l_ref[0] - lam_full * (acc_ref[1] / l_ref[1])
    inv = lax.rsqrt(jnp.mean(o_t * o_t, axis=0, keepdims=True) + NORM_EPS)
    o_t = o_t * inv * (g_ref[...] * (1.0 - lambda_init))
    o_ref[...] = o_t.T.astype(o_ref.dtype)


def _diagonal_tables():
    rk = jnp.arange(Q_TILE, dtype=jnp.int32)[:, None]
    rq = jnp.arange(Q_TILE, dtype=jnp.int32)[None, :]
    visible = (rk // CHUNK) <= (rq // CHUNK)
    alibi_fix = jnp.where(visible, -2.0 * jnp.maximum(rk - rq, 0).astype(F32), -MASKED)
    mask = jnp.where(visible, 0.0, -MASKED).astype(F32)
    return alibi_fix, mask


def _diff_attention2(qt, k, vt, slope_hi, slope_lo, lam, g_col, alibi_fix, batch, seq, lambda_init):
    nq = seq // Q_TILE
    nk = seq // T_TILE
    hh = DIFF_HEADS
    return pl.pallas_call(
        functools.partial(_diff_attn2_kernel, lambda_init=lambda_init),
        out_shape=jax.ShapeDtypeStruct((batch * seq, hh * DIFF_V_DIM), BF16),
        grid=(batch, hh, nq),
        in_specs=[pl.BlockSpec(memory_space=pltpu.SMEM),
                  pl.BlockSpec(memory_space=pltpu.SMEM),
                  pl.BlockSpec(lam.shape, lambda b, h, i: (0, 0)),
                  pl.BlockSpec(g_col.shape, lambda b, h, i: (0, 0)),
                  pl.BlockSpec(alibi_fix.shape, lambda b, h, i: (0, 0)),
                  pl.BlockSpec((GROUP_TILES, None, LANES, T_TILE), lambda b, h, i: (b * nq + i, h, 0, 0)),
                  pl.BlockSpec((seq, LANES), lambda b, h, i: (b, h)),
                  pl.BlockSpec((nk, None, DIFF_V_DIM, T_TILE), lambda b, h, i: (b, h, 0, 0))],
        out_specs=pl.BlockSpec((Q_TILE, LANES), lambda b, h, i: (b * nq + i, h)),
        scratch_shapes=[pltpu.VMEM((2 * GROUP_TILES, T_TILE, Q_TILE), F32),
                        pltpu.VMEM((2 * GROUP_TILES, T_TILE, Q_TILE), F32),
                        pltpu.VMEM((2, 1, Q_TILE), F32), pltpu.VMEM((2, 1, Q_TILE), F32),
                        pltpu.VMEM((2, DIFF_V_DIM, Q_TILE), F32)],
        compiler_params=_cparams(("parallel", "parallel", "arbitrary")),
        name="diff_attention",
    )(slope_hi, slope_lo, lam, g_col, alibi_fix, qt, k, vt)


def _mla_attn2_kernel(mask_ref, qt_ref, k_ref, vt_ref, o_ref, buf_a, buf_b, m_ref, l_ref, acc_ref):
    i = pl.program_id(2)
    qt = jnp.concatenate([qt_ref[u] for u in range(GROUP_TILES)], axis=1)
    _init_softmax(m_ref, l_ref, acc_ref)

    def issue_scores(g, buf):
        for u in range(GROUP_TILES):
            off = pl.multiple_of((g * GROUP_TILES + u) * T_TILE, T_TILE)
            buf[u] = _dot(k_ref[pl.ds(off, T_TILE), :], qt)

    def consume(g, buf, diagonal):
        tiles = []
        for u in range(GROUP_TILES):
            s_t = buf[u]
            if diagonal:
                s_t = s_t + mask_ref[u * T_TILE:(u + 1) * T_TILE, :]
            tiles.append((s_t, 0.0, vt_ref[g * GROUP_TILES + u]))
        _consume_group(tiles, 1, m_ref, l_ref, acc_ref)

    _attention_pipeline(i, issue_scores, consume, buf_a, buf_b)
    o_ref[...] = (acc_ref[0] / l_ref[0]).T.astype(o_ref.dtype)


def _mla_attention2(qt, kcat, vt, mask, batch, seq):
    nq = seq // Q_TILE
    nk = seq // T_TILE
    return pl.pallas_call(
        _mla_attn2_kernel,
        out_shape=jax.ShapeDtypeStruct((batch * seq, MLA_HEADS * MLA_V_DIM), BF16),
        grid=(batch, MLA_HEADS, nq),
        in_specs=[pl.BlockSpec(mask.shape, lambda b, h, i: (0, 0)),
                  pl.BlockSpec((GROUP_TILES, None, MLA_QK_PAD, T_TILE), lambda b, h, i: (b * nq + i, h, 0, 0)),
                  pl.BlockSpec((seq, MLA_QK_PAD), lambda b, h, i: (b, h)),
                  pl.BlockSpec((nk, None, MLA_V_DIM, T_TILE), lambda b, h, i: (b, h, 0, 0))],
        out_specs=pl.BlockSpec((Q_TILE, MLA_V_DIM), lambda b, h, i: (b * nq + i, h)),
        scratch_shapes=[pltpu.VMEM((GROUP_TILES, T_TILE, Q_TILE), F32),
                        pltpu.VMEM((GROUP_TILES, T_TILE, Q_TILE), F32),
                        pltpu.VMEM((1, 1, Q_TILE), F32), pltpu.VMEM((1, 1, Q_TILE), F32),
                        pltpu.VMEM((1, MLA_V_DIM, Q_TILE), F32)],
        compiler_params=_cparams(("parallel", "parallel", "arbitrary")),
        name="mla_attention",
    )(mask, qt, kcat, vt)


def _split_bf16(x):
    hi = x.astype(BF16)
    lo = (x - hi.astype(F32)).astype(BF16)
    return hi, lo


def _route(logits_t, bias_t):
    score = 1.0 / (1.0 + jnp.exp(-logits_t))
    sel = score + bias_t
    a = [sel[N_GROUPS * j:N_GROUPS * (j + 1)] for j in range(GROUP_SIZE)]
    sc = [score[N_GROUPS * j:N_GROUPS * (j + 1)] for j in range(GROUP_SIZE)]
    hi1, lo1 = jnp.maximum(a[0], a[1]), jnp.minimum(a[0], a[1])
    hi2, lo2 = jnp.maximum(a[2], a[3]), jnp.minimum(a[2], a[3])
    group_score = jnp.maximum(hi1, hi2) + jnp.maximum(jnp.minimum(hi1, hi2), jnp.maximum(lo1, lo2))
    gidx = lax.broadcasted_iota(jnp.int32, group_score.shape, 0)
    best = jnp.max(group_score, axis=0, keepdims=True)
    first_best = jnp.min(jnp.where(group_score == best, gidx, N_GROUPS), axis=0, keepdims=True)
    in_group = gidx == first_best
    picked = []
    for j in range(GROUP_SIZE):
        beaten_by = jnp.zeros(a[j].shape, jnp.int32)
        for i in range(GROUP_SIZE):
            if i == j:
                continue
            ahead = (a[i] > a[j]) | ((a[i] == a[j]) if i < j else False)
            beaten_by = beaten_by + ahead.astype(jnp.int32)
        picked.append(jnp.where(beaten_by < 2, sc[j], 0.0))
    denom = (picked[0] + picked[1]) + (picked[2] + picked[3])
    member_w = [jnp.sum(jnp.where(in_group, p / denom, 0.0), axis=0, keepdims=True) for p in picked]
    return in_group, member_w


def _out_norm_route_kernel(a_ref, wo_ref, x_ref, g_ref, b_ref, rw_ref, rb_ref, tri_ref,
                           x1_ref, gid_ref, rank_ref, cnt_ref, seen_ref):
    @pl.when(pl.program_id(0) == 0)
    def _():
        seen_ref[...] = jnp.zeros(seen_ref.shape, F32)

    y = _dot(a_ref[...], wo_ref[...])
    x1 = _ln(DEEPNORM_ALPHA * x_ref[...] + y, g_ref[...], b_ref[...])
    x1_ref[:, :D_MODEL] = x1
    xh, xl = _split_bf16(x1)
    wh, wl = _split_bf16(rw_ref[...])
    logits_t = _dot_nt(wh, xh) + (_dot_nt(wh, xl) + _dot_nt(wl, xh))
    in_group, member_w = _route(logits_t, rb_ref[...])
    tm = logits_t.shape[1]

    row = lax.broadcasted_iota(jnp.int32, (N_GROUPS, tm), 0)
    w_rows = jnp.zeros((N_GROUPS, tm), F32)
    for j in range(GROUP_SIZE):
        w_rows = jnp.where(row == j, member_w[j], w_rows)
    w_rows = jnp.concatenate([w_rows, jnp.zeros((LANES - N_GROUPS, tm), F32)], axis=0)
    x1_ref[:, D_MODEL:] = w_rows.T

    onehot = in_group.astype(F32)
    earlier = _dot(onehot.astype(BF16), tri_ref[...])
    seen = seen_ref[:, 0:1]
    gidx = lax.broadcasted_iota(jnp.int32, onehot.shape, 0)
    gid_ref[...] = jnp.sum(jnp.where(in_group, gidx, 0), axis=0, keepdims=True)
    rank_ref[...] = jnp.sum(onehot * (earlier + seen), axis=0, keepdims=True).astype(jnp.int32)
    seen_new = seen_ref[...] + jnp.sum(onehot, axis=1, keepdims=True)
    seen_ref[...] = seen_new
    cnt_ref[...] = seen_new.astype(jnp.int32)


def _out_norm_route(a, wo, x, g, b, rw_t, rb_t, tri, tm=ROUTE_TILE):
    t, d = x.shape
    return pl.pallas_call(
        _out_norm_route_kernel,
        out_shape=(jax.ShapeDtypeStruct((t, ROW_WIDE), F32),
                   jax.ShapeDtypeStruct((1, t), jnp.int32),
                   jax.ShapeDtypeStruct((1, t), jnp.int32),
                   jax.ShapeDtypeStruct((N_GROUPS, LANES), jnp.int32)),
        grid=(t // tm,),
        in_specs=[pl.BlockSpec((tm, d), lambda i: (i, 0)),
                  pl.BlockSpec((d, d), lambda i: (0, 0)),
                  pl.BlockSpec((tm, d), lambda i: (i, 0)),
                  pl.BlockSpec((1, d), lambda i: (0, 0)),
                  pl.BlockSpec((1, d), lambda i: (0, 0)),
                  pl.BlockSpec((N_EXPERTS, d), lambda i: (0, 0)),
                  pl.BlockSpec((N_EXPERTS, 1), lambda i: (0, 0)),
                  pl.BlockSpec((tm, tm), lambda i: (0, 0))],
        out_specs=(pl.BlockSpec((tm, ROW_WIDE), lambda i: (i, 0)),
                   pl.BlockSpec((1, tm), lambda i: (0, i)),
                   pl.BlockSpec((1, tm), lambda i: (0, i)),
                   pl.BlockSpec((N_GROUPS, LANES), lambda i: (0, 0))),
        scratch_shapes=[pltpu.VMEM((N_GROUPS, LANES), F32)],
        compiler_params=_cparams(("arbitrary",)),
        name="out_norm_route",
    )(a, wo, x, g, b, rw_t, rb_t, tri)


def _row_copy(src, dst, src_row, dst_row, sem):
    return pltpu.make_async_copy(src.at[pl.ds(src_row, 1)], dst.at[pl.ds(dst_row, 1)], sem)


def _copy_rows(src, dst, src_row_of, dst_row_of, sem):
    def start(r, carry):
        _row_copy(src, dst, src_row_of(r), dst_row_of(r), sem).start()
        return carry

    def wait(r, carry):
        _row_copy(src, dst, 0, 0, sem).wait()
        return carry

    lax.fori_loop(0, PERM_ROWS, start, 0, unroll=8)
    lax.fori_loop(0, PERM_ROWS, wait, 0, unroll=8)


def _scatter_rows_kernel(dest_ref, src_ref, base_hbm, out_hbm, sem):
    del base_hbm
    _copy_rows(src_ref, out_hbm, lambda r: r, lambda r: dest_ref[0, r], sem)


def _scatter_rows(src, dest, base):
    t, width = src.shape
    return pl.pallas_call(
        _scatter_rows_kernel,
        out_shape=jax.ShapeDtypeStruct(base.shape, base.dtype),
        grid=(t // PERM_ROWS,),
        in_specs=[pl.BlockSpec((None, 1, PERM_ROWS), lambda i: (i, 0, 0), memory_space=pltpu.SMEM),
                  pl.BlockSpec((PERM_ROWS, width), lambda i: (i, 0)),
                  pl.BlockSpec(memory_space=pl.ANY)],
        out_specs=pl.BlockSpec(memory_space=pl.ANY),
        scratch_shapes=[pltpu.SemaphoreType.DMA],
        input_output_aliases={2: 0},
        compiler_params=_cparams(("arbitrary",)),
        name="scatter_rows",
    )(dest.reshape(t // PERM_ROWS, 1, PERM_ROWS), src, base)


def _gather_rows_kernel(dest_ref, src_hbm, out_ref, sem):
    _copy_rows(src_hbm, out_ref, lambda r: dest_ref[0, r], lambda r: r, sem)


def _gather_rows(src, dest):
    t = dest.shape[0]
    width = src.shape[1]
    return pl.pallas_call(
        _gather_rows_kernel,
        out_shape=jax.ShapeDtypeStruct((t, width), src.dtype),
        grid=(t // PERM_ROWS,),
        in_specs=[pl.BlockSpec((None, 1, PERM_ROWS), lambda i: (i, 0, 0), memory_space=pltpu.SMEM),
                  pl.BlockSpec(memory_space=pl.ANY)],
        out_specs=pl.BlockSpec((PERM_ROWS, width), lambda i: (i, 0)),
        scratch_shapes=[pltpu.SemaphoreType.DMA],
        compiler_params=_cparams(("arbitrary",)),
        name="gather_rows",
    )(dest.reshape(t // PERM_ROWS, 1, PERM_ROWS), src)


def _moe_kernel(tile_group_ref, n_tiles_ref, xs_ref, wg_ref, wu_ref, wd_ref, g_ref, b_ref, o_ref):
    del tile_group_ref
    k = pl.program_id(0)

    @pl.when(k < n_tiles_ref[0])
    def _():
        x = xs_ref[:, :D_MODEL]
        cw = xs_ref[:, D_MODEL:]
        xb = x.astype(BF16)
        y = jnp.zeros(x.shape, F32)
        for j in range(GROUP_SIZE):
            hg = _dot(xb, wg_ref[j])
            hu = _dot(xb, wu_ref[j])
            hid = (hg / (1.0 + jnp.exp(-hg))) * hu * cw[:, j:j + 1]
            y = y + _dot(hid.astype(BF16), wd_ref[j])
        o_ref[...] = _ln(DEEPNORM_ALPHA * x + y, g_ref[...], b_ref[...])

    @pl.when(k >= n_tiles_ref[0])
    def _():
        o_ref[...] = jnp.zeros(o_ref.shape, F32)


def _moe(xs, tile_group, n_tiles, wg, wu, wd, g, b, layer):
    tp = xs.shape[0]
    d, gs, ff = D_MODEL, GROUP_SIZE, EXPERT_FF
    grid_spec = pltpu.PrefetchScalarGridSpec(
        num_scalar_prefetch=2,
        grid=(tp // MOE_TILE,),
        in_specs=[pl.BlockSpec((MOE_TILE, ROW_WIDE), lambda k, tg, nt: (k, 0)),
                  pl.BlockSpec((None, gs, d, ff), lambda k, tg, nt: (layer, tg[k], 0, 0)),
                  pl.BlockSpec((None, gs, d, ff), lambda k, tg, nt: (layer, tg[k], 0, 0)),
                  pl.BlockSpec((None, gs, ff, d), lambda k, tg, nt: (layer, tg[k], 0, 0)),
                  pl.BlockSpec((1, d), lambda k, tg, nt: (0, 0)),
                  pl.BlockSpec((1, d), lambda k, tg, nt: (0, 0))],
        out_specs=pl.BlockSpec((MOE_TILE, d), lambda k, tg, nt: (k, 0)))
    return pl.pallas_call(
        _moe_kernel,
        out_shape=jax.ShapeDtypeStruct((tp, d), F32),
        grid_spec=grid_spec,
        compiler_params=_cparams(("arbitrary",)),
        name="grouped_moe",
    )(tile_group, n_tiles, xs, wg, wu, wd, g, b)


def _sort_plan(gid, rank, counts, n_tiles_max):
    cnt = counts[:, 0]
    padded = ((cnt + MOE_TILE - 1) // MOE_TILE) * MOE_TILE
    ends = jnp.cumsum(padded)
    starts = ends - padded
    dest = starts[gid[0]] + rank[0]
    n_tiles = ends[-1] // MOE_TILE
    tile_row = jnp.arange(n_tiles_max, dtype=jnp.int32) * MOE_TILE
    tile_group = jnp.sum((tile_row[:, None] >= ends[None, :]).astype(jnp.int32), axis=1)
    last_group = jnp.sum((ends[-1] - 1 >= ends).astype(jnp.int32))
    tile_group = jnp.minimum(tile_group, last_group)
    return dest.astype(jnp.int32), tile_group.astype(jnp.int32), n_tiles.reshape(1).astype(jnp.int32)


def _rope_tables(seq):
    inv_freq = ROPE_BASE ** (-jnp.arange(0, MLA_ROPE_DIM, 2, dtype=F32) / MLA_ROPE_DIM)
    ang = jnp.arange(seq, dtype=F32)[:, None] * inv_freq[None, :]
    cos, sin = jnp.cos(ang), jnp.sin(ang)
    cos2 = jnp.concatenate([cos, cos], axis=1)
    sin2 = jnp.concatenate([-sin, sin], axis=1)
    zero = jnp.zeros((seq, LANES - MLA_ROPE_DIM), F32)
    return (jnp.concatenate([cos2, zero], axis=1), jnp.concatenate([sin2, zero], axis=1), cos2.T, sin2.T)


def _swap_halves(w):
    half = MLA_ROPE_DIM // 2
    return jnp.concatenate([w[..., half:], w[..., :half]], axis=-1)


def kernel(x, diff_w_qkv, diff_lambda, diff_subln_g, diff_w_o, mla_w_kv_a, mla_kv_norm_g, mla_w_kv_b,
           mla_w_q_a, mla_q_norm_g, mla_w_q_b, mla_w_o, router_w, router_b, moe_w_gate, moe_w_up,
           moe_w_down, ln_g, ln_b):
    batch, seq, d = x.shape
    t = batch * seq
    xs = x.reshape(t, d)

    wq, wk, wv = jnp.split(diff_w_qkv, 3, axis=-1)
    w_k = wk.astype(BF16)
    w_qv_t = jnp.concatenate([wq * (LOG2E * DIFF_HEAD_DIM ** -0.5), wv], axis=-1).transpose(0, 2, 1).astype(BF16)
    w_o_a = diff_w_o.astype(BF16)
    w_o_b = mla_w_o.astype(BF16)
    wg = moe_w_gate.astype(BF16)
    wu = moe_w_up.astype(BF16)
    wd = moe_w_down.astype(BF16)
    slopes = LOG2E * 2.0 ** (-8.0 * jnp.arange(1, DIFF_HEADS + 1, dtype=F32) / DIFF_HEADS)
    slope_hi = slopes.astype(BF16).astype(F32)
    slope_lo = (slopes - slope_hi).astype(BF16).astype(F32)
    rw_t = router_w.T.reshape(N_GROUPS, GROUP_SIZE, d).transpose(1, 0, 2).reshape(N_EXPERTS, d)
    rb_t = router_b.reshape(N_GROUPS, GROUP_SIZE).T.reshape(N_EXPERTS, 1).astype(F32)
    cs, sn, cs_t, sn_t = _rope_tables(seq)
    kv_rope_w = mla_w_kv_a[:, MLA_KV_RANK:]
    w_kv_a = jnp.concatenate([mla_w_kv_a, _swap_halves(kv_rope_w)], axis=1).astype(BF16)
    w_kv_b = mla_w_kv_b.reshape(MLA_KV_RANK, MLA_HEADS, MLA_NOPE_DIM + MLA_V_DIM)
    w_kn = w_kv_b[..., :MLA_NOPE_DIM].reshape(MLA_KV_RANK, -1).astype(BF16)
    w_v_t = w_kv_b[..., MLA_NOPE_DIM:].reshape(MLA_KV_RANK, -1).T.astype(BF16)
    wqb = mla_w_q_b.reshape(-1, MLA_Q_RANK, MLA_HEADS, MLA_NOPE_DIM + MLA_ROPE_DIM)
    w_q_b_t = jnp.concatenate([wqb, _swap_halves(wqb[..., MLA_NOPE_DIM:])], axis=-1).reshape(
        -1, MLA_Q_RANK, MLA_HEADS * MLA_QK_PAD).transpose(0, 2, 1).astype(BF16)
    w_q_a = mla_w_q_a.astype(BF16)

    alibi_fix, diag_mask = _diagonal_tables()
    tri = jnp.triu(jnp.ones((ROUTE_TILE, ROUTE_TILE), BF16), 1)
    n_tiles_max = t // MOE_TILE + N_GROUPS
    sorted_rows = jnp.zeros((n_tiles_max * MOE_TILE, ROW_WIDE), F32)

    kcat = vt_shared = None
    for layer in range(DEPTH):
        if layer < N_A_LAYERS:
            k, qt, vt = _qkv_proj(xs, w_k[layer], w_qv_t[layer])
            attn = _diff_attention2(qt, k, vt, slope_hi, slope_lo, diff_lambda[layer],
                                    diff_subln_g[layer].reshape(-1, 1), alibi_fix, batch, seq, _lambda_init(layer))
            w_o = w_o_a[layer]
        else:
            j = layer - N_A_LAYERS
            if j == 0:
                kcat, vt_shared = _mla_kv_proj(xs, w_kv_a, mla_kv_norm_g.reshape(1, -1), w_kn, w_v_t, cs, sn, seq)
            qt = _mla_q_proj(xs, w_q_a[j], mla_q_norm_g[j].reshape(1, -1), w_q_b_t[j], cs_t, sn_t, seq)
            attn = _mla_attention2(qt, kcat, vt_shared, diag_mask, batch, seq)
            w_o = w_o_b[j]
        x1, gid, rank, counts = _out_norm_route(attn, w_o, xs, ln_g[layer, 0].reshape(1, -1),
                                                ln_b[layer, 0].reshape(1, -1), rw_t, rb_t, tri)
        dest, tile_group, n_tiles = _sort_plan(gid, rank, counts, n_tiles_max)
        sorted_rows = _scatter_rows(x1, dest, sorted_rows)
        ys = _moe(sorted_rows, tile_group, n_tiles, wg, wu, wd, ln_g[layer, 1].reshape(1, -1),
                  ln_b[layer, 1].reshape(1, -1), layer)
        xs = _gather_rows(ys, dest)
    return xs.reshape(batch, seq, d)
```

```python
import functools
import math

import jax
import jax.numpy as jnp
from jax import lax
from jax.experimental import pallas as pl
from jax.experimental.pallas import tpu as pltpu

D_MODEL = 1024
DEPTH = 4
CHUNK = 64
N_A_LAYERS = DEPTH // 2
DIFF_HEAD_DIM = 64
DIFF_V_DIM = 2 * DIFF_HEAD_DIM
DIFF_HEADS = D_MODEL // DIFF_V_DIM
MLA_NOPE_DIM = 128
MLA_ROPE_DIM = 64
MLA_V_DIM = 128
MLA_HEADS = D_MODEL // MLA_V_DIM
MLA_Q_RANK = 384
MLA_KV_RANK = 256
MLA_QK_PAD = 256
ROPE_BASE = 10000.0
N_EXPERTS = 32
N_GROUPS = 8
GROUP_SIZE = N_EXPERTS // N_GROUPS
EXPERT_FF = 256
NORM_EPS = 1e-5
DEEPNORM_ALPHA = (2 * DEPTH) ** 0.25
LOG2E = math.log2(math.e)

LANES = 128
T_TILE = 256
KV_UNROLL = 4
GROUP_TILES = 2
Q_TILE = GROUP_TILES * T_TILE
MASKED = 1e30
ROUTE_TILE = 256
MOE_TILE = 256
PERM_ROWS = 256
PLAN_CHUNK = 2048
ROW_WIDE = D_MODEL + LANES
VMEM_LIMIT = 48 * 1024 * 1024

F32 = jnp.float32
BF16 = jnp.bfloat16


def _lambda_init(layer):
    return 0.8 - 0.6 * math.exp(-0.3 * layer)


def _cparams(sem):
    return pltpu.CompilerParams(dimension_semantics=sem, vmem_limit_bytes=VMEM_LIMIT)


def _dot(a, b):
    return jnp.dot(a, b, preferred_element_type=F32)


def _dot_nt(a, b):
    return lax.dot_general(a, b, (((1,), (1,)), ((), ())), preferred_element_type=F32)


def _rms(x, g):
    return x * lax.rsqrt(jnp.mean(x * x, axis=-1, keepdims=True) + NORM_EPS) * g


def _ln(x, g, b):
    mu = jnp.mean(x, axis=-1, keepdims=True)
    xc = x - mu
    var = jnp.mean(xc * xc, axis=-1, keepdims=True)
    return xc * lax.rsqrt(var + NORM_EPS) * g + b


def _rope_halves(a, cs, sn):
    return a * cs + pltpu.roll(a, MLA_ROPE_DIM, axis=1) * sn


def _qkv_kernel(x_ref, wk_ref, wqv_ref, k_ref, qt_ref, vt_ref):
    xb = x_ref[...].astype(BF16)
    k_ref[...] = _dot(xb, wk_ref[...]).astype(BF16)
    qv = _dot_nt(wqv_ref[...], xb)
    for h in range(DIFF_HEADS):
        qt_ref[0, h] = qv[h * LANES:(h + 1) * LANES].astype(BF16)
        vt_ref[0, h] = qv[D_MODEL + h * LANES:D_MODEL + (h + 1) * LANES].astype(BF16)


def _qkv_proj(x, wk, wqv):
    t, d = x.shape
    nt = t // T_TILE
    head_major = jax.ShapeDtypeStruct((nt, DIFF_HEADS, LANES, T_TILE), BF16)
    head_block = pl.BlockSpec((1, DIFF_HEADS, LANES, T_TILE), lambda i: (i, 0, 0, 0))
    return pl.pallas_call(
        _qkv_kernel,
        out_shape=(jax.ShapeDtypeStruct((t, d), BF16), head_major, head_major),
        grid=(nt,),
        in_specs=[pl.BlockSpec((T_TILE, d), lambda i: (i, 0)),
                  pl.BlockSpec(wk.shape, lambda i: (0, 0)),
                  pl.BlockSpec(wqv.shape, lambda i: (0, 0))],
        out_specs=(pl.BlockSpec((T_TILE, d), lambda i: (i, 0)), head_block, head_block),
        compiler_params=_cparams(("parallel",)),
        name="qkv_proj",
    )(x, wk, wqv)


def _mla_kv_kernel(x_ref, wa_ref, g_ref, wk_ref, wv_ref, cs_ref, sn_ref, kcat_ref, vt_ref):
    kva = _dot(x_ref[...].astype(BF16), wa_ref[...])
    c_kv = _rms(kva[:, :MLA_KV_RANK], g_ref[...]).astype(BF16)
    kr = _rope_halves(kva[:, MLA_KV_RANK:], cs_ref[...], sn_ref[...]).astype(BF16)
    kn = _dot(c_kv, wk_ref[...])
    vt = _dot_nt(wv_ref[...], c_kv)
    for h in range(MLA_HEADS):
        base = h * MLA_QK_PAD
        kcat_ref[:, base:base + MLA_NOPE_DIM] = kn[:, h * MLA_NOPE_DIM:(h + 1) * MLA_NOPE_DIM].astype(BF16)
        kcat_ref[:, base + MLA_NOPE_DIM:base + MLA_QK_PAD] = kr
        vt_ref[0, h] = vt[h * MLA_V_DIM:(h + 1) * MLA_V_DIM].astype(BF16)


def _mla_kv_proj(x, wa, g, wk, wv, cs, sn, seq):
    t, d = x.shape
    nt = t // T_TILE
    ns = seq // T_TILE
    return pl.pallas_call(
        _mla_kv_kernel,
        out_shape=(jax.ShapeDtypeStruct((t, MLA_HEADS * MLA_QK_PAD), BF16),
                   jax.ShapeDtypeStruct((nt, MLA_HEADS, MLA_V_DIM, T_TILE), BF16)),
        grid=(nt,),
        in_specs=[pl.BlockSpec((T_TILE, d), lambda i: (i, 0)),
                  pl.BlockSpec(wa.shape, lambda i: (0, 0)),
                  pl.BlockSpec(g.shape, lambda i: (0, 0)),
                  pl.BlockSpec(wk.shape, lambda i: (0, 0)),
                  pl.BlockSpec(wv.shape, lambda i: (0, 0)),
                  pl.BlockSpec((T_TILE, LANES), lambda i: (i % ns, 0)),
                  pl.BlockSpec((T_TILE, LANES), lambda i: (i % ns, 0))],
        out_specs=(pl.BlockSpec((T_TILE, MLA_HEADS * MLA_QK_PAD), lambda i: (i, 0)),
                   pl.BlockSpec((1, MLA_HEADS, MLA_V_DIM, T_TILE), lambda i: (i, 0, 0, 0))),
        compiler_params=_cparams(("parallel",)),
        name="mla_kv_proj",
    )(x, wa, g, wk, wv, cs, sn)


def _mla_q_kernel(x_ref, wa_ref, g_ref, wb_ref, cst_ref, snt_ref, qt_ref, *, scale):
    qa = _dot(x_ref[...].astype(BF16), wa_ref[...])
    qn = _rms(qa, g_ref[...]).astype(BF16)
    qt = _dot_nt(wb_ref[...], qn) * scale
    cst = cst_ref[...]
    snt = snt_ref[...]
    zero = jnp.zeros((MLA_ROPE_DIM, T_TILE), BF16)
    for h in range(MLA_HEADS):
        base = h * MLA_QK_PAD
        r0 = base + MLA_NOPE_DIM
        rope = qt[r0:r0 + MLA_ROPE_DIM] * cst + qt[r0 + MLA_ROPE_DIM:base + MLA_QK_PAD] * snt
        qt_ref[0, h, 0:MLA_NOPE_DIM] = qt[base:r0].astype(BF16)
        qt_ref[0, h, MLA_NOPE_DIM:MLA_NOPE_DIM + MLA_ROPE_DIM] = rope.astype(BF16)
        qt_ref[0, h, MLA_NOPE_DIM + MLA_ROPE_DIM:MLA_QK_PAD] = zero


def _mla_q_proj(x, wa, g, wb_t, cst, snt, seq):
    t, d = x.shape
    nt = t // T_TILE
    ns = seq // T_TILE
    scale = LOG2E * (MLA_NOPE_DIM + MLA_ROPE_DIM) ** -0.5
    return pl.pallas_call(
        functools.partial(_mla_q_kernel, scale=scale),
        out_shape=jax.ShapeDtypeStruct((nt, MLA_HEADS, MLA_QK_PAD, T_TILE), BF16),
        grid=(nt,),
        in_specs=[pl.BlockSpec((T_TILE, d), lambda i: (i, 0)),
                  pl.BlockSpec(wa.shape, lambda i: (0, 0)),
                  pl.BlockSpec(g.shape, lambda i: (0, 0)),
                  pl.BlockSpec(wb_t.shape, lambda i: (0, 0)),
                  pl.BlockSpec((MLA_ROPE_DIM, T_TILE), lambda i: (0, i % ns)),
                  pl.BlockSpec((MLA_ROPE_DIM, T_TILE), lambda i: (0, i % ns))],
        out_specs=pl.BlockSpec((1, MLA_HEADS, MLA_QK_PAD, T_TILE), lambda i: (i, 0, 0, 0)),
        compiler_params=_cparams(("parallel",)),
        name="mla_q_proj",
    )(x, wa, g, wb_t, cst, snt)


def _tile_softmax(tiles):
    probs = []
    for s_t, shift, vt in tiles:
        m_t = jnp.max(s_t, axis=0, keepdims=True)
        p_t = jnp.exp2(s_t - m_t)
        probs.append((m_t + shift, jnp.sum(p_t, axis=0, keepdims=True), p_t.astype(BF16), vt))
    return [(m_t, l_t, _dot(vt, p_t)) for m_t, l_t, p_t, vt in probs]


def _merge_tiles(parts, c, m_ref, l_ref, acc_ref):
    m_prev = m_ref[c]
    m_new = m_prev
    for m_t, _, _ in parts:
        m_new = jnp.maximum(m_new, m_t)
    alpha = jnp.exp2(m_prev - m_new)
    l = alpha * l_ref[c]
    acc = alpha * acc_ref[c]
    for m_t, l_t, pv in parts:
        w = jnp.exp2(m_t - m_new)
        l = l + w * l_t
        acc = acc + w * pv
    m_ref[c] = m_new
    l_ref[c] = l
    acc_ref[c] = acc


def _for_key_tiles(n_past, body):
    def group(g, carry):
        body(g * KV_UNROLL, KV_UNROLL, False)
        return carry

    n_groups = n_past // KV_UNROLL
    lax.fori_loop(0, n_groups, group, 0)
    left = n_past - n_groups * KV_UNROLL
    for r in range(KV_UNROLL):
        @pl.when(left == r)
        def _(r=r):
            body(n_groups * KV_UNROLL, r, True)


def _init_softmax(m_ref, l_ref, acc_ref):
    m_ref[...] = jnp.full(m_ref.shape, -jnp.inf, F32)
    l_ref[...] = jnp.zeros(l_ref.shape, F32)
    acc_ref[...] = jnp.zeros(acc_ref.shape, F32)


def _key_query_iota():
    rk = lax.broadcasted_iota(jnp.int32, (T_TILE, T_TILE), 0)
    rq = lax.broadcasted_iota(jnp.int32, (T_TILE, T_TILE), 1)
    return rk, rq, (rk // CHUNK) <= (rq // CHUNK)


def _diff_attn_kernel(slh_ref, sll_ref, lam_ref, g_ref, qt_ref, k_ref, vt_ref, o_ref,
                      m_ref, l_ref, acc_ref, *, lambda_init):
    h = pl.program_id(1)
    i = pl.program_id(2)
    sl_hi = slh_ref[h]
    sl_lo = sll_ref[h]
    slope = sl_hi + sl_lo
    row = lax.broadcasted_iota(jnp.int32, (LANES, T_TILE), 0)
    qf = qt_ref[...].astype(F32)
    qs = (jnp.where(row < DIFF_HEAD_DIM, qf, 0.0).astype(BF16),
          jnp.where(row >= DIFF_HEAD_DIM, qf, 0.0).astype(BF16))
    slope_rows = jnp.where(row == 0, sl_hi, jnp.where(row == 1, sl_lo, 0.0)).astype(BF16)
    qs_aug = [jnp.concatenate([q, slope_rows], axis=0) for q in qs]
    key_off = lax.broadcasted_iota(jnp.int32, (T_TILE, LANES), 0).astype(F32)
    lane = lax.broadcasted_iota(jnp.int32, (T_TILE, LANES), 1)
    key_cols = jnp.where(lane < 2, key_off, 0.0).astype(BF16)
    _init_softmax(m_ref, l_ref, acc_ref)

    def key_tiles(j0, n_past, with_diagonal):
        tiles = []
        for u in range(n_past):
            j = j0 + u
            off = pl.multiple_of(j * T_TILE, T_TILE)
            k_aug = jnp.concatenate([k_ref[pl.ds(off, T_TILE), :], key_cols], axis=1)
            shift = slope * ((j - i) * T_TILE).astype(F32)
            tiles += [(_dot(k_aug, qs_aug[c]), shift, vt_ref[j]) for c in range(2)]
        if with_diagonal:
            off = pl.multiple_of(i * T_TILE, T_TILE)
            kb = k_ref[pl.ds(off, T_TILE), :]
            rk, rq, visible = _key_query_iota()
            bias = slope * (rq - jnp.abs(rq - rk)).astype(F32)
            tiles += [(jnp.where(visible, _dot(kb, qs[c]) + bias, -jnp.inf), 0.0, vt_ref[i]) for c in range(2)]
        parts = _tile_softmax(tiles)
        for c in range(2):
            _merge_tiles(parts[c::2], c, m_ref, l_ref, acc_ref)

    _for_key_tiles(i, key_tiles)

    lam = lam_ref[...]
    lam_full = (jnp.exp(jnp.sum(lam[0:1] * lam[1:2], axis=1, keepdims=True))
                - jnp.exp(jnp.sum(lam[2:3] * lam[3:4], axis=1, keepdims=True)) + lambda_init)
    o_t = acc_ref[0] / l_ref[0] - lam_full * (acc_ref[1] / l_ref[1])
    inv = lax.rsqrt(jnp.mean(o_t * o_t, axis=0, keepdims=True) + NORM_EPS)
    o_t = o_t * inv * (g_ref[...] * (1.0 - lambda_init))
    o_ref[...] = o_t.T.astype(o_ref.dtype)


def _diff_attention(qt, k, vt, slope_hi, slope_lo, lam, g_col, batch, seq, lambda_init):
    nq = seq // T_TILE
    hh = DIFF_HEADS
    return pl.pallas_call(
        functools.partial(_diff_attn_kernel, lambda_init=lambda_init),
        out_shape=jax.ShapeDtypeStruct((batch * seq, hh * DIFF_V_DIM), BF16),
        grid=(batch, hh, nq),
        in_specs=[pl.BlockSpec(memory_space=pltpu.SMEM),
                  pl.BlockSpec(memory_space=pltpu.SMEM),
                  pl.BlockSpec(lam.shape, lambda b, h, i: (0, 0)),
                  pl.BlockSpec(g_col.shape, lambda b, h, i: (0, 0)),
                  pl.BlockSpec((None, None, LANES, T_TILE), lambda b, h, i: (b * nq + i, h, 0, 0)),
                  pl.BlockSpec((seq, LANES), lambda b, h, i: (b, h)),
                  pl.BlockSpec((nq, None, DIFF_V_DIM, T_TILE), lambda b, h, i: (b, h, 0, 0))],
        out_specs=pl.BlockSpec((T_TILE, LANES), lambda b, h, i: (b * nq + i, h)),
        scratch_shapes=[pltpu.VMEM((2, 1, T_TILE), F32), pltpu.VMEM((2, 1, T_TILE), F32),
                        pltpu.VMEM((2, DIFF_V_DIM, T_TILE), F32)],
        compiler_params=_cparams(("parallel", "parallel", "arbitrary")),
        name="diff_attention",
    )(slope_hi, slope_lo, lam, g_col, qt, k, vt)


def _mla_attn_kernel(qt_ref, k_ref, vt_ref, o_ref, m_ref, l_ref, acc_ref):
    i = pl.program_id(2)
    qt = qt_ref[...]
    _init_softmax(m_ref, l_ref, acc_ref)

    def key_tiles(j0, n_past, with_diagonal):
        tiles = []
        for u in range(n_past):
            off = pl.multiple_of((j0 + u) * T_TILE, T_TILE)
            tiles.append((_dot(k_ref[pl.ds(off, T_TILE), :], qt), 0.0, vt_ref[j0 + u]))
        if with_diagonal:
            off = pl.multiple_of(i * T_TILE, T_TILE)
            _, _, visible = _key_query_iota()
            s_t = jnp.where(visible, _dot(k_ref[pl.ds(off, T_TILE), :], qt), -jnp.inf)
            tiles.append((s_t, 0.0, vt_ref[i]))
        _merge_tiles(_tile_softmax(tiles), 0, m_ref, l_ref, acc_ref)

    _for_key_tiles(i, key_tiles)
    o_ref[...] = (acc_ref[0] / l_ref[0]).T.astype(o_ref.dtype)


def _mla_attention(qt, kcat, vt, batch, seq):
    nq = seq // T_TILE
    return pl.pallas_call(
        _mla_attn_kernel,
        out_shape=jax.ShapeDtypeStruct((batch * seq, MLA_HEADS * MLA_V_DIM), BF16),
        grid=(batch, MLA_HEADS, nq),
        in_specs=[pl.BlockSpec((None, None, MLA_QK_PAD, T_TILE), lambda b, h, i: (b * nq + i, h, 0, 0)),
                  pl.BlockSpec((seq, MLA_QK_PAD), lambda b, h, i: (b, h)),
                  pl.BlockSpec((nq, None, MLA_V_DIM, T_TILE), lambda b, h, i: (b, h, 0, 0))],
        out_specs=pl.BlockSpec((T_TILE, MLA_V_DIM), lambda b, h, i: (b * nq + i, h)),
        scratch_shapes=[pltpu.VMEM((1, 1, T_TILE), F32), pltpu.VMEM((1, 1, T_TILE), F32),
                        pltpu.VMEM((1, MLA_V_DIM, T_TILE), F32)],
        compiler_params=_cparams(("parallel", "parallel", "arbitrary")),
        name="mla_attention",
    )(qt, kcat, vt)


def _consume_group(tiles, n_comp, m_ref, l_ref, acc_ref):
    parts = []
    for s_t, shift, vt in tiles:
        m_t = jnp.max(s_t, axis=0, keepdims=True)
        p_t = jnp.exp2(s_t - m_t)
        parts.append((m_t + shift, jnp.sum(p_t, axis=0, keepdims=True), _dot(vt, p_t.astype(BF16))))
    for c in range(n_comp):
        _merge_tiles(parts[c::n_comp], c, m_ref, l_ref, acc_ref)


def _attention_pipeline(n_prev_groups, issue_scores, consume, buf_a, buf_b):
    issue_scores(0, buf_a)

    def pair(p, carry):
        issue_scores(2 * p + 1, buf_b)
        consume(2 * p, buf_a, False)
        issue_scores(2 * p + 2, buf_a)
        consume(2 * p + 1, buf_b, False)
        return carry

    lax.fori_loop(0, n_prev_groups // 2, pair, 0)
    odd = n_prev_groups % 2

    @pl.when(odd == 1)
    def _():
        issue_scores(n_prev_groups, buf_b)
        consume(n_prev_groups - 1, buf_a, False)
        consume(n_prev_groups, buf_b, True)

    @pl.when(odd == 0)
    def _():
        consume(n_prev_groups, buf_a, True)


def _diff_attn2_kernel(slh_ref, sll_ref, lam_ref, g_ref, dm_ref, qt_ref, k_ref, vt_ref, o_ref,
                       buf_a, buf_b, m_ref, l_ref, acc_ref, *, lambda_init):
    h = pl.program_id(1)
    i = pl.program_id(2)
    sl_hi = slh_ref[h]
    sl_lo = sll_ref[h]
    slope = sl_hi + sl_lo
    row = lax.broadcasted_iota(jnp.int32, (LANES, Q_TILE), 0)
    qf = jnp.concatenate([qt_ref[u] for u in range(GROUP_TILES)], axis=1).astype(F32)
    slope_rows = jnp.where(row == 0, sl_hi, jnp.where(row == 1, sl_lo, 0.0)).astype(BF16)
    qs_aug = [jnp.concatenate([jnp.where(keep, qf, 0.0).astype(BF16), slope_rows], axis=0)
              for keep in (row < DIFF_HEAD_DIM, row >= DIFF_HEAD_DIM)]
    key_off = lax.broadcasted_iota(jnp.int32, (T_TILE, LANES), 0).astype(F32)
    lane = lax.broadcasted_iota(jnp.int32, (T_TILE, LANES), 1)
    key_cols = jnp.where(lane < 2, key_off, 0.0).astype(BF16)
    _init_softmax(m_ref, l_ref, acc_ref)

    def issue_scores(g, buf):
        for u in range(GROUP_TILES):
            off = pl.multiple_of((g * GROUP_TILES + u) * T_TILE, T_TILE)
            k_aug = jnp.concatenate([k_ref[pl.ds(off, T_TILE), :], key_cols], axis=1)
            for c in range(2):
                buf[2 * u + c] = _dot(k_aug, qs_aug[c])

    def consume(g, buf, diagonal):
        tiles = []
        for u in range(GROUP_TILES):
            j = g * GROUP_TILES + u
            shift = slope * (j * T_TILE - i * Q_TILE).astype(F32)
            for c in range(2):
                s_t = buf[2 * u + c]
                if diagonal:
                    s_t = s_t + slope * dm_ref[u * T_TILE:(u + 1) * T_TILE, :]
                tiles.append((s_t, shift, vt_ref[j]))
        _consume_group(tiles, 2, m_ref, l_ref, acc_ref)

    _attention_pipeline(i, issue_scores, consume, buf_a, buf_b)

    lam = lam_ref[...]
    lam_full = (jnp.exp(jnp.sum(lam[0:1] * lam[1:2], axis=1, keepdims=True))
                - jnp.exp(jnp.sum(lam[2:3] * lam[3:4], axis=1, keepdims=True)) + lambda_init)
    o_t = acc_ref[0] / l_ref[0] - lam_full * (acc_ref[1] / l_ref[1])
    inv = lax.rsqrt(jnp.mean(o_t * o_t, axis=0, keepdims=True) + NORM_EPS)
    o_t = o_t * inv * (g_ref[...] * (1.0 - lambda_init))
    o_ref[...] = o_t.T.astype(o_ref.dtype)


def _diagonal_tables():
    rk = jnp.arange(Q_TILE, dtype=jnp.int32)[:, None]
    rq = jnp.arange(Q_TILE, dtype=jnp.int32)[None, :]
    visible = (rk // CHUNK) <= (rq // CHUNK)
    alibi_fix = jnp.where(visible, -2.0 * jnp.maximum(rk - rq, 0).astype(F32), -MASKED)
    mask = jnp.where(visible, 0.0, -MASKED).astype(F32)
    return alibi_fix, mask


def _diff_attention2(qt, k, vt, slope_hi, slope_lo, lam, g_col, alibi_fix, batch, seq, lambda_init):
    nq = seq // Q_TILE
    nk = seq // T_TILE
    hh = DIFF_HEADS
    return pl.pallas_call(
        functools.partial(_diff_attn2_kernel, lambda_init=lambda_init),
        out_shape=jax.ShapeDtypeStruct((batch * seq, hh * DIFF_V_DIM), BF16),
        grid=(batch, hh, nq),
        in_specs=[pl.BlockSpec(memory_space=pltpu.SMEM),
                  pl.BlockSpec(memory_space=pltpu.SMEM),
                  pl.BlockSpec(lam.shape, lambda b, h, i: (0, 0)),
                  pl.BlockSpec(g_col.shape, lambda b, h, i: (0, 0)),
                  pl.BlockSpec(alibi_fix.shape, lambda b, h, i: (0, 0)),
                  pl.BlockSpec((GROUP_TILES, None, LANES, T_TILE), lambda b, h, i: (b * nq + i, h, 0, 0)),
                  pl.BlockSpec((seq, LANES), lambda b, h, i: (b, h)),
                  pl.BlockSpec((nk, None, DIFF_V_DIM, T_TILE), lambda b, h, i: (b, h, 0, 0))],
        out_specs=pl.BlockSpec((Q_TILE, LANES), lambda b, h, i: (b * nq + i, h)),
        scratch_shapes=[pltpu.VMEM((2 * GROUP_TILES, T_TILE, Q_TILE), F32),
                        pltpu.VMEM((2 * GROUP_TILES, T_TILE, Q_TILE), F32),
                        pltpu.VMEM((2, 1, Q_TILE), F32), pltpu.VMEM((2, 1, Q_TILE), F32),
                        pltpu.VMEM((2, DIFF_V_DIM, Q_TILE), F32)],
        compiler_params=_cparams(("parallel", "parallel", "arbitrary")),
        name="diff_attention",
    )(slope_hi, slope_lo, lam, g_col, alibi_fix, qt, k, vt)


def _mla_attn2_kernel(mask_ref, qt_ref, k_ref, vt_ref, o_ref, buf_a, buf_b, m_ref, l_ref, acc_ref):
    i = pl.program_id(2)
    qt = jnp.concatenate([qt_ref[u] for u in range(GROUP_TILES)], axis=1)
    _init_softmax(m_ref, l_ref, acc_ref)

    def issue_scores(g, buf):
        for u in range(GROUP_TILES):
            off = pl.multiple_of((g * GROUP_TILES + u) * T_TILE, T_TILE)
            buf[u] = _dot(k_ref[pl.ds(off, T_TILE), :], qt)

    def consume(g, buf, diagonal):
        tiles = []
        for u in range(GROUP_TILES):
            s_t = buf[u]
            if diagonal:
                s_t = s_t + mask_ref[u * T_TILE:(u + 1) * T_TILE, :]
            tiles.append((s_t, 0.0, vt_ref[g * GROUP_TILES + u]))
        _consume_group(tiles, 1, m_ref, l_ref, acc_ref)

    _attention_pipeline(i, issue_scores, consume, buf_a, buf_b)
    o_ref[...] = (acc_ref[0] / l_ref[0]).T.astype(o_ref.dtype)


def _mla_attention2(qt, kcat, vt, mask, batch, seq):
    nq = seq // Q_TILE
    nk = seq // T_TILE
    return pl.pallas_call(
        _mla_attn2_kernel,
        out_shape=jax.ShapeDtypeStruct((batch * seq, MLA_HEADS * MLA_V_DIM), BF16),
        grid=(batch, MLA_HEADS, nq),
        in_specs=[pl.BlockSpec(mask.shape, lambda b, h, i: (0, 0)),
                  pl.BlockSpec((GROUP_TILES, None, MLA_QK_PAD, T_TILE), lambda b, h, i: (b * nq + i, h, 0, 0)),
                  pl.BlockSpec((seq, MLA_QK_PAD), lambda b, h, i: (b, h)),
                  pl.BlockSpec((nk, None, MLA_V_DIM, T_TILE), lambda b, h, i: (b, h, 0, 0))],
        out_specs=pl.BlockSpec((Q_TILE, MLA_V_DIM), lambda b, h, i: (b * nq + i, h)),
        scratch_shapes=[pltpu.VMEM((GROUP_TILES, T_TILE, Q_TILE), F32),
                        pltpu.VMEM((GROUP_TILES, T_TILE, Q_TILE), F32),
                        pltpu.VMEM((1, 1, Q_TILE), F32), pltpu.VMEM((1, 1, Q_TILE), F32),
                        pltpu.VMEM((1, MLA_V_DIM, Q_TILE), F32)],
        compiler_params=_cparams(("parallel", "parallel", "arbitrary")),
        name="mla_attention",
    )(mask, qt, kcat, vt)


def _split_bf16(x):
    hi = x.astype(BF16)
    lo = (x - hi.astype(F32)).astype(BF16)
    return hi, lo


def _route(logits_t, bias_t):
    score = 1.0 / (1.0 + jnp.exp(-logits_t))
    sel = score + bias_t
    a = [sel[N_GROUPS * j:N_GROUPS * (j + 1)] for j in range(GROUP_SIZE)]
    sc = [score[N_GROUPS * j:N_GROUPS * (j + 1)] for j in range(GROUP_SIZE)]
    hi1, lo1 = jnp.maximum(a[0], a[1]), jnp.minimum(a[0], a[1])
    hi2, lo2 = jnp.maximum(a[2], a[3]), jnp.minimum(a[2], a[3])
    group_score = jnp.maximum(hi1, hi2) + jnp.maximum(jnp.minimum(hi1, hi2), jnp.maximum(lo1, lo2))
    gidx = lax.broadcasted_iota(jnp.int32, group_score.shape, 0)
    best = jnp.max(group_score, axis=0, keepdims=True)
    first_best = jnp.min(jnp.where(group_score == best, gidx, N_GROUPS), axis=0, keepdims=True)
    in_group = gidx == first_best
    picked = []
    for j in range(GROUP_SIZE):
        beaten_by = jnp.zeros(a[j].shape, jnp.int32)
        for i in range(GROUP_SIZE):
            if i == j:
                continue
            ahead = (a[i] > a[j]) | ((a[i] == a[j]) if i < j else False)
            beaten_by = beaten_by + ahead.astype(jnp.int32)
        picked.append(jnp.where(beaten_by < 2, sc[j], 0.0))
    denom = (picked[0] + picked[1]) + (picked[2] + picked[3])
    member_w = [jnp.sum(jnp.where(in_group, p / denom, 0.0), axis=0, keepdims=True) for p in picked]
    return in_group, member_w


def _out_norm_route_kernel(a_ref, wo_ref, x_ref, g_ref, b_ref, rw_ref, rb_ref, tri_ref,
                           x1_ref, gid_ref, rank_ref, cnt_ref, seen_ref):
    @pl.when(pl.program_id(0) == 0)
    def _():
        seen_ref[...] = jnp.zeros(seen_ref.shape, F32)

    y = _dot(a_ref[...], wo_ref[...])
    x1 = _ln(DEEPNORM_ALPHA * x_ref[...] + y, g_ref[...], b_ref[...])
    x1_ref[:, :D_MODEL] = x1
    xh, xl = _split_bf16(x1)
    wh, wl = _split_bf16(rw_ref[...])
    logits_t = _dot_nt(wh, xh) + (_dot_nt(wh, xl) + _dot_nt(wl, xh))
    in_group, member_w = _route(logits_t, rb_ref[...])
    tm = logits_t.shape[1]

    row = lax.broadcasted_iota(jnp.int32, (N_GROUPS, tm), 0)
    w_rows = jnp.zeros((N_GROUPS, tm), F32)
    for j in range(GROUP_SIZE):
        w_rows = jnp.where(row == j, member_w[j], w_rows)
    w_rows = jnp.concatenate([w_rows, jnp.zeros((LANES - N_GROUPS, tm), F32)], axis=0)
    x1_ref[:, D_MODEL:] = w_rows.T

    onehot = in_group.astype(F32)
    earlier = _dot(onehot.astype(BF16), tri_ref[...])
    seen = seen_ref[:, 0:1]
    gidx = lax.broadcasted_iota(jnp.int32, onehot.shape, 0)
    gid_ref[...] = jnp.sum(jnp.where(in_group, gidx, 0), axis=0, keepdims=True)
    rank_ref[...] = jnp.sum(onehot * (earlier + seen), axis=0, keepdims=True).astype(jnp.int32)
    seen_new = seen_ref[...] + jnp.sum(onehot, axis=1, keepdims=True)
    seen_ref[...] = seen_new
    cnt_ref[...] = seen_new.astype(jnp.int32)


def _out_norm_route(a, wo, x, g, b, rw_t, rb_t, tri, tm=ROUTE_TILE):
    t, d = x.shape
    return pl.pallas_call(
        _out_norm_route_kernel,
        out_shape=(jax.ShapeDtypeStruct((t, ROW_WIDE), F32),
                   jax.ShapeDtypeStruct((1, t), jnp.int32),
                   jax.ShapeDtypeStruct((1, t), jnp.int32),
                   jax.ShapeDtypeStruct((N_GROUPS, LANES), jnp.int32)),
        grid=(t // tm,),
        in_specs=[pl.BlockSpec((tm, d), lambda i: (i, 0)),
                  pl.BlockSpec((d, d), lambda i: (0, 0)),
                  pl.BlockSpec((tm, d), lambda i: (i, 0)),
                  pl.BlockSpec((1, d), lambda i: (0, 0)),
                  pl.BlockSpec((1, d), lambda i: (0, 0)),
                  pl.BlockSpec((N_EXPERTS, d), lambda i: (0, 0)),
                  pl.BlockSpec((N_EXPERTS, 1), lambda i: (0, 0)),
                  pl.BlockSpec((tm, tm), lambda i: (0, 0))],
        out_specs=(pl.BlockSpec((tm, ROW_WIDE), lambda i: (i, 0)),
                   pl.BlockSpec((1, tm), lambda i: (0, i)),
                   pl.BlockSpec((1, tm), lambda i: (0, i)),
                   pl.BlockSpec((N_GROUPS, LANES), lambda i: (0, 0))),
        scratch_shapes=[pltpu.VMEM((N_GROUPS, LANES), F32)],
        compiler_params=_cparams(("arbitrary",)),
        name="out_norm_route",
    )(a, wo, x, g, b, rw_t, rb_t, tri)


def _row_copy(src, dst, src_row, dst_row, sem):
    return pltpu.make_async_copy(src.at[pl.ds(src_row, 1)], dst.at[pl.ds(dst_row, 1)], sem)


def _copy_rows(src, dst, src_row_of, dst_row_of, sem):
    def start(r, carry):
        _row_copy(src, dst, src_row_of(r), dst_row_of(r), sem).start()
        return carry

    def wait(r, carry):
        _row_copy(src, dst, 0, 0, sem).wait()
        return carry

    lax.fori_loop(0, PERM_ROWS, start, 0, unroll=8)
    lax.fori_loop(0, PERM_ROWS, wait, 0, unroll=8)


def _scatter_rows_kernel(dest_ref, src_ref, base_hbm, out_hbm, sem):
    del base_hbm
    _copy_rows(src_ref, out_hbm, lambda r: r, lambda r: dest_ref[0, r], sem)


def _scatter_rows(src, dest, base):
    t, width = src.shape
    return pl.pallas_call(
        _scatter_rows_kernel,
        out_shape=jax.ShapeDtypeStruct(base.shape, base.dtype),
        grid=(t // PERM_ROWS,),
        in_specs=[pl.BlockSpec((None, 1, PERM_ROWS), lambda i: (i, 0, 0), memory_space=pltpu.SMEM),
                  pl.BlockSpec((PERM_ROWS, width), lambda i: (i, 0)),
                  pl.BlockSpec(memory_space=pl.ANY)],
        out_specs=pl.BlockSpec(memory_space=pl.ANY),
        scratch_shapes=[pltpu.SemaphoreType.DMA],
        input_output_aliases={2: 0},
        compiler_params=_cparams(("arbitrary",)),
        name="scatter_rows",
    )(dest.reshape(t // PERM_ROWS, 1, PERM_ROWS), src, base)


def _gather_rows_kernel(dest_ref, src_hbm, out_ref, sem):
    _copy_rows(src_hbm, out_ref, lambda r: dest_ref[0, r], lambda r: r, sem)


def _gather_rows(src, dest):
    t = dest.shape[0]
    width = src.shape[1]
    return pl.pallas_call(
        _gather_rows_kernel,
        out_shape=jax.ShapeDtypeStruct((t, width), src.dtype),
        grid=(t // PERM_ROWS,),
        in_specs=[pl.BlockSpec((None, 1, PERM_ROWS), lambda i: (i, 0, 0), memory_space=pltpu.SMEM),
                  pl.BlockSpec(memory_space=pl.ANY)],
        out_specs=pl.BlockSpec((PERM_ROWS, width), lambda i: (i, 0)),
        scratch_shapes=[pltpu.SemaphoreType.DMA],
        compiler_params=_cparams(("arbitrary",)),
        name="gather_rows",
    )(dest.reshape(t // PERM_ROWS, 1, PERM_ROWS), src)


def _moe_kernel(tile_group_ref, n_tiles_ref, xs_ref, wg_ref, wu_ref, wd_ref, g_ref, b_ref, o_ref):
    del tile_group_ref
    k = pl.program_id(0)

    @pl.when(k < n_tiles_ref[0])
    def _():
        x = xs_ref[:, :D_MODEL]
        cw = xs_ref[:, D_MODEL:]
        xb = x.astype(BF16)
        y = jnp.zeros(x.shape, F32)
        for j in range(GROUP_SIZE):
            hg = _dot(xb, wg_ref[j])
            hu = _dot(xb, wu_ref[j])
            hid = (hg / (1.0 + jnp.exp(-hg))) * hu * cw[:, j:j + 1]
            y = y + _dot(hid.astype(BF16), wd_ref[j])
        o_ref[...] = _ln(DEEPNORM_ALPHA * x + y, g_ref[...], b_ref[...])

    @pl.when(k >= n_tiles_ref[0])
    def _():
        o_ref[...] = jnp.zeros(o_ref.shape, F32)


def _moe(xs, tile_group, n_tiles, wg, wu, wd, g, b, layer):
    tp = xs.shape[0]
    d, gs, ff = D_MODEL, GROUP_SIZE, EXPERT_FF
    grid_spec = pltpu.PrefetchScalarGridSpec(
        num_scalar_prefetch=2,
        grid=(tp // MOE_TILE,),
        in_specs=[pl.BlockSpec((MOE_TILE, ROW_WIDE), lambda k, tg, nt: (k, 0)),
                  pl.BlockSpec((None, gs, d, ff), lambda k, tg, nt: (layer, tg[k], 0, 0)),
                  pl.BlockSpec((None, gs, d, ff), lambda k, tg, nt: (layer, tg[k], 0, 0)),
                  pl.BlockSpec((None, gs, ff, d), lambda k, tg, nt: (layer, tg[k], 0, 0)),
                  pl.BlockSpec((1, d), lambda k, tg, nt: (0, 0)),
                  pl.BlockSpec((1, d), lambda k, tg, nt: (0, 0))],
        out_specs=pl.BlockSpec((MOE_TILE, d), lambda k, tg, nt: (k, 0)))
    return pl.pallas_call(
        _moe_kernel,
        out_shape=jax.ShapeDtypeStruct((tp, d), F32),
        grid_spec=grid_spec,
        compiler_params=_cparams(("arbitrary",)),
        name="grouped_moe",
    )(tile_group, n_tiles, xs, wg, wu, wd, g, b)


def _sort_plan(gid, rank, counts, n_tiles_max):
    cnt = counts[:, 0]
    padded = ((cnt + MOE_TILE - 1) // MOE_TILE) * MOE_TILE
    ends = jnp.cumsum(padded)
    starts = ends - padded
    dest = starts[gid[0]] + rank[0]
    n_tiles = ends[-1] // MOE_TILE
    tile_row = jnp.arange(n_tiles_max, dtype=jnp.int32) * MOE_TILE
    tile_group = jnp.sum((tile_row[:, None] >= ends[None, :]).astype(jnp.int32), axis=1)
    last_group = jnp.sum((ends[-1] - 1 >= ends).astype(jnp.int32))
    tile_group = jnp.minimum(tile_group, last_group)
    return dest.astype(jnp.int32), tile_group.astype(jnp.int32), n_tiles.reshape(1).astype(jnp.int32)


def _tile_plan(counts, n_tiles_max):
    cnt = counts[:, 0]
    padded = ((cnt + MOE_TILE - 1) // MOE_TILE) * MOE_TILE
    ends = jnp.cumsum(padded)
    starts = ends - padded
    n_tiles = ends[-1] // MOE_TILE
    tile_row = jnp.arange(n_tiles_max, dtype=jnp.int32) * MOE_TILE
    tile_group = jnp.sum((tile_row[:, None] >= ends[None, :]).astype(jnp.int32), axis=1)
    last_group = jnp.sum((ends[-1] - 1 >= ends).astype(jnp.int32))
    tile_group = jnp.minimum(tile_group, last_group)
    rows_left = (starts + cnt)[tile_group] - tile_row
    tile_rows = jnp.clip(rows_left, 0, MOE_TILE)
    return (starts.astype(jnp.int32), tile_group.astype(jnp.int32), tile_rows.astype(jnp.int32),
            n_tiles.reshape(1).astype(jnp.int32))


def _invert_plan_kernel(starts_ref, gid_ref, rank_ref, src_ref):
    step = pl.program_id(0)

    @pl.when(step == 0)
    def _():
        def clear(p, carry):
            src_ref[p] = 0
            return carry
        lax.fori_loop(0, src_ref.shape[0], clear, 0, unroll=8)

    def place(r, carry):
        src_ref[starts_ref[gid_ref[0, r]] + rank_ref[0, r]] = step * PLAN_CHUNK + r
        return carry

    lax.fori_loop(0, PLAN_CHUNK, place, 0, unroll=8)


def _invert_plan(starts, gid, rank, n_rows):
    t = gid.shape[1]
    chunks = t // PLAN_CHUNK
    chunk_spec = pl.BlockSpec((None, 1, PLAN_CHUNK), lambda i: (i, 0, 0), memory_space=pltpu.SMEM)
    return pl.pallas_call(
        _invert_plan_kernel,
        out_shape=jax.ShapeDtypeStruct((n_rows,), jnp.int32),
        grid=(chunks,),
        in_specs=[pl.BlockSpec(memory_space=pltpu.SMEM), chunk_spec, chunk_spec],
        out_specs=pl.BlockSpec((n_rows,), lambda i: (0,), memory_space=pltpu.SMEM),
        compiler_params=_cparams(("arbitrary",)),
        name="invert_plan",
    )(starts, gid.reshape(chunks, 1, PLAN_CHUNK), rank.reshape(chunks, 1, PLAN_CHUNK))


def _for_rows(n_rows, fn):
    def chunk(q, carry):
        for u in range(8):
            fn(q * 8 + u)
        return carry

    def single(r, carry):
        fn(r)
        return carry

    full = n_rows // 8
    lax.fori_loop(0, full, chunk, 0)
    lax.fori_loop(full * 8, n_rows, single, 0)


def _moe_rows_kernel(tile_group_ref, tile_rows_ref, n_tiles_ref, src_ref, src_next_ref, x_hbm,
                     wg_ref, wu_ref, wd_ref, g_ref, b_ref, out_hbm, xbuf, obuf, gsem, ssem):
    del tile_group_ref
    k = pl.program_id(0)
    n_tiles = n_tiles_ref[0]
    slot = k % 2

    def gather(tile, rows_ref, buf_slot, start):
        def one(r):
            copy = _row_copy(x_hbm, xbuf.at[buf_slot], rows_ref[0, r] if start else 0, r if start else 0,
                             gsem.at[buf_slot])
            copy.start() if start else copy.wait()
        _for_rows(tile_rows_ref[tile], one)

    def scatter(tile, rows_ref, buf_slot, start):
        def one(r):
            copy = _row_copy(obuf.at[buf_slot], out_hbm, r if start else 0, rows_ref[0, r] if start else 0,
                             ssem.at[buf_slot])
            copy.start() if start else copy.wait()
        _for_rows(tile_rows_ref[tile], one)

    @pl.when(k == 0)
    def _():
        xbuf[...] = jnp.zeros(xbuf.shape, F32)
        gather(0, src_ref, 0, True)

    @pl.when(k + 1 < n_tiles)
    def _():
        gather(k + 1, src_next_ref, 1 - slot, True)

    @pl.when(k < n_tiles)
    def _():
        gather(k, src_ref, slot, False)

        @pl.when(k >= 2)
        def _():
            scatter(k - 2, src_ref, slot, False)

        x = xbuf[slot, :, :D_MODEL]
        cw = xbuf[slot, :, D_MODEL:]
        xb = x.astype(BF16)
        y = jnp.zeros(x.shape, F32)
        for j in range(GROUP_SIZE):
            hg = _dot(xb, wg_ref[j])
            hu = _dot(xb, wu_ref[j])
            hid = (hg / (1.0 + jnp.exp(-hg))) * hu * cw[:, j:j + 1]
            y = y + _dot(hid.astype(BF16), wd_ref[j])
        obuf[slot] = _ln(DEEPNORM_ALPHA * x + y, g_ref[...], b_ref[...])
        scatter(k, src_ref, slot, True)

    @pl.when(k == n_tiles - 1)
    def _():
        @pl.when(k >= 1)
        def _():
            scatter(k - 1, src_ref, 1 - slot, False)
        scatter(k, src_ref, slot, False)


def _moe_rows(x_rows, src, tile_group, tile_rows, n_tiles, wg, wu, wd, g, b, layer):
    t = x_rows.shape[0]
    n_tiles_max = src.shape[0] // MOE_TILE
    d, gs, ff = D_MODEL, GROUP_SIZE, EXPERT_FF
    src3 = src.reshape(n_tiles_max, 1, MOE_TILE)
    grid_spec = pltpu.PrefetchScalarGridSpec(
        num_scalar_prefetch=3,
        grid=(n_tiles_max,),
        in_specs=[pl.BlockSpec((None, 1, MOE_TILE), lambda k, tg, tr, nt: (k, 0, 0), memory_space=pltpu.SMEM),
                  pl.BlockSpec((None, 1, MOE_TILE), lambda k, tg, tr, nt: (jnp.minimum(k + 1, n_tiles_max - 1), 0, 0),
                               memory_space=pltpu.SMEM),
                  pl.BlockSpec(memory_space=pl.ANY),
                  pl.BlockSpec((None, gs, d, ff), lambda k, tg, tr, nt: (layer, tg[k], 0, 0)),
                  pl.BlockSpec((None, gs, d, ff), lambda k, tg, tr, nt: (layer, tg[k], 0, 0)),
                  pl.BlockSpec((None, gs, ff, d), lambda k, tg, tr, nt: (layer, tg[k], 0, 0)),
                  pl.BlockSpec((1, d), lambda k, tg, tr, nt: (0, 0)),
                  pl.BlockSpec((1, d), lambda k, tg, tr, nt: (0, 0))],
        out_specs=pl.BlockSpec(memory_space=pl.ANY),
        scratch_shapes=[pltpu.VMEM((2, MOE_TILE, ROW_WIDE), F32), pltpu.VMEM((2, MOE_TILE, d), F32),
                        pltpu.SemaphoreType.DMA((2,)), pltpu.SemaphoreType.DMA((2,))])
    return pl.pallas_call(
        _moe_rows_kernel,
        out_shape=jax.ShapeDtypeStruct((t, d), F32),
        grid_spec=grid_spec,
        compiler_params=_cparams(("arbitrary",)),
        name="grouped_moe",
    )(tile_group, tile_rows, n_tiles, src3, src3, x_rows, wg, wu, wd, g, b)


def _rope_tables(seq):
    inv_freq = ROPE_BASE ** (-jnp.arange(0, MLA_ROPE_DIM, 2, dtype=F32) / MLA_ROPE_DIM)
    ang = jnp.arange(seq, dtype=F32)[:, None] * inv_freq[None, :]
    cos, sin = jnp.cos(ang), jnp.sin(ang)
    cos2 = jnp.concatenate([cos, cos], axis=1)
    sin2 = jnp.concatenate([-sin, sin], axis=1)
    zero = jnp.zeros((seq, LANES - MLA_ROPE_DIM), F32)
    return (jnp.concatenate([cos2, zero], axis=1), jnp.concatenate([sin2, zero], axis=1), cos2.T, sin2.T)


def _swap_halves(w):
    half = MLA_ROPE_DIM // 2
    return jnp.concatenate([w[..., half:], w[..., :half]], axis=-1)


def kernel(x, diff_w_qkv, diff_lambda, diff_subln_g, diff_w_o, mla_w_kv_a, mla_kv_norm_g, mla_w_kv_b,
           mla_w_q_a, mla_q_norm_g, mla_w_q_b, mla_w_o, router_w, router_b, moe_w_gate, moe_w_up,
           moe_w_down, ln_g, ln_b):
    batch, seq, d = x.shape
    t = batch * seq
    xs = x.reshape(t, d)

    wq, wk, wv = jnp.split(diff_w_qkv, 3, axis=-1)
    w_k = wk.astype(BF16)
    w_qv_t = jnp.concatenate([wq * (LOG2E * DIFF_HEAD_DIM ** -0.5), wv], axis=-1).transpose(0, 2, 1).astype(BF16)
    w_o_a = diff_w_o.astype(BF16)
    w_o_b = mla_w_o.astype(BF16)
    wg = moe_w_gate.astype(BF16)
    wu = moe_w_up.astype(BF16)
    wd = moe_w_down.astype(BF16)
    slopes = LOG2E * 2.0 ** (-8.0 * jnp.arange(1, DIFF_HEADS + 1, dtype=F32) / DIFF_HEADS)
    slope_hi = slopes.astype(BF16).astype(F32)
    slope_lo = (slopes - slope_hi).astype(BF16).astype(F32)
    rw_t = router_w.T.reshape(N_GROUPS, GROUP_SIZE, d).transpose(1, 0, 2).reshape(N_EXPERTS, d)
    rb_t = router_b.reshape(N_GROUPS, GROUP_SIZE).T.reshape(N_EXPERTS, 1).astype(F32)
    cs, sn, cs_t, sn_t = _rope_tables(seq)
    kv_rope_w = mla_w_kv_a[:, MLA_KV_RANK:]
    w_kv_a = jnp.concatenate([mla_w_kv_a, _swap_halves(kv_rope_w)], axis=1).astype(BF16)
    w_kv_b = mla_w_kv_b.reshape(MLA_KV_RANK, MLA_HEADS, MLA_NOPE_DIM + MLA_V_DIM)
    w_kn = w_kv_b[..., :MLA_NOPE_DIM].reshape(MLA_KV_RANK, -1).astype(BF16)
    w_v_t = w_kv_b[..., MLA_NOPE_DIM:].reshape(MLA_KV_RANK, -1).T.astype(BF16)
    wqb = mla_w_q_b.reshape(-1, MLA_Q_RANK, MLA_HEADS, MLA_NOPE_DIM + MLA_ROPE_DIM)
    w_q_b_t = jnp.concatenate([wqb, _swap_halves(wqb[..., MLA_NOPE_DIM:])], axis=-1).reshape(
        -1, MLA_Q_RANK, MLA_HEADS * MLA_QK_PAD).transpose(0, 2, 1).astype(BF16)
    w_q_a = mla_w_q_a.astype(BF16)

    alibi_fix, diag_mask = _diagonal_tables()
    tri = jnp.triu(jnp.ones((ROUTE_TILE, ROUTE_TILE), BF16), 1)
    n_tiles_max = t // MOE_TILE + N_GROUPS

    kcat = vt_shared = None
    for layer in range(DEPTH):
        if layer < N_A_LAYERS:
            k, qt, vt = _qkv_proj(xs, w_k[layer], w_qv_t[layer])
            attn = _diff_attention2(qt, k, vt, slope_hi, slope_lo, diff_lambda[layer],
                                    diff_subln_g[layer].reshape(-1, 1), alibi_fix, batch, seq, _lambda_init(layer))
            w_o = w_o_a[layer]
        else:
            j = layer - N_A_LAYERS
            if j == 0:
                kcat, vt_shared = _mla_kv_proj(xs, w_kv_a, mla_kv_norm_g.reshape(1, -1), w_kn, w_v_t, cs, sn, seq)
            qt = _mla_q_proj(xs, w_q_a[j], mla_q_norm_g[j].reshape(1, -1), w_q_b_t[j], cs_t, sn_t, seq)
            attn = _mla_attention2(qt, kcat, vt_shared, diag_mask, batch, seq)
            w_o = w_o_b[j]
        x1, gid, rank, counts = _out_norm_route(attn, w_o, xs, ln_g[layer, 0].reshape(1, -1),
                                                ln_b[layer, 0].reshape(1, -1), rw_t, rb_t, tri)
        starts, tile_group, tile_rows, n_tiles = _tile_plan(counts, n_tiles_max)
        src = _invert_plan(starts, gid, rank, n_tiles_max * MOE_TILE)
        xs = _moe_rows(x1, src, tile_group, tile_rows, n_tiles, wg, wu, wd, ln_g[layer, 1].reshape(1, -1),
                       ln_b[layer, 1].reshape(1, -1), layer)
    return xs.reshape(batch, seq, d)
```

```python
import functools
import math

import jax
import jax.numpy as jnp
from jax import lax
from jax.experimental import pallas as pl
from jax.experimental.pallas import tpu as pltpu

D_MODEL = 1024
DEPTH = 4
CHUNK = 64
N_A_LAYERS = DEPTH // 2
DIFF_HEAD_DIM = 64
DIFF_V_DIM = 2 * DIFF_HEAD_DIM
DIFF_HEADS = D_MODEL // DIFF_V_DIM
MLA_NOPE_DIM = 128
MLA_ROPE_DIM = 64
MLA_V_DIM = 128
MLA_HEADS = D_MODEL // MLA_V_DIM
MLA_Q_RANK = 384
MLA_KV_RANK = 256
MLA_QK_PAD = 256
ROPE_BASE = 10000.0
N_EXPERTS = 32
N_GROUPS = 8
GROUP_SIZE = N_EXPERTS // N_GROUPS
EXPERT_FF = 256
NORM_EPS = 1e-5
DEEPNORM_ALPHA = (2 * DEPTH) ** 0.25
LOG2E = math.log2(math.e)

LANES = 128
T_TILE = 256
KV_UNROLL = 4
GROUP_TILES = 2
Q_TILE = GROUP_TILES * T_TILE
MASKED = 1e30
ROUTE_TILE = 256
MOE_TILE = 256
PERM_ROWS = 256
PLAN_CHUNK = 2048
ROW_WIDE = D_MODEL + LANES
VMEM_LIMIT = 48 * 1024 * 1024

F32 = jnp.float32
BF16 = jnp.bfloat16


def _lambda_init(layer):
    return 0.8 - 0.6 * math.exp(-0.3 * layer)


def _cparams(sem):
    return pltpu.CompilerParams(dimension_semantics=sem, vmem_limit_bytes=VMEM_LIMIT)


def _dot(a, b):
    return jnp.dot(a, b, preferred_element_type=F32)


def _dot_nt(a, b):
    return lax.dot_general(a, b, (((1,), (1,)), ((), ())), preferred_element_type=F32)


def _rms(x, g):
    return x * lax.rsqrt(jnp.mean(x * x, axis=-1, keepdims=True) + NORM_EPS) * g


def _ln(x, g, b):
    mu = jnp.mean(x, axis=-1, keepdims=True)
    xc = x - mu
    var = jnp.mean(xc * xc, axis=-1, keepdims=True)
    return xc * lax.rsqrt(var + NORM_EPS) * g + b


def _rope_halves(a, cs, sn):
    return a * cs + pltpu.roll(a, MLA_ROPE_DIM, axis=1) * sn


def _qkv_kernel(x_ref, wk_ref, wqv_ref, k_ref, qt_ref, vt_ref):
    xb = x_ref[...].astype(BF16)
    k_ref[...] = _dot(xb, wk_ref[...]).astype(BF16)
    qv = _dot_nt(wqv_ref[...], xb)
    for h in range(DIFF_HEADS):
        qt_ref[0, h] = qv[h * LANES:(h + 1) * LANES].astype(BF16)
        vt_ref[0, h] = qv[D_MODEL + h * LANES:D_MODEL + (h + 1) * LANES].astype(BF16)


def _qkv_proj(x, wk, wqv):
    t, d = x.shape
    nt = t // T_TILE
    head_major = jax.ShapeDtypeStruct((nt, DIFF_HEADS, LANES, T_TILE), BF16)
    head_block = pl.BlockSpec((1, DIFF_HEADS, LANES, T_TILE), lambda i: (i, 0, 0, 0))
    return pl.pallas_call(
        _qkv_kernel,
        out_shape=(jax.ShapeDtypeStruct((t, d), BF16), head_major, head_major),
        grid=(nt,),
        in_specs=[pl.BlockSpec((T_TILE, d), lambda i: (i, 0)),
                  pl.BlockSpec(wk.shape, lambda i: (0, 0)),
                  pl.BlockSpec(wqv.shape, lambda i: (0, 0))],
        out_specs=(pl.BlockSpec((T_TILE, d), lambda i: (i, 0)), head_block, head_block),
        compiler_params=_cparams(("parallel",)),
        name="qkv_proj",
    )(x, wk, wqv)


def _mla_kv_kernel(x_ref, wa_ref, g_ref, wk_ref, wv_ref, cs_ref, sn_ref, kcat_ref, vt_ref):
    kva = _dot(x_ref[...].astype(BF16), wa_ref[...])
    c_kv = _rms(kva[:, :MLA_KV_RANK], g_ref[...]).astype(BF16)
    kr = _rope_halves(kva[:, MLA_KV_RANK:], cs_ref[...], sn_ref[...]).astype(BF16)
    kn = _dot(c_kv, wk_ref[...])
    vt = _dot_nt(wv_ref[...], c_kv)
    for h in range(MLA_HEADS):
        base = h * MLA_QK_PAD
        kcat_ref[:, base:base + MLA_NOPE_DIM] = kn[:, h * MLA_NOPE_DIM:(h + 1) * MLA_NOPE_DIM].astype(BF16)
        kcat_ref[:, base + MLA_NOPE_DIM:base + MLA_QK_PAD] = kr
        vt_ref[0, h] = vt[h * MLA_V_DIM:(h + 1) * MLA_V_DIM].astype(BF16)


def _mla_kv_proj(x, wa, g, wk, wv, cs, sn, seq):
    t, d = x.shape
    nt = t // T_TILE
    ns = seq // T_TILE
    return pl.pallas_call(
        _mla_kv_kernel,
        out_shape=(jax.ShapeDtypeStruct((t, MLA_HEADS * MLA_QK_PAD), BF16),
                   jax.ShapeDtypeStruct((nt, MLA_HEADS, MLA_V_DIM, T_TILE), BF16)),
        grid=(nt,),
        in_specs=[pl.BlockSpec((T_TILE, d), lambda i: (i, 0)),
                  pl.BlockSpec(wa.shape, lambda i: (0, 0)),
                  pl.BlockSpec(g.shape, lambda i: (0, 0)),
                  pl.BlockSpec(wk.shape, lambda i: (0, 0)),
                  pl.BlockSpec(wv.shape, lambda i: (0, 0)),
                  pl.BlockSpec((T_TILE, LANES), lambda i: (i % ns, 0)),
                  pl.BlockSpec((T_TILE, LANES), lambda i: (i % ns, 0))],
        out_specs=(pl.BlockSpec((T_TILE, MLA_HEADS * MLA_QK_PAD), lambda i: (i, 0)),
                   pl.BlockSpec((1, MLA_HEADS, MLA_V_DIM, T_TILE), lambda i: (i, 0, 0, 0))),
        compiler_params=_cparams(("parallel",)),
        name="mla_kv_proj",
    )(x, wa, g, wk, wv, cs, sn)


def _mla_q_kernel(x_ref, wa_ref, g_ref, wb_ref, cst_ref, snt_ref, qt_ref, *, scale):
    qa = _dot(x_ref[...].astype(BF16), wa_ref[...])
    qn = _rms(qa, g_ref[...]).astype(BF16)
    qt = _dot_nt(wb_ref[...], qn) * scale
    cst = cst_ref[...]
    snt = snt_ref[...]
    zero = jnp.zeros((MLA_ROPE_DIM, T_TILE), BF16)
    for h in range(MLA_HEADS):
        base = h * MLA_QK_PAD
        r0 = base + MLA_NOPE_DIM
        rope = qt[r0:r0 + MLA_ROPE_DIM] * cst + qt[r0 + MLA_ROPE_DIM:base + MLA_QK_PAD] * snt
        qt_ref[0, h, 0:MLA_NOPE_DIM] = qt[base:r0].astype(BF16)
        qt_ref[0, h, MLA_NOPE_DIM:MLA_NOPE_DIM + MLA_ROPE_DIM] = rope.astype(BF16)
        qt_ref[0, h, MLA_NOPE_DIM + MLA_ROPE_DIM:MLA_QK_PAD] = zero


def _mla_q_proj(x, wa, g, wb_t, cst, snt, seq):
    t, d = x.shape
    nt = t // T_TILE
    ns = seq // T_TILE
    scale = LOG2E * (MLA_NOPE_DIM + MLA_ROPE_DIM) ** -0.5
    return pl.pallas_call(
        functools.partial(_mla_q_kernel, scale=scale),
        out_shape=jax.ShapeDtypeStruct((nt, MLA_HEADS, MLA_QK_PAD, T_TILE), BF16),
        grid=(nt,),
        in_specs=[pl.BlockSpec((T_TILE, d), lambda i: (i, 0)),
                  pl.BlockSpec(wa.shape, lambda i: (0, 0)),
                  pl.BlockSpec(g.shape, lambda i: (0, 0)),
                  pl.BlockSpec(wb_t.shape, lambda i: (0, 0)),
                  pl.BlockSpec((MLA_ROPE_DIM, T_TILE), lambda i: (0, i % ns)),
                  pl.BlockSpec((MLA_ROPE_DIM, T_TILE), lambda i: (0, i % ns))],
        out_specs=pl.BlockSpec((1, MLA_HEADS, MLA_QK_PAD, T_TILE), lambda i: (i, 0, 0, 0)),
        compiler_params=_cparams(("parallel",)),
        name="mla_q_proj",
    )(x, wa, g, wb_t, cst, snt)


def _tile_softmax(tiles):
    probs = []
    for s_t, shift, vt in tiles:
        m_t = jnp.max(s_t, axis=0, keepdims=True)
        p_t = jnp.exp2(s_t - m_t)
        probs.append((m_t + shift, jnp.sum(p_t, axis=0, keepdims=True), p_t.astype(BF16), vt))
    return [(m_t, l_t, _dot(vt, p_t)) for m_t, l_t, p_t, vt in probs]


def _merge_tiles(parts, c, m_ref, l_ref, acc_ref):
    m_prev = m_ref[c]
    m_new = m_prev
    for m_t, _, _ in parts:
        m_new = jnp.maximum(m_new, m_t)
    alpha = jnp.exp2(m_prev - m_new)
    l = alpha * l_ref[c]
    acc = alpha * acc_ref[c]
    for m_t, l_t, pv in parts:
        w = jnp.exp2(m_t - m_new)
        l = l + w * l_t
        acc = acc + w * pv
    m_ref[c] = m_new
    l_ref[c] = l
    acc_ref[c] = acc


def _for_key_tiles(n_past, body):
    def group(g, carry):
        body(g * KV_UNROLL, KV_UNROLL, False)
        return carry

    n_groups = n_past // KV_UNROLL
    lax.fori_loop(0, n_groups, group, 0)
    left = n_past - n_groups * KV_UNROLL
    for r in range(KV_UNROLL):
        @pl.when(left == r)
        def _(r=r):
            body(n_groups * KV_UNROLL, r, True)


def _init_softmax(m_ref, l_ref, acc_ref):
    m_ref[...] = jnp.full(m_ref.shape, -jnp.inf, F32)
    l_ref[...] = jnp.zeros(l_ref.shape, F32)
    acc_ref[...] = jnp.zeros(acc_ref.shape, F32)


def _key_query_iota():
    rk = lax.broadcasted_iota(jnp.int32, (T_TILE, T_TILE), 0)
    rq = lax.broadcasted_iota(jnp.int32, (T_TILE, T_TILE), 1)
    return rk, rq, (rk // CHUNK) <= (rq // CHUNK)


def _diff_attn_kernel(slh_ref, sll_ref, lam_ref, g_ref, qt_ref, k_ref, vt_ref, o_ref,
                      m_ref, l_ref, acc_ref, *, lambda_init):
    h = pl.program_id(1)
    i = pl.program_id(2)
    sl_hi = slh_ref[h]
    sl_lo = sll_ref[h]
    slope = sl_hi + sl_lo
    row = lax.broadcasted_iota(jnp.int32, (LANES, T_TILE), 0)
    qf = qt_ref[...].astype(F32)
    qs = (jnp.where(row < DIFF_HEAD_DIM, qf, 0.0).astype(BF16),
          jnp.where(row >= DIFF_HEAD_DIM, qf, 0.0).astype(BF16))
    slope_rows = jnp.where(row == 0, sl_hi, jnp.where(row == 1, sl_lo, 0.0)).astype(BF16)
    qs_aug = [jnp.concatenate([q, slope_rows], axis=0) for q in qs]
    key_off = lax.broadcasted_iota(jnp.int32, (T_TILE, LANES), 0).astype(F32)
    lane = lax.broadcasted_iota(jnp.int32, (T_TILE, LANES), 1)
    key_cols = jnp.where(lane < 2, key_off, 0.0).astype(BF16)
    _init_softmax(m_ref, l_ref, acc_ref)

    def key_tiles(j0, n_past, with_diagonal):
        tiles = []
        for u in range(n_past):
            j = j0 + u
            off = pl.multiple_of(j * T_TILE, T_TILE)
            k_aug = jnp.concatenate([k_ref[pl.ds(off, T_TILE), :], key_cols], axis=1)
            shift = slope * ((j - i) * T_TILE).astype(F32)
            tiles += [(_dot(k_aug, qs_aug[c]), shift, vt_ref[j]) for c in range(2)]
        if with_diagonal:
            off = pl.multiple_of(i * T_TILE, T_TILE)
            kb = k_ref[pl.ds(off, T_TILE), :]
            rk, rq, visible = _key_query_iota()
            bias = slope * (rq - jnp.abs(rq - rk)).astype(F32)
            tiles += [(jnp.where(visible, _dot(kb, qs[c]) + bias, -jnp.inf), 0.0, vt_ref[i]) for c in range(2)]
        parts = _tile_softmax(tiles)
        for c in range(2):
            _merge_tiles(parts[c::2], c, m_ref, l_ref, acc_ref)

    _for_key_tiles(i, key_tiles)

    lam = lam_ref[...]
    lam_full = (jnp.exp(jnp.sum(lam[0:1] * lam[1:2], axis=1, keepdims=True))
                - jnp.exp(jnp.sum(lam[2:3] * lam[3:4], axis=1, keepdims=True)) + lambda_init)
    o_t = acc_ref[0] / l_ref[0] - lam_full * (acc_ref[1] / l_ref[1])
    inv = lax.rsqrt(jnp.mean(o_t * o_t, axis=0, keepdims=True) + NORM_EPS)
    o_t = o_t * inv * (g_ref[...] * (1.0 - lambda_init))
    o_ref[...] = o_t.T.astype(o_ref.dtype)


def _diff_attention(qt, k, vt, slope_hi, slope_lo, lam, g_col, batch, seq, lambda_init):
    nq = seq // T_TILE
    hh = DIFF_HEADS
    return pl.pallas_call(
        functools.partial(_diff_attn_kernel, lambda_init=lambda_init),
        out_shape=jax.ShapeDtypeStruct((batch * seq, hh * DIFF_V_DIM), BF16),
        grid=(batch, hh, nq),
        in_specs=[pl.BlockSpec(memory_space=pltpu.SMEM),
                  pl.BlockSpec(memory_space=pltpu.SMEM),
                  pl.BlockSpec(lam.shape, lambda b, h, i: (0, 0)),
                  pl.BlockSpec(g_col.shape, lambda b, h, i: (0, 0)),
                  pl.BlockSpec((None, None, LANES, T_TILE), lambda b, h, i: (b * nq + i, h, 0, 0)),
                  pl.BlockSpec((seq, LANES), lambda b, h, i: (b, h)),
                  pl.BlockSpec((nq, None, DIFF_V_DIM, T_TILE), lambda b, h, i: (b, h, 0, 0))],
        out_specs=pl.BlockSpec((T_TILE, LANES), lambda b, h, i: (b * nq + i, h)),
        scratch_shapes=[pltpu.VMEM((2, 1, T_TILE), F32), pltpu.VMEM((2, 1, T_TILE), F32),
                        pltpu.VMEM((2, DIFF_V_DIM, T_TILE), F32)],
        compiler_params=_cparams(("parallel", "parallel", "arbitrary")),
        name="diff_attention",
    )(slope_hi, slope_lo, lam, g_col, qt, k, vt)


def _mla_attn_kernel(qt_ref, k_ref, vt_ref, o_ref, m_ref, l_ref, acc_ref):
    i = pl.program_id(2)
    qt = qt_ref[...]
    _init_softmax(m_ref, l_ref, acc_ref)

    def key_tiles(j0, n_past, with_diagonal):
        tiles = []
        for u in range(n_past):
            off = pl.multiple_of((j0 + u) * T_TILE, T_TILE)
            tiles.append((_dot(k_ref[pl.ds(off, T_TILE), :], qt), 0.0, vt_ref[j0 + u]))
        if with_diagonal:
            off = pl.multiple_of(i * T_TILE, T_TILE)
            _, _, visible = _key_query_iota()
            s_t = jnp.where(visible, _dot(k_ref[pl.ds(off, T_TILE), :], qt), -jnp.inf)
            tiles.append((s_t, 0.0, vt_ref[i]))
        _merge_tiles(_tile_softmax(tiles), 0, m_ref, l_ref, acc_ref)

    _for_key_tiles(i, key_tiles)
    o_ref[...] = (acc_ref[0] / l_ref[0]).T.astype(o_ref.dtype)


def _mla_attention(qt, kcat, vt, batch, seq):
    nq = seq // T_TILE
    return pl.pallas_call(
        _mla_attn_kernel,
        out_shape=jax.ShapeDtypeStruct((batch * seq, MLA_HEADS * MLA_V_DIM), BF16),
        grid=(batch, MLA_HEADS, nq),
        in_specs=[pl.BlockSpec((None, None, MLA_QK_PAD, T_TILE), lambda b, h, i: (b * nq + i, h, 0, 0)),
                  pl.BlockSpec((seq, MLA_QK_PAD), lambda b, h, i: (b, h)),
                  pl.BlockSpec((nq, None, MLA_V_DIM, T_TILE), lambda b, h, i: (b, h, 0, 0))],
        out_specs=pl.BlockSpec((T_TILE, MLA_V_DIM), lambda b, h, i: (b * nq + i, h)),
        scratch_shapes=[pltpu.VMEM((1, 1, T_TILE), F32), pltpu.VMEM((1, 1, T_TILE), F32),
                        pltpu.VMEM((1, MLA_V_DIM, T_TILE), F32)],
        compiler_params=_cparams(("parallel", "parallel", "arbitrary")),
        name="mla_attention",
    )(qt, kcat, vt)


def _consume_group(tiles, n_comp, m_ref, l_ref, acc_ref):
    parts = []
    for s_t, shift, vt in tiles:
        m_t = jnp.max(s_t, axis=0, keepdims=True)
        p_t = jnp.exp2(s_t - m_t)
        parts.append((m_t + shift, jnp.sum(p_t, axis=0, keepdims=True), _dot(vt, p_t.astype(BF16))))
    for c in range(n_comp):
        _merge_tiles(parts[c::n_comp], c, m_ref, l_ref, acc_ref)


def _attention_pipeline(n_prev_groups, issue_scores, consume, buf_a, buf_b):
    issue_scores(0, buf_a)

    def pair(p, carry):
        issue_scores(2 * p + 1, buf_b)
        consume(2 * p, buf_a, False)
        issue_scores(2 * p + 2, buf_a)
        consume(2 * p + 1, buf_b, False)
        return carry

    lax.fori_loop(0, n_prev_groups // 2, pair, 0)
    odd = n_prev_groups % 2

    @pl.when(odd == 1)
    def _():
        issue_scores(n_prev_groups, buf_b)
        consume(n_prev_groups - 1, buf_a, False)
        consume(n_prev_groups, buf_b, True)

    @pl.when(odd == 0)
    def _():
        consume(n_prev_groups, buf_a, True)


def _diff_attn2_kernel(slh_ref, sll_ref, lam_ref, g_ref, dm_ref, qt_ref, k_ref, vt_ref, o_ref,
                       buf_a, buf_b, m_ref, l_ref, acc_ref, *, lambda_init):
    h = pl.program_id(1)
    i = pl.program_id(2)
    sl_hi = slh_ref[h]
    sl_lo = sll_ref[h]
    slope = sl_hi + sl_lo
    row = lax.broadcasted_iota(jnp.int32, (LANES, Q_TILE), 0)
    qf = jnp.concatenate([qt_ref[u] for u in range(GROUP_TILES)], axis=1).astype(F32)
    slope_rows = jnp.where(row == 0, sl_hi, jnp.where(row == 1, sl_lo, 0.0)).astype(BF16)
    qs_aug = [jnp.concatenate([jnp.where(keep, qf, 0.0).astype(BF16), slope_rows], axis=0)
              for keep in (row < DIFF_HEAD_DIM, row >= DIFF_HEAD_DIM)]
    key_off = lax.broadcasted_iota(jnp.int32, (T_TILE, LANES), 0).astype(F32)
    lane = lax.broadcasted_iota(jnp.int32, (T_TILE, LANES), 1)
    key_cols = jnp.where(lane < 2, key_off, 0.0).astype(BF16)
    _init_softmax(m_ref, l_ref, acc_ref)

    def issue_scores(g, buf):
        for u in range(GROUP_TILES):
            off = pl.multiple_of((g * GROUP_TILES + u) * T_TILE, T_TILE)
            k_aug = jnp.concatenate([k_ref[pl.ds(off, T_TILE), :], key_cols], axis=1)
            for c in range(2):
                buf[2 * u + c] = _dot(k_aug, qs_aug[c])

    def consume(g, buf, diagonal):
        tiles = []
        for u in range(GROUP_TILES):
            j = g * GROUP_TILES + u
            shift = slope * (j * T_TILE - i * Q_TILE).astype(F32)
            for c in range(2):
                s_t = buf[2 * u + c]
                if diagonal:
                    s_t = s_t + slope * dm_ref[u * T_TILE:(u + 1) * T_TILE, :]
                tiles.append((s_t, shift, vt_ref[j]))
        _consume_group(tiles, 2, m_ref, l_ref, acc_ref)

    _attention_pipeline(i, issue_scores, consume, buf_a, buf_b)

    lam = lam_ref[...]
    lam_full = (jnp.exp(jnp.sum(lam[0:1] * lam[1:2], axis=1, keepdims=True))
                - jnp.exp(jnp.sum(lam[2:3] * lam[3:4], axis=1, keepdims=True)) + lambda_init)
    o_t = acc_ref[0] / l_ref[0] - lam_full * (acc_ref[1] / l_ref[1])
    inv = lax.rsqrt(jnp.mean(o_t * o_t, axis=0, keepdims=True) + NORM_EPS)
    o_t = o_t * inv * (g_ref[...] * (1.0 - lambda_init))
    o_ref[...] = o_t.T.astype(o_ref.dtype)


def _diagonal_tables():
    rk = jnp.arange(Q_TILE, dtype=jnp.int32)[:, None]
    rq = jnp.arange(Q_TILE, dtype=jnp.int32)[None, :]
    visible = (rk // CHUNK) <= (rq // CHUNK)
    alibi_fix = jnp.where(visible, -2.0 * jnp.maximum(rk - rq, 0).astype(F32), -MASKED)
    mask = jnp.where(visible, 0.0, -MASKED).astype(F32)
    return alibi_fix, mask


def _diff_attention2(qt, k, vt, slope_hi, slope_lo, lam, g_col, alibi_fix, batch, seq, lambda_init):
    nq = seq // Q_TILE
    nk = seq // T_TILE
    hh = DIFF_HEADS
    return pl.pallas_call(
        functools.partial(_diff_attn2_kernel, lambda_init=lambda_init),
        out_shape=jax.ShapeDtypeStruct((batch * seq, hh * DIFF_V_DIM), BF16),
        grid=(batch, hh, nq),
        in_specs=[pl.BlockSpec(memory_space=pltpu.SMEM),
                  pl.BlockSpec(memory_space=pltpu.SMEM),
                  pl.BlockSpec(lam.shape, lambda b, h, i: (0, 0)),
                  pl.BlockSpec(g_col.shape, lambda b, h, i: (0, 0)),
                  pl.BlockSpec(alibi_fix.shape, lambda b, h, i: (0, 0)),
                  pl.BlockSpec((GROUP_TILES, None, LANES, T_TILE), lambda b, h, i: (b * nq + i, h, 0, 0)),
                  pl.BlockSpec((seq, LANES), lambda b, h, i: (b, h)),
                  pl.BlockSpec((nk, None, DIFF_V_DIM, T_TILE), lambda b, h, i: (b, h, 0, 0))],
        out_specs=pl.BlockSpec((Q_TILE, LANES), lambda b, h, i: (b * nq + i, h)),
        scratch_shapes=[pltpu.VMEM((2 * GROUP_TILES, T_TILE, Q_TILE), F32),
                        pltpu.VMEM((2 * GROUP_TILES, T_TILE, Q_TILE), F32),
                        pltpu.VMEM((2, 1, Q_TILE), F32), pltpu.VMEM((2, 1, Q_TILE), F32),
                        pltpu.VMEM((2, DIFF_V_DIM, Q_TILE), F32)],
        compiler_params=_cparams(("parallel", "parallel", "arbitrary")),
        name="diff_attention",
    )(slope_hi, slope_lo, lam, g_col, alibi_fix, qt, k, vt)


def _mla_attn2_kernel(mask_ref, qt_ref, k_ref, vt_ref, o_ref, buf_a, buf_b, m_ref, l_ref, acc_ref):
    i = pl.program_id(2)
    qt = jnp.concatenate([qt_ref[u] for u in range(GROUP_TILES)], axis=1)
    _init_softmax(m_ref, l_ref, acc_ref)

    def issue_scores(g, buf):
        for u in range(GROUP_TILES):
            off = pl.multiple_of((g * GROUP_TILES + u) * T_TILE, T_TILE)
            buf[u] = _dot(k_ref[pl.ds(off, T_TILE), :], qt)

    def consume(g, buf, diagonal):
        tiles = []
        for u in range(GROUP_TILES):
            s_t = buf[u]
            if diagonal:
                s_t = s_t + mask_ref[u * T_TILE:(u + 1) * T_TILE, :]
            tiles.append((s_t, 0.0, vt_ref[g * GROUP_TILES + u]))
        _consume_group(tiles, 1, m_ref, l_ref, acc_ref)

    _attention_pipeline(i, issue_scores, consume, buf_a, buf_b)
    o_ref[...] = (acc_ref[0] / l_ref[0]).T.astype(o_ref.dtype)


def _mla_attention2(qt, kcat, vt, mask, batch, seq):
    nq = seq // Q_TILE
    nk = seq // T_TILE
    return pl.pallas_call(
        _mla_attn2_kernel,
        out_shape=jax.ShapeDtypeStruct((batch * seq, MLA_HEADS * MLA_V_DIM), BF16),
        grid=(batch, MLA_HEADS, nq),
        in_specs=[pl.BlockSpec(mask.shape, lambda b, h, i: (0, 0)),
                  pl.BlockSpec((GROUP_TILES, None, MLA_QK_PAD, T_TILE), lambda b, h, i: (b * nq + i, h, 0, 0)),
                  pl.BlockSpec((seq, MLA_QK_PAD), lambda b, h, i: (b, h)),
                  pl.BlockSpec((nk, None, MLA_V_DIM, T_TILE), lambda b, h, i: (b, h, 0, 0))],
        out_specs=pl.BlockSpec((Q_TILE, MLA_V_DIM), lambda b, h, i: (b * nq + i, h)),
        scratch_shapes=[pltpu.VMEM((GROUP_TILES, T_TILE, Q_TILE), F32),
                        pltpu.VMEM((GROUP_TILES, T_TILE, Q_TILE), F32),
                        pltpu.VMEM((1, 1, Q_TILE), F32), pltpu.VMEM((1, 1, Q_TILE), F32),
                        pltpu.VMEM((1, MLA_V_DIM, Q_TILE), F32)],
        compiler_params=_cparams(("parallel", "parallel", "arbitrary")),
        name="mla_attention",
    )(mask, qt, kcat, vt)


def _split_bf16(x):
    hi = x.astype(BF16)
    lo = (x - hi.astype(F32)).astype(BF16)
    return hi, lo


def _route(logits_t, bias_t):
    score = 1.0 / (1.0 + jnp.exp(-logits_t))
    sel = score + bias_t
    a = [sel[N_GROUPS * j:N_GROUPS * (j + 1)] for j in range(GROUP_SIZE)]
    sc = [score[N_GROUPS * j:N_GROUPS * (j + 1)] for j in range(GROUP_SIZE)]
    hi1, lo1 = jnp.maximum(a[0], a[1]), jnp.minimum(a[0], a[1])
    hi2, lo2 = jnp.maximum(a[2], a[3]), jnp.minimum(a[2], a[3])
    group_score = jnp.maximum(hi1, hi2) + jnp.maximum(jnp.minimum(hi1, hi2), jnp.maximum(lo1, lo2))
    gidx = lax.broadcasted_iota(jnp.int32, group_score.shape, 0)
    best = jnp.max(group_score, axis=0, keepdims=True)
    first_best = jnp.min(jnp.where(group_score == best, gidx, N_GROUPS), axis=0, keepdims=True)
    in_group = gidx == first_best
    picked = []
    for j in range(GROUP_SIZE):
        beaten_by = jnp.zeros(a[j].shape, jnp.int32)
        for i in range(GROUP_SIZE):
            if i == j:
                continue
            ahead = (a[i] > a[j]) | ((a[i] == a[j]) if i < j else False)
            beaten_by = beaten_by + ahead.astype(jnp.int32)
        picked.append(jnp.where(beaten_by < 2, sc[j], 0.0))
    denom = (picked[0] + picked[1]) + (picked[2] + picked[3])
    member_w = [jnp.sum(jnp.where(in_group, p / denom, 0.0), axis=0, keepdims=True) for p in picked]
    return in_group, member_w


def _out_norm_route_kernel(a_ref, wo_ref, x_ref, g_ref, b_ref, rw_ref, rb_ref, tri_ref,
                           x1_ref, gid_ref, rank_ref, cnt_ref, seen_ref):
    @pl.when(pl.program_id(0) == 0)
    def _():
        seen_ref[...] = jnp.zeros(seen_ref.shape, F32)

    y = _dot(a_ref[...], wo_ref[...])
    x1 = _ln(DEEPNORM_ALPHA * x_ref[...] + y, g_ref[...], b_ref[...])
    x1_ref[:, :D_MODEL] = x1
    xh, xl = _split_bf16(x1)
    wh, wl = _split_bf16(rw_ref[...])
    logits_t = _dot_nt(wh, xh) + (_dot_nt(wh, xl) + _dot_nt(wl, xh))
    in_group, member_w = _route(logits_t, rb_ref[...])
    tm = logits_t.shape[1]

    row = lax.broadcasted_iota(jnp.int32, (N_GROUPS, tm), 0)
    w_rows = jnp.zeros((N_GROUPS, tm), F32)
    for j in range(GROUP_SIZE):
        w_rows = jnp.where(row == j, member_w[j], w_rows)
    w_rows = jnp.concatenate([w_rows, jnp.zeros((LANES - N_GROUPS, tm), F32)], axis=0)
    x1_ref[:, D_MODEL:] = w_rows.T

    onehot = in_group.astype(F32)
    earlier = _dot(onehot.astype(BF16), tri_ref[...])
    seen = seen_ref[:, 0:1]
    gidx = lax.broadcasted_iota(jnp.int32, onehot.shape, 0)
    gid_ref[...] = jnp.sum(jnp.where(in_group, gidx, 0), axis=0, keepdims=True)
    rank_ref[...] = jnp.sum(onehot * (earlier + seen), axis=0, keepdims=True).astype(jnp.int32)
    seen_new = seen_ref[...] + jnp.sum(onehot, axis=1, keepdims=True)
    seen_ref[...] = seen_new
    cnt_ref[...] = seen_new.astype(jnp.int32)


def _out_norm_route(a, wo, x, g, b, rw_t, rb_t, tri, tm=ROUTE_TILE):
    t, d = x.shape
    return pl.pallas_call(
        _out_norm_route_kernel,
        out_shape=(jax.ShapeDtypeStruct((t, ROW_WIDE), F32),
                   jax.ShapeDtypeStruct((1, t), jnp.int32),
                   jax.ShapeDtypeStruct((1, t), jnp.int32),
                   jax.ShapeDtypeStruct((N_GROUPS, LANES), jnp.int32)),
        grid=(t // tm,),
        in_specs=[pl.BlockSpec((tm, d), lambda i: (i, 0)),
                  pl.BlockSpec((d, d), lambda i: (0, 0)),
                  pl.BlockSpec((tm, d), lambda i: (i, 0)),
                  pl.BlockSpec((1, d), lambda i: (0, 0)),
                  pl.BlockSpec((1, d), lambda i: (0, 0)),
                  pl.BlockSpec((N_EXPERTS, d), lambda i: (0, 0)),
                  pl.BlockSpec((N_EXPERTS, 1), lambda i: (0, 0)),
                  pl.BlockSpec((tm, tm), lambda i: (0, 0))],
        out_specs=(pl.BlockSpec((tm, ROW_WIDE), lambda i: (i, 0)),
                   pl.BlockSpec((1, tm), lambda i: (0, i)),
                   pl.BlockSpec((1, tm), lambda i: (0, i)),
                   pl.BlockSpec((N_GROUPS, LANES), lambda i: (0, 0))),
        scratch_shapes=[pltpu.VMEM((N_GROUPS, LANES), F32)],
        compiler_params=_cparams(("arbitrary",)),
        name="out_norm_route",
    )(a, wo, x, g, b, rw_t, rb_t, tri)


def _row_copy(src, dst, src_row, dst_row, sem):
    return pltpu.make_async_copy(src.at[pl.ds(src_row, 1)], dst.at[pl.ds(dst_row, 1)], sem)


def _copy_rows(src, dst, src_row_of, dst_row_of, sem):
    def start(r, carry):
        _row_copy(src, dst, src_row_of(r), dst_row_of(r), sem).start()
        return carry

    def wait(r, carry):
        _row_copy(src, dst, 0, 0, sem).wait()
        return carry

    lax.fori_loop(0, PERM_ROWS, start, 0, unroll=8)
    lax.fori_loop(0, PERM_ROWS, wait, 0, unroll=8)


def _scatter_rows_kernel(dest_ref, src_ref, base_hbm, out_hbm, sem):
    del base_hbm
    _copy_rows(src_ref, out_hbm, lambda r: r, lambda r: dest_ref[0, r], sem)


def _scatter_rows(src, dest, base):
    t, width = src.shape
    return pl.pallas_call(
        _scatter_rows_kernel,
        out_shape=jax.ShapeDtypeStruct(base.shape, base.dtype),
        grid=(t // PERM_ROWS,),
        in_specs=[pl.BlockSpec((None, 1, PERM_ROWS), lambda i: (i, 0, 0), memory_space=pltpu.SMEM),
                  pl.BlockSpec((PERM_ROWS, width), lambda i: (i, 0)),
                  pl.BlockSpec(memory_space=pl.ANY)],
        out_specs=pl.BlockSpec(memory_space=pl.ANY),
        scratch_shapes=[pltpu.SemaphoreType.DMA],
        input_output_aliases={2: 0},
        compiler_params=_cparams(("arbitrary",)),
        name="scatter_rows",
    )(dest.reshape(t // PERM_ROWS, 1, PERM_ROWS), src, base)


def _gather_rows_kernel(dest_ref, src_hbm, out_ref, sem):
    _copy_rows(src_hbm, out_ref, lambda r: dest_ref[0, r], lambda r: r, sem)


def _gather_rows(src, dest):
    t = dest.shape[0]
    width = src.shape[1]
    return pl.pallas_call(
        _gather_rows_kernel,
        out_shape=jax.ShapeDtypeStruct((t, width), src.dtype),
        grid=(t // PERM_ROWS,),
        in_specs=[pl.BlockSpec((None, 1, PERM_ROWS), lambda i: (i, 0, 0), memory_space=pltpu.SMEM),
                  pl.BlockSpec(memory_space=pl.ANY)],
        out_specs=pl.BlockSpec((PERM_ROWS, width), lambda i: (i, 0)),
        scratch_shapes=[pltpu.SemaphoreType.DMA],
        compiler_params=_cparams(("arbitrary",)),
        name="gather_rows",
    )(dest.reshape(t // PERM_ROWS, 1, PERM_ROWS), src)


def _moe_kernel(tile_group_ref, n_tiles_ref, xs_ref, wg_ref, wu_ref, wd_ref, g_ref, b_ref, o_ref):
    del tile_group_ref
    k = pl.program_id(0)

    @pl.when(k < n_tiles_ref[0])
    def _():
        x = xs_ref[:, :D_MODEL]
        cw = xs_ref[:, D_MODEL:]
        xb = x.astype(BF16)
        y = jnp.zeros(x.shape, F32)
        for j in range(GROUP_SIZE):
            hg = _dot(xb, wg_ref[j])
            hu = _dot(xb, wu_ref[j])
            hid = (hg / (1.0 + jnp.exp(-hg))) * hu * cw[:, j:j + 1]
            y = y + _dot(hid.astype(BF16), wd_ref[j])
        o_ref[...] = _ln(DEEPNORM_ALPHA * x + y, g_ref[...], b_ref[...])

    @pl.when(k >= n_tiles_ref[0])
    def _():
        o_ref[...] = jnp.zeros(o_ref.shape, F32)


def _moe(xs, tile_group, n_tiles, wg, wu, wd, g, b, layer):
    tp = xs.shape[0]
    d, gs, ff = D_MODEL, GROUP_SIZE, EXPERT_FF
    grid_spec = pltpu.PrefetchScalarGridSpec(
        num_scalar_prefetch=2,
        grid=(tp // MOE_TILE,),
        in_specs=[pl.BlockSpec((MOE_TILE, ROW_WIDE), lambda k, tg, nt: (k, 0)),
                  pl.BlockSpec((None, gs, d, ff), lambda k, tg, nt: (layer, tg[k], 0, 0)),
                  pl.BlockSpec((None, gs, d, ff), lambda k, tg, nt: (layer, tg[k], 0, 0)),
                  pl.BlockSpec((None, gs, ff, d), lambda k, tg, nt: (layer, tg[k], 0, 0)),
                  pl.BlockSpec((1, d), lambda k, tg, nt: (0, 0)),
                  pl.BlockSpec((1, d), lambda k, tg, nt: (0, 0))],
        out_specs=pl.BlockSpec((MOE_TILE, d), lambda k, tg, nt: (k, 0)))
    return pl.pallas_call(
        _moe_kernel,
        out_shape=jax.ShapeDtypeStruct((tp, d), F32),
        grid_spec=grid_spec,
        compiler_params=_cparams(("arbitrary",)),
        name="grouped_moe",
    )(tile_group, n_tiles, xs, wg, wu, wd, g, b)


def _sort_plan(gid, rank, counts, n_tiles_max):
    cnt = counts[:, 0]
    padded = ((cnt + MOE_TILE - 1) // MOE_TILE) * MOE_TILE
    ends = jnp.cumsum(padded)
    starts = ends - padded
    dest = starts[gid[0]] + rank[0]
    n_tiles = ends[-1] // MOE_TILE
    tile_row = jnp.arange(n_tiles_max, dtype=jnp.int32) * MOE_TILE
    tile_group = jnp.sum((tile_row[:, None] >= ends[None, :]).astype(jnp.int32), axis=1)
    last_group = jnp.sum((ends[-1] - 1 >= ends).astype(jnp.int32))
    tile_group = jnp.minimum(tile_group, last_group)
    return dest.astype(jnp.int32), tile_group.astype(jnp.int32), n_tiles.reshape(1).astype(jnp.int32)


def _tile_plan(counts, n_tiles_max):
    cnt = counts[:, 0]
    padded = ((cnt + MOE_TILE - 1) // MOE_TILE) * MOE_TILE
    ends = jnp.cumsum(padded)
    starts = ends - padded
    n_tiles = ends[-1] // MOE_TILE
    tile_row = jnp.arange(n_tiles_max, dtype=jnp.int32) * MOE_TILE
    tile_group = jnp.sum((tile_row[:, None] >= ends[None, :]).astype(jnp.int32), axis=1)
    last_group = jnp.sum((ends[-1] - 1 >= ends).astype(jnp.int32))
    tile_group = jnp.minimum(tile_group, last_group)
    rows_left = (starts + cnt)[tile_group] - tile_row
    tile_rows = jnp.clip(rows_left, 0, MOE_TILE)
    return (starts.astype(jnp.int32), tile_group.astype(jnp.int32), tile_rows.astype(jnp.int32),
            n_tiles.reshape(1).astype(jnp.int32))


def _invert_plan_kernel(starts_ref, gid_ref, rank_ref, src_ref):
    step = pl.program_id(0)

    @pl.when(step == 0)
    def _():
        def clear(p, carry):
            src_ref[p] = 0
            return carry
        lax.fori_loop(0, src_ref.shape[0], clear, 0, unroll=8)

    def place(r, carry):
        src_ref[starts_ref[gid_ref[0, r]] + rank_ref[0, r]] = step * PLAN_CHUNK + r
        return carry

    lax.fori_loop(0, PLAN_CHUNK, place, 0, unroll=8)


def _invert_plan(starts, gid, rank, n_rows):
    t = gid.shape[1]
    chunks = t // PLAN_CHUNK
    chunk_spec = pl.BlockSpec((None, 1, PLAN_CHUNK), lambda i: (i, 0, 0), memory_space=pltpu.SMEM)
    return pl.pallas_call(
        _invert_plan_kernel,
        out_shape=jax.ShapeDtypeStruct((n_rows,), jnp.int32),
        grid=(chunks,),
        in_specs=[pl.BlockSpec(memory_space=pltpu.SMEM), chunk_spec, chunk_spec],
        out_specs=pl.BlockSpec((n_rows,), lambda i: (0,), memory_space=pltpu.SMEM),
        compiler_params=_cparams(("arbitrary",)),
        name="invert_plan",
    )(starts, gid.reshape(chunks, 1, PLAN_CHUNK), rank.reshape(chunks, 1, PLAN_CHUNK))


def _for_rows(n_rows, fn):
    def chunk(q, carry):
        for u in range(8):
            fn(q * 8 + u)
        return carry

    def single(r, carry):
        fn(r)
        return carry

    full = n_rows // 8
    lax.fori_loop(0, full, chunk, 0)
    lax.fori_loop(full * 8, n_rows, single, 0)


def _moe_rows_kernel(tile_group_ref, tile_rows_ref, n_tiles_ref, src_ref, src_next_ref, x_hbm,
                     wg_ref, wu_ref, wd_ref, g_ref, b_ref, out_hbm, xbuf, obuf, gsem, ssem):
    del tile_group_ref
    k = pl.program_id(0)
    n_tiles = n_tiles_ref[0]
    slot = k % 2

    def row_copies(tile, one_start, one_wait, wait_all, start):
        n_rows = tile_rows_ref[tile]

        @pl.when(n_rows == MOE_TILE)
        def _():
            if start:
                for r in range(MOE_TILE):
                    one_start(r)
            else:
                wait_all()

        @pl.when(n_rows != MOE_TILE)
        def _():
            _for_rows(n_rows, one_start if start else one_wait)

    def gather(tile, rows_ref, buf_slot, start):
        dst = xbuf.at[buf_slot]
        sem = gsem.at[buf_slot]
        row_copies(tile,
                   lambda r: _row_copy(x_hbm, dst, rows_ref[0, r], r, sem).start(),
                   lambda r: _row_copy(x_hbm, dst, 0, 0, sem).wait(),
                   lambda: pltpu.make_async_copy(x_hbm.at[pl.ds(0, MOE_TILE)], dst, sem).wait(),
                   start)

    def scatter(tile, rows_ref, buf_slot, start):
        src = obuf.at[buf_slot]
        sem = ssem.at[buf_slot]
        row_copies(tile,
                   lambda r: _row_copy(src, out_hbm, r, rows_ref[0, r], sem).start(),
                   lambda r: _row_copy(src, out_hbm, 0, 0, sem).wait(),
                   lambda: pltpu.make_async_copy(src, out_hbm.at[pl.ds(0, MOE_TILE)], sem).wait(),
                   start)

    @pl.when(k == 0)
    def _():
        xbuf[...] = jnp.zeros(xbuf.shape, F32)
        gather(0, src_ref, 0, True)

    @pl.when(k + 1 < n_tiles)
    def _():
        gather(k + 1, src_next_ref, 1 - slot, True)

    @pl.when(k < n_tiles)
    def _():
        gather(k, src_ref, slot, False)

        @pl.when(k >= 2)
        def _():
            scatter(k - 2, src_ref, slot, False)

        x = xbuf[slot, :, :D_MODEL]
        cw = xbuf[slot, :, D_MODEL:]
        xb = x.astype(BF16)
        y = jnp.zeros(x.shape, F32)
        for j in range(GROUP_SIZE):
            hg = _dot(xb, wg_ref[j])
            hu = _dot(xb, wu_ref[j])
            hid = (hg / (1.0 + jnp.exp(-hg))) * hu * cw[:, j:j + 1]
            y = y + _dot(hid.astype(BF16), wd_ref[j])
        obuf[slot] = _ln(DEEPNORM_ALPHA * x + y, g_ref[...], b_ref[...])
        scatter(k, src_ref, slot, True)

    @pl.when(k == n_tiles - 1)
    def _():
        @pl.when(k >= 1)
        def _():
            scatter(k - 1, src_ref, 1 - slot, False)
        scatter(k, src_ref, slot, False)


def _moe_rows(x_rows, src, tile_group, tile_rows, n_tiles, wg, wu, wd, g, b, layer):
    t = x_rows.shape[0]
    n_tiles_max = src.shape[0] // MOE_TILE
    d, gs, ff = D_MODEL, GROUP_SIZE, EXPERT_FF
    src3 = src.reshape(n_tiles_max, 1, MOE_TILE)
    grid_spec = pltpu.PrefetchScalarGridSpec(
        num_scalar_prefetch=3,
        grid=(n_tiles_max,),
        in_specs=[pl.BlockSpec((None, 1, MOE_TILE), lambda k, tg, tr, nt: (k, 0, 0), memory_space=pltpu.SMEM),
                  pl.BlockSpec((None, 1, MOE_TILE), lambda k, tg, tr, nt: (jnp.minimum(k + 1, n_tiles_max - 1), 0, 0),
                               memory_space=pltpu.SMEM),
                  pl.BlockSpec(memory_space=pl.ANY),
                  pl.BlockSpec((None, gs, d, ff), lambda k, tg, tr, nt: (layer, tg[k], 0, 0)),
                  pl.BlockSpec((None, gs, d, ff), lambda k, tg, tr, nt: (layer, tg[k], 0, 0)),
                  pl.BlockSpec((None, gs, ff, d), lambda k, tg, tr, nt: (layer, tg[k], 0, 0)),
                  pl.BlockSpec((1, d), lambda k, tg, tr, nt: (0, 0)),
                  pl.BlockSpec((1, d), lambda k, tg, tr, nt: (0, 0))],
        out_specs=pl.BlockSpec(memory_space=pl.ANY),
        scratch_shapes=[pltpu.VMEM((2, MOE_TILE, ROW_WIDE), F32), pltpu.VMEM((2, MOE_TILE, d), F32),
                        pltpu.SemaphoreType.DMA((2,)), pltpu.SemaphoreType.DMA((2,))])
    return pl.pallas_call(
        _moe_rows_kernel,
        out_shape=jax.ShapeDtypeStruct((t, d), F32),
        grid_spec=grid_spec,
        compiler_params=_cparams(("arbitrary",)),
        name="grouped_moe",
    )(tile_group, tile_rows, n_tiles, src3, src3, x_rows, wg, wu, wd, g, b)


def _rope_tables(seq):
    inv_freq = ROPE_BASE ** (-jnp.arange(0, MLA_ROPE_DIM, 2, dtype=F32) / MLA_ROPE_DIM)
    ang = jnp.arange(seq, dtype=F32)[:, None] * inv_freq[None, :]
    cos, sin = jnp.cos(ang), jnp.sin(ang)
    cos2 = jnp.concatenate([cos, cos], axis=1)
    sin2 = jnp.concatenate([-sin, sin], axis=1)
    zero = jnp.zeros((seq, LANES - MLA_ROPE_DIM), F32)
    return (jnp.concatenate([cos2, zero], axis=1), jnp.concatenate([sin2, zero], axis=1), cos2.T, sin2.T)


def _swap_halves(w):
    half = MLA_ROPE_DIM // 2
    return jnp.concatenate([w[..., half:], w[..., :half]], axis=-1)


def kernel(x, diff_w_qkv, diff_lambda, diff_subln_g, diff_w_o, mla_w_kv_a, mla_kv_norm_g, mla_w_kv_b,
           mla_w_q_a, mla_q_norm_g, mla_w_q_b, mla_w_o, router_w, router_b, moe_w_gate, moe_w_up,
           moe_w_down, ln_g, ln_b):
    batch, seq, d = x.shape
    t = batch * seq
    xs = x.reshape(t, d)

    wq, wk, wv = jnp.split(diff_w_qkv, 3, axis=-1)
    w_k = wk.astype(BF16)
    w_qv_t = jnp.concatenate([wq * (LOG2E * DIFF_HEAD_DIM ** -0.5), wv], axis=-1).transpose(0, 2, 1).astype(BF16)
    w_o_a = diff_w_o.astype(BF16)
    w_o_b = mla_w_o.astype(BF16)
    wg = moe_w_gate.astype(BF16)
    wu = moe_w_up.astype(BF16)
    wd = moe_w_down.astype(BF16)
    slopes = LOG2E * 2.0 ** (-8.0 * jnp.arange(1, DIFF_HEADS + 1, dtype=F32) / DIFF_HEADS)
    slope_hi = slopes.astype(BF16).astype(F32)
    slope_lo = (slopes - slope_hi).astype(BF16).astype(F32)
    rw_t = router_w.T.reshape(N_GROUPS, GROUP_SIZE, d).transpose(1, 0, 2).reshape(N_EXPERTS, d)
    rb_t = router_b.reshape(N_GROUPS, GROUP_SIZE).T.reshape(N_EXPERTS, 1).astype(F32)
    cs, sn, cs_t, sn_t = _rope_tables(seq)
    kv_rope_w = mla_w_kv_a[:, MLA_KV_RANK:]
    w_kv_a = jnp.concatenate([mla_w_kv_a, _swap_halves(kv_rope_w)], axis=1).astype(BF16)
    w_kv_b = mla_w_kv_b.reshape(MLA_KV_RANK, MLA_HEADS, MLA_NOPE_DIM + MLA_V_DIM)
    w_kn = w_kv_b[..., :MLA_NOPE_DIM].reshape(MLA_KV_RANK, -1).astype(BF16)
    w_v_t = w_kv_b[..., MLA_NOPE_DIM:].reshape(MLA_KV_RANK, -1).T.astype(BF16)
    wqb = mla_w_q_b.reshape(-1, MLA_Q_RANK, MLA_HEADS, MLA_NOPE_DIM + MLA_ROPE_DIM)
    w_q_b_t = jnp.concatenate([wqb, _swap_halves(wqb[..., MLA_NOPE_DIM:])], axis=-1).reshape(
        -1, MLA_Q_RANK, MLA_HEADS * MLA_QK_PAD).transpose(0, 2, 1).astype(BF16)
    w_q_a = mla_w_q_a.astype(BF16)

    alibi_fix, diag_mask = _diagonal_tables()
    tri = jnp.triu(jnp.ones((ROUTE_TILE, ROUTE_TILE), BF16), 1)
    n_tiles_max = t // MOE_TILE + N_GROUPS

    kcat = vt_shared = None
    for layer in range(DEPTH):
        if layer < N_A_LAYERS:
            k, qt, vt = _qkv_proj(xs, w_k[layer], w_qv_t[layer])
            attn = _diff_attention2(qt, k, vt, slope_hi, slope_lo, diff_lambda[layer],
                                    diff_subln_g[layer].reshape(-1, 1), alibi_fix, batch, seq, _lambda_init(layer))
            w_o = w_o_a[layer]
        else:
            j = layer - N_A_LAYERS
            if j == 0:
                kcat, vt_shared = _mla_kv_proj(xs, w_kv_a, mla_kv_norm_g.reshape(1, -1), w_kn, w_v_t, cs, sn, seq)
            qt = _mla_q_proj(xs, w_q_a[j], mla_q_norm_g[j].reshape(1, -1), w_q_b_t[j], cs_t, sn_t, seq)
            attn = _mla_attention2(qt, kcat, vt_shared, diag_mask, batch, seq)
            w_o = w_o_b[j]
        x1, gid, rank, counts = _out_norm_route(attn, w_o, xs, ln_g[layer, 0].reshape(1, -1),
                                                ln_b[layer, 0].reshape(1, -1), rw_t, rb_t, tri)
        starts, tile_group, tile_rows, n_tiles = _tile_plan(counts, n_tiles_max)
        src = _invert_plan(starts, gid, rank, n_tiles_max * MOE_TILE)
        xs = _moe_rows(x1, src, tile_group, tile_rows, n_tiles, wg, wu, wd, ln_g[layer, 1].reshape(1, -1),
                       ln_b[layer, 1].reshape(1, -1), layer)
    return xs.reshape(batch, seq, d)
```

```python
import functools
import math

import jax
import jax.numpy as jnp
from jax import lax
from jax.experimental import pallas as pl
from jax.experimental.pallas import tpu as pltpu

D_MODEL = 1024
DEPTH = 4
CHUNK = 64
N_A_LAYERS = DEPTH // 2
DIFF_HEAD_DIM = 64
DIFF_V_DIM = 2 * DIFF_HEAD_DIM
DIFF_HEADS = D_MODEL // DIFF_V_DIM
MLA_NOPE_DIM = 128
MLA_ROPE_DIM = 64
MLA_V_DIM = 128
MLA_HEADS = D_MODEL // MLA_V_DIM
MLA_Q_RANK = 384
MLA_KV_RANK = 256
MLA_QK_PAD = 256
ROPE_BASE = 10000.0
N_EXPERTS = 32
N_GROUPS = 8
GROUP_SIZE = N_EXPERTS // N_GROUPS
EXPERT_FF = 256
NORM_EPS = 1e-5
DEEPNORM_ALPHA = (2 * DEPTH) ** 0.25
LOG2E = math.log2(math.e)

LANES = 128
T_TILE = 256
GROUP_TILES = 2
Q_TILE = GROUP_TILES * T_TILE
MASKED = 1e30
V_ROWS = DIFF_V_DIM + 16
ROUTE_TILE = 256
MOE_TILE = 256
PLAN_CHUNK = 2048
ROW_WIDE = D_MODEL + LANES
VMEM_LIMIT = 48 * 1024 * 1024

F32 = jnp.float32
BF16 = jnp.bfloat16


def _lambda_init(layer):
    return 0.8 - 0.6 * math.exp(-0.3 * layer)


def _cparams(sem):
    return pltpu.CompilerParams(dimension_semantics=sem, vmem_limit_bytes=VMEM_LIMIT)


def _dot(a, b):
    return jnp.dot(a, b, preferred_element_type=F32)


def _dot_nt(a, b):
    return lax.dot_general(a, b, (((1,), (1,)), ((), ())), preferred_element_type=F32)


def _rms(x, g):
    return x * lax.rsqrt(jnp.mean(x * x, axis=-1, keepdims=True) + NORM_EPS) * g


def _ln(x, g, b):
    mu = jnp.mean(x, axis=-1, keepdims=True)
    xc = x - mu
    var = jnp.mean(xc * xc, axis=-1, keepdims=True)
    return xc * lax.rsqrt(var + NORM_EPS) * g + b


def _rope_halves(a, cs, sn):
    return a * cs + pltpu.roll(a, MLA_ROPE_DIM, axis=1) * sn


def _qkv_kernel(x_ref, wk_ref, wqv_ref, k_ref, qt_ref, vt_ref):
    xb = x_ref[...].astype(BF16)
    k_ref[...] = _dot(xb, wk_ref[...]).astype(BF16)
    qv = _dot_nt(wqv_ref[...], xb)
    for h in range(DIFF_HEADS):
        qt_ref[0, h] = qv[h * LANES:(h + 1) * LANES].astype(BF16)
        vt_ref[0, h, :DIFF_V_DIM] = qv[D_MODEL + h * LANES:D_MODEL + (h + 1) * LANES].astype(BF16)
        vt_ref[0, h, DIFF_V_DIM:] = jnp.ones((V_ROWS - DIFF_V_DIM, T_TILE), BF16)


def _qkv_proj(x, wk, wqv):
    t, d = x.shape
    nt = t // T_TILE

    def head_major(rows):
        return jax.ShapeDtypeStruct((nt, DIFF_HEADS, rows, T_TILE), BF16)

    def head_block(rows):
        return pl.BlockSpec((1, DIFF_HEADS, rows, T_TILE), lambda i: (i, 0, 0, 0))

    return pl.pallas_call(
        _qkv_kernel,
        out_shape=(jax.ShapeDtypeStruct((t, d), BF16), head_major(LANES), head_major(V_ROWS)),
        grid=(nt,),
        in_specs=[pl.BlockSpec((T_TILE, d), lambda i: (i, 0)),
                  pl.BlockSpec(wk.shape, lambda i: (0, 0)),
                  pl.BlockSpec(wqv.shape, lambda i: (0, 0))],
        out_specs=(pl.BlockSpec((T_TILE, d), lambda i: (i, 0)), head_block(LANES), head_block(V_ROWS)),
        compiler_params=_cparams(("parallel",)),
        name="qkv_proj",
    )(x, wk, wqv)


def _mla_kv_kernel(x_ref, wa_ref, g_ref, wk_ref, wv_ref, cs_ref, sn_ref, kcat_ref, vt_ref):
    kva = _dot(x_ref[...].astype(BF16), wa_ref[...])
    c_kv = _rms(kva[:, :MLA_KV_RANK], g_ref[...]).astype(BF16)
    kr = _rope_halves(kva[:, MLA_KV_RANK:], cs_ref[...], sn_ref[...]).astype(BF16)
    kn = _dot(c_kv, wk_ref[...])
    vt = _dot_nt(wv_ref[...], c_kv)
    for h in range(MLA_HEADS):
        base = h * MLA_QK_PAD
        kcat_ref[:, base:base + MLA_NOPE_DIM] = kn[:, h * MLA_NOPE_DIM:(h + 1) * MLA_NOPE_DIM].astype(BF16)
        kcat_ref[:, base + MLA_NOPE_DIM:base + MLA_QK_PAD] = kr
        vt_ref[0, h, :MLA_V_DIM] = vt[h * MLA_V_DIM:(h + 1) * MLA_V_DIM].astype(BF16)
        vt_ref[0, h, MLA_V_DIM:] = jnp.ones((V_ROWS - MLA_V_DIM, T_TILE), BF16)


def _mla_kv_proj(x, wa, g, wk, wv, cs, sn, seq):
    t, d = x.shape
    nt = t // T_TILE
    ns = seq // T_TILE
    return pl.pallas_call(
        _mla_kv_kernel,
        out_shape=(jax.ShapeDtypeStruct((t, MLA_HEADS * MLA_QK_PAD), BF16),
                   jax.ShapeDtypeStruct((nt, MLA_HEADS, V_ROWS, T_TILE), BF16)),
        grid=(nt,),
        in_specs=[pl.BlockSpec((T_TILE, d), lambda i: (i, 0)),
                  pl.BlockSpec(wa.shape, lambda i: (0, 0)),
                  pl.BlockSpec(g.shape, lambda i: (0, 0)),
                  pl.BlockSpec(wk.shape, lambda i: (0, 0)),
                  pl.BlockSpec(wv.shape, lambda i: (0, 0)),
                  pl.BlockSpec((T_TILE, LANES), lambda i: (i % ns, 0)),
                  pl.BlockSpec((T_TILE, LANES), lambda i: (i % ns, 0))],
        out_specs=(pl.BlockSpec((T_TILE, MLA_HEADS * MLA_QK_PAD), lambda i: (i, 0)),
                   pl.BlockSpec((1, MLA_HEADS, V_ROWS, T_TILE), lambda i: (i, 0, 0, 0))),
        compiler_params=_cparams(("parallel",)),
        name="mla_kv_proj",
    )(x, wa, g, wk, wv, cs, sn)


def _mla_q_kernel(x_ref, wa_ref, g_ref, wb_ref, cst_ref, snt_ref, qt_ref, *, scale):
    qa = _dot(x_ref[...].astype(BF16), wa_ref[...])
    qn = _rms(qa, g_ref[...]).astype(BF16)
    qt = _dot_nt(wb_ref[...], qn) * scale
    cst = cst_ref[...]
    snt = snt_ref[...]
    zero = jnp.zeros((MLA_ROPE_DIM, T_TILE), BF16)
    for h in range(MLA_HEADS):
        base = h * MLA_QK_PAD
        r0 = base + MLA_NOPE_DIM
        rope = qt[r0:r0 + MLA_ROPE_DIM] * cst + qt[r0 + MLA_ROPE_DIM:base + MLA_QK_PAD] * snt
        qt_ref[0, h, 0:MLA_NOPE_DIM] = qt[base:r0].astype(BF16)
        qt_ref[0, h, MLA_NOPE_DIM:MLA_NOPE_DIM + MLA_ROPE_DIM] = rope.astype(BF16)
        qt_ref[0, h, MLA_NOPE_DIM + MLA_ROPE_DIM:MLA_QK_PAD] = zero


def _mla_q_proj(x, wa, g, wb_t, cst, snt, seq):
    t, d = x.shape
    nt = t // T_TILE
    ns = seq // T_TILE
    scale = LOG2E * (MLA_NOPE_DIM + MLA_ROPE_DIM) ** -0.5
    return pl.pallas_call(
        functools.partial(_mla_q_kernel, scale=scale),
        out_shape=jax.ShapeDtypeStruct((nt, MLA_HEADS, MLA_QK_PAD, T_TILE), BF16),
        grid=(nt,),
        in_specs=[pl.BlockSpec((T_TILE, d), lambda i: (i, 0)),
                  pl.BlockSpec(wa.shape, lambda i: (0, 0)),
                  pl.BlockSpec(g.shape, lambda i: (0, 0)),
                  pl.BlockSpec(wb_t.shape, lambda i: (0, 0)),
                  pl.BlockSpec((MLA_ROPE_DIM, T_TILE), lambda i: (0, i % ns)),
                  pl.BlockSpec((MLA_ROPE_DIM, T_TILE), lambda i: (0, i % ns))],
        out_specs=pl.BlockSpec((1, MLA_HEADS, MLA_QK_PAD, T_TILE), lambda i: (i, 0, 0, 0)),
        compiler_params=_cparams(("parallel",)),
        name="mla_q_proj",
    )(x, wa, g, wb_t, cst, snt)


def _consume_group(tiles, m_ref, acc_ref):
    for c, comp_tiles in enumerate(tiles):
        m_group = None
        for s_t, shift, _ in comp_tiles:
            m_t = jnp.max(s_t, axis=0, keepdims=True) + shift
            m_group = m_t if m_group is None else jnp.maximum(m_group, m_t)
        pv = None
        for s_t, shift, vt in comp_tiles:
            part = _dot(vt, jnp.exp2(s_t - (m_group - shift)).astype(BF16))
            pv = part if pv is None else pv + part
        m_prev = m_ref[c]
        m_new = jnp.maximum(m_prev, m_group)
        acc_ref[c] = jnp.exp2(m_prev - m_new) * acc_ref[c] + jnp.exp2(m_group - m_new) * pv
        m_ref[c] = m_new


def _init_online_softmax(m_ref, acc_ref):
    m_ref[...] = jnp.full(m_ref.shape, -jnp.inf, F32)
    acc_ref[...] = jnp.zeros(acc_ref.shape, F32)


def _normalised(acc):
    return acc[:DIFF_V_DIM] / acc[DIFF_V_DIM:DIFF_V_DIM + 1]


def _attention_pipeline(n_prev_groups, issue_scores, consume, buf_a, buf_b):
    issue_scores(0, buf_a)

    def pair(p, carry):
        issue_scores(2 * p + 1, buf_b)
        consume(2 * p, buf_a, False)
        issue_scores(2 * p + 2, buf_a)
        consume(2 * p + 1, buf_b, False)
        return carry

    lax.fori_loop(0, n_prev_groups // 2, pair, 0)
    odd = n_prev_groups % 2

    @pl.when(odd == 1)
    def _():
        issue_scores(n_prev_groups, buf_b)
        consume(n_prev_groups - 1, buf_a, False)
        consume(n_prev_groups, buf_b, True)

    @pl.when(odd == 0)
    def _():
        consume(n_prev_groups, buf_a, True)


def _diff_attn_kernel(slh_ref, sll_ref, lam_ref, g_ref, dm_ref, qt_ref, k_ref, vt_ref, o_ref,
                      buf_a, buf_b, m_ref, acc_ref, *, lambda_init):
    h = pl.program_id(1)
    i = pl.program_id(2)
    sl_hi = slh_ref[h]
    sl_lo = sll_ref[h]
    slope = sl_hi + sl_lo
    row = lax.broadcasted_iota(jnp.int32, (LANES, Q_TILE), 0)
    qf = jnp.concatenate([qt_ref[u] for u in range(GROUP_TILES)], axis=1).astype(F32)
    slope_rows = jnp.where(row == 0, sl_hi, jnp.where(row == 1, sl_lo, 0.0)).astype(BF16)
    qs_aug = [jnp.concatenate([jnp.where(keep, qf, 0.0).astype(BF16), slope_rows], axis=0)
              for keep in (row < DIFF_HEAD_DIM, row >= DIFF_HEAD_DIM)]
    key_off = lax.broadcasted_iota(jnp.int32, (T_TILE, LANES), 0).astype(F32)
    lane = lax.broadcasted_iota(jnp.int32, (T_TILE, LANES), 1)
    key_cols = jnp.where(lane < 2, key_off, 0.0).astype(BF16)
    _init_online_softmax(m_ref, acc_ref)

    def issue_scores(g, buf):
        for u in range(GROUP_TILES):
            off = pl.multiple_of((g * GROUP_TILES + u) * T_TILE, T_TILE)
            k_aug = jnp.concatenate([k_ref[pl.ds(off, T_TILE), :], key_cols], axis=1)
            for c in range(2):
                buf[2 * u + c] = _dot(k_aug, qs_aug[c])

    def consume(g, buf, diagonal):
        tiles = ([], [])
        for u in range(GROUP_TILES):
            j = g * GROUP_TILES + u
            shift = slope * (j * T_TILE - i * Q_TILE).astype(F32)
            for c in range(2):
                s_t = buf[2 * u + c]
                if diagonal:
                    s_t = s_t + slope * dm_ref[u * T_TILE:(u + 1) * T_TILE, :]
                tiles[c].append((s_t, shift, vt_ref[j]))
        _consume_group(tiles, m_ref, acc_ref)

    _attention_pipeline(i, issue_scores, consume, buf_a, buf_b)

    lam = lam_ref[...]
    lam_full = (jnp.exp(jnp.sum(lam[0:1] * lam[1:2], axis=1, keepdims=True))
                - jnp.exp(jnp.sum(lam[2:3] * lam[3:4], axis=1, keepdims=True)) + lambda_init)
    o_t = _normalised(acc_ref[0]) - lam_full * _normalised(acc_ref[1])
    inv = lax.rsqrt(jnp.mean(o_t * o_t, axis=0, keepdims=True) + NORM_EPS)
    o_t = o_t * inv * (g_ref[...] * (1.0 - lambda_init))
    o_ref[...] = o_t.T.astype(o_ref.dtype)


def _diagonal_tables():
    rk = jnp.arange(Q_TILE, dtype=jnp.int32)[:, None]
    rq = jnp.arange(Q_TILE, dtype=jnp.int32)[None, :]
    visible = (rk // CHUNK) <= (rq // CHUNK)
    alibi_fix = jnp.where(visible, -2.0 * jnp.maximum(rk - rq, 0).astype(F32), -MASKED)
    mask = jnp.where(visible, 0.0, -MASKED).astype(F32)
    return alibi_fix, mask


def _diff_attention(qt, k, vt, slope_hi, slope_lo, lam, g_col, alibi_fix, batch, seq, lambda_init):
    nq = seq // Q_TILE
    nk = seq // T_TILE
    hh = DIFF_HEADS
    return pl.pallas_call(
        functools.partial(_diff_attn_kernel, lambda_init=lambda_init),
        out_shape=jax.ShapeDtypeStruct((batch * seq, hh * DIFF_V_DIM), BF16),
        grid=(batch, hh, nq),
        in_specs=[pl.BlockSpec(memory_space=pltpu.SMEM),
                  pl.BlockSpec(memory_space=pltpu.SMEM),
                  pl.BlockSpec(lam.shape, lambda b, h, i: (0, 0)),
                  pl.BlockSpec(g_col.shape, lambda b, h, i: (0, 0)),
                  pl.BlockSpec(alibi_fix.shape, lambda b, h, i: (0, 0)),
                  pl.BlockSpec((GROUP_TILES, None, LANES, T_TILE), lambda b, h, i: (b * nq + i, h, 0, 0)),
                  pl.BlockSpec((seq, LANES), lambda b, h, i: (b, h)),
                  pl.BlockSpec((nk, None, V_ROWS, T_TILE), lambda b, h, i: (b, h, 0, 0))],
        out_specs=pl.BlockSpec((Q_TILE, LANES), lambda b, h, i: (b * nq + i, h)),
        scratch_shapes=[pltpu.VMEM((2 * GROUP_TILES, T_TILE, Q_TILE), F32),
                        pltpu.VMEM((2 * GROUP_TILES, T_TILE, Q_TILE), F32),
                        pltpu.VMEM((2, 1, Q_TILE), F32),
                        pltpu.VMEM((2, V_ROWS, Q_TILE), F32)],
        compiler_params=_cparams(("parallel", "parallel", "arbitrary")),
        name="diff_attention",
    )(slope_hi, slope_lo, lam, g_col, alibi_fix, qt, k, vt)


def _mla_attn_kernel(mask_ref, qt_ref, k_ref, vt_ref, o_ref, buf_a, buf_b, m_ref, acc_ref):
    i = pl.program_id(2)
    qt = jnp.concatenate([qt_ref[u] for u in range(GROUP_TILES)], axis=1)
    _init_online_softmax(m_ref, acc_ref)

    def issue_scores(g, buf):
        for u in range(GROUP_TILES):
            off = pl.multiple_of((g * GROUP_TILES + u) * T_TILE, T_TILE)
            buf[u] = _dot(k_ref[pl.ds(off, T_TILE), :], qt)

    def consume(g, buf, diagonal):
        tiles = []
        for u in range(GROUP_TILES):
            s_t = buf[u]
            if diagonal:
                s_t = s_t + mask_ref[u * T_TILE:(u + 1) * T_TILE, :]
            tiles.append((s_t, 0.0, vt_ref[g * GROUP_TILES + u]))
        _consume_group([tiles], m_ref, acc_ref)

    _attention_pipeline(i, issue_scores, consume, buf_a, buf_b)
    o_ref[...] = _normalised(acc_ref[0]).T.astype(o_ref.dtype)


def _mla_attention(qt, kcat, vt, mask, batch, seq):
    nq = seq // Q_TILE
    nk = seq // T_TILE
    return pl.pallas_call(
        _mla_attn_kernel,
        out_shape=jax.ShapeDtypeStruct((batch * seq, MLA_HEADS * MLA_V_DIM), BF16),
        grid=(batch, MLA_HEADS, nq),
        in_specs=[pl.BlockSpec(mask.shape, lambda b, h, i: (0, 0)),
                  pl.BlockSpec((GROUP_TILES, None, MLA_QK_PAD, T_TILE), lambda b, h, i: (b * nq + i, h, 0, 0)),
                  pl.BlockSpec((seq, MLA_QK_PAD), lambda b, h, i: (b, h)),
                  pl.BlockSpec((nk, None, V_ROWS, T_TILE), lambda b, h, i: (b, h, 0, 0))],
        out_specs=pl.BlockSpec((Q_TILE, MLA_V_DIM), lambda b, h, i: (b * nq + i, h)),
        scratch_shapes=[pltpu.VMEM((GROUP_TILES, T_TILE, Q_TILE), F32),
                        pltpu.VMEM((GROUP_TILES, T_TILE, Q_TILE), F32),
                        pltpu.VMEM((1, 1, Q_TILE), F32),
                        pltpu.VMEM((1, V_ROWS, Q_TILE), F32)],
        compiler_params=_cparams(("parallel", "parallel", "arbitrary")),
        name="mla_attention",
    )(mask, qt, kcat, vt)


def _split_bf16(x):
    hi = x.astype(BF16)
    lo = (x - hi.astype(F32)).astype(BF16)
    return hi, lo


def _route(logits_t, bias_t):
    score = 1.0 / (1.0 + jnp.exp(-logits_t))
    sel = score + bias_t
    a = [sel[N_GROUPS * j:N_GROUPS * (j + 1)] for j in range(GROUP_SIZE)]
    sc = [score[N_GROUPS * j:N_GROUPS * (j + 1)] for j in range(GROUP_SIZE)]
    hi1, lo1 = jnp.maximum(a[0], a[1]), jnp.minimum(a[0], a[1])
    hi2, lo2 = jnp.maximum(a[2], a[3]), jnp.minimum(a[2], a[3])
    group_score = jnp.maximum(hi1, hi2) + jnp.maximum(jnp.minimum(hi1, hi2), jnp.maximum(lo1, lo2))
    gidx = lax.broadcasted_iota(jnp.int32, group_score.shape, 0)
    best = jnp.max(group_score, axis=0, keepdims=True)
    first_best = jnp.min(jnp.where(group_score == best, gidx, N_GROUPS), axis=0, keepdims=True)
    in_group = gidx == first_best
    picked = []
    for j in range(GROUP_SIZE):
        beaten_by = jnp.zeros(a[j].shape, jnp.int32)
        for i in range(GROUP_SIZE):
            if i == j:
                continue
            ahead = (a[i] > a[j]) | ((a[i] == a[j]) if i < j else False)
            beaten_by = beaten_by + ahead.astype(jnp.int32)
        picked.append(jnp.where(beaten_by < 2, sc[j], 0.0))
    denom = (picked[0] + picked[1]) + (picked[2] + picked[3])
    member_w = [jnp.sum(jnp.where(in_group, p / denom, 0.0), axis=0, keepdims=True) for p in picked]
    return in_group, member_w


def _out_norm_route_kernel(a_ref, wo_ref, x_ref, g_ref, b_ref, rw_ref, rb_ref, tri_ref,
                           x1_ref, gid_ref, rank_ref, cnt_ref, seen_ref):
    @pl.when(pl.program_id(0) == 0)
    def _():
        seen_ref[...] = jnp.zeros(seen_ref.shape, F32)

    y = _dot(a_ref[...], wo_ref[...])
    x1 = _ln(DEEPNORM_ALPHA * x_ref[...] + y, g_ref[...], b_ref[...])
    x1_ref[:, :D_MODEL] = x1
    xh, xl = _split_bf16(x1)
    wh, wl = _split_bf16(rw_ref[...])
    logits_t = _dot_nt(wh, xh) + (_dot_nt(wh, xl) + _dot_nt(wl, xh))
    in_group, member_w = _route(logits_t, rb_ref[...])
    tm = logits_t.shape[1]

    row = lax.broadcasted_iota(jnp.int32, (N_GROUPS, tm), 0)
    w_rows = jnp.zeros((N_GROUPS, tm), F32)
    for j in range(GROUP_SIZE):
        w_rows = jnp.where(row == j, member_w[j], w_rows)
    w_rows = jnp.concatenate([w_rows, jnp.zeros((LANES - N_GROUPS, tm), F32)], axis=0)
    x1_ref[:, D_MODEL:] = w_rows.T

    onehot = in_group.astype(F32)
    earlier = _dot(onehot.astype(BF16), tri_ref[...])
    seen = seen_ref[:, 0:1]
    gidx = lax.broadcasted_iota(jnp.int32, onehot.shape, 0)
    gid_ref[...] = jnp.sum(jnp.where(in_group, gidx, 0), axis=0, keepdims=True)
    rank_ref[...] = jnp.sum(onehot * (earlier + seen), axis=0, keepdims=True).astype(jnp.int32)
    seen_new = seen_ref[...] + jnp.sum(onehot, axis=1, keepdims=True)
    seen_ref[...] = seen_new
    cnt_ref[...] = seen_new.astype(jnp.int32)


def _out_norm_route(a, wo, x, g, b, rw_t, rb_t, tri, tm=ROUTE_TILE):
    t, d = x.shape
    return pl.pallas_call(
        _out_norm_route_kernel,
        out_shape=(jax.ShapeDtypeStruct((t, ROW_WIDE), F32),
                   jax.ShapeDtypeStruct((1, t), jnp.int32),
                   jax.ShapeDtypeStruct((1, t), jnp.int32),
                   jax.ShapeDtypeStruct((N_GROUPS, LANES), jnp.int32)),
        grid=(t // tm,),
        in_specs=[pl.BlockSpec((tm, d), lambda i: (i, 0)),
                  pl.BlockSpec((d, d), lambda i: (0, 0)),
                  pl.BlockSpec((tm, d), lambda i: (i, 0)),
                  pl.BlockSpec((1, d), lambda i: (0, 0)),
                  pl.BlockSpec((1, d), lambda i: (0, 0)),
                  pl.BlockSpec((N_EXPERTS, d), lambda i: (0, 0)),
                  pl.BlockSpec((N_EXPERTS, 1), lambda i: (0, 0)),
                  pl.BlockSpec((tm, tm), lambda i: (0, 0))],
        out_specs=(pl.BlockSpec((tm, ROW_WIDE), lambda i: (i, 0)),
                   pl.BlockSpec((1, tm), lambda i: (0, i)),
                   pl.BlockSpec((1, tm), lambda i: (0, i)),
                   pl.BlockSpec((N_GROUPS, LANES), lambda i: (0, 0))),
        scratch_shapes=[pltpu.VMEM((N_GROUPS, LANES), F32)],
        compiler_params=_cparams(("arbitrary",)),
        name="out_norm_route",
    )(a, wo, x, g, b, rw_t, rb_t, tri)


def _tile_plan(gid, rank, counts, n_tiles_max):
    cnt = counts[:, 0]
    padded = ((cnt + MOE_TILE - 1) // MOE_TILE) * MOE_TILE
    ends = jnp.cumsum(padded)
    starts = ends - padded
    n_tiles = ends[-1] // MOE_TILE
    tile_row = jnp.arange(n_tiles_max, dtype=jnp.int32) * MOE_TILE
    tile_group = jnp.sum((tile_row[:, None] >= ends[None, :]).astype(jnp.int32), axis=1)
    last_group = jnp.sum((ends[-1] - 1 >= ends).astype(jnp.int32))
    tile_group = jnp.minimum(tile_group, last_group)
    rows_left = (starts + cnt)[tile_group] - tile_row
    tile_rows = jnp.clip(rows_left, 0, MOE_TILE)
    dest = starts[gid[0]] + rank[0]
    return (dest.astype(jnp.int32), tile_group.astype(jnp.int32), tile_rows.astype(jnp.int32),
            n_tiles.reshape(1).astype(jnp.int32))


def _invert_plan_kernel(dest_ref, src_ref):
    step = pl.program_id(0)

    @pl.when(step == 0)
    def _():
        def clear(p, carry):
            src_ref[p] = 0
            return carry
        lax.fori_loop(0, src_ref.shape[0], clear, 0, unroll=8)

    def place(r, carry):
        src_ref[dest_ref[0, r]] = step * PLAN_CHUNK + r
        return carry

    lax.fori_loop(0, PLAN_CHUNK, place, 0, unroll=8)


def _invert_plan(dest, n_rows):
    t = dest.shape[0]
    chunks = t // PLAN_CHUNK
    return pl.pallas_call(
        _invert_plan_kernel,
        out_shape=jax.ShapeDtypeStruct((n_rows,), jnp.int32),
        grid=(chunks,),
        in_specs=[pl.BlockSpec((None, 1, PLAN_CHUNK), lambda i: (i, 0, 0), memory_space=pltpu.SMEM)],
        out_specs=pl.BlockSpec((n_rows,), lambda i: (0,), memory_space=pltpu.SMEM),
        compiler_params=_cparams(("arbitrary",)),
        name="invert_plan",
    )(dest.reshape(chunks, 1, PLAN_CHUNK))


def _row_copy(src, dst, src_row, dst_row, sem):
    return pltpu.make_async_copy(src.at[pl.ds(src_row, 1)], dst.at[pl.ds(dst_row, 1)], sem)


def _for_rows(n_rows, fn):
    def chunk(q, carry):
        for u in range(8):
            fn(q * 8 + u)
        return carry

    def single(r, carry):
        fn(r)
        return carry

    full = n_rows // 8
    lax.fori_loop(0, full, chunk, 0)
    lax.fori_loop(full * 8, n_rows, single, 0)


def _moe_rows_kernel(tile_group_ref, tile_rows_ref, n_tiles_ref, src_ref, src_next_ref, x_hbm,
                     wg_ref, wu_ref, wd_ref, g_ref, b_ref, out_hbm, xbuf, obuf, wg_bf, wu_bf, wd_bf, gsem, ssem):
    k = pl.program_id(0)
    n_tiles = n_tiles_ref[0]
    slot = k % 2

    def row_copies(tile, one_start, one_wait, wait_all, start):
        n_rows = tile_rows_ref[tile]

        @pl.when(n_rows == MOE_TILE)
        def _():
            if start:
                for r in range(MOE_TILE):
                    one_start(r)
            else:
                wait_all()

        @pl.when(n_rows != MOE_TILE)
        def _():
            _for_rows(n_rows, one_start if start else one_wait)

    def gather(tile, rows_ref, buf_slot, start):
        dst = xbuf.at[buf_slot]
        sem = gsem.at[buf_slot]
        row_copies(tile,
                   lambda r: _row_copy(x_hbm, dst, rows_ref[0, r], r, sem).start(),
                   lambda r: _row_copy(x_hbm, dst, 0, 0, sem).wait(),
                   lambda: pltpu.make_async_copy(x_hbm.at[pl.ds(0, MOE_TILE)], dst, sem).wait(),
                   start)

    def scatter(tile, rows_ref, buf_slot, start):
        src = obuf.at[buf_slot]
        sem = ssem.at[buf_slot]
        row_copies(tile,
                   lambda r: _row_copy(src, out_hbm, r, rows_ref[0, r], sem).start(),
                   lambda r: _row_copy(src, out_hbm, 0, 0, sem).wait(),
                   lambda: pltpu.make_async_copy(src, out_hbm.at[pl.ds(0, MOE_TILE)], sem).wait(),
                   start)

    @pl.when(k == 0)
    def _():
        xbuf[...] = jnp.zeros(xbuf.shape, F32)
        gather(0, src_ref, 0, True)

    @pl.when(k + 1 < n_tiles)
    def _():
        gather(k + 1, src_next_ref, 1 - slot, True)

    @pl.when(k < n_tiles)
    def _():
        gather(k, src_ref, slot, False)

        @pl.when(k >= 2)
        def _():
            scatter(k - 2, src_ref, slot, False)

        @pl.when((k == 0) | (tile_group_ref[k] != tile_group_ref[jnp.maximum(k - 1, 0)]))
        def _():
            wg_bf[...] = wg_ref[...].astype(BF16)
            wu_bf[...] = wu_ref[...].astype(BF16)
            wd_bf[...] = wd_ref[...].astype(BF16)

        x = xbuf[slot, :, :D_MODEL]
        cw = xbuf[slot, :, D_MODEL:]
        xb = x.astype(BF16)
        y = jnp.zeros(x.shape, F32)
        for j in range(GROUP_SIZE):
            hg = _dot(xb, wg_bf[j])
            hu = _dot(xb, wu_bf[j])
            hid = (hg / (1.0 + jnp.exp(-hg))) * hu * cw[:, j:j + 1]
            y = y + _dot(hid.astype(BF16), wd_bf[j])
        obuf[slot] = _ln(DEEPNORM_ALPHA * x + y, g_ref[...], b_ref[...])
        scatter(k, src_ref, slot, True)

    @pl.when(k == n_tiles - 1)
    def _():
        @pl.when(k >= 1)
        def _():
            scatter(k - 1, src_ref, 1 - slot, False)
        scatter(k, src_ref, slot, False)


def _moe_rows(x_rows, src, tile_group, tile_rows, n_tiles, wg, wu, wd, g, b, layer):
    t = x_rows.shape[0]
    n_tiles_max = src.shape[0] // MOE_TILE
    d, gs, ff = D_MODEL, GROUP_SIZE, EXPERT_FF
    src3 = src.reshape(n_tiles_max, 1, MOE_TILE)
    grid_spec = pltpu.PrefetchScalarGridSpec(
        num_scalar_prefetch=3,
        grid=(n_tiles_max,),
        in_specs=[pl.BlockSpec((None, 1, MOE_TILE), lambda k, tg, tr, nt: (k, 0, 0), memory_space=pltpu.SMEM),
                  pl.BlockSpec((None, 1, MOE_TILE), lambda k, tg, tr, nt: (jnp.minimum(k + 1, n_tiles_max - 1), 0, 0),
                               memory_space=pltpu.SMEM),
                  pl.BlockSpec(memory_space=pl.ANY),
                  pl.BlockSpec((None, gs, d, ff), lambda k, tg, tr, nt: (layer, tg[k], 0, 0)),
                  pl.BlockSpec((None, gs, d, ff), lambda k, tg, tr, nt: (layer, tg[k], 0, 0)),
                  pl.BlockSpec((None, gs, ff, d), lambda k, tg, tr, nt: (layer, tg[k], 0, 0)),
                  pl.BlockSpec((1, d), lambda k, tg, tr, nt: (0, 0)),
                  pl.BlockSpec((1, d), lambda k, tg, tr, nt: (0, 0))],
        out_specs=pl.BlockSpec(memory_space=pl.ANY),
        scratch_shapes=[pltpu.VMEM((2, MOE_TILE, ROW_WIDE), F32), pltpu.VMEM((2, MOE_TILE, d), F32),
                        pltpu.VMEM((gs, d, ff), BF16), pltpu.VMEM((gs, d, ff), BF16), pltpu.VMEM((gs, ff, d), BF16),
                        pltpu.SemaphoreType.DMA((2,)), pltpu.SemaphoreType.DMA((2,))])
    return pl.pallas_call(
        _moe_rows_kernel,
        out_shape=jax.ShapeDtypeStruct((t, d), F32),
        grid_spec=grid_spec,
        compiler_params=_cparams(("arbitrary",)),
        name="grouped_moe",
    )(tile_group, tile_rows, n_tiles, src3, src3, x_rows, wg, wu, wd, g, b)


def _rope_tables(seq):
    inv_freq = ROPE_BASE ** (-jnp.arange(0, MLA_ROPE_DIM, 2, dtype=F32) / MLA_ROPE_DIM)
    ang = jnp.arange(seq, dtype=F32)[:, None] * inv_freq[None, :]
    cos, sin = jnp.cos(ang), jnp.sin(ang)
    cos2 = jnp.concatenate([cos, cos], axis=1)
    sin2 = jnp.concatenate([-sin, sin], axis=1)
    zero = jnp.zeros((seq, LANES - MLA_ROPE_DIM), F32)
    return (jnp.concatenate([cos2, zero], axis=1), jnp.concatenate([sin2, zero], axis=1), cos2.T, sin2.T)


def _swap_halves(w):
    half = MLA_ROPE_DIM // 2
    return jnp.concatenate([w[..., half:], w[..., :half]], axis=-1)


def kernel(x, diff_w_qkv, diff_lambda, diff_subln_g, diff_w_o, mla_w_kv_a, mla_kv_norm_g, mla_w_kv_b,
           mla_w_q_a, mla_q_norm_g, mla_w_q_b, mla_w_o, router_w, router_b, moe_w_gate, moe_w_up,
           moe_w_down, ln_g, ln_b):
    batch, seq, d = x.shape
    t = batch * seq
    xs = x.reshape(t, d)

    wq, wk, wv = jnp.split(diff_w_qkv, 3, axis=-1)
    w_k = wk.astype(BF16)
    w_qv_t = jnp.concatenate([wq * (LOG2E * DIFF_HEAD_DIM ** -0.5), wv], axis=-1).transpose(0, 2, 1).astype(BF16)
    w_o_a = diff_w_o.astype(BF16)
    w_o_b = mla_w_o.astype(BF16)
    slopes = LOG2E * 2.0 ** (-8.0 * jnp.arange(1, DIFF_HEADS + 1, dtype=F32) / DIFF_HEADS)
    slope_hi = slopes.astype(BF16).astype(F32)
    slope_lo = (slopes - slope_hi).astype(BF16).astype(F32)
    rw_t = router_w.T.reshape(N_GROUPS, GROUP_SIZE, d).transpose(1, 0, 2).reshape(N_EXPERTS, d)
    rb_t = router_b.reshape(N_GROUPS, GROUP_SIZE).T.reshape(N_EXPERTS, 1).astype(F32)
    cs, sn, cs_t, sn_t = _rope_tables(seq)
    kv_rope_w = mla_w_kv_a[:, MLA_KV_RANK:]
    w_kv_a = jnp.concatenate([mla_w_kv_a, _swap_halves(kv_rope_w)], axis=1).astype(BF16)
    w_kv_b = mla_w_kv_b.reshape(MLA_KV_RANK, MLA_HEADS, MLA_NOPE_DIM + MLA_V_DIM)
    w_kn = w_kv_b[..., :MLA_NOPE_DIM].reshape(MLA_KV_RANK, -1).astype(BF16)
    w_v_t = w_kv_b[..., MLA_NOPE_DIM:].reshape(MLA_KV_RANK, -1).T.astype(BF16)
    wqb = mla_w_q_b.reshape(-1, MLA_Q_RANK, MLA_HEADS, MLA_NOPE_DIM + MLA_ROPE_DIM)
    w_q_b_t = jnp.concatenate([wqb, _swap_halves(wqb[..., MLA_NOPE_DIM:])], axis=-1).reshape(
        -1, MLA_Q_RANK, MLA_HEADS * MLA_QK_PAD).transpose(0, 2, 1).astype(BF16)
    w_q_a = mla_w_q_a.astype(BF16)

    alibi_fix, diag_mask = _diagonal_tables()
    tri = jnp.triu(jnp.ones((ROUTE_TILE, ROUTE_TILE), BF16), 1)
    n_tiles_max = t // MOE_TILE + N_GROUPS

    kcat = vt_shared = None
    for layer in range(DEPTH):
        if layer < N_A_LAYERS:
            k, qt, vt = _qkv_proj(xs, w_k[layer], w_qv_t[layer])
            attn = _diff_attention(qt, k, vt, slope_hi, slope_lo, diff_lambda[layer],
                                   diff_subln_g[layer].reshape(-1, 1), alibi_fix, batch, seq, _lambda_init(layer))
            w_o = w_o_a[layer]
        else:
            j = layer - N_A_LAYERS
            if j == 0:
                kcat, vt_shared = _mla_kv_proj(xs, w_kv_a, mla_kv_norm_g.reshape(1, -1), w_kn, w_v_t, cs, sn, seq)
            qt = _mla_q_proj(xs, w_q_a[j], mla_q_norm_g[j].reshape(1, -1), w_q_b_t[j], cs_t, sn_t, seq)
            attn = _mla_attention(qt, kcat, vt_shared, diag_mask, batch, seq)
            w_o = w_o_b[j]
        x1, gid, rank, counts = _out_norm_route(attn, w_o, xs, ln_g[layer, 0].reshape(1, -1),
                                                ln_b[layer, 0].reshape(1, -1), rw_t, rb_t, tri)
        dest, tile_group, tile_rows, n_tiles = _tile_plan(gid, rank, counts, n_tiles_max)
        src = _invert_plan(dest, n_tiles_max * MOE_TILE)
        xs = _moe_rows(x1, src, tile_group, tile_rows, n_tiles, moe_w_gate, moe_w_up, moe_w_down,
                       ln_g[layer, 1].reshape(1, -1), ln_b[layer, 1].reshape(1, -1), layer)
    return xs.reshape(batch, seq, d)
```

```python
import functools
import math

import jax
import jax.numpy as jnp
from jax import lax
from jax.experimental import pallas as pl
from jax.experimental.pallas import tpu as pltpu

D_MODEL = 1024
DEPTH = 4
CHUNK = 64
N_A_LAYERS = DEPTH // 2
DIFF_HEAD_DIM = 64
DIFF_V_DIM = 2 * DIFF_HEAD_DIM
DIFF_HEADS = D_MODEL // DIFF_V_DIM
MLA_NOPE_DIM = 128
MLA_ROPE_DIM = 64
MLA_V_DIM = 128
MLA_HEADS = D_MODEL // MLA_V_DIM
MLA_Q_RANK = 384
MLA_KV_RANK = 256
MLA_QK_PAD = 256
ROPE_BASE = 10000.0
N_EXPERTS = 32
N_GROUPS = 8
GROUP_SIZE = N_EXPERTS // N_GROUPS
EXPERT_FF = 256
NORM_EPS = 1e-5
DEEPNORM_ALPHA = (2 * DEPTH) ** 0.25
LOG2E = math.log2(math.e)

LANES = 128
T_TILE = 256
GROUP_TILES = 2
Q_TILE = GROUP_TILES * T_TILE
MASKED = 1e30
V_ROWS = DIFF_V_DIM + 16
ROUTE_TILE = 512
MOE_TILE = 256
PLAN_CHUNK = 2048
ROW_WIDE = D_MODEL + LANES
VMEM_LIMIT = 48 * 1024 * 1024

F32 = jnp.float32
BF16 = jnp.bfloat16


def _lambda_init(layer):
    return 0.8 - 0.6 * math.exp(-0.3 * layer)


def _cparams(sem):
    return pltpu.CompilerParams(dimension_semantics=sem, vmem_limit_bytes=VMEM_LIMIT)


def _dot(a, b):
    return jnp.dot(a, b, preferred_element_type=F32)


def _dot_nt(a, b):
    return lax.dot_general(a, b, (((1,), (1,)), ((), ())), preferred_element_type=F32)


def _rms(x, g):
    return x * lax.rsqrt(jnp.mean(x * x, axis=-1, keepdims=True) + NORM_EPS) * g


def _ln(x, g, b):
    mu = jnp.mean(x, axis=-1, keepdims=True)
    xc = x - mu
    var = jnp.mean(xc * xc, axis=-1, keepdims=True)
    return xc * lax.rsqrt(var + NORM_EPS) * g + b


def _rope_halves(a, cs, sn):
    return a * cs + pltpu.roll(a, MLA_ROPE_DIM, axis=1) * sn


def _qkv_kernel(x_ref, wk_ref, wqv_ref, k_ref, qt_ref, vt_ref):
    xb = x_ref[...].astype(BF16)
    k_ref[...] = _dot(xb, wk_ref[...]).astype(BF16)
    qv = _dot_nt(wqv_ref[...], xb)
    for h in range(DIFF_HEADS):
        qt_ref[0, h] = qv[h * LANES:(h + 1) * LANES].astype(BF16)
        vt_ref[0, h, :DIFF_V_DIM] = qv[D_MODEL + h * LANES:D_MODEL + (h + 1) * LANES].astype(BF16)
        vt_ref[0, h, DIFF_V_DIM:] = jnp.ones((V_ROWS - DIFF_V_DIM, T_TILE), BF16)


def _qkv_proj(x, wk, wqv):
    t, d = x.shape
    nt = t // T_TILE

    def head_major(rows):
        return jax.ShapeDtypeStruct((nt, DIFF_HEADS, rows, T_TILE), BF16)

    def head_block(rows):
        return pl.BlockSpec((1, DIFF_HEADS, rows, T_TILE), lambda i: (i, 0, 0, 0))

    return pl.pallas_call(
        _qkv_kernel,
        out_shape=(jax.ShapeDtypeStruct((t, d), BF16), head_major(LANES), head_major(V_ROWS)),
        grid=(nt,),
        in_specs=[pl.BlockSpec((T_TILE, d), lambda i: (i, 0)),
                  pl.BlockSpec(wk.shape, lambda i: (0, 0)),
                  pl.BlockSpec(wqv.shape, lambda i: (0, 0))],
        out_specs=(pl.BlockSpec((T_TILE, d), lambda i: (i, 0)), head_block(LANES), head_block(V_ROWS)),
        compiler_params=_cparams(("parallel",)),
        name="qkv_proj",
    )(x, wk, wqv)


def _mla_kv_kernel(x_ref, wa_ref, g_ref, wk_ref, wv_ref, cs_ref, sn_ref, kcat_ref, vt_ref):
    kva = _dot(x_ref[...].astype(BF16), wa_ref[...])
    c_kv = _rms(kva[:, :MLA_KV_RANK], g_ref[...]).astype(BF16)
    kr = _rope_halves(kva[:, MLA_KV_RANK:], cs_ref[...], sn_ref[...]).astype(BF16)
    kn = _dot(c_kv, wk_ref[...])
    vt = _dot_nt(wv_ref[...], c_kv)
    for h in range(MLA_HEADS):
        base = h * MLA_QK_PAD
        kcat_ref[:, base:base + MLA_NOPE_DIM] = kn[:, h * MLA_NOPE_DIM:(h + 1) * MLA_NOPE_DIM].astype(BF16)
        kcat_ref[:, base + MLA_NOPE_DIM:base + MLA_QK_PAD] = kr
        vt_ref[0, h, :MLA_V_DIM] = vt[h * MLA_V_DIM:(h + 1) * MLA_V_DIM].astype(BF16)
        vt_ref[0, h, MLA_V_DIM:] = jnp.ones((V_ROWS - MLA_V_DIM, T_TILE), BF16)


def _mla_kv_proj(x, wa, g, wk, wv, cs, sn, seq):
    t, d = x.shape
    nt = t // T_TILE
    ns = seq // T_TILE
    return pl.pallas_call(
        _mla_kv_kernel,
        out_shape=(jax.ShapeDtypeStruct((t, MLA_HEADS * MLA_QK_PAD), BF16),
                   jax.ShapeDtypeStruct((nt, MLA_HEADS, V_ROWS, T_TILE), BF16)),
        grid=(nt,),
        in_specs=[pl.BlockSpec((T_TILE, d), lambda i: (i, 0)),
                  pl.BlockSpec(wa.shape, lambda i: (0, 0)),
                  pl.BlockSpec(g.shape, lambda i: (0, 0)),
                  pl.BlockSpec(wk.shape, lambda i: (0, 0)),
                  pl.BlockSpec(wv.shape, lambda i: (0, 0)),
                  pl.BlockSpec((T_TILE, LANES), lambda i: (i % ns, 0)),
                  pl.BlockSpec((T_TILE, LANES), lambda i: (i % ns, 0))],
        out_specs=(pl.BlockSpec((T_TILE, MLA_HEADS * MLA_QK_PAD), lambda i: (i, 0)),
                   pl.BlockSpec((1, MLA_HEADS, V_ROWS, T_TILE), lambda i: (i, 0, 0, 0))),
        compiler_params=_cparams(("parallel",)),
        name="mla_kv_proj",
    )(x, wa, g, wk, wv, cs, sn)


def _mla_q_kernel(x_ref, wa_ref, g_ref, wb_ref, cst_ref, snt_ref, qt_ref, *, scale):
    qa = _dot(x_ref[...].astype(BF16), wa_ref[...])
    qn = _rms(qa, g_ref[...]).astype(BF16)
    qt = _dot_nt(wb_ref[...], qn) * scale
    cst = cst_ref[...]
    snt = snt_ref[...]
    zero = jnp.zeros((MLA_ROPE_DIM, T_TILE), BF16)
    for h in range(MLA_HEADS):
        base = h * MLA_QK_PAD
        r0 = base + MLA_NOPE_DIM
        rope = qt[r0:r0 + MLA_ROPE_DIM] * cst + qt[r0 + MLA_ROPE_DIM:base + MLA_QK_PAD] * snt
        qt_ref[0, h, 0:MLA_NOPE_DIM] = qt[base:r0].astype(BF16)
        qt_ref[0, h, MLA_NOPE_DIM:MLA_NOPE_DIM + MLA_ROPE_DIM] = rope.astype(BF16)
        qt_ref[0, h, MLA_NOPE_DIM + MLA_ROPE_DIM:MLA_QK_PAD] = zero


def _mla_q_proj(x, wa, g, wb_t, cst, snt, seq):
    t, d = x.shape
    nt = t // T_TILE
    ns = seq // T_TILE
    scale = LOG2E * (MLA_NOPE_DIM + MLA_ROPE_DIM) ** -0.5
    return pl.pallas_call(
        functools.partial(_mla_q_kernel, scale=scale),
        out_shape=jax.ShapeDtypeStruct((nt, MLA_HEADS, MLA_QK_PAD, T_TILE), BF16),
        grid=(nt,),
        in_specs=[pl.BlockSpec((T_TILE, d), lambda i: (i, 0)),
                  pl.BlockSpec(wa.shape, lambda i: (0, 0)),
                  pl.BlockSpec(g.shape, lambda i: (0, 0)),
                  pl.BlockSpec(wb_t.shape, lambda i: (0, 0)),
                  pl.BlockSpec((MLA_ROPE_DIM, T_TILE), lambda i: (0, i % ns)),
                  pl.BlockSpec((MLA_ROPE_DIM, T_TILE), lambda i: (0, i % ns))],
        out_specs=pl.BlockSpec((1, MLA_HEADS, MLA_QK_PAD, T_TILE), lambda i: (i, 0, 0, 0)),
        compiler_params=_cparams(("parallel",)),
        name="mla_q_proj",
    )(x, wa, g, wb_t, cst, snt)


def _consume_group(tiles, m_ref, acc_ref):
    for c, comp_tiles in enumerate(tiles):
        m_group = None
        for s_t, shift, _ in comp_tiles:
            m_t = jnp.max(s_t, axis=0, keepdims=True) + shift
            m_group = m_t if m_group is None else jnp.maximum(m_group, m_t)
        pv = None
        for s_t, shift, vt in comp_tiles:
            part = _dot(vt, jnp.exp2(s_t - (m_group - shift)).astype(BF16))
            pv = part if pv is None else pv + part
        m_prev = m_ref[c]
        m_new = jnp.maximum(m_prev, m_group)
        acc_ref[c] = jnp.exp2(m_prev - m_new) * acc_ref[c] + jnp.exp2(m_group - m_new) * pv
        m_ref[c] = m_new


def _init_online_softmax(m_ref, acc_ref):
    m_ref[...] = jnp.full(m_ref.shape, -jnp.inf, F32)
    acc_ref[...] = jnp.zeros(acc_ref.shape, F32)


def _normalised(acc):
    return acc[:DIFF_V_DIM] / acc[DIFF_V_DIM:DIFF_V_DIM + 1]


def _attention_pipeline(n_prev_groups, issue_scores, consume, buf_a, buf_b):
    issue_scores(0, buf_a)

    def pair(p, carry):
        issue_scores(2 * p + 1, buf_b)
        consume(2 * p, buf_a, False)
        issue_scores(2 * p + 2, buf_a)
        consume(2 * p + 1, buf_b, False)
        return carry

    lax.fori_loop(0, n_prev_groups // 2, pair, 0)
    odd = n_prev_groups % 2

    @pl.when(odd == 1)
    def _():
        issue_scores(n_prev_groups, buf_b)
        consume(n_prev_groups - 1, buf_a, False)
        consume(n_prev_groups, buf_b, True)

    @pl.when(odd == 0)
    def _():
        consume(n_prev_groups, buf_a, True)


def _diff_attn_kernel(slh_ref, sll_ref, lam_ref, g_ref, dm_ref, qt_ref, k_ref, vt_ref, o_ref,
                      buf_a, buf_b, m_ref, acc_ref, *, lambda_init):
    h = pl.program_id(1)
    i = pl.program_id(2)
    sl_hi = slh_ref[h]
    sl_lo = sll_ref[h]
    slope = sl_hi + sl_lo
    row = lax.broadcasted_iota(jnp.int32, (LANES, Q_TILE), 0)
    qf = jnp.concatenate([qt_ref[u] for u in range(GROUP_TILES)], axis=1).astype(F32)
    slope_rows = jnp.where(row == 0, sl_hi, jnp.where(row == 1, sl_lo, 0.0)).astype(BF16)
    qs_aug = [jnp.concatenate([jnp.where(keep, qf, 0.0).astype(BF16), slope_rows], axis=0)
              for keep in (row < DIFF_HEAD_DIM, row >= DIFF_HEAD_DIM)]
    key_off = lax.broadcasted_iota(jnp.int32, (T_TILE, LANES), 0).astype(F32)
    lane = lax.broadcasted_iota(jnp.int32, (T_TILE, LANES), 1)
    key_cols = jnp.where(lane < 2, key_off, 0.0).astype(BF16)
    _init_online_softmax(m_ref, acc_ref)

    def issue_scores(g, buf):
        for u in range(GROUP_TILES):
            off = pl.multiple_of((g * GROUP_TILES + u) * T_TILE, T_TILE)
            k_aug = jnp.concatenate([k_ref[pl.ds(off, T_TILE), :], key_cols], axis=1)
            for c in range(2):
                buf[2 * u + c] = _dot(k_aug, qs_aug[c])

    def consume(g, buf, diagonal):
        tiles = ([], [])
        for u in range(GROUP_TILES):
            j = g * GROUP_TILES + u
            shift = slope * (j * T_TILE - i * Q_TILE).astype(F32)
            for c in range(2):
                s_t = buf[2 * u + c]
                if diagonal:
                    s_t = s_t + slope * dm_ref[u * T_TILE:(u + 1) * T_TILE, :]
                tiles[c].append((s_t, shift, vt_ref[j]))
        _consume_group(tiles, m_ref, acc_ref)

    _attention_pipeline(i, issue_scores, consume, buf_a, buf_b)

    lam = lam_ref[...]
    lam_full = (jnp.exp(jnp.sum(lam[0:1] * lam[1:2], axis=1, keepdims=True))
                - jnp.exp(jnp.sum(lam[2:3] * lam[3:4], axis=1, keepdims=True)) + lambda_init)
    o_t = _normalised(acc_ref[0]) - lam_full * _normalised(acc_ref[1])
    inv = lax.rsqrt(jnp.mean(o_t * o_t, axis=0, keepdims=True) + NORM_EPS)
    o_t = o_t * inv * (g_ref[...] * (1.0 - lambda_init))
    o_ref[...] = o_t.T.astype(o_ref.dtype)


def _diagonal_tables():
    rk = jnp.arange(Q_TILE, dtype=jnp.int32)[:, None]
    rq = jnp.arange(Q_TILE, dtype=jnp.int32)[None, :]
    visible = (rk // CHUNK) <= (rq // CHUNK)
    alibi_fix = jnp.where(visible, -2.0 * jnp.maximum(rk - rq, 0).astype(F32), -MASKED)
    mask = jnp.where(visible, 0.0, -MASKED).astype(F32)
    return alibi_fix, mask


def _diff_attention(qt, k, vt, slope_hi, slope_lo, lam, g_col, alibi_fix, batch, seq, lambda_init):
    nq = seq // Q_TILE
    nk = seq // T_TILE
    hh = DIFF_HEADS
    return pl.pallas_call(
        functools.partial(_diff_attn_kernel, lambda_init=lambda_init),
        out_shape=jax.ShapeDtypeStruct((batch * seq, hh * DIFF_V_DIM), BF16),
        grid=(batch, hh, nq),
        in_specs=[pl.BlockSpec(memory_space=pltpu.SMEM),
                  pl.BlockSpec(memory_space=pltpu.SMEM),
                  pl.BlockSpec(lam.shape, lambda b, h, i: (0, 0)),
                  pl.BlockSpec(g_col.shape, lambda b, h, i: (0, 0)),
                  pl.BlockSpec(alibi_fix.shape, lambda b, h, i: (0, 0)),
                  pl.BlockSpec((GROUP_TILES, None, LANES, T_TILE), lambda b, h, i: (b * nq + i, h, 0, 0)),
                  pl.BlockSpec((seq, LANES), lambda b, h, i: (b, h)),
                  pl.BlockSpec((nk, None, V_ROWS, T_TILE), lambda b, h, i: (b, h, 0, 0))],
        out_specs=pl.BlockSpec((Q_TILE, LANES), lambda b, h, i: (b * nq + i, h)),
        scratch_shapes=[pltpu.VMEM((2 * GROUP_TILES, T_TILE, Q_TILE), F32),
                        pltpu.VMEM((2 * GROUP_TILES, T_TILE, Q_TILE), F32),
                        pltpu.VMEM((2, 1, Q_TILE), F32),
                        pltpu.VMEM((2, V_ROWS, Q_TILE), F32)],
        compiler_params=_cparams(("parallel", "parallel", "arbitrary")),
        name="diff_attention",
    )(slope_hi, slope_lo, lam, g_col, alibi_fix, qt, k, vt)


def _mla_attn_kernel(mask_ref, qt_ref, k_ref, vt_ref, o_ref, buf_a, buf_b, m_ref, acc_ref):
    i = pl.program_id(2)
    qt = jnp.concatenate([qt_ref[u] for u in range(GROUP_TILES)], axis=1)
    _init_online_softmax(m_ref, acc_ref)

    def issue_scores(g, buf):
        for u in range(GROUP_TILES):
            off = pl.multiple_of((g * GROUP_TILES + u) * T_TILE, T_TILE)
            buf[u] = _dot(k_ref[pl.ds(off, T_TILE), :], qt)

    def consume(g, buf, diagonal):
        tiles = []
        for u in range(GROUP_TILES):
            s_t = buf[u]
            if diagonal:
                s_t = s_t + mask_ref[u * T_TILE:(u + 1) * T_TILE, :]
            tiles.append((s_t, 0.0, vt_ref[g * GROUP_TILES + u]))
        _consume_group([tiles], m_ref, acc_ref)

    _attention_pipeline(i, issue_scores, consume, buf_a, buf_b)
    o_ref[...] = _normalised(acc_ref[0]).T.astype(o_ref.dtype)


def _mla_attention(qt, kcat, vt, mask, batch, seq):
    nq = seq // Q_TILE
    nk = seq // T_TILE
    return pl.pallas_call(
        _mla_attn_kernel,
        out_shape=jax.ShapeDtypeStruct((batch * seq, MLA_HEADS * MLA_V_DIM), BF16),
        grid=(batch, MLA_HEADS, nq),
        in_specs=[pl.BlockSpec(mask.shape, lambda b, h, i: (0, 0)),
                  pl.BlockSpec((GROUP_TILES, None, MLA_QK_PAD, T_TILE), lambda b, h, i: (b * nq + i, h, 0, 0)),
                  pl.BlockSpec((seq, MLA_QK_PAD), lambda b, h, i: (b, h)),
                  pl.BlockSpec((nk, None, V_ROWS, T_TILE), lambda b, h, i: (b, h, 0, 0))],
        out_specs=pl.BlockSpec((Q_TILE, MLA_V_DIM), lambda b, h, i: (b * nq + i, h)),
        scratch_shapes=[pltpu.VMEM((GROUP_TILES, T_TILE, Q_TILE), F32),
                        pltpu.VMEM((GROUP_TILES, T_TILE, Q_TILE), F32),
                        pltpu.VMEM((1, 1, Q_TILE), F32),
                        pltpu.VMEM((1, V_ROWS, Q_TILE), F32)],
        compiler_params=_cparams(("parallel", "parallel", "arbitrary")),
        name="mla_attention",
    )(mask, qt, kcat, vt)


def _split_bf16(x):
    hi = x.astype(BF16)
    lo = (x - hi.astype(F32)).astype(BF16)
    return hi, lo


def _route(logits_t, bias_t):
    score = 1.0 / (1.0 + jnp.exp(-logits_t))
    sel = score + bias_t
    a = [sel[N_GROUPS * j:N_GROUPS * (j + 1)] for j in range(GROUP_SIZE)]
    sc = [score[N_GROUPS * j:N_GROUPS * (j + 1)] for j in range(GROUP_SIZE)]
    hi1, lo1 = jnp.maximum(a[0], a[1]), jnp.minimum(a[0], a[1])
    hi2, lo2 = jnp.maximum(a[2], a[3]), jnp.minimum(a[2], a[3])
    group_score = jnp.maximum(hi1, hi2) + jnp.maximum(jnp.minimum(hi1, hi2), jnp.maximum(lo1, lo2))
    gidx = lax.broadcasted_iota(jnp.int32, group_score.shape, 0)
    best = jnp.max(group_score, axis=0, keepdims=True)
    first_best = jnp.min(jnp.where(group_score == best, gidx, N_GROUPS), axis=0, keepdims=True)
    in_group = gidx == first_best
    picked = []
    for j in range(GROUP_SIZE):
        beaten_by = jnp.zeros(a[j].shape, jnp.int32)
        for i in range(GROUP_SIZE):
            if i == j:
                continue
            ahead = (a[i] > a[j]) | ((a[i] == a[j]) if i < j else False)
            beaten_by = beaten_by + ahead.astype(jnp.int32)
        picked.append(jnp.where(beaten_by < 2, sc[j], 0.0))
    denom = (picked[0] + picked[1]) + (picked[2] + picked[3])
    member_w = [jnp.sum(jnp.where(in_group, p / denom, 0.0), axis=0, keepdims=True) for p in picked]
    return in_group, member_w


def _out_norm_route_kernel(a_ref, wo_ref, x_ref, g_ref, b_ref, rw_ref, rb_ref, tri_ref,
                           x1_ref, gid_ref, rank_ref, cnt_ref, seen_ref):
    @pl.when(pl.program_id(0) == 0)
    def _():
        seen_ref[...] = jnp.zeros(seen_ref.shape, F32)

    y = _dot(a_ref[...], wo_ref[...])
    x1 = _ln(DEEPNORM_ALPHA * x_ref[...] + y, g_ref[...], b_ref[...])
    x1_ref[:, :D_MODEL] = x1
    xh, xl = _split_bf16(x1)
    wh, wl = _split_bf16(rw_ref[...])
    logits_t = _dot_nt(wh, xh) + (_dot_nt(wh, xl) + _dot_nt(wl, xh))
    in_group, member_w = _route(logits_t, rb_ref[...])
    tm = logits_t.shape[1]

    row = lax.broadcasted_iota(jnp.int32, (N_GROUPS, tm), 0)
    w_rows = jnp.zeros((N_GROUPS, tm), F32)
    for j in range(GROUP_SIZE):
        w_rows = jnp.where(row == j, member_w[j], w_rows)
    w_rows = jnp.concatenate([w_rows, jnp.zeros((LANES - N_GROUPS, tm), F32)], axis=0)
    x1_ref[:, D_MODEL:] = w_rows.T

    onehot = in_group.astype(F32)
    earlier = _dot(onehot.astype(BF16), tri_ref[...])
    seen = seen_ref[:, 0:1]
    gidx = lax.broadcasted_iota(jnp.int32, onehot.shape, 0)
    gid_ref[...] = jnp.sum(jnp.where(in_group, gidx, 0), axis=0, keepdims=True)
    rank_ref[...] = jnp.sum(onehot * (earlier + seen), axis=0, keepdims=True).astype(jnp.int32)
    seen_new = seen_ref[...] + jnp.sum(onehot, axis=1, keepdims=True)
    seen_ref[...] = seen_new
    cnt_ref[...] = seen_new.astype(jnp.int32)


def _out_norm_route(a, wo, x, g, b, rw_t, rb_t, tri, tm=ROUTE_TILE):
    t, d = x.shape
    return pl.pallas_call(
        _out_norm_route_kernel,
        out_shape=(jax.ShapeDtypeStruct((t, ROW_WIDE), F32),
                   jax.ShapeDtypeStruct((1, t), jnp.int32),
                   jax.ShapeDtypeStruct((1, t), jnp.int32),
                   jax.ShapeDtypeStruct((N_GROUPS, LANES), jnp.int32)),
        grid=(t // tm,),
        in_specs=[pl.BlockSpec((tm, d), lambda i: (i, 0)),
                  pl.BlockSpec((d, d), lambda i: (0, 0)),
                  pl.BlockSpec((tm, d), lambda i: (i, 0)),
                  pl.BlockSpec((1, d), lambda i: (0, 0)),
                  pl.BlockSpec((1, d), lambda i: (0, 0)),
                  pl.BlockSpec((N_EXPERTS, d), lambda i: (0, 0)),
                  pl.BlockSpec((N_EXPERTS, 1), lambda i: (0, 0)),
                  pl.BlockSpec((tm, tm), lambda i: (0, 0))],
        out_specs=(pl.BlockSpec((tm, ROW_WIDE), lambda i: (i, 0)),
                   pl.BlockSpec((1, tm), lambda i: (0, i)),
                   pl.BlockSpec((1, tm), lambda i: (0, i)),
                   pl.BlockSpec((N_GROUPS, LANES), lambda i: (0, 0))),
        scratch_shapes=[pltpu.VMEM((N_GROUPS, LANES), F32)],
        compiler_params=_cparams(("arbitrary",)),
        name="out_norm_route",
    )(a, wo, x, g, b, rw_t, rb_t, tri)


def _tile_plan(gid, rank, counts, n_tiles_max):
    cnt = counts[:, 0]
    padded = ((cnt + MOE_TILE - 1) // MOE_TILE) * MOE_TILE
    ends = jnp.cumsum(padded)
    starts = ends - padded
    n_tiles = ends[-1] // MOE_TILE
    tile_row = jnp.arange(n_tiles_max, dtype=jnp.int32) * MOE_TILE
    tile_group = jnp.sum((tile_row[:, None] >= ends[None, :]).astype(jnp.int32), axis=1)
    last_group = jnp.sum((ends[-1] - 1 >= ends).astype(jnp.int32))
    tile_group = jnp.minimum(tile_group, last_group)
    rows_left = (starts + cnt)[tile_group] - tile_row
    tile_rows = jnp.clip(rows_left, 0, MOE_TILE)
    dest = starts[gid[0]] + rank[0]
    return (dest.astype(jnp.int32), tile_group.astype(jnp.int32), tile_rows.astype(jnp.int32),
            n_tiles.reshape(1).astype(jnp.int32))


def _invert_plan_kernel(dest_ref, src_ref):
    step = pl.program_id(0)

    @pl.when(step == 0)
    def _():
        def clear(p, carry):
            src_ref[p] = 0
            return carry
        lax.fori_loop(0, src_ref.shape[0], clear, 0, unroll=8)

    def place(r, carry):
        src_ref[dest_ref[0, r]] = step * PLAN_CHUNK + r
        return carry

    lax.fori_loop(0, PLAN_CHUNK, place, 0, unroll=8)


def _invert_plan(dest, n_rows):
    t = dest.shape[0]
    chunks = t // PLAN_CHUNK
    return pl.pallas_call(
        _invert_plan_kernel,
        out_shape=jax.ShapeDtypeStruct((n_rows,), jnp.int32),
        grid=(chunks,),
        in_specs=[pl.BlockSpec((None, 1, PLAN_CHUNK), lambda i: (i, 0, 0), memory_space=pltpu.SMEM)],
        out_specs=pl.BlockSpec((n_rows,), lambda i: (0,), memory_space=pltpu.SMEM),
        compiler_params=_cparams(("arbitrary",)),
        name="invert_plan",
    )(dest.reshape(chunks, 1, PLAN_CHUNK))


def _row_copy(src, dst, src_row, dst_row, sem):
    return pltpu.make_async_copy(src.at[pl.ds(src_row, 1)], dst.at[pl.ds(dst_row, 1)], sem)


def _for_rows(n_rows, fn):
    def chunk(q, carry):
        for u in range(8):
            fn(q * 8 + u)
        return carry

    def single(r, carry):
        fn(r)
        return carry

    full = n_rows // 8
    lax.fori_loop(0, full, chunk, 0)
    lax.fori_loop(full * 8, n_rows, single, 0)


def _moe_rows_kernel(tile_group_ref, tile_rows_ref, n_tiles_ref, src_ref, src_next_ref, x_hbm,
                     wg_ref, wu_ref, wd_ref, g_ref, b_ref, out_hbm, xbuf, obuf, wg_bf, wu_bf, wd_bf, gsem, ssem):
    k = pl.program_id(0)
    n_tiles = n_tiles_ref[0]
    slot = k % 2

    def row_copies(tile, one_start, one_wait, wait_all, start):
        n_rows = tile_rows_ref[tile]

        @pl.when(n_rows == MOE_TILE)
        def _():
            if start:
                for r in range(MOE_TILE):
                    one_start(r)
            else:
                wait_all()

        @pl.when(n_rows != MOE_TILE)
        def _():
            _for_rows(n_rows, one_start if start else one_wait)

    def gather(tile, rows_ref, buf_slot, start):
        dst = xbuf.at[buf_slot]
        sem = gsem.at[buf_slot]
        row_copies(tile,
                   lambda r: _row_copy(x_hbm, dst, rows_ref[0, r], r, sem).start(),
                   lambda r: _row_copy(x_hbm, dst, 0, 0, sem).wait(),
                   lambda: pltpu.make_async_copy(x_hbm.at[pl.ds(0, MOE_TILE)], dst, sem).wait(),
                   start)

    def scatter(tile, rows_ref, buf_slot, start):
        src = obuf.at[buf_slot]
        sem = ssem.at[buf_slot]
        row_copies(tile,
                   lambda r: _row_copy(src, out_hbm, r, rows_ref[0, r], sem).start(),
                   lambda r: _row_copy(src, out_hbm, 0, 0, sem).wait(),
                   lambda: pltpu.make_async_copy(src, out_hbm.at[pl.ds(0, MOE_TILE)], sem).wait(),
                   start)

    @pl.when(k == 0)
    def _():
        xbuf[...] = jnp.zeros(xbuf.shape, F32)
        gather(0, src_ref, 0, True)

    @pl.when(k + 1 < n_tiles)
    def _():
        gather(k + 1, src_next_ref, 1 - slot, True)

    @pl.when(k < n_tiles)
    def _():
        gather(k, src_ref, slot, False)

        @pl.when(k >= 2)
        def _():
            scatter(k - 2, src_ref, slot, False)

        @pl.when((k == 0) | (tile_group_ref[k] != tile_group_ref[jnp.maximum(k - 1, 0)]))
        def _():
            for j in range(GROUP_SIZE):
                cols = slice(j * EXPERT_FF, (j + 1) * EXPERT_FF)
                wg_bf[:, cols] = wg_ref[j].astype(BF16)
                wu_bf[:, cols] = wu_ref[j].astype(BF16)
                wd_bf[cols, :] = wd_ref[j].astype(BF16)

        x = xbuf[slot, :, :D_MODEL]
        cw = xbuf[slot, :, D_MODEL:]
        xb = x.astype(BF16)
        hg = _dot(xb, wg_bf[...])
        hu = _dot(xb, wu_bf[...])
        cw_wide = jnp.concatenate([jnp.broadcast_to(cw[:, j:j + 1], (MOE_TILE, EXPERT_FF))
                                   for j in range(GROUP_SIZE)], axis=1)
        hid = (hg / (1.0 + jnp.exp(-hg))) * hu * cw_wide
        y = _dot(hid.astype(BF16), wd_bf[...])
        obuf[slot] = _ln(DEEPNORM_ALPHA * x + y, g_ref[...], b_ref[...])
        scatter(k, src_ref, slot, True)

    @pl.when(k == n_tiles - 1)
    def _():
        @pl.when(k >= 1)
        def _():
            scatter(k - 1, src_ref, 1 - slot, False)
        scatter(k, src_ref, slot, False)


def _moe_rows(x_rows, src, tile_group, tile_rows, n_tiles, wg, wu, wd, g, b, layer):
    t = x_rows.shape[0]
    n_tiles_max = src.shape[0] // MOE_TILE
    d, gs, ff = D_MODEL, GROUP_SIZE, EXPERT_FF
    src3 = src.reshape(n_tiles_max, 1, MOE_TILE)
    grid_spec = pltpu.PrefetchScalarGridSpec(
        num_scalar_prefetch=3,
        grid=(n_tiles_max,),
        in_specs=[pl.BlockSpec((None, 1, MOE_TILE), lambda k, tg, tr, nt: (k, 0, 0), memory_space=pltpu.SMEM),
                  pl.BlockSpec((None, 1, MOE_TILE), lambda k, tg, tr, nt: (jnp.minimum(k + 1, n_tiles_max - 1), 0, 0),
                               memory_space=pltpu.SMEM),
                  pl.BlockSpec(memory_space=pl.ANY),
                  pl.BlockSpec((None, gs, d, ff), lambda k, tg, tr, nt: (layer, tg[k], 0, 0)),
                  pl.BlockSpec((None, gs, d, ff), lambda k, tg, tr, nt: (layer, tg[k], 0, 0)),
                  pl.BlockSpec((None, gs, ff, d), lambda k, tg, tr, nt: (layer, tg[k], 0, 0)),
                  pl.BlockSpec((1, d), lambda k, tg, tr, nt: (0, 0)),
                  pl.BlockSpec((1, d), lambda k, tg, tr, nt: (0, 0))],
        out_specs=pl.BlockSpec(memory_space=pl.ANY),
        scratch_shapes=[pltpu.VMEM((2, MOE_TILE, ROW_WIDE), F32), pltpu.VMEM((2, MOE_TILE, d), F32),
                        pltpu.VMEM((d, gs * ff), BF16), pltpu.VMEM((d, gs * ff), BF16), pltpu.VMEM((gs * ff, d), BF16),
                        pltpu.SemaphoreType.DMA((2,)), pltpu.SemaphoreType.DMA((2,))])
    return pl.pallas_call(
        _moe_rows_kernel,
        out_shape=jax.ShapeDtypeStruct((t, d), F32),
        grid_spec=grid_spec,
        compiler_params=_cparams(("arbitrary",)),
        name="grouped_moe",
    )(tile_group, tile_rows, n_tiles, src3, src3, x_rows, wg, wu, wd, g, b)


def _rope_tables(seq):
    inv_freq = ROPE_BASE ** (-jnp.arange(0, MLA_ROPE_DIM, 2, dtype=F32) / MLA_ROPE_DIM)
    ang = jnp.arange(seq, dtype=F32)[:, None] * inv_freq[None, :]
    cos, sin = jnp.cos(ang), jnp.sin(ang)
    cos2 = jnp.concatenate([cos, cos], axis=1)
    sin2 = jnp.concatenate([-sin, sin], axis=1)
    zero = jnp.zeros((seq, LANES - MLA_ROPE_DIM), F32)
    return (jnp.concatenate([cos2, zero], axis=1), jnp.concatenate([sin2, zero], axis=1), cos2.T, sin2.T)


def _swap_halves(w):
    half = MLA_ROPE_DIM // 2
    return jnp.concatenate([w[..., half:], w[..., :half]], axis=-1)


def kernel(x, diff_w_qkv, diff_lambda, diff_subln_g, diff_w_o, mla_w_kv_a, mla_kv_norm_g, mla_w_kv_b,
           mla_w_q_a, mla_q_norm_g, mla_w_q_b, mla_w_o, router_w, router_b, moe_w_gate, moe_w_up,
           moe_w_down, ln_g, ln_b):
    batch, seq, d = x.shape
    t = batch * seq
    xs = x.reshape(t, d)

    wq, wk, wv = jnp.split(diff_w_qkv, 3, axis=-1)
    w_k = wk.astype(BF16)
    w_qv_t = jnp.concatenate([wq * (LOG2E * DIFF_HEAD_DIM ** -0.5), wv], axis=-1).transpose(0, 2, 1).astype(BF16)
    w_o_a = diff_w_o.astype(BF16)
    w_o_b = mla_w_o.astype(BF16)
    slopes = LOG2E * 2.0 ** (-8.0 * jnp.arange(1, DIFF_HEADS + 1, dtype=F32) / DIFF_HEADS)
    slope_hi = slopes.astype(BF16).astype(F32)
    slope_lo = (slopes - slope_hi).astype(BF16).astype(F32)
    rw_t = router_w.T.reshape(N_GROUPS, GROUP_SIZE, d).transpose(1, 0, 2).reshape(N_EXPERTS, d)
    rb_t = router_b.reshape(N_GROUPS, GROUP_SIZE).T.reshape(N_EXPERTS, 1).astype(F32)
    cs, sn, cs_t, sn_t = _rope_tables(seq)
    kv_rope_w = mla_w_kv_a[:, MLA_KV_RANK:]
    w_kv_a = jnp.concatenate([mla_w_kv_a, _swap_halves(kv_rope_w)], axis=1).astype(BF16)
    w_kv_b = mla_w_kv_b.reshape(MLA_KV_RANK, MLA_HEADS, MLA_NOPE_DIM + MLA_V_DIM)
    w_kn = w_kv_b[..., :MLA_NOPE_DIM].reshape(MLA_KV_RANK, -1).astype(BF16)
    w_v_t = w_kv_b[..., MLA_NOPE_DIM:].reshape(MLA_KV_RANK, -1).T.astype(BF16)
    wqb = mla_w_q_b.reshape(-1, MLA_Q_RANK, MLA_HEADS, MLA_NOPE_DIM + MLA_ROPE_DIM)
    w_q_b_t = jnp.concatenate([wqb, _swap_halves(wqb[..., MLA_NOPE_DIM:])], axis=-1).reshape(
        -1, MLA_Q_RANK, MLA_HEADS * MLA_QK_PAD).transpose(0, 2, 1).astype(BF16)
    w_q_a = mla_w_q_a.astype(BF16)

    alibi_fix, diag_mask = _diagonal_tables()
    tri = jnp.triu(jnp.ones((ROUTE_TILE, ROUTE_TILE), BF16), 1)
    n_tiles_max = t // MOE_TILE + N_GROUPS

    kcat = vt_shared = None
    for layer in range(DEPTH):
        if layer < N_A_LAYERS:
            k, qt, vt = _qkv_proj(xs, w_k[layer], w_qv_t[layer])
            attn = _diff_attention(qt, k, vt, slope_hi, slope_lo, diff_lambda[layer],
                                   diff_subln_g[layer].reshape(-1, 1), alibi_fix, batch, seq, _lambda_init(layer))
            w_o = w_o_a[layer]
        else:
            j = layer - N_A_LAYERS
            if j == 0:
                kcat, vt_shared = _mla_kv_proj(xs, w_kv_a, mla_kv_norm_g.reshape(1, -1), w_kn, w_v_t, cs, sn, seq)
            qt = _mla_q_proj(xs, w_q_a[j], mla_q_norm_g[j].reshape(1, -1), w_q_b_t[j], cs_t, sn_t, seq)
            attn = _mla_attention(qt, kcat, vt_shared, diag_mask, batch, seq)
            w_o = w_o_b[j]
        x1, gid, rank, counts = _out_norm_route(attn, w_o, xs, ln_g[layer, 0].reshape(1, -1),
                                                ln_b[layer, 0].reshape(1, -1), rw_t, rb_t, tri)
        dest, tile_group, tile_rows, n_tiles = _tile_plan(gid, rank, counts, n_tiles_max)
        src = _invert_plan(dest, n_tiles_max * MOE_TILE)
        xs = _moe_rows(x1, src, tile_group, tile_rows, n_tiles, moe_w_gate, moe_w_up, moe_w_down,
                       ln_g[layer, 1].reshape(1, -1), ln_b[layer, 1].reshape(1, -1), layer)
    return xs.reshape(batch, seq, d)
```

```python
import functools
import math

import jax
import jax.numpy as jnp
from jax import lax
from jax.experimental import pallas as pl
from jax.experimental.pallas import tpu as pltpu

D_MODEL = 1024
DEPTH = 4
CHUNK = 64
N_A_LAYERS = DEPTH // 2
DIFF_HEAD_DIM = 64
DIFF_V_DIM = 2 * DIFF_HEAD_DIM
DIFF_HEADS = D_MODEL // DIFF_V_DIM
MLA_NOPE_DIM = 128
MLA_ROPE_DIM = 64
MLA_V_DIM = 128
MLA_HEADS = D_MODEL // MLA_V_DIM
MLA_Q_RANK = 384
MLA_KV_RANK = 256
MLA_QK_PAD = 256
ROPE_BASE = 10000.0
N_EXPERTS = 32
N_GROUPS = 8
GROUP_SIZE = N_EXPERTS // N_GROUPS
EXPERT_FF = 256
NORM_EPS = 1e-5
DEEPNORM_ALPHA = (2 * DEPTH) ** 0.25
LOG2E = math.log2(math.e)

LANES = 128
T_TILE = 256
GROUP_TILES = 2
Q_TILE = GROUP_TILES * T_TILE
MASKED = 1e30
V_ROWS = DIFF_V_DIM + 16
ROUTE_TILE = 512
MOE_TILE = 256
PLAN_CHUNK = 2048
ROW_WIDE = D_MODEL + LANES
VMEM_LIMIT = 48 * 1024 * 1024

F32 = jnp.float32
BF16 = jnp.bfloat16


def _lambda_init(layer):
    return 0.8 - 0.6 * math.exp(-0.3 * layer)


def _cparams(sem):
    return pltpu.CompilerParams(dimension_semantics=sem, vmem_limit_bytes=VMEM_LIMIT)


def _dot(a, b):
    return jnp.dot(a, b, preferred_element_type=F32)


def _dot_nt(a, b):
    return lax.dot_general(a, b, (((1,), (1,)), ((), ())), preferred_element_type=F32)


def _rms(x, g):
    return x * lax.rsqrt(jnp.mean(x * x, axis=-1, keepdims=True) + NORM_EPS) * g


def _ln(x, g, b):
    mu = jnp.mean(x, axis=-1, keepdims=True)
    xc = x - mu
    var = jnp.mean(xc * xc, axis=-1, keepdims=True)
    return xc * lax.rsqrt(var + NORM_EPS) * g + b


def _rope_halves(a, cs, sn):
    return a * cs + pltpu.roll(a, MLA_ROPE_DIM, axis=1) * sn


def _qkv_kernel(x_ref, wk_ref, wqv_ref, k_ref, qt_ref, vt_ref):
    xb = x_ref[...].astype(BF16)
    k_ref[...] = _dot(xb, wk_ref[...]).astype(BF16)
    qv = _dot_nt(wqv_ref[...], xb)
    for h in range(DIFF_HEADS):
        qt_ref[0, h] = qv[h * LANES:(h + 1) * LANES].astype(BF16)
        vt_ref[0, h, :DIFF_V_DIM] = qv[D_MODEL + h * LANES:D_MODEL + (h + 1) * LANES].astype(BF16)
        vt_ref[0, h, DIFF_V_DIM:] = jnp.ones((V_ROWS - DIFF_V_DIM, T_TILE), BF16)


def _qkv_proj(x, wk, wqv):
    t, d = x.shape
    nt = t // T_TILE

    def head_major(rows):
        return jax.ShapeDtypeStruct((nt, DIFF_HEADS, rows, T_TILE), BF16)

    def head_block(rows):
        return pl.BlockSpec((1, DIFF_HEADS, rows, T_TILE), lambda i: (i, 0, 0, 0))

    return pl.pallas_call(
        _qkv_kernel,
        out_shape=(jax.ShapeDtypeStruct((t, d), BF16), head_major(LANES), head_major(V_ROWS)),
        grid=(nt,),
        in_specs=[pl.BlockSpec((T_TILE, d), lambda i: (i, 0)),
                  pl.BlockSpec(wk.shape, lambda i: (0, 0)),
                  pl.BlockSpec(wqv.shape, lambda i: (0, 0))],
        out_specs=(pl.BlockSpec((T_TILE, d), lambda i: (i, 0)), head_block(LANES), head_block(V_ROWS)),
        compiler_params=_cparams(("parallel",)),
        name="qkv_proj",
    )(x, wk, wqv)


def _mla_kv_kernel(x_ref, wa_ref, g_ref, wk_ref, wv_ref, cs_ref, sn_ref, kcat_ref, vt_ref):
    kva = _dot(x_ref[...].astype(BF16), wa_ref[...])
    c_kv = _rms(kva[:, :MLA_KV_RANK], g_ref[...]).astype(BF16)
    kr = _rope_halves(kva[:, MLA_KV_RANK:], cs_ref[...], sn_ref[...]).astype(BF16)
    kn = _dot(c_kv, wk_ref[...])
    vt = _dot_nt(wv_ref[...], c_kv)
    for h in range(MLA_HEADS):
        base = h * MLA_QK_PAD
        kcat_ref[:, base:base + MLA_NOPE_DIM] = kn[:, h * MLA_NOPE_DIM:(h + 1) * MLA_NOPE_DIM].astype(BF16)
        kcat_ref[:, base + MLA_NOPE_DIM:base + MLA_QK_PAD] = kr
        vt_ref[0, h, :MLA_V_DIM] = vt[h * MLA_V_DIM:(h + 1) * MLA_V_DIM].astype(BF16)
        vt_ref[0, h, MLA_V_DIM:] = jnp.ones((V_ROWS - MLA_V_DIM, T_TILE), BF16)


def _mla_kv_proj(x, wa, g, wk, wv, cs, sn, seq):
    t, d = x.shape
    nt = t // T_TILE
    ns = seq // T_TILE
    return pl.pallas_call(
        _mla_kv_kernel,
        out_shape=(jax.ShapeDtypeStruct((t, MLA_HEADS * MLA_QK_PAD), BF16),
                   jax.ShapeDtypeStruct((nt, MLA_HEADS, V_ROWS, T_TILE), BF16)),
        grid=(nt,),
        in_specs=[pl.BlockSpec((T_TILE, d), lambda i: (i, 0)),
                  pl.BlockSpec(wa.shape, lambda i: (0, 0)),
                  pl.BlockSpec(g.shape, lambda i: (0, 0)),
                  pl.BlockSpec(wk.shape, lambda i: (0, 0)),
                  pl.BlockSpec(wv.shape, lambda i: (0, 0)),
                  pl.BlockSpec((T_TILE, LANES), lambda i: (i % ns, 0)),
                  pl.BlockSpec((T_TILE, LANES), lambda i: (i % ns, 0))],
        out_specs=(pl.BlockSpec((T_TILE, MLA_HEADS * MLA_QK_PAD), lambda i: (i, 0)),
                   pl.BlockSpec((1, MLA_HEADS, V_ROWS, T_TILE), lambda i: (i, 0, 0, 0))),
        compiler_params=_cparams(("parallel",)),
        name="mla_kv_proj",
    )(x, wa, g, wk, wv, cs, sn)


def _mla_q_kernel(x_ref, wa_ref, g_ref, wb_ref, cst_ref, snt_ref, qt_ref, *, scale):
    qa = _dot(x_ref[...].astype(BF16), wa_ref[...])
    qn = _rms(qa, g_ref[...]).astype(BF16)
    qt = _dot_nt(wb_ref[...], qn) * scale
    cst = cst_ref[...]
    snt = snt_ref[...]
    zero = jnp.zeros((MLA_ROPE_DIM, T_TILE), BF16)
    for h in range(MLA_HEADS):
        base = h * MLA_QK_PAD
        r0 = base + MLA_NOPE_DIM
        rope = qt[r0:r0 + MLA_ROPE_DIM] * cst + qt[r0 + MLA_ROPE_DIM:base + MLA_QK_PAD] * snt
        qt_ref[0, h, 0:MLA_NOPE_DIM] = qt[base:r0].astype(BF16)
        qt_ref[0, h, MLA_NOPE_DIM:MLA_NOPE_DIM + MLA_ROPE_DIM] = rope.astype(BF16)
        qt_ref[0, h, MLA_NOPE_DIM + MLA_ROPE_DIM:MLA_QK_PAD] = zero


def _mla_q_proj(x, wa, g, wb_t, cst, snt, seq):
    t, d = x.shape
    nt = t // T_TILE
    ns = seq // T_TILE
    scale = LOG2E * (MLA_NOPE_DIM + MLA_ROPE_DIM) ** -0.5
    return pl.pallas_call(
        functools.partial(_mla_q_kernel, scale=scale),
        out_shape=jax.ShapeDtypeStruct((nt, MLA_HEADS, MLA_QK_PAD, T_TILE), BF16),
        grid=(nt,),
        in_specs=[pl.BlockSpec((T_TILE, d), lambda i: (i, 0)),
                  pl.BlockSpec(wa.shape, lambda i: (0, 0)),
                  pl.BlockSpec(g.shape, lambda i: (0, 0)),
                  pl.BlockSpec(wb_t.shape, lambda i: (0, 0)),
                  pl.BlockSpec((MLA_ROPE_DIM, T_TILE), lambda i: (0, i % ns)),
                  pl.BlockSpec((MLA_ROPE_DIM, T_TILE), lambda i: (0, i % ns))],
        out_specs=pl.BlockSpec((1, MLA_HEADS, MLA_QK_PAD, T_TILE), lambda i: (i, 0, 0, 0)),
        compiler_params=_cparams(("parallel",)),
        name="mla_q_proj",
    )(x, wa, g, wb_t, cst, snt)


def _consume_group(tiles, m_ref, acc_ref):
    for c, comp_tiles in enumerate(tiles):
        m_group = None
        for _, m_t, _, _ in comp_tiles:
            m_group = m_t if m_group is None else jnp.maximum(m_group, m_t)
        pv = None
        for s_t, _, shift, vt in comp_tiles:
            part = _dot(vt, jnp.exp2(s_t - (m_group - shift)).astype(BF16))
            pv = part if pv is None else pv + part
        m_prev = m_ref[c]
        m_new = jnp.maximum(m_prev, m_group)
        acc_ref[c] = jnp.exp2(m_prev - m_new) * acc_ref[c] + jnp.exp2(m_group - m_new) * pv
        m_ref[c] = m_new


def _init_online_softmax(m_ref, acc_ref):
    m_ref[...] = jnp.full(m_ref.shape, -jnp.inf, F32)
    acc_ref[...] = jnp.zeros(acc_ref.shape, F32)


def _normalised(acc):
    return acc[:DIFF_V_DIM] / acc[DIFF_V_DIM:DIFF_V_DIM + 1]


def _attention_pipeline(n, issue_scores, consume):
    @pl.when(n == 0)
    def _():
        issue_scores(0, 0, True)
        consume(0, 0)

    @pl.when(n > 0)
    def _():
        issue_scores(0, 0, False)

        def pair(p, carry):
            issue_scores(2 * p + 1, 1, False)
            consume(2 * p, 0)
            issue_scores(2 * p + 2, 0, False)
            consume(2 * p + 1, 1)
            return carry

        lax.fori_loop(0, (n - 1) // 2, pair, 0)

        @pl.when(n % 2 == 1)
        def _():
            issue_scores(n, 1, True)
            consume(n - 1, 0)
            consume(n, 1)

        @pl.when(n % 2 == 0)
        def _():
            issue_scores(n - 1, 1, False)
            consume(n - 2, 0)
            issue_scores(n, 0, True)
            consume(n - 1, 1)
            consume(n, 0)


def _diff_attn_kernel(slh_ref, sll_ref, lam_ref, g_ref, dm_ref, qt_ref, k_ref, vt_ref, o_ref,
                      s_buf, max_buf, m_ref, acc_ref, *, lambda_init):
    h = pl.program_id(1)
    i = pl.program_id(2)
    sl_hi = slh_ref[h]
    sl_lo = sll_ref[h]
    slope = sl_hi + sl_lo
    row = lax.broadcasted_iota(jnp.int32, (LANES, Q_TILE), 0)
    qf = jnp.concatenate([qt_ref[u] for u in range(GROUP_TILES)], axis=1).astype(F32)
    slope_rows = jnp.where(row == 0, sl_hi, jnp.where(row == 1, sl_lo, 0.0)).astype(BF16)
    qs_aug = [jnp.concatenate([jnp.where(keep, qf, 0.0).astype(BF16), slope_rows], axis=0)
              for keep in (row < DIFF_HEAD_DIM, row >= DIFF_HEAD_DIM)]
    key_off = lax.broadcasted_iota(jnp.int32, (T_TILE, LANES), 0).astype(F32)
    lane = lax.broadcasted_iota(jnp.int32, (T_TILE, LANES), 1)
    key_cols = jnp.where(lane < 2, key_off, 0.0).astype(BF16)
    _init_online_softmax(m_ref, acc_ref)

    def tile_shift(j):
        return slope * (j * T_TILE - i * Q_TILE).astype(F32)

    def issue_scores(g, slot, diagonal):
        for u in range(GROUP_TILES):
            j = g * GROUP_TILES + u
            off = pl.multiple_of(j * T_TILE, T_TILE)
            k_aug = jnp.concatenate([k_ref[pl.ds(off, T_TILE), :], key_cols], axis=1)
            for c in range(2):
                s_t = _dot(k_aug, qs_aug[c])
                if diagonal:
                    s_t = s_t + slope * dm_ref[u * T_TILE:(u + 1) * T_TILE, :]
                s_buf[slot, 2 * u + c] = s_t
                max_buf[slot, 2 * u + c] = jnp.max(s_t, axis=0, keepdims=True) + tile_shift(j)

    def consume(g, slot):
        tiles = ([], [])
        for u in range(GROUP_TILES):
            j = g * GROUP_TILES + u
            for c in range(2):
                tiles[c].append((s_buf[slot, 2 * u + c], max_buf[slot, 2 * u + c], tile_shift(j), vt_ref[j]))
        _consume_group(tiles, m_ref, acc_ref)

    _attention_pipeline(i, issue_scores, consume)

    lam = lam_ref[...]
    lam_full = (jnp.exp(jnp.sum(lam[0:1] * lam[1:2], axis=1, keepdims=True))
                - jnp.exp(jnp.sum(lam[2:3] * lam[3:4], axis=1, keepdims=True)) + lambda_init)
    o_t = _normalised(acc_ref[0]) - lam_full * _normalised(acc_ref[1])
    inv = lax.rsqrt(jnp.mean(o_t * o_t, axis=0, keepdims=True) + NORM_EPS)
    o_t = o_t * inv * (g_ref[...] * (1.0 - lambda_init))
    o_ref[...] = o_t.T.astype(o_ref.dtype)


def _diagonal_tables():
    rk = jnp.arange(Q_TILE, dtype=jnp.int32)[:, None]
    rq = jnp.arange(Q_TILE, dtype=jnp.int32)[None, :]
    visible = (rk // CHUNK) <= (rq // CHUNK)
    alibi_fix = jnp.where(visible, -2.0 * jnp.maximum(rk - rq, 0).astype(F32), -MASKED)
    mask = jnp.where(visible, 0.0, -MASKED).astype(F32)
    return alibi_fix, mask


def _diff_attention(qt, k, vt, slope_hi, slope_lo, lam, g_col, alibi_fix, batch, seq, lambda_init):
    nq = seq // Q_TILE
    nk = seq // T_TILE
    hh = DIFF_HEADS
    return pl.pallas_call(
        functools.partial(_diff_attn_kernel, lambda_init=lambda_init),
        out_shape=jax.ShapeDtypeStruct((batch * seq, hh * DIFF_V_DIM), BF16),
        grid=(batch, hh, nq),
        in_specs=[pl.BlockSpec(memory_space=pltpu.SMEM),
                  pl.BlockSpec(memory_space=pltpu.SMEM),
                  pl.BlockSpec(lam.shape, lambda b, h, i: (0, 0)),
                  pl.BlockSpec(g_col.shape, lambda b, h, i: (0, 0)),
                  pl.BlockSpec(alibi_fix.shape, lambda b, h, i: (0, 0)),
                  pl.BlockSpec((GROUP_TILES, None, LANES, T_TILE), lambda b, h, i: (b * nq + i, h, 0, 0)),
                  pl.BlockSpec((seq, LANES), lambda b, h, i: (b, h)),
                  pl.BlockSpec((nk, None, V_ROWS, T_TILE), lambda b, h, i: (b, h, 0, 0))],
        out_specs=pl.BlockSpec((Q_TILE, LANES), lambda b, h, i: (b * nq + i, h)),
        scratch_shapes=[pltpu.VMEM((2, 2 * GROUP_TILES, T_TILE, Q_TILE), F32),
                        pltpu.VMEM((2, 2 * GROUP_TILES, 1, Q_TILE), F32),
                        pltpu.VMEM((2, 1, Q_TILE), F32),
                        pltpu.VMEM((2, V_ROWS, Q_TILE), F32)],
        compiler_params=_cparams(("parallel", "parallel", "arbitrary")),
        name="diff_attention",
    )(slope_hi, slope_lo, lam, g_col, alibi_fix, qt, k, vt)


def _mla_attn_kernel(mask_ref, qt_ref, k_ref, vt_ref, o_ref, s_buf, max_buf, m_ref, acc_ref):
    i = pl.program_id(2)
    qt = jnp.concatenate([qt_ref[u] for u in range(GROUP_TILES)], axis=1)
    _init_online_softmax(m_ref, acc_ref)

    def issue_scores(g, slot, diagonal):
        for u in range(GROUP_TILES):
            off = pl.multiple_of((g * GROUP_TILES + u) * T_TILE, T_TILE)
            s_t = _dot(k_ref[pl.ds(off, T_TILE), :], qt)
            if diagonal:
                s_t = s_t + mask_ref[u * T_TILE:(u + 1) * T_TILE, :]
            s_buf[slot, u] = s_t
            max_buf[slot, u] = jnp.max(s_t, axis=0, keepdims=True)

    def consume(g, slot):
        tiles = [(s_buf[slot, u], max_buf[slot, u], 0.0, vt_ref[g * GROUP_TILES + u]) for u in range(GROUP_TILES)]
        _consume_group([tiles], m_ref, acc_ref)

    _attention_pipeline(i, issue_scores, consume)
    o_ref[...] = _normalised(acc_ref[0]).T.astype(o_ref.dtype)


def _mla_attention(qt, kcat, vt, mask, batch, seq):
    nq = seq // Q_TILE
    nk = seq // T_TILE
    return pl.pallas_call(
        _mla_attn_kernel,
        out_shape=jax.ShapeDtypeStruct((batch * seq, MLA_HEADS * MLA_V_DIM), BF16),
        grid=(batch, MLA_HEADS, nq),
        in_specs=[pl.BlockSpec(mask.shape, lambda b, h, i: (0, 0)),
                  pl.BlockSpec((GROUP_TILES, None, MLA_QK_PAD, T_TILE), lambda b, h, i: (b * nq + i, h, 0, 0)),
                  pl.BlockSpec((seq, MLA_QK_PAD), lambda b, h, i: (b, h)),
                  pl.BlockSpec((nk, None, V_ROWS, T_TILE), lambda b, h, i: (b, h, 0, 0))],
        out_specs=pl.BlockSpec((Q_TILE, MLA_V_DIM), lambda b, h, i: (b * nq + i, h)),
        scratch_shapes=[pltpu.VMEM((2, GROUP_TILES, T_TILE, Q_TILE), F32),
                        pltpu.VMEM((2, GROUP_TILES, 1, Q_TILE), F32),
                        pltpu.VMEM((1, 1, Q_TILE), F32),
                        pltpu.VMEM((1, V_ROWS, Q_TILE), F32)],
        compiler_params=_cparams(("parallel", "parallel", "arbitrary")),
        name="mla_attention",
    )(mask, qt, kcat, vt)


def _split_bf16(x):
    hi = x.astype(BF16)
    lo = (x - hi.astype(F32)).astype(BF16)
    return hi, lo


def _route(logits_t, bias_t):
    score = 1.0 / (1.0 + jnp.exp(-logits_t))
    sel = score + bias_t
    a = [sel[N_GROUPS * j:N_GROUPS * (j + 1)] for j in range(GROUP_SIZE)]
    sc = [score[N_GROUPS * j:N_GROUPS * (j + 1)] for j in range(GROUP_SIZE)]
    hi1, lo1 = jnp.maximum(a[0], a[1]), jnp.minimum(a[0], a[1])
    hi2, lo2 = jnp.maximum(a[2], a[3]), jnp.minimum(a[2], a[3])
    group_score = jnp.maximum(hi1, hi2) + jnp.maximum(jnp.minimum(hi1, hi2), jnp.maximum(lo1, lo2))
    gidx = lax.broadcasted_iota(jnp.int32, group_score.shape, 0)
    best = jnp.max(group_score, axis=0, keepdims=True)
    first_best = jnp.min(jnp.where(group_score == best, gidx, N_GROUPS), axis=0, keepdims=True)
    in_group = gidx == first_best
    picked = []
    for j in range(GROUP_SIZE):
        beaten_by = jnp.zeros(a[j].shape, jnp.int32)
        for i in range(GROUP_SIZE):
            if i == j:
                continue
            ahead = (a[i] > a[j]) | ((a[i] == a[j]) if i < j else False)
            beaten_by = beaten_by + ahead.astype(jnp.int32)
        picked.append(jnp.where(beaten_by < 2, sc[j], 0.0))
    denom = (picked[0] + picked[1]) + (picked[2] + picked[3])
    member_w = [jnp.sum(jnp.where(in_group, p / denom, 0.0), axis=0, keepdims=True) for p in picked]
    return in_group, member_w


def _out_norm_route_kernel(a_ref, wo_ref, x_ref, g_ref, b_ref, rw_ref, rb_ref, tri_ref,
                           x1_ref, gid_ref, rank_ref, cnt_ref, seen_ref):
    @pl.when(pl.program_id(0) == 0)
    def _():
        seen_ref[...] = jnp.zeros(seen_ref.shape, F32)

    y = _dot(a_ref[...], wo_ref[...])
    x1 = _ln(DEEPNORM_ALPHA * x_ref[...] + y, g_ref[...], b_ref[...])
    x1_ref[:, :D_MODEL] = x1
    xh, xl = _split_bf16(x1)
    wh, wl = _split_bf16(rw_ref[...])
    logits_t = _dot_nt(wh, xh) + (_dot_nt(wh, xl) + _dot_nt(wl, xh))
    in_group, member_w = _route(logits_t, rb_ref[...])
    tm = logits_t.shape[1]

    row = lax.broadcasted_iota(jnp.int32, (N_GROUPS, tm), 0)
    w_rows = jnp.zeros((N_GROUPS, tm), F32)
    for j in range(GROUP_SIZE):
        w_rows = jnp.where(row == j, member_w[j], w_rows)
    w_rows = jnp.concatenate([w_rows, jnp.zeros((LANES - N_GROUPS, tm), F32)], axis=0)
    x1_ref[:, D_MODEL:] = w_rows.T

    onehot = in_group.astype(F32)
    earlier = _dot(onehot.astype(BF16), tri_ref[...])
    seen = seen_ref[:, 0:1]
    gidx = lax.broadcasted_iota(jnp.int32, onehot.shape, 0)
    gid_ref[...] = jnp.sum(jnp.where(in_group, gidx, 0), axis=0, keepdims=True)
    rank_ref[...] = jnp.sum(onehot * (earlier + seen), axis=0, keepdims=True).astype(jnp.int32)
    seen_new = seen_ref[...] + jnp.sum(onehot, axis=1, keepdims=True)
    seen_ref[...] = seen_new
    cnt_ref[...] = seen_new.astype(jnp.int32)


def _out_norm_route(a, wo, x, g, b, rw_t, rb_t, tri, tm=ROUTE_TILE):
    t, d = x.shape
    return pl.pallas_call(
        _out_norm_route_kernel,
        out_shape=(jax.ShapeDtypeStruct((t, ROW_WIDE), F32),
                   jax.ShapeDtypeStruct((1, t), jnp.int32),
                   jax.ShapeDtypeStruct((1, t), jnp.int32),
                   jax.ShapeDtypeStruct((N_GROUPS, LANES), jnp.int32)),
        grid=(t // tm,),
        in_specs=[pl.BlockSpec((tm, d), lambda i: (i, 0)),
                  pl.BlockSpec((d, d), lambda i: (0, 0)),
                  pl.BlockSpec((tm, d), lambda i: (i, 0)),
                  pl.BlockSpec((1, d), lambda i: (0, 0)),
                  pl.BlockSpec((1, d), lambda i: (0, 0)),
                  pl.BlockSpec((N_EXPERTS, d), lambda i: (0, 0)),
                  pl.BlockSpec((N_EXPERTS, 1), lambda i: (0, 0)),
                  pl.BlockSpec((tm, tm), lambda i: (0, 0))],
        out_specs=(pl.BlockSpec((tm, ROW_WIDE), lambda i: (i, 0)),
                   pl.BlockSpec((1, tm), lambda i: (0, i)),
                   pl.BlockSpec((1, tm), lambda i: (0, i)),
                   pl.BlockSpec((N_GROUPS, LANES), lambda i: (0, 0))),
        scratch_shapes=[pltpu.VMEM((N_GROUPS, LANES), F32)],
        compiler_params=_cparams(("arbitrary",)),
        name="out_norm_route",
    )(a, wo, x, g, b, rw_t, rb_t, tri)


def _tile_plan(gid, rank, counts, n_tiles_max):
    cnt = counts[:, 0]
    padded = ((cnt + MOE_TILE - 1) // MOE_TILE) * MOE_TILE
    ends = jnp.cumsum(padded)
    starts = ends - padded
    n_tiles = ends[-1] // MOE_TILE
    tile_row = jnp.arange(n_tiles_max, dtype=jnp.int32) * MOE_TILE
    tile_group = jnp.sum((tile_row[:, None] >= ends[None, :]).astype(jnp.int32), axis=1)
    last_group = jnp.sum((ends[-1] - 1 >= ends).astype(jnp.int32))
    tile_group = jnp.minimum(tile_group, last_group)
    rows_left = (starts + cnt)[tile_group] - tile_row
    tile_rows = jnp.clip(rows_left, 0, MOE_TILE)
    dest = starts[gid[0]] + rank[0]
    pad_lo = jnp.concatenate([starts + cnt, ends[-1:]])
    pad_hi = jnp.concatenate([ends, jnp.full((1,), n_tiles_max * MOE_TILE, ends.dtype)])
    return (dest.astype(jnp.int32), pad_lo.astype(jnp.int32), pad_hi.astype(jnp.int32),
            tile_group.astype(jnp.int32), tile_rows.astype(jnp.int32), n_tiles.reshape(1).astype(jnp.int32))


def _invert_plan_kernel(pad_lo_ref, pad_hi_ref, dest_ref, src_ref):
    step = pl.program_id(0)

    @pl.when(step == 0)
    def _():
        def clear(p, carry):
            src_ref[p] = 0
            return carry
        for g in range(N_GROUPS + 1):
            lax.fori_loop(pad_lo_ref[g], pad_hi_ref[g], clear, 0)

    def place(r, carry):
        src_ref[dest_ref[0, r]] = step * PLAN_CHUNK + r
        return carry

    lax.fori_loop(0, PLAN_CHUNK, place, 0, unroll=8)


def _invert_plan(dest, pad_lo, pad_hi, n_rows):
    t = dest.shape[0]
    chunks = t // PLAN_CHUNK
    return pl.pallas_call(
        _invert_plan_kernel,
        out_shape=jax.ShapeDtypeStruct((n_rows,), jnp.int32),
        grid=(chunks,),
        in_specs=[pl.BlockSpec(memory_space=pltpu.SMEM),
                  pl.BlockSpec(memory_space=pltpu.SMEM),
                  pl.BlockSpec((None, 1, PLAN_CHUNK), lambda i: (i, 0, 0), memory_space=pltpu.SMEM)],
        out_specs=pl.BlockSpec((n_rows,), lambda i: (0,), memory_space=pltpu.SMEM),
        compiler_params=_cparams(("arbitrary",)),
        name="invert_plan",
    )(pad_lo, pad_hi, dest.reshape(chunks, 1, PLAN_CHUNK))


def _row_copy(src, dst, src_row, dst_row, sem):
    return pltpu.make_async_copy(src.at[pl.ds(src_row, 1)], dst.at[pl.ds(dst_row, 1)], sem)


def _for_rows(n_rows, fn):
    def chunk(q, carry):
        for u in range(8):
            fn(q * 8 + u)
        return carry

    def single(r, carry):
        fn(r)
        return carry

    full = n_rows // 8
    lax.fori_loop(0, full, chunk, 0)
    lax.fori_loop(full * 8, n_rows, single, 0)


def _moe_rows_kernel(tile_group_ref, tile_rows_ref, n_tiles_ref, src_ref, src_next_ref, x_hbm,
                     wg_ref, wu_ref, wd_ref, g_ref, b_ref, out_hbm, xbuf, obuf, wg_bf, wu_bf, wd_bf, gsem, ssem):
    k = pl.program_id(0)
    n_tiles = n_tiles_ref[0]
    slot = k % 2

    def row_copies(tile, one_start, one_wait, wait_all, start):
        n_rows = tile_rows_ref[tile]

        @pl.when(n_rows == MOE_TILE)
        def _():
            if start:
                for r in range(MOE_TILE):
                    one_start(r)
            else:
                wait_all()

        @pl.when(n_rows != MOE_TILE)
        def _():
            _for_rows(n_rows, one_start if start else one_wait)

    def gather(tile, rows_ref, buf_slot, start):
        dst = xbuf.at[buf_slot]
        sem = gsem.at[buf_slot]
        row_copies(tile,
                   lambda r: _row_copy(x_hbm, dst, rows_ref[0, r], r, sem).start(),
                   lambda r: _row_copy(x_hbm, dst, 0, 0, sem).wait(),
                   lambda: pltpu.make_async_copy(x_hbm.at[pl.ds(0, MOE_TILE)], dst, sem).wait(),
                   start)

    def scatter(tile, rows_ref, buf_slot, start):
        src = obuf.at[buf_slot]
        sem = ssem.at[buf_slot]
        row_copies(tile,
                   lambda r: _row_copy(src, out_hbm, r, rows_ref[0, r], sem).start(),
                   lambda r: _row_copy(src, out_hbm, 0, 0, sem).wait(),
                   lambda: pltpu.make_async_copy(src, out_hbm.at[pl.ds(0, MOE_TILE)], sem).wait(),
                   start)

    @pl.when(k == 0)
    def _():
        xbuf[...] = jnp.zeros(xbuf.shape, F32)
        gather(0, src_ref, 0, True)

    @pl.when(k + 1 < n_tiles)
    def _():
        gather(k + 1, src_next_ref, 1 - slot, True)

    @pl.when(k < n_tiles)
    def _():
        gather(k, src_ref, slot, False)

        @pl.when(k >= 2)
        def _():
            scatter(k - 2, src_ref, slot, False)

        @pl.when((k == 0) | (tile_group_ref[k] != tile_group_ref[jnp.maximum(k - 1, 0)]))
        def _():
            for j in range(GROUP_SIZE):
                cols = slice(j * EXPERT_FF, (j + 1) * EXPERT_FF)
                wg_bf[:, cols] = wg_ref[j].astype(BF16)
                wu_bf[:, cols] = wu_ref[j].astype(BF16)
                wd_bf[cols, :] = wd_ref[j].astype(BF16)

        x = xbuf[slot, :, :D_MODEL]
        cw = xbuf[slot, :, D_MODEL:]
        xb = x.astype(BF16)
        hg = _dot(xb, wg_bf[...])
        hu = _dot(xb, wu_bf[...])
        cw_wide = jnp.concatenate([jnp.broadcast_to(cw[:, j:j + 1], (MOE_TILE, EXPERT_FF))
                                   for j in range(GROUP_SIZE)], axis=1)
        hid = (hg / (1.0 + jnp.exp(-hg))) * hu * cw_wide
        y = _dot(hid.astype(BF16), wd_bf[...])
        obuf[slot] = _ln(DEEPNORM_ALPHA * x + y, g_ref[...], b_ref[...])
        scatter(k, src_ref, slot, True)

    @pl.when(k == n_tiles - 1)
    def _():
        @pl.when(k >= 1)
        def _():
            scatter(k - 1, src_ref, 1 - slot, False)
        scatter(k, src_ref, slot, False)


def _moe_rows(x_rows, src, tile_group, tile_rows, n_tiles, wg, wu, wd, g, b, layer):
    t = x_rows.shape[0]
    n_tiles_max = src.shape[0] // MOE_TILE
    d, gs, ff = D_MODEL, GROUP_SIZE, EXPERT_FF
    src3 = src.reshape(n_tiles_max, 1, MOE_TILE)
    grid_spec = pltpu.PrefetchScalarGridSpec(
        num_scalar_prefetch=3,
        grid=(n_tiles_max,),
        in_specs=[pl.BlockSpec((None, 1, MOE_TILE), lambda k, tg, tr, nt: (k, 0, 0), memory_space=pltpu.SMEM),
                  pl.BlockSpec((None, 1, MOE_TILE), lambda k, tg, tr, nt: (jnp.minimum(k + 1, n_tiles_max - 1), 0, 0),
                               memory_space=pltpu.SMEM),
                  pl.BlockSpec(memory_space=pl.ANY),
                  pl.BlockSpec((None, gs, d, ff), lambda k, tg, tr, nt: (layer, tg[k], 0, 0)),
                  pl.BlockSpec((None, gs, d, ff), lambda k, tg, tr, nt: (layer, tg[k], 0, 0)),
                  pl.BlockSpec((None, gs, ff, d), lambda k, tg, tr, nt: (layer, tg[k], 0, 0)),
                  pl.BlockSpec((1, d), lambda k, tg, tr, nt: (0, 0)),
                  pl.BlockSpec((1, d), lambda k, tg, tr, nt: (0, 0))],
        out_specs=pl.BlockSpec(memory_space=pl.ANY),
        scratch_shapes=[pltpu.VMEM((2, MOE_TILE, ROW_WIDE), F32), pltpu.VMEM((2, MOE_TILE, d), F32),
                        pltpu.VMEM((d, gs * ff), BF16), pltpu.VMEM((d, gs * ff), BF16), pltpu.VMEM((gs * ff, d), BF16),
                        pltpu.SemaphoreType.DMA((2,)), pltpu.SemaphoreType.DMA((2,))])
    return pl.pallas_call(
        _moe_rows_kernel,
        out_shape=jax.ShapeDtypeStruct((t, d), F32),
        grid_spec=grid_spec,
        compiler_params=_cparams(("arbitrary",)),
        name="grouped_moe",
    )(tile_group, tile_rows, n_tiles, src3, src3, x_rows, wg, wu, wd, g, b)


def _rope_tables(seq):
    inv_freq = ROPE_BASE ** (-jnp.arange(0, MLA_ROPE_DIM, 2, dtype=F32) / MLA_ROPE_DIM)
    ang = jnp.arange(seq, dtype=F32)[:, None] * inv_freq[None, :]
    cos, sin = jnp.cos(ang), jnp.sin(ang)
    cos2 = jnp.concatenate([cos, cos], axis=1)
    sin2 = jnp.concatenate([-sin, sin], axis=1)
    zero = jnp.zeros((seq, LANES - MLA_ROPE_DIM), F32)
    return (jnp.concatenate([cos2, zero], axis=1), jnp.concatenate([sin2, zero], axis=1), cos2.T, sin2.T)


def _swap_halves(w):
    half = MLA_ROPE_DIM // 2
    return jnp.concatenate([w[..., half:], w[..., :half]], axis=-1)


def kernel(x, diff_w_qkv, diff_lambda, diff_subln_g, diff_w_o, mla_w_kv_a, mla_kv_norm_g, mla_w_kv_b,
           mla_w_q_a, mla_q_norm_g, mla_w_q_b, mla_w_o, router_w, router_b, moe_w_gate, moe_w_up,
           moe_w_down, ln_g, ln_b):
    batch, seq, d = x.shape
    t = batch * seq
    xs = x.reshape(t, d)

    wq, wk, wv = jnp.split(diff_w_qkv, 3, axis=-1)
    w_k = wk.astype(BF16)
    w_qv_t = jnp.concatenate([wq * (LOG2E * DIFF_HEAD_DIM ** -0.5), wv], axis=-1).transpose(0, 2, 1).astype(BF16)
    w_o_a = diff_w_o.astype(BF16)
    w_o_b = mla_w_o.astype(BF16)
    slopes = LOG2E * 2.0 ** (-8.0 * jnp.arange(1, DIFF_HEADS + 1, dtype=F32) / DIFF_HEADS)
    slope_hi = slopes.astype(BF16).astype(F32)
    slope_lo = (slopes - slope_hi).astype(BF16).astype(F32)
    rw_t = router_w.T.reshape(N_GROUPS, GROUP_SIZE, d).transpose(1, 0, 2).reshape(N_EXPERTS, d)
    rb_t = router_b.reshape(N_GROUPS, GROUP_SIZE).T.reshape(N_EXPERTS, 1).astype(F32)
    cs, sn, cs_t, sn_t = _rope_tables(seq)
    kv_rope_w = mla_w_kv_a[:, MLA_KV_RANK:]
    w_kv_a = jnp.concatenate([mla_w_kv_a, _swap_halves(kv_rope_w)], axis=1).astype(BF16)
    w_kv_b = mla_w_kv_b.reshape(MLA_KV_RANK, MLA_HEADS, MLA_NOPE_DIM + MLA_V_DIM)
    w_kn = w_kv_b[..., :MLA_NOPE_DIM].reshape(MLA_KV_RANK, -1).astype(BF16)
    w_v_t = w_kv_b[..., MLA_NOPE_DIM:].reshape(MLA_KV_RANK, -1).T.astype(BF16)
    wqb = mla_w_q_b.reshape(-1, MLA_Q_RANK, MLA_HEADS, MLA_NOPE_DIM + MLA_ROPE_DIM)
    w_q_b_t = jnp.concatenate([wqb, _swap_halves(wqb[..., MLA_NOPE_DIM:])], axis=-1).reshape(
        -1, MLA_Q_RANK, MLA_HEADS * MLA_QK_PAD).transpose(0, 2, 1).astype(BF16)
    w_q_a = mla_w_q_a.astype(BF16)

    alibi_fix, diag_mask = _diagonal_tables()
    tri = jnp.triu(jnp.ones((ROUTE_TILE, ROUTE_TILE), BF16), 1)
    n_tiles_max = t // MOE_TILE + N_GROUPS

    kcat = vt_shared = None
    for layer in range(DEPTH):
        if layer < N_A_LAYERS:
            k, qt, vt = _qkv_proj(xs, w_k[layer], w_qv_t[layer])
            attn = _diff_attention(qt, k, vt, slope_hi, slope_lo, diff_lambda[layer],
                                   diff_subln_g[layer].reshape(-1, 1), alibi_fix, batch, seq, _lambda_init(layer))
            w_o = w_o_a[layer]
        else:
            j = layer - N_A_LAYERS
            if j == 0:
                kcat, vt_shared = _mla_kv_proj(xs, w_kv_a, mla_kv_norm_g.reshape(1, -1), w_kn, w_v_t, cs, sn, seq)
            qt = _mla_q_proj(xs, w_q_a[j], mla_q_norm_g[j].reshape(1, -1), w_q_b_t[j], cs_t, sn_t, seq)
            attn = _mla_attention(qt, kcat, vt_shared, diag_mask, batch, seq)
            w_o = w_o_b[j]
        x1, gid, rank, counts = _out_norm_route(attn, w_o, xs, ln_g[layer, 0].reshape(1, -1),
                                                ln_b[layer, 0].reshape(1, -1), rw_t, rb_t, tri)
        dest, pad_lo, pad_hi, tile_group, tile_rows, n_tiles = _tile_plan(gid, rank, counts, n_tiles_max)
        src = _invert_plan(dest, pad_lo, pad_hi, n_tiles_max * MOE_TILE)
        xs = _moe_rows(x1, src, tile_group, tile_rows, n_tiles, moe_w_gate, moe_w_up, moe_w_down,
                       ln_g[layer, 1].reshape(1, -1), ln_b[layer, 1].reshape(1, -1), layer)
    return xs.reshape(batch, seq, d)
```

```python
import functools
import math

import jax
import jax.numpy as jnp
from jax import lax
from jax.experimental import pallas as pl
from jax.experimental.pallas import tpu as pltpu

D_MODEL = 1024
DEPTH = 4
CHUNK = 64
N_A_LAYERS = DEPTH // 2
DIFF_HEAD_DIM = 64
DIFF_V_DIM = 2 * DIFF_HEAD_DIM
DIFF_HEADS = D_MODEL // DIFF_V_DIM
MLA_NOPE_DIM = 128
MLA_ROPE_DIM = 64
MLA_V_DIM = 128
MLA_HEADS = D_MODEL // MLA_V_DIM
MLA_Q_RANK = 384
MLA_KV_RANK = 256
MLA_QK_PAD = 256
ROPE_BASE = 10000.0
N_EXPERTS = 32
N_GROUPS = 8
GROUP_SIZE = N_EXPERTS // N_GROUPS
EXPERT_FF = 256
NORM_EPS = 1e-5
DEEPNORM_ALPHA = (2 * DEPTH) ** 0.25
LOG2E = math.log2(math.e)

LANES = 128
T_TILE = 256
GROUP_TILES = 2
Q_TILE = GROUP_TILES * T_TILE
MASKED = 1e30
V_ROWS = DIFF_V_DIM + 16
PROJ_TILES = 2
PROJ_ROWS = PROJ_TILES * T_TILE
ROUTE_TILE = 1024
MOE_TILE = 256
PLAN_CHUNK = 2048
ROW_WIDE = D_MODEL + LANES
VMEM_LIMIT = 48 * 1024 * 1024

F32 = jnp.float32
BF16 = jnp.bfloat16


def _lambda_init(layer):
    return 0.8 - 0.6 * math.exp(-0.3 * layer)


def _cparams(sem):
    return pltpu.CompilerParams(dimension_semantics=sem, vmem_limit_bytes=VMEM_LIMIT)


def _dot(a, b):
    return jnp.dot(a, b, preferred_element_type=F32)


def _dot_nt(a, b):
    return lax.dot_general(a, b, (((1,), (1,)), ((), ())), preferred_element_type=F32)


def _rms(x, g):
    return x * lax.rsqrt(jnp.mean(x * x, axis=-1, keepdims=True) + NORM_EPS) * g


def _ln(x, g, b):
    mu = jnp.mean(x, axis=-1, keepdims=True)
    xc = x - mu
    var = jnp.mean(xc * xc, axis=-1, keepdims=True)
    return xc * lax.rsqrt(var + NORM_EPS) * g + b


def _rope_halves(a, cs, sn):
    return a * cs + pltpu.roll(a, MLA_ROPE_DIM, axis=1) * sn


def _qkv_kernel(x_ref, wk_ref, wqv_ref, k_ref, qt_ref, vt_ref):
    xb = x_ref[...].astype(BF16)
    k_ref[...] = _dot(xb, wk_ref[...]).astype(BF16)
    qv = _dot_nt(wqv_ref[...], xb)
    for h in range(DIFF_HEADS):
        qt_ref[0, h] = qv[h * LANES:(h + 1) * LANES].astype(BF16)
        vt_ref[0, h, :DIFF_V_DIM] = qv[D_MODEL + h * LANES:D_MODEL + (h + 1) * LANES].astype(BF16)
        vt_ref[0, h, DIFF_V_DIM:] = jnp.ones((V_ROWS - DIFF_V_DIM, T_TILE), BF16)


def _qkv_proj(x, wk, wqv):
    t, d = x.shape
    nt = t // T_TILE

    def head_major(rows):
        return jax.ShapeDtypeStruct((nt, DIFF_HEADS, rows, T_TILE), BF16)

    def head_block(rows):
        return pl.BlockSpec((1, DIFF_HEADS, rows, T_TILE), lambda i: (i, 0, 0, 0))

    return pl.pallas_call(
        _qkv_kernel,
        out_shape=(jax.ShapeDtypeStruct((t, d), BF16), head_major(LANES), head_major(V_ROWS)),
        grid=(nt,),
        in_specs=[pl.BlockSpec((T_TILE, d), lambda i: (i, 0)),
                  pl.BlockSpec(wk.shape, lambda i: (0, 0)),
                  pl.BlockSpec(wqv.shape, lambda i: (0, 0))],
        out_specs=(pl.BlockSpec((T_TILE, d), lambda i: (i, 0)), head_block(LANES), head_block(V_ROWS)),
        compiler_params=_cparams(("parallel",)),
        name="qkv_proj",
    )(x, wk, wqv)


def _mla_kv_kernel(x_ref, wa_ref, g_ref, wk_ref, wv_ref, cs_ref, sn_ref, kcat_ref, vt_ref):
    kva = _dot(x_ref[...].astype(BF16), wa_ref[...])
    c_kv = _rms(kva[:, :MLA_KV_RANK], g_ref[...]).astype(BF16)
    kr = _rope_halves(kva[:, MLA_KV_RANK:], cs_ref[...], sn_ref[...]).astype(BF16)
    kn = _dot(c_kv, wk_ref[...])
    vt = _dot_nt(wv_ref[...], c_kv)
    for h in range(MLA_HEADS):
        base = h * MLA_QK_PAD
        kcat_ref[:, base:base + MLA_NOPE_DIM] = kn[:, h * MLA_NOPE_DIM:(h + 1) * MLA_NOPE_DIM].astype(BF16)
        kcat_ref[:, base + MLA_NOPE_DIM:base + MLA_QK_PAD] = kr
        for u in range(PROJ_TILES):
            cols = slice(u * T_TILE, (u + 1) * T_TILE)
            vt_ref[u, h, :MLA_V_DIM] = vt[h * MLA_V_DIM:(h + 1) * MLA_V_DIM, cols].astype(BF16)
            vt_ref[u, h, MLA_V_DIM:] = jnp.ones((V_ROWS - MLA_V_DIM, T_TILE), BF16)


def _mla_kv_proj(x, wa, g, wk, wv, cs, sn, seq):
    t, d = x.shape
    ns = seq // PROJ_ROWS
    return pl.pallas_call(
        _mla_kv_kernel,
        out_shape=(jax.ShapeDtypeStruct((t, MLA_HEADS * MLA_QK_PAD), BF16),
                   jax.ShapeDtypeStruct((t // T_TILE, MLA_HEADS, V_ROWS, T_TILE), BF16)),
        grid=(t // PROJ_ROWS,),
        in_specs=[pl.BlockSpec((PROJ_ROWS, d), lambda i: (i, 0)),
                  pl.BlockSpec(wa.shape, lambda i: (0, 0)),
                  pl.BlockSpec(g.shape, lambda i: (0, 0)),
                  pl.BlockSpec(wk.shape, lambda i: (0, 0)),
                  pl.BlockSpec(wv.shape, lambda i: (0, 0)),
                  pl.BlockSpec((PROJ_ROWS, LANES), lambda i: (i % ns, 0)),
                  pl.BlockSpec((PROJ_ROWS, LANES), lambda i: (i % ns, 0))],
        out_specs=(pl.BlockSpec((PROJ_ROWS, MLA_HEADS * MLA_QK_PAD), lambda i: (i, 0)),
                   pl.BlockSpec((PROJ_TILES, MLA_HEADS, V_ROWS, T_TILE), lambda i: (i, 0, 0, 0))),
        compiler_params=_cparams(("parallel",)),
        name="mla_kv_proj",
    )(x, wa, g, wk, wv, cs, sn)


def _mla_q_kernel(x_ref, wa_ref, g_ref, wb_ref, cst_ref, snt_ref, qt_ref, *, scale):
    qa = _dot(x_ref[...].astype(BF16), wa_ref[...])
    qn = _rms(qa, g_ref[...]).astype(BF16)
    qt = _dot_nt(wb_ref[...], qn) * scale
    cst = cst_ref[...]
    snt = snt_ref[...]
    zero = jnp.zeros((MLA_ROPE_DIM, T_TILE), BF16)
    for h in range(MLA_HEADS):
        base = h * MLA_QK_PAD
        r0 = base + MLA_NOPE_DIM
        rope = qt[r0:r0 + MLA_ROPE_DIM] * cst + qt[r0 + MLA_ROPE_DIM:base + MLA_QK_PAD] * snt
        for u in range(PROJ_TILES):
            cols = slice(u * T_TILE, (u + 1) * T_TILE)
            qt_ref[u, h, 0:MLA_NOPE_DIM] = qt[base:r0, cols].astype(BF16)
            qt_ref[u, h, MLA_NOPE_DIM:MLA_NOPE_DIM + MLA_ROPE_DIM] = rope[:, cols].astype(BF16)
            qt_ref[u, h, MLA_NOPE_DIM + MLA_ROPE_DIM:MLA_QK_PAD] = zero


def _mla_q_proj(x, wa, g, wb_t, cst, snt, seq):
    t, d = x.shape
    ns = seq // PROJ_ROWS
    scale = LOG2E * (MLA_NOPE_DIM + MLA_ROPE_DIM) ** -0.5
    return pl.pallas_call(
        functools.partial(_mla_q_kernel, scale=scale),
        out_shape=jax.ShapeDtypeStruct((t // T_TILE, MLA_HEADS, MLA_QK_PAD, T_TILE), BF16),
        grid=(t // PROJ_ROWS,),
        in_specs=[pl.BlockSpec((PROJ_ROWS, d), lambda i: (i, 0)),
                  pl.BlockSpec(wa.shape, lambda i: (0, 0)),
                  pl.BlockSpec(g.shape, lambda i: (0, 0)),
                  pl.BlockSpec(wb_t.shape, lambda i: (0, 0)),
                  pl.BlockSpec((MLA_ROPE_DIM, PROJ_ROWS), lambda i: (0, i % ns)),
                  pl.BlockSpec((MLA_ROPE_DIM, PROJ_ROWS), lambda i: (0, i % ns))],
        out_specs=pl.BlockSpec((PROJ_TILES, MLA_HEADS, MLA_QK_PAD, T_TILE), lambda i: (i, 0, 0, 0)),
        compiler_params=_cparams(("parallel",)),
        name="mla_q_proj",
    )(x, wa, g, wb_t, cst, snt)


def _consume_group(tiles, m_ref, acc_ref):
    for c, comp_tiles in enumerate(tiles):
        m_group = None
        for _, m_t, _, _ in comp_tiles:
            m_group = m_t if m_group is None else jnp.maximum(m_group, m_t)
        pv = None
        for s_t, _, shift, vt in comp_tiles:
            part = _dot(vt, jnp.exp2(s_t - (m_group - shift)).astype(BF16))
            pv = part if pv is None else pv + part
        m_prev = m_ref[c]
        m_new = jnp.maximum(m_prev, m_group)
        acc_ref[c] = jnp.exp2(m_prev - m_new) * acc_ref[c] + jnp.exp2(m_group - m_new) * pv
        m_ref[c] = m_new


def _init_online_softmax(m_ref, acc_ref):
    m_ref[...] = jnp.full(m_ref.shape, -jnp.inf, F32)
    acc_ref[...] = jnp.zeros(acc_ref.shape, F32)


def _normalised(acc):
    return acc[:DIFF_V_DIM] / acc[DIFF_V_DIM:DIFF_V_DIM + 1]


def _attention_pipeline(n, issue_scores, consume):
    @pl.when(n == 0)
    def _():
        issue_scores(0, 0, True)
        consume(0, 0)

    @pl.when(n > 0)
    def _():
        issue_scores(0, 0, False)

        def pair(p, carry):
            issue_scores(2 * p + 1, 1, False)
            consume(2 * p, 0)
            issue_scores(2 * p + 2, 0, False)
            consume(2 * p + 1, 1)
            return carry

        lax.fori_loop(0, (n - 1) // 2, pair, 0)

        @pl.when(n % 2 == 1)
        def _():
            issue_scores(n, 1, True)
            consume(n - 1, 0)
            consume(n, 1)

        @pl.when(n % 2 == 0)
        def _():
            issue_scores(n - 1, 1, False)
            consume(n - 2, 0)
            issue_scores(n, 0, True)
            consume(n - 1, 1)
            consume(n, 0)


def _diff_attn_kernel(slh_ref, sll_ref, lam_ref, g_ref, dm_ref, qt_ref, k_ref, vt_ref, o_ref,
                      s_buf, max_buf, m_ref, acc_ref, *, lambda_init):
    h = pl.program_id(1)
    i = pl.program_id(2)
    sl_hi = slh_ref[h]
    sl_lo = sll_ref[h]
    slope = sl_hi + sl_lo
    row = lax.broadcasted_iota(jnp.int32, (LANES, Q_TILE), 0)
    qf = jnp.concatenate([qt_ref[u] for u in range(GROUP_TILES)], axis=1).astype(F32)
    slope_rows = jnp.where(row == 0, sl_hi, jnp.where(row == 1, sl_lo, 0.0)).astype(BF16)
    qs_aug = [jnp.concatenate([jnp.where(keep, qf, 0.0).astype(BF16), slope_rows], axis=0)
              for keep in (row < DIFF_HEAD_DIM, row >= DIFF_HEAD_DIM)]
    key_off = lax.broadcasted_iota(jnp.int32, (T_TILE, LANES), 0).astype(F32)
    lane = lax.broadcasted_iota(jnp.int32, (T_TILE, LANES), 1)
    key_cols = jnp.where(lane < 2, key_off, 0.0).astype(BF16)
    _init_online_softmax(m_ref, acc_ref)

    def tile_shift(j):
        return slope * (j * T_TILE - i * Q_TILE).astype(F32)

    def issue_scores(g, slot, diagonal):
        for u in range(GROUP_TILES):
            j = g * GROUP_TILES + u
            off = pl.multiple_of(j * T_TILE, T_TILE)
            k_aug = jnp.concatenate([k_ref[pl.ds(off, T_TILE), :], key_cols], axis=1)
            for c in range(2):
                s_t = _dot(k_aug, qs_aug[c])
                if diagonal:
                    s_t = s_t + slope * dm_ref[u * T_TILE:(u + 1) * T_TILE, :]
                s_buf[slot, 2 * u + c] = s_t
                max_buf[slot, 2 * u + c] = jnp.max(s_t, axis=0, keepdims=True) + tile_shift(j)

    def consume(g, slot):
        tiles = ([], [])
        for u in range(GROUP_TILES):
            j = g * GROUP_TILES + u
            for c in range(2):
                tiles[c].append((s_buf[slot, 2 * u + c], max_buf[slot, 2 * u + c], tile_shift(j), vt_ref[j]))
        _consume_group(tiles, m_ref, acc_ref)

    _attention_pipeline(i, issue_scores, consume)

    lam = lam_ref[...]
    lam_full = (jnp.exp(jnp.sum(lam[0:1] * lam[1:2], axis=1, keepdims=True))
                - jnp.exp(jnp.sum(lam[2:3] * lam[3:4], axis=1, keepdims=True)) + lambda_init)
    o_t = _normalised(acc_ref[0]) - lam_full * _normalised(acc_ref[1])
    inv = lax.rsqrt(jnp.mean(o_t * o_t, axis=0, keepdims=True) + NORM_EPS)
    o_t = o_t * inv * (g_ref[...] * (1.0 - lambda_init))
    o_ref[...] = o_t.T.astype(o_ref.dtype)


def _diagonal_tables():
    rk = jnp.arange(Q_TILE, dtype=jnp.int32)[:, None]
    rq = jnp.arange(Q_TILE, dtype=jnp.int32)[None, :]
    visible = (rk // CHUNK) <= (rq // CHUNK)
    alibi_fix = jnp.where(visible, -2.0 * jnp.maximum(rk - rq, 0).astype(F32), -MASKED)
    mask = jnp.where(visible, 0.0, -MASKED).astype(F32)
    return alibi_fix, mask


def _diff_attention(qt, k, vt, slope_hi, slope_lo, lam, g_col, alibi_fix, batch, seq, lambda_init):
    nq = seq // Q_TILE
    nk = seq // T_TILE
    hh = DIFF_HEADS
    return pl.pallas_call(
        functools.partial(_diff_attn_kernel, lambda_init=lambda_init),
        out_shape=jax.ShapeDtypeStruct((batch * seq, hh * DIFF_V_DIM), BF16),
        grid=(batch, hh, nq),
        in_specs=[pl.BlockSpec(memory_space=pltpu.SMEM),
                  pl.BlockSpec(memory_space=pltpu.SMEM),
                  pl.BlockSpec(lam.shape, lambda b, h, i: (0, 0)),
                  pl.BlockSpec(g_col.shape, lambda b, h, i: (0, 0)),
                  pl.BlockSpec(alibi_fix.shape, lambda b, h, i: (0, 0)),
                  pl.BlockSpec((GROUP_TILES, None, LANES, T_TILE), lambda b, h, i: (b * nq + i, h, 0, 0)),
                  pl.BlockSpec((seq, LANES), lambda b, h, i: (b, h)),
                  pl.BlockSpec((nk, None, V_ROWS, T_TILE), lambda b, h, i: (b, h, 0, 0))],
        out_specs=pl.BlockSpec((Q_TILE, LANES), lambda b, h, i: (b * nq + i, h)),
        scratch_shapes=[pltpu.VMEM((2, 2 * GROUP_TILES, T_TILE, Q_TILE), F32),
                        pltpu.VMEM((2, 2 * GROUP_TILES, 1, Q_TILE), F32),
                        pltpu.VMEM((2, 1, Q_TILE), F32),
                        pltpu.VMEM((2, V_ROWS, Q_TILE), F32)],
        compiler_params=_cparams(("parallel", "parallel", "arbitrary")),
        name="diff_attention",
    )(slope_hi, slope_lo, lam, g_col, alibi_fix, qt, k, vt)


def _mla_attn_kernel(mask_ref, qt_ref, k_ref, vt_ref, o_ref, s_buf, max_buf, m_ref, acc_ref):
    i = pl.program_id(2)
    qt = jnp.concatenate([qt_ref[u] for u in range(GROUP_TILES)], axis=1)
    _init_online_softmax(m_ref, acc_ref)

    def issue_scores(g, slot, diagonal):
        for u in range(GROUP_TILES):
            off = pl.multiple_of((g * GROUP_TILES + u) * T_TILE, T_TILE)
            s_t = _dot(k_ref[pl.ds(off, T_TILE), :], qt)
            if diagonal:
                s_t = s_t + mask_ref[u * T_TILE:(u + 1) * T_TILE, :]
            s_buf[slot, u] = s_t
            max_buf[slot, u] = jnp.max(s_t, axis=0, keepdims=True)

    def consume(g, slot):
        tiles = [(s_buf[slot, u], max_buf[slot, u], 0.0, vt_ref[g * GROUP_TILES + u]) for u in range(GROUP_TILES)]
        _consume_group([tiles], m_ref, acc_ref)

    _attention_pipeline(i, issue_scores, consume)
    o_ref[...] = _normalised(acc_ref[0]).T.astype(o_ref.dtype)


def _mla_attention(qt, kcat, vt, mask, batch, seq):
    nq = seq // Q_TILE
    nk = seq // T_TILE
    return pl.pallas_call(
        _mla_attn_kernel,
        out_shape=jax.ShapeDtypeStruct((batch * seq, MLA_HEADS * MLA_V_DIM), BF16),
        grid=(batch, MLA_HEADS, nq),
        in_specs=[pl.BlockSpec(mask.shape, lambda b, h, i: (0, 0)),
                  pl.BlockSpec((GROUP_TILES, None, MLA_QK_PAD, T_TILE), lambda b, h, i: (b * nq + i, h, 0, 0)),
                  pl.BlockSpec((seq, MLA_QK_PAD), lambda b, h, i: (b, h)),
                  pl.BlockSpec((nk, None, V_ROWS, T_TILE), lambda b, h, i: (b, h, 0, 0))],
        out_specs=pl.BlockSpec((Q_TILE, MLA_V_DIM), lambda b, h, i: (b * nq + i, h)),
        scratch_shapes=[pltpu.VMEM((2, GROUP_TILES, T_TILE, Q_TILE), F32),
                        pltpu.VMEM((2, GROUP_TILES, 1, Q_TILE), F32),
                        pltpu.VMEM((1, 1, Q_TILE), F32),
                        pltpu.VMEM((1, V_ROWS, Q_TILE), F32)],
        compiler_params=_cparams(("parallel", "parallel", "arbitrary")),
        name="mla_attention",
    )(mask, qt, kcat, vt)


def _split_bf16(x):
    hi = x.astype(BF16)
    lo = (x - hi.astype(F32)).astype(BF16)
    return hi, lo


def _route(logits_t, bias_t):
    score = 1.0 / (1.0 + jnp.exp(-logits_t))
    sel = score + bias_t
    a = [sel[N_GROUPS * j:N_GROUPS * (j + 1)] for j in range(GROUP_SIZE)]
    sc = [score[N_GROUPS * j:N_GROUPS * (j + 1)] for j in range(GROUP_SIZE)]
    hi1, lo1 = jnp.maximum(a[0], a[1]), jnp.minimum(a[0], a[1])
    hi2, lo2 = jnp.maximum(a[2], a[3]), jnp.minimum(a[2], a[3])
    group_score = jnp.maximum(hi1, hi2) + jnp.maximum(jnp.minimum(hi1, hi2), jnp.maximum(lo1, lo2))
    gidx = lax.broadcasted_iota(jnp.int32, group_score.shape, 0)
    best = jnp.max(group_score, axis=0, keepdims=True)
    first_best = jnp.min(jnp.where(group_score == best, gidx, N_GROUPS), axis=0, keepdims=True)
    in_group = gidx == first_best
    picked = []
    for j in range(GROUP_SIZE):
        beaten_by = jnp.zeros(a[j].shape, jnp.int32)
        for i in range(GROUP_SIZE):
            if i == j:
                continue
            ahead = (a[i] > a[j]) | ((a[i] == a[j]) if i < j else False)
            beaten_by = beaten_by + ahead.astype(jnp.int32)
        picked.append(jnp.where(beaten_by < 2, sc[j], 0.0))
    denom = (picked[0] + picked[1]) + (picked[2] + picked[3])
    member_w = [jnp.sum(jnp.where(in_group, p / denom, 0.0), axis=0, keepdims=True) for p in picked]
    return in_group, member_w


def _out_norm_route_kernel(a_ref, wo_ref, x_ref, g_ref, b_ref, rw_ref, rb_ref, tri_ref,
                           x1_ref, gid_ref, rank_ref, cnt_ref, seen_ref):
    @pl.when(pl.program_id(0) == 0)
    def _():
        seen_ref[...] = jnp.zeros(seen_ref.shape, F32)

    y = _dot(a_ref[...], wo_ref[...])
    x1 = _ln(DEEPNORM_ALPHA * x_ref[...] + y, g_ref[...], b_ref[...])
    x1_ref[:, :D_MODEL] = x1
    xh, xl = _split_bf16(x1)
    wh, wl = _split_bf16(rw_ref[...])
    logits_t = _dot_nt(wh, xh) + (_dot_nt(wh, xl) + _dot_nt(wl, xh))
    in_group, member_w = _route(logits_t, rb_ref[...])
    tm = logits_t.shape[1]

    row = lax.broadcasted_iota(jnp.int32, (N_GROUPS, tm), 0)
    w_rows = jnp.zeros((N_GROUPS, tm), F32)
    for j in range(GROUP_SIZE):
        w_rows = jnp.where(row == j, member_w[j], w_rows)
    w_rows = jnp.concatenate([w_rows, jnp.zeros((LANES - N_GROUPS, tm), F32)], axis=0)
    x1_ref[:, D_MODEL:] = w_rows.T

    onehot = in_group.astype(F32)
    earlier = _dot(onehot.astype(BF16), tri_ref[...])
    seen = seen_ref[:, 0:1]
    gidx = lax.broadcasted_iota(jnp.int32, onehot.shape, 0)
    gid_ref[...] = jnp.sum(jnp.where(in_group, gidx, 0), axis=0, keepdims=True)
    rank_ref[...] = jnp.sum(onehot * (earlier + seen), axis=0, keepdims=True).astype(jnp.int32)
    seen_new = seen_ref[...] + jnp.sum(onehot, axis=1, keepdims=True)
    seen_ref[...] = seen_new
    cnt_ref[...] = seen_new.astype(jnp.int32)


def _out_norm_route(a, wo, x, g, b, rw_t, rb_t, tri, tm=ROUTE_TILE):
    t, d = x.shape
    return pl.pallas_call(
        _out_norm_route_kernel,
        out_shape=(jax.ShapeDtypeStruct((t, ROW_WIDE), F32),
                   jax.ShapeDtypeStruct((1, t), jnp.int32),
                   jax.ShapeDtypeStruct((1, t), jnp.int32),
                   jax.ShapeDtypeStruct((N_GROUPS, LANES), jnp.int32)),
        grid=(t // tm,),
        in_specs=[pl.BlockSpec((tm, d), lambda i: (i, 0)),
                  pl.BlockSpec((d, d), lambda i: (0, 0)),
                  pl.BlockSpec((tm, d), lambda i: (i, 0)),
                  pl.BlockSpec((1, d), lambda i: (0, 0)),
                  pl.BlockSpec((1, d), lambda i: (0, 0)),
                  pl.BlockSpec((N_EXPERTS, d), lambda i: (0, 0)),
                  pl.BlockSpec((N_EXPERTS, 1), lambda i: (0, 0)),
                  pl.BlockSpec((tm, tm), lambda i: (0, 0))],
        out_specs=(pl.BlockSpec((tm, ROW_WIDE), lambda i: (i, 0)),
                   pl.BlockSpec((1, tm), lambda i: (0, i)),
                   pl.BlockSpec((1, tm), lambda i: (0, i)),
                   pl.BlockSpec((N_GROUPS, LANES), lambda i: (0, 0))),
        scratch_shapes=[pltpu.VMEM((N_GROUPS, LANES), F32)],
        compiler_params=_cparams(("arbitrary",)),
        name="out_norm_route",
    )(a, wo, x, g, b, rw_t, rb_t, tri)


def _tile_plan(gid, rank, counts, n_tiles_max):
    cnt = counts[:, 0]
    padded = ((cnt + MOE_TILE - 1) // MOE_TILE) * MOE_TILE
    ends = jnp.cumsum(padded)
    starts = ends - padded
    n_tiles = ends[-1] // MOE_TILE
    tile_row = jnp.arange(n_tiles_max, dtype=jnp.int32) * MOE_TILE
    tile_group = jnp.sum((tile_row[:, None] >= ends[None, :]).astype(jnp.int32), axis=1)
    last_group = jnp.sum((ends[-1] - 1 >= ends).astype(jnp.int32))
    tile_group = jnp.minimum(tile_group, last_group)
    rows_left = (starts + cnt)[tile_group] - tile_row
    tile_rows = jnp.clip(rows_left, 0, MOE_TILE)
    dest = starts[gid[0]] + rank[0]
    pad_lo = jnp.concatenate([starts + cnt, ends[-1:]])
    pad_hi = jnp.concatenate([ends, jnp.full((1,), n_tiles_max * MOE_TILE, ends.dtype)])
    return (dest.astype(jnp.int32), pad_lo.astype(jnp.int32), pad_hi.astype(jnp.int32),
            tile_group.astype(jnp.int32), tile_rows.astype(jnp.int32), n_tiles.reshape(1).astype(jnp.int32))


def _invert_plan_kernel(pad_lo_ref, pad_hi_ref, dest_ref, src_ref):
    step = pl.program_id(0)

    @pl.when(step == 0)
    def _():
        def clear(p, carry):
            src_ref[p] = 0
            return carry
        for g in range(N_GROUPS + 1):
            lax.fori_loop(pad_lo_ref[g], pad_hi_ref[g], clear, 0)

    def place(r, carry):
        src_ref[dest_ref[0, r]] = step * PLAN_CHUNK + r
        return carry

    lax.fori_loop(0, PLAN_CHUNK, place, 0, unroll=8)


def _invert_plan(dest, pad_lo, pad_hi, n_rows):
    t = dest.shape[0]
    chunks = t // PLAN_CHUNK
    return pl.pallas_call(
        _invert_plan_kernel,
        out_shape=jax.ShapeDtypeStruct((n_rows,), jnp.int32),
        grid=(chunks,),
        in_specs=[pl.BlockSpec(memory_space=pltpu.SMEM),
                  pl.BlockSpec(memory_space=pltpu.SMEM),
                  pl.BlockSpec((None, 1, PLAN_CHUNK), lambda i: (i, 0, 0), memory_space=pltpu.SMEM)],
        out_specs=pl.BlockSpec((n_rows,), lambda i: (0,), memory_space=pltpu.SMEM),
        compiler_params=_cparams(("arbitrary",)),
        name="invert_plan",
    )(pad_lo, pad_hi, dest.reshape(chunks, 1, PLAN_CHUNK))


def _row_copy(src, dst, src_row, dst_row, sem):
    return pltpu.make_async_copy(src.at[pl.ds(src_row, 1)], dst.at[pl.ds(dst_row, 1)], sem)


def _for_rows(n_rows, fn):
    def chunk(q, carry):
        for u in range(8):
            fn(q * 8 + u)
        return carry

    def single(r, carry):
        fn(r)
        return carry

    full = n_rows // 8
    lax.fori_loop(0, full, chunk, 0)
    lax.fori_loop(full * 8, n_rows, single, 0)


def _moe_rows_kernel(tile_group_ref, tile_rows_ref, n_tiles_ref, src_ref, src_next_ref, x_hbm,
                     wg_ref, wu_ref, wd_ref, g_ref, b_ref, out_hbm, xbuf, obuf, wg_bf, wu_bf, wd_bf, gsem, ssem):
    k = pl.program_id(0)
    n_tiles = n_tiles_ref[0]
    slot = k % 2

    def row_copies(tile, one_start, one_wait, wait_all, start):
        n_rows = tile_rows_ref[tile]

        @pl.when(n_rows == MOE_TILE)
        def _():
            if start:
                for r in range(MOE_TILE):
                    one_start(r)
            else:
                wait_all()

        @pl.when(n_rows != MOE_TILE)
        def _():
            _for_rows(n_rows, one_start if start else one_wait)

    def gather(tile, rows_ref, buf_slot, start):
        dst = xbuf.at[buf_slot]
        sem = gsem.at[buf_slot]
        row_copies(tile,
                   lambda r: _row_copy(x_hbm, dst, rows_ref[0, r], r, sem).start(),
                   lambda r: _row_copy(x_hbm, dst, 0, 0, sem).wait(),
                   lambda: pltpu.make_async_copy(x_hbm.at[pl.ds(0, MOE_TILE)], dst, sem).wait(),
                   start)

    def scatter(tile, rows_ref, buf_slot, start):
        src = obuf.at[buf_slot]
        sem = ssem.at[buf_slot]
        row_copies(tile,
                   lambda r: _row_copy(src, out_hbm, r, rows_ref[0, r], sem).start(),
                   lambda r: _row_copy(src, out_hbm, 0, 0, sem).wait(),
                   lambda: pltpu.make_async_copy(src, out_hbm.at[pl.ds(0, MOE_TILE)], sem).wait(),
                   start)

    @pl.when(k == 0)
    def _():
        xbuf[...] = jnp.zeros(xbuf.shape, F32)
        gather(0, src_ref, 0, True)

    @pl.when(k + 1 < n_tiles)
    def _():
        gather(k + 1, src_next_ref, 1 - slot, True)

    @pl.when(k < n_tiles)
    def _():
        gather(k, src_ref, slot, False)

        @pl.when(k >= 2)
        def _():
            scatter(k - 2, src_ref, slot, False)

        @pl.when((k == 0) | (tile_group_ref[k] != tile_group_ref[jnp.maximum(k - 1, 0)]))
        def _():
            for j in range(GROUP_SIZE):
                cols = slice(j * EXPERT_FF, (j + 1) * EXPERT_FF)
                wg_bf[:, cols] = wg_ref[j].astype(BF16)
                wu_bf[:, cols] = wu_ref[j].astype(BF16)
                wd_bf[cols, :] = wd_ref[j].astype(BF16)

        x = xbuf[slot, :, :D_MODEL]
        cw = xbuf[slot, :, D_MODEL:]
        xb = x.astype(BF16)
        hg = _dot(xb, wg_bf[...])
        hu = _dot(xb, wu_bf[...])
        cw_wide = jnp.concatenate([jnp.broadcast_to(cw[:, j:j + 1], (MOE_TILE, EXPERT_FF))
                                   for j in range(GROUP_SIZE)], axis=1)
        hid = (hg / (1.0 + jnp.exp(-hg))) * hu * cw_wide
        y = _dot(hid.astype(BF16), wd_bf[...])
        obuf[slot] = _ln(DEEPNORM_ALPHA * x + y, g_ref[...], b_ref[...])
        scatter(k, src_ref, slot, True)

    @pl.when(k == n_tiles - 1)
    def _():
        @pl.when(k >= 1)
        def _():
            scatter(k - 1, src_ref, 1 - slot, False)
        scatter(k, src_ref, slot, False)


def _moe_rows(x_rows, src, tile_group, tile_rows, n_tiles, wg, wu, wd, g, b, layer):
    t = x_rows.shape[0]
    n_tiles_max = src.shape[0] // MOE_TILE
    d, gs, ff = D_MODEL, GROUP_SIZE, EXPERT_FF
    src3 = src.reshape(n_tiles_max, 1, MOE_TILE)
    grid_spec = pltpu.PrefetchScalarGridSpec(
        num_scalar_prefetch=3,
        grid=(n_tiles_max,),
        in_specs=[pl.BlockSpec((None, 1, MOE_TILE), lambda k, tg, tr, nt: (k, 0, 0), memory_space=pltpu.SMEM),
                  pl.BlockSpec((None, 1, MOE_TILE), lambda k, tg, tr, nt: (jnp.minimum(k + 1, n_tiles_max - 1), 0, 0),
                               memory_space=pltpu.SMEM),
                  pl.BlockSpec(memory_space=pl.ANY),
                  pl.BlockSpec((None, gs, d, ff), lambda k, tg, tr, nt: (layer, tg[k], 0, 0)),
                  pl.BlockSpec((None, gs, d, ff), lambda k, tg, tr, nt: (layer, tg[k], 0, 0)),
                  pl.BlockSpec((None, gs, ff, d), lambda k, tg, tr, nt: (layer, tg[k], 0, 0)),
                  pl.BlockSpec((1, d), lambda k, tg, tr, nt: (0, 0)),
                  pl.BlockSpec((1, d), lambda k, tg, tr, nt: (0, 0))],
        out_specs=pl.BlockSpec(memory_space=pl.ANY),
        scratch_shapes=[pltpu.VMEM((2, MOE_TILE, ROW_WIDE), F32), pltpu.VMEM((2, MOE_TILE, d), F32),
                        pltpu.VMEM((d, gs * ff), BF16), pltpu.VMEM((d, gs * ff), BF16), pltpu.VMEM((gs * ff, d), BF16),
                        pltpu.SemaphoreType.DMA((2,)), pltpu.SemaphoreType.DMA((2,))])
    return pl.pallas_call(
        _moe_rows_kernel,
        out_shape=jax.ShapeDtypeStruct((t, d), F32),
        grid_spec=grid_spec,
        compiler_params=_cparams(("arbitrary",)),
        name="grouped_moe",
    )(tile_group, tile_rows, n_tiles, src3, src3, x_rows, wg, wu, wd, g, b)


def _rope_tables(seq):
    inv_freq = ROPE_BASE ** (-jnp.arange(0, MLA_ROPE_DIM, 2, dtype=F32) / MLA_ROPE_DIM)
    ang = jnp.arange(seq, dtype=F32)[:, None] * inv_freq[None, :]
    cos, sin = jnp.cos(ang), jnp.sin(ang)
    cos2 = jnp.concatenate([cos, cos], axis=1)
    sin2 = jnp.concatenate([-sin, sin], axis=1)
    zero = jnp.zeros((seq, LANES - MLA_ROPE_DIM), F32)
    return (jnp.concatenate([cos2, zero], axis=1), jnp.concatenate([sin2, zero], axis=1), cos2.T, sin2.T)


def _swap_halves(w):
    half = MLA_ROPE_DIM // 2
    return jnp.concatenate([w[..., half:], w[..., :half]], axis=-1)


def kernel(x, diff_w_qkv, diff_lambda, diff_subln_g, diff_w_o, mla_w_kv_a, mla_kv_norm_g, mla_w_kv_b,
           mla_w_q_a, mla_q_norm_g, mla_w_q_b, mla_w_o, router_w, router_b, moe_w_gate, moe_w_up,
           moe_w_down, ln_g, ln_b):
    batch, seq, d = x.shape
    t = batch * seq
    xs = x.reshape(t, d)

    wq, wk, wv = jnp.split(diff_w_qkv, 3, axis=-1)
    w_k = wk.astype(BF16)
    w_qv_t = jnp.concatenate([wq * (LOG2E * DIFF_HEAD_DIM ** -0.5), wv], axis=-1).transpose(0, 2, 1).astype(BF16)
    w_o_a = diff_w_o.astype(BF16)
    w_o_b = mla_w_o.astype(BF16)
    slopes = LOG2E * 2.0 ** (-8.0 * jnp.arange(1, DIFF_HEADS + 1, dtype=F32) / DIFF_HEADS)
    slope_hi = slopes.astype(BF16).astype(F32)
    slope_lo = (slopes - slope_hi).astype(BF16).astype(F32)
    rw_t = router_w.T.reshape(N_GROUPS, GROUP_SIZE, d).transpose(1, 0, 2).reshape(N_EXPERTS, d)
    rb_t = router_b.reshape(N_GROUPS, GROUP_SIZE).T.reshape(N_EXPERTS, 1).astype(F32)
    cs, sn, cs_t, sn_t = _rope_tables(seq)
    kv_rope_w = mla_w_kv_a[:, MLA_KV_RANK:]
    w_kv_a = jnp.concatenate([mla_w_kv_a, _swap_halves(kv_rope_w)], axis=1).astype(BF16)
    w_kv_b = mla_w_kv_b.reshape(MLA_KV_RANK, MLA_HEADS, MLA_NOPE_DIM + MLA_V_DIM)
    w_kn = w_kv_b[..., :MLA_NOPE_DIM].reshape(MLA_KV_RANK, -1).astype(BF16)
    w_v_t = w_kv_b[..., MLA_NOPE_DIM:].reshape(MLA_KV_RANK, -1).T.astype(BF16)
    wqb = mla_w_q_b.reshape(-1, MLA_Q_RANK, MLA_HEADS, MLA_NOPE_DIM + MLA_ROPE_DIM)
    w_q_b_t = jnp.concatenate([wqb, _swap_halves(wqb[..., MLA_NOPE_DIM:])], axis=-1).reshape(
        -1, MLA_Q_RANK, MLA_HEADS * MLA_QK_PAD).transpose(0, 2, 1).astype(BF16)
    w_q_a = mla_w_q_a.astype(BF16)

    alibi_fix, diag_mask = _diagonal_tables()
    tri = jnp.triu(jnp.ones((ROUTE_TILE, ROUTE_TILE), BF16), 1)
    n_tiles_max = t // MOE_TILE + N_GROUPS

    kcat = vt_shared = None
    for layer in range(DEPTH):
        if layer < N_A_LAYERS:
            k, qt, vt = _qkv_proj(xs, w_k[layer], w_qv_t[layer])
            attn = _diff_attention(qt, k, vt, slope_hi, slope_lo, diff_lambda[layer],
                                   diff_subln_g[layer].reshape(-1, 1), alibi_fix, batch, seq, _lambda_init(layer))
            w_o = w_o_a[layer]
        else:
            j = layer - N_A_LAYERS
            if j == 0:
                kcat, vt_shared = _mla_kv_proj(xs, w_kv_a, mla_kv_norm_g.reshape(1, -1), w_kn, w_v_t, cs, sn, seq)
            qt = _mla_q_proj(xs, w_q_a[j], mla_q_norm_g[j].reshape(1, -1), w_q_b_t[j], cs_t, sn_t, seq)
            attn = _mla_attention(qt, kcat, vt_shared, diag_mask, batch, seq)
            w_o = w_o_b[j]
        x1, gid, rank, counts = _out_norm_route(attn, w_o, xs, ln_g[layer, 0].reshape(1, -1),
                                                ln_b[layer, 0].reshape(1, -1), rw_t, rb_t, tri)
        dest, pad_lo, pad_hi, tile_group, tile_rows, n_tiles = _tile_plan(gid, rank, counts, n_tiles_max)
        src = _invert_plan(dest, pad_lo, pad_hi, n_tiles_max * MOE_TILE)
        xs = _moe_rows(x1, src, tile_group, tile_rows, n_tiles, moe_w_gate, moe_w_up, moe_w_down,
                       ln_g[layer, 1].reshape(1, -1), ln_b[layer, 1].reshape(1, -1), layer)
    return xs.reshape(batch, seq, d)
```

```python
import functools
import math

import jax
import jax.numpy as jnp
from jax import lax
from jax.experimental import pallas as pl
from jax.experimental.pallas import tpu as pltpu

D_MODEL = 1024
DEPTH = 4
CHUNK = 64
N_A_LAYERS = DEPTH // 2
DIFF_HEAD_DIM = 64
DIFF_V_DIM = 2 * DIFF_HEAD_DIM
DIFF_HEADS = D_MODEL // DIFF_V_DIM
MLA_NOPE_DIM = 128
MLA_ROPE_DIM = 64
MLA_V_DIM = 128
MLA_HEADS = D_MODEL // MLA_V_DIM
MLA_Q_RANK = 384
MLA_KV_RANK = 256
MLA_QK_PAD = 256
ROPE_BASE = 10000.0
N_EXPERTS = 32
N_GROUPS = 8
GROUP_SIZE = N_EXPERTS // N_GROUPS
EXPERT_FF = 256
NORM_EPS = 1e-5
DEEPNORM_ALPHA = (2 * DEPTH) ** 0.25
LOG2E = math.log2(math.e)

LANES = 128
T_TILE = 256
GROUP_TILES = 2
Q_TILE = GROUP_TILES * T_TILE
MASKED = 1e30
V_ROWS = DIFF_V_DIM + 16
DIFF_HEAD_PAIR = 2
MLA_HEAD_PAIR = 2
PROJ_TILES = 2
PROJ_ROWS = PROJ_TILES * T_TILE
ROUTE_TILE = 1024
MOE_TILE = 256
PLAN_CHUNK = 2048
ROW_WIDE = D_MODEL + LANES
VMEM_LIMIT = 48 * 1024 * 1024

F32 = jnp.float32
BF16 = jnp.bfloat16


def _lambda_init(layer):
    return 0.8 - 0.6 * math.exp(-0.3 * layer)


def _cparams(sem):
    return pltpu.CompilerParams(dimension_semantics=sem, vmem_limit_bytes=VMEM_LIMIT)


def _dot(a, b):
    return jnp.dot(a, b, preferred_element_type=F32)


def _dot_nt(a, b):
    return lax.dot_general(a, b, (((1,), (1,)), ((), ())), preferred_element_type=F32)


def _rms(x, g):
    return x * lax.rsqrt(jnp.mean(x * x, axis=-1, keepdims=True) + NORM_EPS) * g


def _ln(x, g, b):
    mu = jnp.mean(x, axis=-1, keepdims=True)
    xc = x - mu
    var = jnp.mean(xc * xc, axis=-1, keepdims=True)
    return xc * lax.rsqrt(var + NORM_EPS) * g + b


def _rope_halves(a, cs, sn):
    return a * cs + pltpu.roll(a, MLA_ROPE_DIM, axis=1) * sn


def _qkv_kernel(x_ref, wk_ref, wqv_ref, k_ref, qt_ref, vt_ref):
    xb = x_ref[...].astype(BF16)
    k_ref[...] = _dot(xb, wk_ref[...]).astype(BF16)
    qv = _dot_nt(wqv_ref[...], xb)
    for h in range(DIFF_HEADS):
        qt_ref[0, h] = qv[h * LANES:(h + 1) * LANES].astype(BF16)
        vt_ref[0, h, :DIFF_V_DIM] = qv[D_MODEL + h * LANES:D_MODEL + (h + 1) * LANES].astype(BF16)
        vt_ref[0, h, DIFF_V_DIM:] = jnp.ones((V_ROWS - DIFF_V_DIM, T_TILE), BF16)


def _qkv_proj(x, wk, wqv):
    t, d = x.shape
    nt = t // T_TILE

    def head_major(rows):
        return jax.ShapeDtypeStruct((nt, DIFF_HEADS, rows, T_TILE), BF16)

    def head_block(rows):
        return pl.BlockSpec((1, DIFF_HEADS, rows, T_TILE), lambda i: (i, 0, 0, 0))

    return pl.pallas_call(
        _qkv_kernel,
        out_shape=(jax.ShapeDtypeStruct((t, d), BF16), head_major(LANES), head_major(V_ROWS)),
        grid=(nt,),
        in_specs=[pl.BlockSpec((T_TILE, d), lambda i: (i, 0)),
                  pl.BlockSpec(wk.shape, lambda i: (0, 0)),
                  pl.BlockSpec(wqv.shape, lambda i: (0, 0))],
        out_specs=(pl.BlockSpec((T_TILE, d), lambda i: (i, 0)), head_block(LANES), head_block(V_ROWS)),
        compiler_params=_cparams(("parallel",)),
        name="qkv_proj",
    )(x, wk, wqv)


def _mla_kv_kernel(x_ref, wa_ref, g_ref, wk_ref, wv_ref, cs_ref, sn_ref, kcat_ref, vt_ref):
    kva = _dot(x_ref[...].astype(BF16), wa_ref[...])
    c_kv = _rms(kva[:, :MLA_KV_RANK], g_ref[...]).astype(BF16)
    kr = _rope_halves(kva[:, MLA_KV_RANK:], cs_ref[...], sn_ref[...]).astype(BF16)
    kn = _dot(c_kv, wk_ref[...])
    vt = _dot_nt(wv_ref[...], c_kv)
    for h in range(MLA_HEADS):
        base = h * MLA_QK_PAD
        kcat_ref[:, base:base + MLA_NOPE_DIM] = kn[:, h * MLA_NOPE_DIM:(h + 1) * MLA_NOPE_DIM].astype(BF16)
        kcat_ref[:, base + MLA_NOPE_DIM:base + MLA_QK_PAD] = kr
        for u in range(PROJ_TILES):
            cols = slice(u * T_TILE, (u + 1) * T_TILE)
            vt_ref[u, h, :MLA_V_DIM] = vt[h * MLA_V_DIM:(h + 1) * MLA_V_DIM, cols].astype(BF16)
            vt_ref[u, h, MLA_V_DIM:] = jnp.ones((V_ROWS - MLA_V_DIM, T_TILE), BF16)


def _mla_kv_proj(x, wa, g, wk, wv, cs, sn, seq):
    t, d = x.shape
    ns = seq // PROJ_ROWS
    return pl.pallas_call(
        _mla_kv_kernel,
        out_shape=(jax.ShapeDtypeStruct((t, MLA_HEADS * MLA_QK_PAD), BF16),
                   jax.ShapeDtypeStruct((t // T_TILE, MLA_HEADS, V_ROWS, T_TILE), BF16)),
        grid=(t // PROJ_ROWS,),
        in_specs=[pl.BlockSpec((PROJ_ROWS, d), lambda i: (i, 0)),
                  pl.BlockSpec(wa.shape, lambda i: (0, 0)),
                  pl.BlockSpec(g.shape, lambda i: (0, 0)),
                  pl.BlockSpec(wk.shape, lambda i: (0, 0)),
                  pl.BlockSpec(wv.shape, lambda i: (0, 0)),
                  pl.BlockSpec((PROJ_ROWS, LANES), lambda i: (i % ns, 0)),
                  pl.BlockSpec((PROJ_ROWS, LANES), lambda i: (i % ns, 0))],
        out_specs=(pl.BlockSpec((PROJ_ROWS, MLA_HEADS * MLA_QK_PAD), lambda i: (i, 0)),
                   pl.BlockSpec((PROJ_TILES, MLA_HEADS, V_ROWS, T_TILE), lambda i: (i, 0, 0, 0))),
        compiler_params=_cparams(("parallel",)),
        name="mla_kv_proj",
    )(x, wa, g, wk, wv, cs, sn)


def _mla_q_kernel(x_ref, wa_ref, g_ref, wb_ref, cst_ref, snt_ref, qt_ref, *, scale):
    qa = _dot(x_ref[...].astype(BF16), wa_ref[...])
    qn = _rms(qa, g_ref[...]).astype(BF16)
    qt = _dot_nt(wb_ref[...], qn) * scale
    cst = cst_ref[...]
    snt = snt_ref[...]
    zero = jnp.zeros((MLA_ROPE_DIM, T_TILE), BF16)
    for h in range(MLA_HEADS):
        base = h * MLA_QK_PAD
        r0 = base + MLA_NOPE_DIM
        rope = qt[r0:r0 + MLA_ROPE_DIM] * cst + qt[r0 + MLA_ROPE_DIM:base + MLA_QK_PAD] * snt
        for u in range(PROJ_TILES):
            cols = slice(u * T_TILE, (u + 1) * T_TILE)
            qt_ref[u, h, 0:MLA_NOPE_DIM] = qt[base:r0, cols].astype(BF16)
            qt_ref[u, h, MLA_NOPE_DIM:MLA_NOPE_DIM + MLA_ROPE_DIM] = rope[:, cols].astype(BF16)
            qt_ref[u, h, MLA_NOPE_DIM + MLA_ROPE_DIM:MLA_QK_PAD] = zero


def _mla_q_proj(x, wa, g, wb_t, cst, snt, seq):
    t, d = x.shape
    ns = seq // PROJ_ROWS
    scale = LOG2E * (MLA_NOPE_DIM + MLA_ROPE_DIM) ** -0.5
    return pl.pallas_call(
        functools.partial(_mla_q_kernel, scale=scale),
        out_shape=jax.ShapeDtypeStruct((t // T_TILE, MLA_HEADS, MLA_QK_PAD, T_TILE), BF16),
        grid=(t // PROJ_ROWS,),
        in_specs=[pl.BlockSpec((PROJ_ROWS, d), lambda i: (i, 0)),
                  pl.BlockSpec(wa.shape, lambda i: (0, 0)),
                  pl.BlockSpec(g.shape, lambda i: (0, 0)),
                  pl.BlockSpec(wb_t.shape, lambda i: (0, 0)),
                  pl.BlockSpec((MLA_ROPE_DIM, PROJ_ROWS), lambda i: (0, i % ns)),
                  pl.BlockSpec((MLA_ROPE_DIM, PROJ_ROWS), lambda i: (0, i % ns))],
        out_specs=pl.BlockSpec((PROJ_TILES, MLA_HEADS, MLA_QK_PAD, T_TILE), lambda i: (i, 0, 0, 0)),
        compiler_params=_cparams(("parallel",)),
        name="mla_q_proj",
    )(x, wa, g, wb_t, cst, snt)


def _consume_group(tiles, m_ref, acc_ref):
    for c, comp_tiles in enumerate(tiles):
        m_group = None
        for _, m_t, _, _ in comp_tiles:
            m_group = m_t if m_group is None else jnp.maximum(m_group, m_t)
        pv = None
        for s_t, _, shift, vt in comp_tiles:
            part = _dot(vt, jnp.exp2(s_t - (m_group - shift)).astype(BF16))
            pv = part if pv is None else pv + part
        m_prev = m_ref[c]
        m_new = jnp.maximum(m_prev, m_group)
        acc_ref[c] = jnp.exp2(m_prev - m_new) * acc_ref[c] + jnp.exp2(m_group - m_new) * pv
        m_ref[c] = m_new


def _init_online_softmax(m_ref, acc_ref):
    m_ref[...] = jnp.full(m_ref.shape, -jnp.inf, F32)
    acc_ref[...] = jnp.zeros(acc_ref.shape, F32)


def _normalised(acc):
    return acc[:DIFF_V_DIM] / acc[DIFF_V_DIM:DIFF_V_DIM + 1]


def _attention_pipeline(n, issue_scores, consume):
    @pl.when(n == 0)
    def _():
        issue_scores(0, 0, True)
        consume(0, 0)

    @pl.when(n > 0)
    def _():
        issue_scores(0, 0, False)

        def pair(p, carry):
            issue_scores(2 * p + 1, 1, False)
            consume(2 * p, 0)
            issue_scores(2 * p + 2, 0, False)
            consume(2 * p + 1, 1)
            return carry

        lax.fori_loop(0, (n - 1) // 2, pair, 0)

        @pl.when(n % 2 == 1)
        def _():
            issue_scores(n, 1, True)
            consume(n - 1, 0)
            consume(n, 1)

        @pl.when(n % 2 == 0)
        def _():
            issue_scores(n - 1, 1, False)
            consume(n - 2, 0)
            issue_scores(n, 0, True)
            consume(n - 1, 1)
            consume(n, 0)


def _diff_attn_kernel(slh_ref, sll_ref, lam_ref, g_ref, dm_ref, qt_ref, k_ref, vt_ref, o_ref,
                      s_buf, max_buf, m_ref, acc_ref, *, lambda_init):
    i = pl.program_id(2)
    heads = range(DIFF_HEAD_PAIR)
    row = lax.broadcasted_iota(jnp.int32, (LANES, Q_TILE), 0)
    slopes, qs_aug = [], []
    for hh in heads:
        head = pl.program_id(1) * DIFF_HEAD_PAIR + hh
        sl_hi = slh_ref[head]
        sl_lo = sll_ref[head]
        slopes.append(sl_hi + sl_lo)
        qf = jnp.concatenate([qt_ref[u, hh] for u in range(GROUP_TILES)], axis=1).astype(F32)
        slope_rows = jnp.where(row == 0, sl_hi, jnp.where(row == 1, sl_lo, 0.0)).astype(BF16)
        qs_aug += [jnp.concatenate([jnp.where(keep, qf, 0.0).astype(BF16), slope_rows], axis=0)
                   for keep in (row < DIFF_HEAD_DIM, row >= DIFF_HEAD_DIM)]
    n_chains = 2 * DIFF_HEAD_PAIR
    key_off = lax.broadcasted_iota(jnp.int32, (T_TILE, LANES), 0).astype(F32)
    lane = lax.broadcasted_iota(jnp.int32, (T_TILE, LANES), 1)
    key_cols = jnp.where(lane < 2, key_off, 0.0).astype(BF16)
    _init_online_softmax(m_ref, acc_ref)

    def tile_shift(j, hh):
        return slopes[hh] * (j * T_TILE - i * Q_TILE).astype(F32)

    def issue_scores(g, slot, diagonal):
        for u in range(GROUP_TILES):
            j = g * GROUP_TILES + u
            off = pl.multiple_of(j * T_TILE, T_TILE)
            for hh in heads:
                k_aug = jnp.concatenate([k_ref[pl.ds(off, T_TILE), hh * LANES:(hh + 1) * LANES], key_cols],
                                        axis=1)
                for c in range(2):
                    n = 2 * hh + c
                    s_t = _dot(k_aug, qs_aug[n])
                    if diagonal:
                        s_t = s_t + slopes[hh] * dm_ref[u * T_TILE:(u + 1) * T_TILE, :]
                    s_buf[slot, n_chains * u + n] = s_t
                    max_buf[slot, n_chains * u + n] = jnp.max(s_t, axis=0, keepdims=True) + tile_shift(j, hh)

    def consume(g, slot):
        tiles = [[] for _ in range(n_chains)]
        for u in range(GROUP_TILES):
            j = g * GROUP_TILES + u
            for hh in heads:
                for c in range(2):
                    n = 2 * hh + c
                    tiles[n].append((s_buf[slot, n_chains * u + n], max_buf[slot, n_chains * u + n],
                                     tile_shift(j, hh), vt_ref[j, hh]))
        _consume_group(tiles, m_ref, acc_ref)

    _attention_pipeline(i, issue_scores, consume)

    lam = lam_ref[...]
    lam_full = (jnp.exp(jnp.sum(lam[0:1] * lam[1:2], axis=1, keepdims=True))
                - jnp.exp(jnp.sum(lam[2:3] * lam[3:4], axis=1, keepdims=True)) + lambda_init)
    for hh in heads:
        o_t = _normalised(acc_ref[2 * hh]) - lam_full * _normalised(acc_ref[2 * hh + 1])
        inv = lax.rsqrt(jnp.mean(o_t * o_t, axis=0, keepdims=True) + NORM_EPS)
        o_t = o_t * inv * (g_ref[...] * (1.0 - lambda_init))
        o_ref[:, hh * DIFF_V_DIM:(hh + 1) * DIFF_V_DIM] = o_t.T.astype(o_ref.dtype)


def _diagonal_tables():
    rk = jnp.arange(Q_TILE, dtype=jnp.int32)[:, None]
    rq = jnp.arange(Q_TILE, dtype=jnp.int32)[None, :]
    visible = (rk // CHUNK) <= (rq // CHUNK)
    alibi_fix = jnp.where(visible, -2.0 * jnp.maximum(rk - rq, 0).astype(F32), -MASKED)
    mask = jnp.where(visible, 0.0, -MASKED).astype(F32)
    return alibi_fix, mask


def _diff_attention(qt, k, vt, slope_hi, slope_lo, lam, g_col, alibi_fix, batch, seq, lambda_init):
    nq = seq // Q_TILE
    nk = seq // T_TILE
    hp = DIFF_HEAD_PAIR
    n_chains = 2 * hp
    return pl.pallas_call(
        functools.partial(_diff_attn_kernel, lambda_init=lambda_init),
        out_shape=jax.ShapeDtypeStruct((batch * seq, DIFF_HEADS * DIFF_V_DIM), BF16),
        grid=(batch, DIFF_HEADS // hp, nq),
        in_specs=[pl.BlockSpec(memory_space=pltpu.SMEM),
                  pl.BlockSpec(memory_space=pltpu.SMEM),
                  pl.BlockSpec(lam.shape, lambda b, h, i: (0, 0)),
                  pl.BlockSpec(g_col.shape, lambda b, h, i: (0, 0)),
                  pl.BlockSpec(alibi_fix.shape, lambda b, h, i: (0, 0)),
                  pl.BlockSpec((GROUP_TILES, hp, LANES, T_TILE), lambda b, h, i: (b * nq + i, h, 0, 0)),
                  pl.BlockSpec((seq, hp * LANES), lambda b, h, i: (b, h)),
                  pl.BlockSpec((nk, hp, V_ROWS, T_TILE), lambda b, h, i: (b, h, 0, 0))],
        out_specs=pl.BlockSpec((Q_TILE, hp * LANES), lambda b, h, i: (b * nq + i, h)),
        scratch_shapes=[pltpu.VMEM((2, n_chains * GROUP_TILES, T_TILE, Q_TILE), F32),
                        pltpu.VMEM((2, n_chains * GROUP_TILES, 1, Q_TILE), F32),
                        pltpu.VMEM((n_chains, 1, Q_TILE), F32),
                        pltpu.VMEM((n_chains, V_ROWS, Q_TILE), F32)],
        compiler_params=_cparams(("parallel", "parallel", "arbitrary")),
        name="diff_attention",
    )(slope_hi, slope_lo, lam, g_col, alibi_fix, qt, k, vt)


def _mla_attn_kernel(mask_ref, qt_ref, k_ref, vt_ref, o_ref, s_buf, max_buf, m_ref, acc_ref):
    i = pl.program_id(2)
    heads = range(MLA_HEAD_PAIR)
    qts = [jnp.concatenate([qt_ref[u, hh] for u in range(GROUP_TILES)], axis=1) for hh in heads]
    _init_online_softmax(m_ref, acc_ref)

    def issue_scores(g, slot, diagonal):
        for u in range(GROUP_TILES):
            off = pl.multiple_of((g * GROUP_TILES + u) * T_TILE, T_TILE)
            for hh in heads:
                s_t = _dot(k_ref[pl.ds(off, T_TILE), hh * MLA_QK_PAD:(hh + 1) * MLA_QK_PAD], qts[hh])
                if diagonal:
                    s_t = s_t + mask_ref[u * T_TILE:(u + 1) * T_TILE, :]
                s_buf[slot, MLA_HEAD_PAIR * u + hh] = s_t
                max_buf[slot, MLA_HEAD_PAIR * u + hh] = jnp.max(s_t, axis=0, keepdims=True)

    def consume(g, slot):
        tiles = [[(s_buf[slot, MLA_HEAD_PAIR * u + hh], max_buf[slot, MLA_HEAD_PAIR * u + hh], 0.0,
                   vt_ref[g * GROUP_TILES + u, hh]) for u in range(GROUP_TILES)] for hh in heads]
        _consume_group(tiles, m_ref, acc_ref)

    _attention_pipeline(i, issue_scores, consume)
    for hh in heads:
        o_ref[:, hh * MLA_V_DIM:(hh + 1) * MLA_V_DIM] = _normalised(acc_ref[hh]).T.astype(o_ref.dtype)


def _mla_attention(qt, kcat, vt, mask, batch, seq):
    nq = seq // Q_TILE
    nk = seq // T_TILE
    hp = MLA_HEAD_PAIR
    return pl.pallas_call(
        _mla_attn_kernel,
        out_shape=jax.ShapeDtypeStruct((batch * seq, MLA_HEADS * MLA_V_DIM), BF16),
        grid=(batch, MLA_HEADS // hp, nq),
        in_specs=[pl.BlockSpec(mask.shape, lambda b, h, i: (0, 0)),
                  pl.BlockSpec((GROUP_TILES, hp, MLA_QK_PAD, T_TILE), lambda b, h, i: (b * nq + i, h, 0, 0)),
                  pl.BlockSpec((seq, hp * MLA_QK_PAD), lambda b, h, i: (b, h)),
                  pl.BlockSpec((nk, hp, V_ROWS, T_TILE), lambda b, h, i: (b, h, 0, 0))],
        out_specs=pl.BlockSpec((Q_TILE, hp * MLA_V_DIM), lambda b, h, i: (b * nq + i, h)),
        scratch_shapes=[pltpu.VMEM((2, hp * GROUP_TILES, T_TILE, Q_TILE), F32),
                        pltpu.VMEM((2, hp * GROUP_TILES, 1, Q_TILE), F32),
                        pltpu.VMEM((hp, 1, Q_TILE), F32),
                        pltpu.VMEM((hp, V_ROWS, Q_TILE), F32)],
        compiler_params=_cparams(("parallel", "parallel", "arbitrary")),
        name="mla_attention",
    )(mask, qt, kcat, vt)


def _split_bf16(x):
    hi = x.astype(BF16)
    lo = (x - hi.astype(F32)).astype(BF16)
    return hi, lo


def _route(logits_t, bias_t):
    score = 1.0 / (1.0 + jnp.exp(-logits_t))
    sel = score + bias_t
    a = [sel[N_GROUPS * j:N_GROUPS * (j + 1)] for j in range(GROUP_SIZE)]
    sc = [score[N_GROUPS * j:N_GROUPS * (j + 1)] for j in range(GROUP_SIZE)]
    hi1, lo1 = jnp.maximum(a[0], a[1]), jnp.minimum(a[0], a[1])
    hi2, lo2 = jnp.maximum(a[2], a[3]), jnp.minimum(a[2], a[3])
    group_score = jnp.maximum(hi1, hi2) + jnp.maximum(jnp.minimum(hi1, hi2), jnp.maximum(lo1, lo2))
    gidx = lax.broadcasted_iota(jnp.int32, group_score.shape, 0)
    best = jnp.max(group_score, axis=0, keepdims=True)
    first_best = jnp.min(jnp.where(group_score == best, gidx, N_GROUPS), axis=0, keepdims=True)
    in_group = gidx == first_best
    picked = []
    for j in range(GROUP_SIZE):
        beaten_by = jnp.zeros(a[j].shape, jnp.int32)
        for i in range(GROUP_SIZE):
            if i == j:
                continue
            ahead = (a[i] > a[j]) | ((a[i] == a[j]) if i < j else False)
            beaten_by = beaten_by + ahead.astype(jnp.int32)
        picked.append(jnp.where(beaten_by < 2, sc[j], 0.0))
    denom = (picked[0] + picked[1]) + (picked[2] + picked[3])
    member_w = [jnp.sum(jnp.where(in_group, p / denom, 0.0), axis=0, keepdims=True) for p in picked]
    return in_group, member_w


def _out_norm_route_kernel(a_ref, wo_ref, x_ref, g_ref, b_ref, rw_ref, rb_ref, tri_ref,
                           x1_ref, gid_ref, rank_ref, cnt_ref, seen_ref):
    @pl.when(pl.program_id(0) == 0)
    def _():
        seen_ref[...] = jnp.zeros(seen_ref.shape, F32)

    y = _dot(a_ref[...], wo_ref[...])
    x1 = _ln(DEEPNORM_ALPHA * x_ref[...] + y, g_ref[...], b_ref[...])
    x1_ref[:, :D_MODEL] = x1
    xh, xl = _split_bf16(x1)
    wh, wl = _split_bf16(rw_ref[...])
    by_xh = _dot_nt(jnp.concatenate([wh, wl], axis=0), xh)
    logits_t = by_xh[:N_EXPERTS] + (by_xh[N_EXPERTS:] + _dot_nt(wh, xl))
    in_group, member_w = _route(logits_t, rb_ref[...])
    tm = logits_t.shape[1]

    row = lax.broadcasted_iota(jnp.int32, (N_GROUPS, tm), 0)
    w_rows = jnp.zeros((N_GROUPS, tm), F32)
    for j in range(GROUP_SIZE):
        w_rows = jnp.where(row == j, member_w[j], w_rows)
    w_rows = jnp.concatenate([w_rows, jnp.zeros((LANES - N_GROUPS, tm), F32)], axis=0)
    x1_ref[:, D_MODEL:] = w_rows.T

    onehot = in_group.astype(F32)
    earlier = _dot(onehot.astype(BF16), tri_ref[...])
    seen = seen_ref[:, 0:1]
    gidx = lax.broadcasted_iota(jnp.int32, onehot.shape, 0)
    gid_ref[...] = jnp.sum(jnp.where(in_group, gidx, 0), axis=0, keepdims=True)
    rank_ref[...] = jnp.sum(onehot * (earlier + seen), axis=0, keepdims=True).astype(jnp.int32)
    seen_new = seen_ref[...] + jnp.sum(onehot, axis=1, keepdims=True)
    seen_ref[...] = seen_new
    cnt_ref[...] = seen_new.astype(jnp.int32)


def _out_norm_route(a, wo, x, g, b, rw_t, rb_t, tri, tm=ROUTE_TILE):
    t, d = x.shape
    return pl.pallas_call(
        _out_norm_route_kernel,
        out_shape=(jax.ShapeDtypeStruct((t, ROW_WIDE), F32),
                   jax.ShapeDtypeStruct((1, t), jnp.int32),
                   jax.ShapeDtypeStruct((1, t), jnp.int32),
                   jax.ShapeDtypeStruct((N_GROUPS, LANES), jnp.int32)),
        grid=(t // tm,),
        in_specs=[pl.BlockSpec((tm, d), lambda i: (i, 0)),
                  pl.BlockSpec((d, d), lambda i: (0, 0)),
                  pl.BlockSpec((tm, d), lambda i: (i, 0)),
                  pl.BlockSpec((1, d), lambda i: (0, 0)),
                  pl.BlockSpec((1, d), lambda i: (0, 0)),
                  pl.BlockSpec((N_EXPERTS, d), lambda i: (0, 0)),
                  pl.BlockSpec((N_EXPERTS, 1), lambda i: (0, 0)),
                  pl.BlockSpec((tm, tm), lambda i: (0, 0))],
        out_specs=(pl.BlockSpec((tm, ROW_WIDE), lambda i: (i, 0)),
                   pl.BlockSpec((1, tm), lambda i: (0, i)),
                   pl.BlockSpec((1, tm), lambda i: (0, i)),
                   pl.BlockSpec((N_GROUPS, LANES), lambda i: (0, 0))),
        scratch_shapes=[pltpu.VMEM((N_GROUPS, LANES), F32)],
        compiler_params=_cparams(("arbitrary",)),
        name="out_norm_route",
    )(a, wo, x, g, b, rw_t, rb_t, tri)


def _tile_plan(gid, rank, counts, n_tiles_max):
    cnt = counts[:, 0]
    padded = ((cnt + MOE_TILE - 1) // MOE_TILE) * MOE_TILE
    ends = jnp.cumsum(padded)
    starts = ends - padded
    n_tiles = ends[-1] // MOE_TILE
    tile_row = jnp.arange(n_tiles_max, dtype=jnp.int32) * MOE_TILE
    tile_group = jnp.sum((tile_row[:, None] >= ends[None, :]).astype(jnp.int32), axis=1)
    last_group = jnp.sum((ends[-1] - 1 >= ends).astype(jnp.int32))
    tile_group = jnp.minimum(tile_group, last_group)
    rows_left = (starts + cnt)[tile_group] - tile_row
    tile_rows = jnp.clip(rows_left, 0, MOE_TILE)
    dest = starts[gid[0]] + rank[0]
    pad_lo = jnp.concatenate([starts + cnt, ends[-1:]])
    pad_hi = jnp.concatenate([ends, jnp.full((1,), n_tiles_max * MOE_TILE, ends.dtype)])
    return (dest.astype(jnp.int32), pad_lo.astype(jnp.int32), pad_hi.astype(jnp.int32),
            tile_group.astype(jnp.int32), tile_rows.astype(jnp.int32), n_tiles.reshape(1).astype(jnp.int32))


def _invert_plan_kernel(pad_lo_ref, pad_hi_ref, dest_ref, src_ref):
    step = pl.program_id(0)

    @pl.when(step == 0)
    def _():
        def clear(p, carry):
            src_ref[p] = 0
            return carry
        for g in range(N_GROUPS + 1):
            lax.fori_loop(pad_lo_ref[g], pad_hi_ref[g], clear, 0)

    def place(r, carry):
        src_ref[dest_ref[0, r]] = step * PLAN_CHUNK + r
        return carry

    lax.fori_loop(0, PLAN_CHUNK, place, 0, unroll=8)


def _invert_plan(dest, pad_lo, pad_hi, n_rows):
    t = dest.shape[0]
    chunks = t // PLAN_CHUNK
    return pl.pallas_call(
        _invert_plan_kernel,
        out_shape=jax.ShapeDtypeStruct((n_rows,), jnp.int32),
        grid=(chunks,),
        in_specs=[pl.BlockSpec(memory_space=pltpu.SMEM),
                  pl.BlockSpec(memory_space=pltpu.SMEM),
                  pl.BlockSpec((None, 1, PLAN_CHUNK), lambda i: (i, 0, 0), memory_space=pltpu.SMEM)],
        out_specs=pl.BlockSpec((n_rows,), lambda i: (0,), memory_space=pltpu.SMEM),
        compiler_params=_cparams(("arbitrary",)),
        name="invert_plan",
    )(pad_lo, pad_hi, dest.reshape(chunks, 1, PLAN_CHUNK))


def _row_copy(src, dst, src_row, dst_row, sem):
    return pltpu.make_async_copy(src.at[pl.ds(src_row, 1)], dst.at[pl.ds(dst_row, 1)], sem)


def _for_rows(n_rows, fn):
    def chunk(q, carry):
        for u in range(8):
            fn(q * 8 + u)
        return carry

    def single(r, carry):
        fn(r)
        return carry

    full = n_rows // 8
    lax.fori_loop(0, full, chunk, 0)
    lax.fori_loop(full * 8, n_rows, single, 0)


def _moe_rows_kernel(tile_group_ref, tile_rows_ref, n_tiles_ref, src_ref, src_next_ref, x_hbm,
                     wg_ref, wu_ref, wd_ref, g_ref, b_ref, out_hbm, xbuf, obuf, wg_bf, wu_bf, wd_bf, gsem, ssem):
    k = pl.program_id(0)
    n_tiles = n_tiles_ref[0]
    slot = k % 2

    def row_copies(tile, one_start, one_wait, wait_all, start):
        n_rows = tile_rows_ref[tile]

        @pl.when(n_rows == MOE_TILE)
        def _():
            if start:
                for r in range(MOE_TILE):
                    one_start(r)
            else:
                wait_all()

        @pl.when(n_rows != MOE_TILE)
        def _():
            _for_rows(n_rows, one_start if start else one_wait)

    def gather(tile, rows_ref, buf_slot, start):
        dst = xbuf.at[buf_slot]
        sem = gsem.at[buf_slot]
        row_copies(tile,
                   lambda r: _row_copy(x_hbm, dst, rows_ref[0, r], r, sem).start(),
                   lambda r: _row_copy(x_hbm, dst, 0, 0, sem).wait(),
                   lambda: pltpu.make_async_copy(x_hbm.at[pl.ds(0, MOE_TILE)], dst, sem).wait(),
                   start)

    def scatter(tile, rows_ref, buf_slot, start):
        src = obuf.at[buf_slot]
        sem = ssem.at[buf_slot]
        row_copies(tile,
                   lambda r: _row_copy(src, out_hbm, r, rows_ref[0, r], sem).start(),
                   lambda r: _row_copy(src, out_hbm, 0, 0, sem).wait(),
                   lambda: pltpu.make_async_copy(src, out_hbm.at[pl.ds(0, MOE_TILE)], sem).wait(),
                   start)

    @pl.when(k == 0)
    def _():
        xbuf[...] = jnp.zeros(xbuf.shape, F32)
        gather(0, src_ref, 0, True)

    @pl.when(k + 1 < n_tiles)
    def _():
        gather(k + 1, src_next_ref, 1 - slot, True)

    @pl.when(k < n_tiles)
    def _():
        gather(k, src_ref, slot, False)

        @pl.when(k >= 2)
        def _():
            scatter(k - 2, src_ref, slot, False)

        @pl.when((k == 0) | (tile_group_ref[k] != tile_group_ref[jnp.maximum(k - 1, 0)]))
        def _():
            for j in range(GROUP_SIZE):
                cols = slice(j * EXPERT_FF, (j + 1) * EXPERT_FF)
                wg_bf[:, cols] = wg_ref[j].astype(BF16)
                wu_bf[:, cols] = wu_ref[j].astype(BF16)
                wd_bf[cols, :] = wd_ref[j].astype(BF16)

        x = xbuf[slot, :, :D_MODEL]
        cw = xbuf[slot, :, D_MODEL:]
        xb = x.astype(BF16)
        hg = _dot(xb, wg_bf[...])
        hu = _dot(xb, wu_bf[...])
        cw_wide = jnp.concatenate([jnp.broadcast_to(cw[:, j:j + 1], (MOE_TILE, EXPERT_FF))
                                   for j in range(GROUP_SIZE)], axis=1)
        hid = (hg / (1.0 + jnp.exp(-hg))) * hu * cw_wide
        y = _dot(hid.astype(BF16), wd_bf[...])
        obuf[slot] = _ln(DEEPNORM_ALPHA * x + y, g_ref[...], b_ref[...])
        scatter(k, src_ref, slot, True)

    @pl.when(k == n_tiles - 1)
    def _():
        @pl.when(k >= 1)
        def _():
            scatter(k - 1, src_ref, 1 - slot, False)
        scatter(k, src_ref, slot, False)


def _moe_rows(x_rows, src, tile_group, tile_rows, n_tiles, wg, wu, wd, g, b, layer):
    t = x_rows.shape[0]
    n_tiles_max = src.shape[0] // MOE_TILE
    d, gs, ff = D_MODEL, GROUP_SIZE, EXPERT_FF
    src3 = src.reshape(n_tiles_max, 1, MOE_TILE)
    grid_spec = pltpu.PrefetchScalarGridSpec(
        num_scalar_prefetch=3,
        grid=(n_tiles_max,),
        in_specs=[pl.BlockSpec((None, 1, MOE_TILE), lambda k, tg, tr, nt: (k, 0, 0), memory_space=pltpu.SMEM),
                  pl.BlockSpec((None, 1, MOE_TILE), lambda k, tg, tr, nt: (jnp.minimum(k + 1, n_tiles_max - 1), 0, 0),
                               memory_space=pltpu.SMEM),
                  pl.BlockSpec(memory_space=pl.ANY),
                  pl.BlockSpec((None, gs, d, ff), lambda k, tg, tr, nt: (layer, tg[k], 0, 0)),
                  pl.BlockSpec((None, gs, d, ff), lambda k, tg, tr, nt: (layer, tg[k], 0, 0)),
                  pl.BlockSpec((None, gs, ff, d), lambda k, tg, tr, nt: (layer, tg[k], 0, 0)),
                  pl.BlockSpec((1, d), lambda k, tg, tr, nt: (0, 0)),
                  pl.BlockSpec((1, d), lambda k, tg, tr, nt: (0, 0))],
        out_specs=pl.BlockSpec(memory_space=pl.ANY),
        scratch_shapes=[pltpu.VMEM((2, MOE_TILE, ROW_WIDE), F32), pltpu.VMEM((2, MOE_TILE, d), F32),
                        pltpu.VMEM((d, gs * ff), BF16), pltpu.VMEM((d, gs * ff), BF16), pltpu.VMEM((gs * ff, d), BF16),
                        pltpu.SemaphoreType.DMA((2,)), pltpu.SemaphoreType.DMA((2,))])
    return pl.pallas_call(
        _moe_rows_kernel,
        out_shape=jax.ShapeDtypeStruct((t, d), F32),
        grid_spec=grid_spec,
        compiler_params=_cparams(("arbitrary",)),
        name="grouped_moe",
    )(tile_group, tile_rows, n_tiles, src3, src3, x_rows, wg, wu, wd, g, b)


def _rope_tables(seq):
    inv_freq = ROPE_BASE ** (-jnp.arange(0, MLA_ROPE_DIM, 2, dtype=F32) / MLA_ROPE_DIM)
    ang = jnp.arange(seq, dtype=F32)[:, None] * inv_freq[None, :]
    cos, sin = jnp.cos(ang), jnp.sin(ang)
    cos2 = jnp.concatenate([cos, cos], axis=1)
    sin2 = jnp.concatenate([-sin, sin], axis=1)
    zero = jnp.zeros((seq, LANES - MLA_ROPE_DIM), F32)
    return (jnp.concatenate([cos2, zero], axis=1), jnp.concatenate([sin2, zero], axis=1), cos2.T, sin2.T)


def _swap_halves(w):
    half = MLA_ROPE_DIM // 2
    return jnp.concatenate([w[..., half:], w[..., :half]], axis=-1)


def kernel(x, diff_w_qkv, diff_lambda, diff_subln_g, diff_w_o, mla_w_kv_a, mla_kv_norm_g, mla_w_kv_b,
           mla_w_q_a, mla_q_norm_g, mla_w_q_b, mla_w_o, router_w, router_b, moe_w_gate, moe_w_up,
           moe_w_down, ln_g, ln_b):
    batch, seq, d = x.shape
    t = batch * seq
    xs = x.reshape(t, d)

    wq, wk, wv = jnp.split(diff_w_qkv, 3, axis=-1)
    w_k = wk.astype(BF16)
    w_qv_t = jnp.concatenate([wq * (LOG2E * DIFF_HEAD_DIM ** -0.5), wv], axis=-1).transpose(0, 2, 1).astype(BF16)
    w_o_a = diff_w_o.astype(BF16)
    w_o_b = mla_w_o.astype(BF16)
    slopes = LOG2E * 2.0 ** (-8.0 * jnp.arange(1, DIFF_HEADS + 1, dtype=F32) / DIFF_HEADS)
    slope_hi = slopes.astype(BF16).astype(F32)
    slope_lo = (slopes - slope_hi).astype(BF16).astype(F32)
    rw_t = router_w.T.reshape(N_GROUPS, GROUP_SIZE, d).transpose(1, 0, 2).reshape(N_EXPERTS, d)
    rb_t = router_b.reshape(N_GROUPS, GROUP_SIZE).T.reshape(N_EXPERTS, 1).astype(F32)
    cs, sn, cs_t, sn_t = _rope_tables(seq)
    kv_rope_w = mla_w_kv_a[:, MLA_KV_RANK:]
    w_kv_a = jnp.concatenate([mla_w_kv_a, _swap_halves(kv_rope_w)], axis=1).astype(BF16)
    w_kv_b = mla_w_kv_b.reshape(MLA_KV_RANK, MLA_HEADS, MLA_NOPE_DIM + MLA_V_DIM)
    w_kn = w_kv_b[..., :MLA_NOPE_DIM].reshape(MLA_KV_RANK, -1).astype(BF16)
    w_v_t = w_kv_b[..., MLA_NOPE_DIM:].reshape(MLA_KV_RANK, -1).T.astype(BF16)
    wqb = mla_w_q_b.reshape(-1, MLA_Q_RANK, MLA_HEADS, MLA_NOPE_DIM + MLA_ROPE_DIM)
    w_q_b_t = jnp.concatenate([wqb, _swap_halves(wqb[..., MLA_NOPE_DIM:])], axis=-1).reshape(
        -1, MLA_Q_RANK, MLA_HEADS * MLA_QK_PAD).transpose(0, 2, 1).astype(BF16)
    w_q_a = mla_w_q_a.astype(BF16)

    alibi_fix, diag_mask = _diagonal_tables()
    tri = jnp.triu(jnp.ones((ROUTE_TILE, ROUTE_TILE), BF16), 1)
    n_tiles_max = t // MOE_TILE + N_GROUPS

    kcat = vt_shared = None
    for layer in range(DEPTH):
        if layer < N_A_LAYERS:
            k, qt, vt = _qkv_proj(xs, w_k[layer], w_qv_t[layer])
            attn = _diff_attention(qt, k, vt, slope_hi, slope_lo, diff_lambda[layer],
                                   diff_subln_g[layer].reshape(-1, 1), alibi_fix, batch, seq, _lambda_init(layer))
            w_o = w_o_a[layer]
        else:
            j = layer - N_A_LAYERS
            if j == 0:
                kcat, vt_shared = _mla_kv_proj(xs, w_kv_a, mla_kv_norm_g.reshape(1, -1), w_kn, w_v_t, cs, sn, seq)
            qt = _mla_q_proj(xs, w_q_a[j], mla_q_norm_g[j].reshape(1, -1), w_q_b_t[j], cs_t, sn_t, seq)
            attn = _mla_attention(qt, kcat, vt_shared, diag_mask, batch, seq)
            w_o = w_o_b[j]
        x1, gid, rank, counts = _out_norm_route(attn, w_o, xs, ln_g[layer, 0].reshape(1, -1),
                                                ln_b[layer, 0].reshape(1, -1), rw_t, rb_t, tri)
        dest, pad_lo, pad_hi, tile_group, tile_rows, n_tiles = _tile_plan(gid, rank, counts, n_tiles_max)
        src = _invert_plan(dest, pad_lo, pad_hi, n_tiles_max * MOE_TILE)
        xs = _moe_rows(x1, src, tile_group, tile_rows, n_tiles, moe_w_gate, moe_w_up, moe_w_down,
                       ln_g[layer, 1].reshape(1, -1), ln_b[layer, 1].reshape(1, -1), layer)
    return xs.reshape(batch, seq, d)
```

```python
import functools
import math

import jax
import jax.numpy as jnp
from jax import lax
from jax.experimental import pallas as pl
from jax.experimental.pallas import tpu as pltpu

D_MODEL = 1024
DEPTH = 4
CHUNK = 64
N_A_LAYERS = DEPTH // 2
DIFF_HEAD_DIM = 64
DIFF_V_DIM = 2 * DIFF_HEAD_DIM
DIFF_HEADS = D_MODEL // DIFF_V_DIM
MLA_NOPE_DIM = 128
MLA_ROPE_DIM = 64
MLA_V_DIM = 128
MLA_HEADS = D_MODEL // MLA_V_DIM
MLA_Q_RANK = 384
MLA_KV_RANK = 256
MLA_QK_PAD = 256
ROPE_BASE = 10000.0
N_EXPERTS = 32
N_GROUPS = 8
GROUP_SIZE = N_EXPERTS // N_GROUPS
EXPERT_FF = 256
NORM_EPS = 1e-5
DEEPNORM_ALPHA = (2 * DEPTH) ** 0.25
LOG2E = math.log2(math.e)

LANES = 128
T_TILE = 256
GROUP_TILES = 2
Q_TILE = GROUP_TILES * T_TILE
MASKED = 1e30
V_ROWS = DIFF_V_DIM + 16
DIFF_HEAD_PAIR = 4
MLA_HEAD_PAIR = 4
PROJ_TILES = 2
PROJ_ROWS = PROJ_TILES * T_TILE
ROUTE_TILE = 1024
MOE_TILE = 256
PLAN_CHUNK = 2048
ROW_WIDE = D_MODEL + LANES
VMEM_LIMIT = 48 * 1024 * 1024

F32 = jnp.float32
BF16 = jnp.bfloat16


def _lambda_init(layer):
    return 0.8 - 0.6 * math.exp(-0.3 * layer)


def _cparams(sem):
    return pltpu.CompilerParams(dimension_semantics=sem, vmem_limit_bytes=VMEM_LIMIT)


def _dot(a, b):
    return jnp.dot(a, b, preferred_element_type=F32)


def _dot_nt(a, b):
    return lax.dot_general(a, b, (((1,), (1,)), ((), ())), preferred_element_type=F32)


def _rms(x, g):
    return x * lax.rsqrt(jnp.mean(x * x, axis=-1, keepdims=True) + NORM_EPS) * g


def _ln(x, g, b):
    mu = jnp.mean(x, axis=-1, keepdims=True)
    xc = x - mu
    var = jnp.mean(xc * xc, axis=-1, keepdims=True)
    return xc * lax.rsqrt(var + NORM_EPS) * g + b


def _rope_halves(a, cs, sn):
    return a * cs + pltpu.roll(a, MLA_ROPE_DIM, axis=1) * sn


def _qkv_kernel(x_ref, wk_ref, wqv_ref, k_ref, qt_ref, vt_ref):
    xb = x_ref[...].astype(BF16)
    k_ref[...] = _dot(xb, wk_ref[...]).astype(BF16)
    qv = _dot_nt(wqv_ref[...], xb)
    for h in range(DIFF_HEADS):
        qt_ref[0, h] = qv[h * LANES:(h + 1) * LANES].astype(BF16)
        vt_ref[0, h, :DIFF_V_DIM] = qv[D_MODEL + h * LANES:D_MODEL + (h + 1) * LANES].astype(BF16)
        vt_ref[0, h, DIFF_V_DIM:] = jnp.ones((V_ROWS - DIFF_V_DIM, T_TILE), BF16)


def _qkv_proj(x, wk, wqv):
    t, d = x.shape
    nt = t // T_TILE

    def head_major(rows):
        return jax.ShapeDtypeStruct((nt, DIFF_HEADS, rows, T_TILE), BF16)

    def head_block(rows):
        return pl.BlockSpec((1, DIFF_HEADS, rows, T_TILE), lambda i: (i, 0, 0, 0))

    return pl.pallas_call(
        _qkv_kernel,
        out_shape=(jax.ShapeDtypeStruct((t, d), BF16), head_major(LANES), head_major(V_ROWS)),
        grid=(nt,),
        in_specs=[pl.BlockSpec((T_TILE, d), lambda i: (i, 0)),
                  pl.BlockSpec(wk.shape, lambda i: (0, 0)),
                  pl.BlockSpec(wqv.shape, lambda i: (0, 0))],
        out_specs=(pl.BlockSpec((T_TILE, d), lambda i: (i, 0)), head_block(LANES), head_block(V_ROWS)),
        compiler_params=_cparams(("parallel",)),
        name="qkv_proj",
    )(x, wk, wqv)


def _mla_kv_kernel(x_ref, wa_ref, g_ref, wk_ref, wv_ref, cs_ref, sn_ref, kcat_ref, vt_ref):
    kva = _dot(x_ref[...].astype(BF16), wa_ref[...])
    c_kv = _rms(kva[:, :MLA_KV_RANK], g_ref[...]).astype(BF16)
    kr = _rope_halves(kva[:, MLA_KV_RANK:], cs_ref[...], sn_ref[...]).astype(BF16)
    kn = _dot(c_kv, wk_ref[...])
    vt = _dot_nt(wv_ref[...], c_kv)
    for h in range(MLA_HEADS):
        base = h * MLA_QK_PAD
        kcat_ref[:, base:base + MLA_NOPE_DIM] = kn[:, h * MLA_NOPE_DIM:(h + 1) * MLA_NOPE_DIM].astype(BF16)
        kcat_ref[:, base + MLA_NOPE_DIM:base + MLA_QK_PAD] = kr
        for u in range(PROJ_TILES):
            cols = slice(u * T_TILE, (u + 1) * T_TILE)
            vt_ref[u, h, :MLA_V_DIM] = vt[h * MLA_V_DIM:(h + 1) * MLA_V_DIM, cols].astype(BF16)
            vt_ref[u, h, MLA_V_DIM:] = jnp.ones((V_ROWS - MLA_V_DIM, T_TILE), BF16)


def _mla_kv_proj(x, wa, g, wk, wv, cs, sn, seq):
    t, d = x.shape
    ns = seq // PROJ_ROWS
    return pl.pallas_call(
        _mla_kv_kernel,
        out_shape=(jax.ShapeDtypeStruct((t, MLA_HEADS * MLA_QK_PAD), BF16),
                   jax.ShapeDtypeStruct((t // T_TILE, MLA_HEADS, V_ROWS, T_TILE), BF16)),
        grid=(t // PROJ_ROWS,),
        in_specs=[pl.BlockSpec((PROJ_ROWS, d), lambda i: (i, 0)),
                  pl.BlockSpec(wa.shape, lambda i: (0, 0)),
                  pl.BlockSpec(g.shape, lambda i: (0, 0)),
                  pl.BlockSpec(wk.shape, lambda i: (0, 0)),
                  pl.BlockSpec(wv.shape, lambda i: (0, 0)),
                  pl.BlockSpec((PROJ_ROWS, LANES), lambda i: (i % ns, 0)),
                  pl.BlockSpec((PROJ_ROWS, LANES), lambda i: (i % ns, 0))],
        out_specs=(pl.BlockSpec((PROJ_ROWS, MLA_HEADS * MLA_QK_PAD), lambda i: (i, 0)),
                   pl.BlockSpec((PROJ_TILES, MLA_HEADS, V_ROWS, T_TILE), lambda i: (i, 0, 0, 0))),
        compiler_params=_cparams(("parallel",)),
        name="mla_kv_proj",
    )(x, wa, g, wk, wv, cs, sn)


def _mla_q_kernel(x_ref, wa_ref, g_ref, wb_ref, cst_ref, snt_ref, qt_ref, *, scale):
    qa = _dot(x_ref[...].astype(BF16), wa_ref[...])
    qn = _rms(qa, g_ref[...]).astype(BF16)
    qt = _dot_nt(wb_ref[...], qn) * scale
    cst = cst_ref[...]
    snt = snt_ref[...]
    zero = jnp.zeros((MLA_ROPE_DIM, T_TILE), BF16)
    for h in range(MLA_HEADS):
        base = h * MLA_QK_PAD
        r0 = base + MLA_NOPE_DIM
        rope = qt[r0:r0 + MLA_ROPE_DIM] * cst + qt[r0 + MLA_ROPE_DIM:base + MLA_QK_PAD] * snt
        for u in range(PROJ_TILES):
            cols = slice(u * T_TILE, (u + 1) * T_TILE)
            qt_ref[u, h, 0:MLA_NOPE_DIM] = qt[base:r0, cols].astype(BF16)
            qt_ref[u, h, MLA_NOPE_DIM:MLA_NOPE_DIM + MLA_ROPE_DIM] = rope[:, cols].astype(BF16)
            qt_ref[u, h, MLA_NOPE_DIM + MLA_ROPE_DIM:MLA_QK_PAD] = zero


def _mla_q_proj(x, wa, g, wb_t, cst, snt, seq):
    t, d = x.shape
    ns = seq // PROJ_ROWS
    scale = LOG2E * (MLA_NOPE_DIM + MLA_ROPE_DIM) ** -0.5
    return pl.pallas_call(
        functools.partial(_mla_q_kernel, scale=scale),
        out_shape=jax.ShapeDtypeStruct((t // T_TILE, MLA_HEADS, MLA_QK_PAD, T_TILE), BF16),
        grid=(t // PROJ_ROWS,),
        in_specs=[pl.BlockSpec((PROJ_ROWS, d), lambda i: (i, 0)),
                  pl.BlockSpec(wa.shape, lambda i: (0, 0)),
                  pl.BlockSpec(g.shape, lambda i: (0, 0)),
                  pl.BlockSpec(wb_t.shape, lambda i: (0, 0)),
                  pl.BlockSpec((MLA_ROPE_DIM, PROJ_ROWS), lambda i: (0, i % ns)),
                  pl.BlockSpec((MLA_ROPE_DIM, PROJ_ROWS), lambda i: (0, i % ns))],
        out_specs=pl.BlockSpec((PROJ_TILES, MLA_HEADS, MLA_QK_PAD, T_TILE), lambda i: (i, 0, 0, 0)),
        compiler_params=_cparams(("parallel",)),
        name="mla_q_proj",
    )(x, wa, g, wb_t, cst, snt)


def _consume_group(tiles, m_ref, acc_ref):
    for c, comp_tiles in enumerate(tiles):
        m_group = None
        for _, m_t, _, _ in comp_tiles:
            m_group = m_t if m_group is None else jnp.maximum(m_group, m_t)
        pv = None
        for s_t, _, shift, vt in comp_tiles:
            part = _dot(vt, jnp.exp2(s_t - (m_group - shift)).astype(BF16))
            pv = part if pv is None else pv + part
        m_prev = m_ref[c]
        m_new = jnp.maximum(m_prev, m_group)
        acc_ref[c] = jnp.exp2(m_prev - m_new) * acc_ref[c] + jnp.exp2(m_group - m_new) * pv
        m_ref[c] = m_new


def _init_online_softmax(m_ref, acc_ref):
    m_ref[...] = jnp.full(m_ref.shape, -jnp.inf, F32)
    acc_ref[...] = jnp.zeros(acc_ref.shape, F32)


def _normalised(acc):
    return acc[:DIFF_V_DIM] / acc[DIFF_V_DIM:DIFF_V_DIM + 1]


def _attention_pipeline(n, issue_scores, consume):
    @pl.when(n == 0)
    def _():
        issue_scores(0, 0, True)
        consume(0, 0)

    @pl.when(n > 0)
    def _():
        issue_scores(0, 0, False)

        def pair(p, carry):
            issue_scores(2 * p + 1, 1, False)
            consume(2 * p, 0)
            issue_scores(2 * p + 2, 0, False)
            consume(2 * p + 1, 1)
            return carry

        lax.fori_loop(0, (n - 1) // 2, pair, 0)

        @pl.when(n % 2 == 1)
        def _():
            issue_scores(n, 1, True)
            consume(n - 1, 0)
            consume(n, 1)

        @pl.when(n % 2 == 0)
        def _():
            issue_scores(n - 1, 1, False)
            consume(n - 2, 0)
            issue_scores(n, 0, True)
            consume(n - 1, 1)
            consume(n, 0)


def _diff_attn_kernel(slh_ref, sll_ref, lam_ref, g_ref, dm_ref, qt_ref, k_ref, vt_ref, o_ref,
                      s_buf, max_buf, m_ref, acc_ref, *, lambda_init):
    i = pl.program_id(2)
    heads = range(DIFF_HEAD_PAIR)
    row = lax.broadcasted_iota(jnp.int32, (LANES, Q_TILE), 0)
    slopes, qs_aug = [], []
    for hh in heads:
        head = pl.program_id(1) * DIFF_HEAD_PAIR + hh
        sl_hi = slh_ref[head]
        sl_lo = sll_ref[head]
        slopes.append(sl_hi + sl_lo)
        qf = jnp.concatenate([qt_ref[u, hh] for u in range(GROUP_TILES)], axis=1).astype(F32)
        slope_rows = jnp.where(row == 0, sl_hi, jnp.where(row == 1, sl_lo, 0.0)).astype(BF16)
        qs_aug += [jnp.concatenate([jnp.where(keep, qf, 0.0).astype(BF16), slope_rows], axis=0)
                   for keep in (row < DIFF_HEAD_DIM, row >= DIFF_HEAD_DIM)]
    n_chains = 2 * DIFF_HEAD_PAIR
    key_off = lax.broadcasted_iota(jnp.int32, (T_TILE, LANES), 0).astype(F32)
    lane = lax.broadcasted_iota(jnp.int32, (T_TILE, LANES), 1)
    key_cols = jnp.where(lane < 2, key_off, 0.0).astype(BF16)
    _init_online_softmax(m_ref, acc_ref)

    def tile_shift(j, hh):
        return slopes[hh] * (j * T_TILE - i * Q_TILE).astype(F32)

    def issue_scores(g, slot, diagonal):
        for u in range(GROUP_TILES):
            j = g * GROUP_TILES + u
            off = pl.multiple_of(j * T_TILE, T_TILE)
            for hh in heads:
                k_aug = jnp.concatenate([k_ref[pl.ds(off, T_TILE), hh * LANES:(hh + 1) * LANES], key_cols],
                                        axis=1)
                for c in range(2):
                    n = 2 * hh + c
                    s_t = _dot(k_aug, qs_aug[n])
                    if diagonal:
                        s_t = s_t + slopes[hh] * dm_ref[u * T_TILE:(u + 1) * T_TILE, :]
                    s_buf[slot, n_chains * u + n] = s_t
                    max_buf[slot, n_chains * u + n] = jnp.max(s_t, axis=0, keepdims=True) + tile_shift(j, hh)

    def consume(g, slot):
        tiles = [[] for _ in range(n_chains)]
        for u in range(GROUP_TILES):
            j = g * GROUP_TILES + u
            for hh in heads:
                for c in range(2):
                    n = 2 * hh + c
                    tiles[n].append((s_buf[slot, n_chains * u + n], max_buf[slot, n_chains * u + n],
                                     tile_shift(j, hh), vt_ref[j, hh]))
        _consume_group(tiles, m_ref, acc_ref)

    _attention_pipeline(i, issue_scores, consume)

    lam = lam_ref[...]
    lam_full = (jnp.exp(jnp.sum(lam[0:1] * lam[1:2], axis=1, keepdims=True))
                - jnp.exp(jnp.sum(lam[2:3] * lam[3:4], axis=1, keepdims=True)) + lambda_init)
    for hh in heads:
        o_t = _normalised(acc_ref[2 * hh]) - lam_full * _normalised(acc_ref[2 * hh + 1])
        inv = lax.rsqrt(jnp.mean(o_t * o_t, axis=0, keepdims=True) + NORM_EPS)
        o_t = o_t * inv * (g_ref[...] * (1.0 - lambda_init))
        o_ref[:, hh * DIFF_V_DIM:(hh + 1) * DIFF_V_DIM] = o_t.T.astype(o_ref.dtype)


def _diagonal_tables():
    rk = jnp.arange(Q_TILE, dtype=jnp.int32)[:, None]
    rq = jnp.arange(Q_TILE, dtype=jnp.int32)[None, :]
    visible = (rk // CHUNK) <= (rq // CHUNK)
    alibi_fix = jnp.where(visible, -2.0 * jnp.maximum(rk - rq, 0).astype(F32), -MASKED)
    mask = jnp.where(visible, 0.0, -MASKED).astype(F32)
    return alibi_fix, mask


def _diff_attention(qt, k, vt, slope_hi, slope_lo, lam, g_col, alibi_fix, batch, seq, lambda_init):
    nq = seq // Q_TILE
    nk = seq // T_TILE
    hp = DIFF_HEAD_PAIR
    n_chains = 2 * hp
    return pl.pallas_call(
        functools.partial(_diff_attn_kernel, lambda_init=lambda_init),
        out_shape=jax.ShapeDtypeStruct((batch * seq, DIFF_HEADS * DIFF_V_DIM), BF16),
        grid=(batch, DIFF_HEADS // hp, nq),
        in_specs=[pl.BlockSpec(memory_space=pltpu.SMEM),
                  pl.BlockSpec(memory_space=pltpu.SMEM),
                  pl.BlockSpec(lam.shape, lambda b, h, i: (0, 0)),
                  pl.BlockSpec(g_col.shape, lambda b, h, i: (0, 0)),
                  pl.BlockSpec(alibi_fix.shape, lambda b, h, i: (0, 0)),
                  pl.BlockSpec((GROUP_TILES, hp, LANES, T_TILE), lambda b, h, i: (b * nq + i, h, 0, 0)),
                  pl.BlockSpec((seq, hp * LANES), lambda b, h, i: (b, h)),
                  pl.BlockSpec((nk, hp, V_ROWS, T_TILE), lambda b, h, i: (b, h, 0, 0))],
        out_specs=pl.BlockSpec((Q_TILE, hp * LANES), lambda b, h, i: (b * nq + i, h)),
        scratch_shapes=[pltpu.VMEM((2, n_chains * GROUP_TILES, T_TILE, Q_TILE), F32),
                        pltpu.VMEM((2, n_chains * GROUP_TILES, 1, Q_TILE), F32),
                        pltpu.VMEM((n_chains, 1, Q_TILE), F32),
                        pltpu.VMEM((n_chains, V_ROWS, Q_TILE), F32)],
        compiler_params=_cparams(("parallel", "parallel", "arbitrary")),
        name="diff_attention",
    )(slope_hi, slope_lo, lam, g_col, alibi_fix, qt, k, vt)


def _mla_attn_kernel(mask_ref, qt_ref, k_ref, vt_ref, o_ref, s_buf, max_buf, m_ref, acc_ref):
    i = pl.program_id(2)
    heads = range(MLA_HEAD_PAIR)
    qts = [jnp.concatenate([qt_ref[u, hh] for u in range(GROUP_TILES)], axis=1) for hh in heads]
    _init_online_softmax(m_ref, acc_ref)

    def issue_scores(g, slot, diagonal):
        for u in range(GROUP_TILES):
            off = pl.multiple_of((g * GROUP_TILES + u) * T_TILE, T_TILE)
            for hh in heads:
                s_t = _dot(k_ref[pl.ds(off, T_TILE), hh * MLA_QK_PAD:(hh + 1) * MLA_QK_PAD], qts[hh])
                if diagonal:
                    s_t = s_t + mask_ref[u * T_TILE:(u + 1) * T_TILE, :]
                s_buf[slot, MLA_HEAD_PAIR * u + hh] = s_t
                max_buf[slot, MLA_HEAD_PAIR * u + hh] = jnp.max(s_t, axis=0, keepdims=True)

    def consume(g, slot):
        tiles = [[(s_buf[slot, MLA_HEAD_PAIR * u + hh], max_buf[slot, MLA_HEAD_PAIR * u + hh], 0.0,
                   vt_ref[g * GROUP_TILES + u, hh]) for u in range(GROUP_TILES)] for hh in heads]
        _consume_group(tiles, m_ref, acc_ref)

    _attention_pipeline(i, issue_scores, consume)
    for hh in heads:
        o_ref[:, hh * MLA_V_DIM:(hh + 1) * MLA_V_DIM] = _normalised(acc_ref[hh]).T.astype(o_ref.dtype)


def _mla_attention(qt, kcat, vt, mask, batch, seq):
    nq = seq // Q_TILE
    nk = seq // T_TILE
    hp = MLA_HEAD_PAIR
    return pl.pallas_call(
        _mla_attn_kernel,
        out_shape=jax.ShapeDtypeStruct((batch * seq, MLA_HEADS * MLA_V_DIM), BF16),
        grid=(batch, MLA_HEADS // hp, nq),
        in_specs=[pl.BlockSpec(mask.shape, lambda b, h, i: (0, 0)),
                  pl.BlockSpec((GROUP_TILES, hp, MLA_QK_PAD, T_TILE), lambda b, h, i: (b * nq + i, h, 0, 0)),
                  pl.BlockSpec((seq, hp * MLA_QK_PAD), lambda b, h, i: (b, h)),
                  pl.BlockSpec((nk, hp, V_ROWS, T_TILE), lambda b, h, i: (b, h, 0, 0))],
        out_specs=pl.BlockSpec((Q_TILE, hp * MLA_V_DIM), lambda b, h, i: (b * nq + i, h)),
        scratch_shapes=[pltpu.VMEM((2, hp * GROUP_TILES, T_TILE, Q_TILE), F32),
                        pltpu.VMEM((2, hp * GROUP_TILES, 1, Q_TILE), F32),
                        pltpu.VMEM((hp, 1, Q_TILE), F32),
                        pltpu.VMEM((hp, V_ROWS, Q_TILE), F32)],
        compiler_params=_cparams(("parallel", "parallel", "arbitrary")),
        name="mla_attention",
    )(mask, qt, kcat, vt)


def _split_bf16(x):
    hi = x.astype(BF16)
    lo = (x - hi.astype(F32)).astype(BF16)
    return hi, lo


def _route(logits_t, bias_t):
    score = 1.0 / (1.0 + jnp.exp(-logits_t))
    sel = score + bias_t
    a = [sel[N_GROUPS * j:N_GROUPS * (j + 1)] for j in range(GROUP_SIZE)]
    sc = [score[N_GROUPS * j:N_GROUPS * (j + 1)] for j in range(GROUP_SIZE)]
    hi1, lo1 = jnp.maximum(a[0], a[1]), jnp.minimum(a[0], a[1])
    hi2, lo2 = jnp.maximum(a[2], a[3]), jnp.minimum(a[2], a[3])
    group_score = jnp.maximum(hi1, hi2) + jnp.maximum(jnp.minimum(hi1, hi2), jnp.maximum(lo1, lo2))
    gidx = lax.broadcasted_iota(jnp.int32, group_score.shape, 0)
    best = jnp.max(group_score, axis=0, keepdims=True)
    first_best = jnp.min(jnp.where(group_score == best, gidx, N_GROUPS), axis=0, keepdims=True)
    in_group = gidx == first_best
    picked = []
    for j in range(GROUP_SIZE):
        beaten_by = jnp.zeros(a[j].shape, jnp.int32)
        for i in range(GROUP_SIZE):
            if i == j:
                continue
            ahead = (a[i] > a[j]) | ((a[i] == a[j]) if i < j else False)
            beaten_by = beaten_by + ahead.astype(jnp.int32)
        picked.append(jnp.where(beaten_by < 2, sc[j], 0.0))
    denom = (picked[0] + picked[1]) + (picked[2] + picked[3])
    member_w = [jnp.sum(jnp.where(in_group, p / denom, 0.0), axis=0, keepdims=True) for p in picked]
    return in_group, member_w


def _out_norm_route_kernel(a_ref, wo_ref, x_ref, g_ref, b_ref, rw_ref, rb_ref, tri_ref,
                           x1_ref, gid_ref, rank_ref, cnt_ref, seen_ref):
    @pl.when(pl.program_id(0) == 0)
    def _():
        seen_ref[...] = jnp.zeros(seen_ref.shape, F32)

    y = _dot(a_ref[...], wo_ref[...])
    x1 = _ln(DEEPNORM_ALPHA * x_ref[...] + y, g_ref[...], b_ref[...])
    x1_ref[:, :D_MODEL] = x1
    xh, xl = _split_bf16(x1)
    wh, wl = _split_bf16(rw_ref[...])
    by_xh = _dot_nt(jnp.concatenate([wh, wl], axis=0), xh)
    logits_t = by_xh[:N_EXPERTS] + (by_xh[N_EXPERTS:] + _dot_nt(wh, xl))
    in_group, member_w = _route(logits_t, rb_ref[...])
    tm = logits_t.shape[1]

    row = lax.broadcasted_iota(jnp.int32, (N_GROUPS, tm), 0)
    w_rows = jnp.zeros((N_GROUPS, tm), F32)
    for j in range(GROUP_SIZE):
        w_rows = jnp.where(row == j, member_w[j], w_rows)
    w_rows = jnp.concatenate([w_rows, jnp.zeros((LANES - N_GROUPS, tm), F32)], axis=0)
    x1_ref[:, D_MODEL:] = w_rows.T

    onehot = in_group.astype(F32)
    earlier = _dot(onehot.astype(BF16), tri_ref[...])
    seen = seen_ref[:, 0:1]
    gidx = lax.broadcasted_iota(jnp.int32, onehot.shape, 0)
    gid_ref[...] = jnp.sum(jnp.where(in_group, gidx, 0), axis=0, keepdims=True)
    rank_ref[...] = jnp.sum(onehot * (earlier + seen), axis=0, keepdims=True).astype(jnp.int32)
    seen_new = seen_ref[...] + jnp.sum(onehot, axis=1, keepdims=True)
    seen_ref[...] = seen_new
    cnt_ref[...] = seen_new.astype(jnp.int32)


def _out_norm_route(a, wo, x, g, b, rw_t, rb_t, tri, tm=ROUTE_TILE):
    t, d = x.shape
    return pl.pallas_call(
        _out_norm_route_kernel,
        out_shape=(jax.ShapeDtypeStruct((t, ROW_WIDE), F32),
                   jax.ShapeDtypeStruct((1, t), jnp.int32),
                   jax.ShapeDtypeStruct((1, t), jnp.int32),
                   jax.ShapeDtypeStruct((N_GROUPS, LANES), jnp.int32)),
        grid=(t // tm,),
        in_specs=[pl.BlockSpec((tm, d), lambda i: (i, 0)),
                  pl.BlockSpec((d, d), lambda i: (0, 0)),
                  pl.BlockSpec((tm, d), lambda i: (i, 0)),
                  pl.BlockSpec((1, d), lambda i: (0, 0)),
                  pl.BlockSpec((1, d), lambda i: (0, 0)),
                  pl.BlockSpec((N_EXPERTS, d), lambda i: (0, 0)),
                  pl.BlockSpec((N_EXPERTS, 1), lambda i: (0, 0)),
                  pl.BlockSpec((tm, tm), lambda i: (0, 0))],
        out_specs=(pl.BlockSpec((tm, ROW_WIDE), lambda i: (i, 0)),
                   pl.BlockSpec((1, tm), lambda i: (0, i)),
                   pl.BlockSpec((1, tm), lambda i: (0, i)),
                   pl.BlockSpec((N_GROUPS, LANES), lambda i: (0, 0))),
        scratch_shapes=[pltpu.VMEM((N_GROUPS, LANES), F32)],
        compiler_params=_cparams(("arbitrary",)),
        name="out_norm_route",
    )(a, wo, x, g, b, rw_t, rb_t, tri)


def _tile_plan(gid, rank, counts, n_tiles_max):
    cnt = counts[:, 0]
    padded = ((cnt + MOE_TILE - 1) // MOE_TILE) * MOE_TILE
    ends = jnp.cumsum(padded)
    starts = ends - padded
    n_tiles = ends[-1] // MOE_TILE
    tile_row = jnp.arange(n_tiles_max, dtype=jnp.int32) * MOE_TILE
    tile_group = jnp.sum((tile_row[:, None] >= ends[None, :]).astype(jnp.int32), axis=1)
    last_group = jnp.sum((ends[-1] - 1 >= ends).astype(jnp.int32))
    tile_group = jnp.minimum(tile_group, last_group)
    rows_left = (starts + cnt)[tile_group] - tile_row
    tile_rows = jnp.clip(rows_left, 0, MOE_TILE)
    dest = starts[gid[0]] + rank[0]
    pad_lo = jnp.concatenate([starts + cnt, ends[-1:]])
    pad_hi = jnp.concatenate([ends, jnp.full((1,), n_tiles_max * MOE_TILE, ends.dtype)])
    return (dest.astype(jnp.int32), pad_lo.astype(jnp.int32), pad_hi.astype(jnp.int32),
            tile_group.astype(jnp.int32), tile_rows.astype(jnp.int32), n_tiles.reshape(1).astype(jnp.int32))


def _invert_plan_kernel(pad_lo_ref, pad_hi_ref, dest_ref, src_ref):
    step = pl.program_id(0)

    @pl.when(step == 0)
    def _():
        def clear(p, carry):
            src_ref[p] = 0
            return carry
        for g in range(N_GROUPS + 1):
            lax.fori_loop(pad_lo_ref[g], pad_hi_ref[g], clear, 0)

    def place(r, carry):
        src_ref[dest_ref[0, r]] = step * PLAN_CHUNK + r
        return carry

    lax.fori_loop(0, PLAN_CHUNK, place, 0, unroll=8)


def _invert_plan(dest, pad_lo, pad_hi, n_rows):
    t = dest.shape[0]
    chunks = t // PLAN_CHUNK
    return pl.pallas_call(
        _invert_plan_kernel,
        out_shape=jax.ShapeDtypeStruct((n_rows,), jnp.int32),
        grid=(chunks,),
        in_specs=[pl.BlockSpec(memory_space=pltpu.SMEM),
                  pl.BlockSpec(memory_space=pltpu.SMEM),
                  pl.BlockSpec((None, 1, PLAN_CHUNK), lambda i: (i, 0, 0), memory_space=pltpu.SMEM)],
        out_specs=pl.BlockSpec((n_rows,), lambda i: (0,), memory_space=pltpu.SMEM),
        compiler_params=_cparams(("arbitrary",)),
        name="invert_plan",
    )(pad_lo, pad_hi, dest.reshape(chunks, 1, PLAN_CHUNK))


def _row_copy(src, dst, src_row, dst_row, sem):
    return pltpu.make_async_copy(src.at[pl.ds(src_row, 1)], dst.at[pl.ds(dst_row, 1)], sem)


def _for_rows(n_rows, fn):
    def chunk(q, carry):
        for u in range(8):
            fn(q * 8 + u)
        return carry

    def single(r, carry):
        fn(r)
        return carry

    full = n_rows // 8
    lax.fori_loop(0, full, chunk, 0)
    lax.fori_loop(full * 8, n_rows, single, 0)


def _moe_rows_kernel(tile_group_ref, tile_rows_ref, n_tiles_ref, src_ref, src_next_ref, x_hbm,
                     wg_ref, wu_ref, wd_ref, g_ref, b_ref, out_hbm, xbuf, obuf, wg_bf, wu_bf, wd_bf, gsem, ssem):
    k = pl.program_id(0)
    n_tiles = n_tiles_ref[0]
    slot = k % 2

    def row_copies(tile, one_start, one_wait, wait_all, start):
        n_rows = tile_rows_ref[tile]

        @pl.when(n_rows == MOE_TILE)
        def _():
            if start:
                for r in range(MOE_TILE):
                    one_start(r)
            else:
                wait_all()

        @pl.when(n_rows != MOE_TILE)
        def _():
            _for_rows(n_rows, one_start if start else one_wait)

    def gather(tile, rows_ref, buf_slot, start):
        dst = xbuf.at[buf_slot]
        sem = gsem.at[buf_slot]
        row_copies(tile,
                   lambda r: _row_copy(x_hbm, dst, rows_ref[0, r], r, sem).start(),
                   lambda r: _row_copy(x_hbm, dst, 0, 0, sem).wait(),
                   lambda: pltpu.make_async_copy(x_hbm.at[pl.ds(0, MOE_TILE)], dst, sem).wait(),
                   start)

    def scatter(tile, rows_ref, buf_slot, start):
        src = obuf.at[buf_slot]
        sem = ssem.at[buf_slot]
        row_copies(tile,
                   lambda r: _row_copy(src, out_hbm, r, rows_ref[0, r], sem).start(),
                   lambda r: _row_copy(src, out_hbm, 0, 0, sem).wait(),
                   lambda: pltpu.make_async_copy(src, out_hbm.at[pl.ds(0, MOE_TILE)], sem).wait(),
                   start)

    @pl.when(k == 0)
    def _():
        xbuf[...] = jnp.zeros(xbuf.shape, F32)
        gather(0, src_ref, 0, True)

    @pl.when(k + 1 < n_tiles)
    def _():
        gather(k + 1, src_next_ref, 1 - slot, True)

    @pl.when(k < n_tiles)
    def _():
        gather(k, src_ref, slot, False)

        @pl.when(k >= 2)
        def _():
            scatter(k - 2, src_ref, slot, False)

        @pl.when((k == 0) | (tile_group_ref[k] != tile_group_ref[jnp.maximum(k - 1, 0)]))
        def _():
            for j in range(GROUP_SIZE):
                cols = slice(j * EXPERT_FF, (j + 1) * EXPERT_FF)
                wg_bf[:, cols] = wg_ref[j].astype(BF16)
                wu_bf[:, cols] = wu_ref[j].astype(BF16)
                wd_bf[cols, :] = wd_ref[j].astype(BF16)

        x = xbuf[slot, :, :D_MODEL]
        cw = xbuf[slot, :, D_MODEL:]
        xb = x.astype(BF16)
        hg = _dot(xb, wg_bf[...])
        hu = _dot(xb, wu_bf[...])
        cw_wide = jnp.concatenate([jnp.broadcast_to(cw[:, j:j + 1], (MOE_TILE, EXPERT_FF))
                                   for j in range(GROUP_SIZE)], axis=1)
        hid = (hg / (1.0 + jnp.exp(-hg))) * hu * cw_wide
        y = _dot(hid.astype(BF16), wd_bf[...])
        obuf[slot] = _ln(DEEPNORM_ALPHA * x + y, g_ref[...], b_ref[...])
        scatter(k, src_ref, slot, True)

    @pl.when(k == n_tiles - 1)
    def _():
        @pl.when(k >= 1)
        def _():
            scatter(k - 1, src_ref, 1 - slot, False)
        scatter(k, src_ref, slot, False)


def _moe_rows(x_rows, src, tile_group, tile_rows, n_tiles, wg, wu, wd, g, b, layer):
    t = x_rows.shape[0]
    n_tiles_max = src.shape[0] // MOE_TILE
    d, gs, ff = D_MODEL, GROUP_SIZE, EXPERT_FF
    src3 = src.reshape(n_tiles_max, 1, MOE_TILE)
    grid_spec = pltpu.PrefetchScalarGridSpec(
        num_scalar_prefetch=3,
        grid=(n_tiles_max,),
        in_specs=[pl.BlockSpec((None, 1, MOE_TILE), lambda k, tg, tr, nt: (k, 0, 0), memory_space=pltpu.SMEM),
                  pl.BlockSpec((None, 1, MOE_TILE), lambda k, tg, tr, nt: (jnp.minimum(k + 1, n_tiles_max - 1), 0, 0),
                               memory_space=pltpu.SMEM),
                  pl.BlockSpec(memory_space=pl.ANY),
                  pl.BlockSpec((None, gs, d, ff), lambda k, tg, tr, nt: (layer, tg[k], 0, 0)),
                  pl.BlockSpec((None, gs, d, ff), lambda k, tg, tr, nt: (layer, tg[k], 0, 0)),
                  pl.BlockSpec((None, gs, ff, d), lambda k, tg, tr, nt: (layer, tg[k], 0, 0)),
                  pl.BlockSpec((1, d), lambda k, tg, tr, nt: (0, 0)),
                  pl.BlockSpec((1, d), lambda k, tg, tr, nt: (0, 0))],
        out_specs=pl.BlockSpec(memory_space=pl.ANY),
        scratch_shapes=[pltpu.VMEM((2, MOE_TILE, ROW_WIDE), F32), pltpu.VMEM((2, MOE_TILE, d), F32),
                        pltpu.VMEM((d, gs * ff), BF16), pltpu.VMEM((d, gs * ff), BF16), pltpu.VMEM((gs * ff, d), BF16),
                        pltpu.SemaphoreType.DMA((2,)), pltpu.SemaphoreType.DMA((2,))])
    return pl.pallas_call(
        _moe_rows_kernel,
        out_shape=jax.ShapeDtypeStruct((t, d), F32),
        grid_spec=grid_spec,
        compiler_params=_cparams(("arbitrary",)),
        name="grouped_moe",
    )(tile_group, tile_rows, n_tiles, src3, src3, x_rows, wg, wu, wd, g, b)


def _rope_tables(seq):
    inv_freq = ROPE_BASE ** (-jnp.arange(0, MLA_ROPE_DIM, 2, dtype=F32) / MLA_ROPE_DIM)
    ang = jnp.arange(seq, dtype=F32)[:, None] * inv_freq[None, :]
    cos, sin = jnp.cos(ang), jnp.sin(ang)
    cos2 = jnp.concatenate([cos, cos], axis=1)
    sin2 = jnp.concatenate([-sin, sin], axis=1)
    zero = jnp.zeros((seq, LANES - MLA_ROPE_DIM), F32)
    return (jnp.concatenate([cos2, zero], axis=1), jnp.concatenate([sin2, zero], axis=1), cos2.T, sin2.T)


def _swap_halves(w):
    half = MLA_ROPE_DIM // 2
    return jnp.concatenate([w[..., half:], w[..., :half]], axis=-1)


def kernel(x, diff_w_qkv, diff_lambda, diff_subln_g, diff_w_o, mla_w_kv_a, mla_kv_norm_g, mla_w_kv_b,
           mla_w_q_a, mla_q_norm_g, mla_w_q_b, mla_w_o, router_w, router_b, moe_w_gate, moe_w_up,
           moe_w_down, ln_g, ln_b):
    batch, seq, d = x.shape
    t = batch * seq
    xs = x.reshape(t, d)

    wq, wk, wv = jnp.split(diff_w_qkv, 3, axis=-1)
    w_k = wk.astype(BF16)
    w_qv_t = jnp.concatenate([wq * (LOG2E * DIFF_HEAD_DIM ** -0.5), wv], axis=-1).transpose(0, 2, 1).astype(BF16)
    w_o_a = diff_w_o.astype(BF16)
    w_o_b = mla_w_o.astype(BF16)
    slopes = LOG2E * 2.0 ** (-8.0 * jnp.arange(1, DIFF_HEADS + 1, dtype=F32) / DIFF_HEADS)
    slope_hi = slopes.astype(BF16).astype(F32)
    slope_lo = (slopes - slope_hi).astype(BF16).astype(F32)
    rw_t = router_w.T.reshape(N_GROUPS, GROUP_SIZE, d).transpose(1, 0, 2).reshape(N_EXPERTS, d)
    rb_t = router_b.reshape(N_GROUPS, GROUP_SIZE).T.reshape(N_EXPERTS, 1).astype(F32)
    cs, sn, cs_t, sn_t = _rope_tables(seq)
    kv_rope_w = mla_w_kv_a[:, MLA_KV_RANK:]
    w_kv_a = jnp.concatenate([mla_w_kv_a, _swap_halves(kv_rope_w)], axis=1).astype(BF16)
    w_kv_b = mla_w_kv_b.reshape(MLA_KV_RANK, MLA_HEADS, MLA_NOPE_DIM + MLA_V_DIM)
    w_kn = w_kv_b[..., :MLA_NOPE_DIM].reshape(MLA_KV_RANK, -1).astype(BF16)
    w_v_t = w_kv_b[..., MLA_NOPE_DIM:].reshape(MLA_KV_RANK, -1).T.astype(BF16)
    wqb = mla_w_q_b.reshape(-1, MLA_Q_RANK, MLA_HEADS, MLA_NOPE_DIM + MLA_ROPE_DIM)
    w_q_b_t = jnp.concatenate([wqb, _swap_halves(wqb[..., MLA_NOPE_DIM:])], axis=-1).reshape(
        -1, MLA_Q_RANK, MLA_HEADS * MLA_QK_PAD).transpose(0, 2, 1).astype(BF16)
    w_q_a = mla_w_q_a.astype(BF16)

    alibi_fix, diag_mask = _diagonal_tables()
    tri = jnp.triu(jnp.ones((ROUTE_TILE, ROUTE_TILE), BF16), 1)
    n_tiles_max = t // MOE_TILE + N_GROUPS

    kcat = vt_shared = None
    for layer in range(DEPTH):
        if layer < N_A_LAYERS:
            k, qt, vt = _qkv_proj(xs, w_k[layer], w_qv_t[layer])
            attn = _diff_attention(qt, k, vt, slope_hi, slope_lo, diff_lambda[layer],
                                   diff_subln_g[layer].reshape(-1, 1), alibi_fix, batch, seq, _lambda_init(layer))
            w_o = w_o_a[layer]
        else:
            j = layer - N_A_LAYERS
            if j == 0:
                kcat, vt_shared = _mla_kv_proj(xs, w_kv_a, mla_kv_norm_g.reshape(1, -1), w_kn, w_v_t, cs, sn, seq)
            qt = _mla_q_proj(xs, w_q_a[j], mla_q_norm_g[j].reshape(1, -1), w_q_b_t[j], cs_t, sn_t, seq)
            attn = _mla_attention(qt, kcat, vt_shared, diag_mask, batch, seq)
            w_o = w_o_b[j]
        x1, gid, rank, counts = _out_norm_route(attn, w_o, xs, ln_g[layer, 0].reshape(1, -1),
                                                ln_b[layer, 0].reshape(1, -1), rw_t, rb_t, tri)
        dest, pad_lo, pad_hi, tile_group, tile_rows, n_tiles = _tile_plan(gid, rank, counts, n_tiles_max)
        src = _invert_plan(dest, pad_lo, pad_hi, n_tiles_max * MOE_TILE)
        xs = _moe_rows(x1, src, tile_group, tile_rows, n_tiles, moe_w_gate, moe_w_up, moe_w_down,
                       ln_g[layer, 1].reshape(1, -1), ln_b[layer, 1].reshape(1, -1), layer)
    return xs.reshape(batch, seq, d)
```

```python
import functools
import math

import jax
import jax.numpy as jnp
from jax import lax
from jax.experimental import pallas as pl
from jax.experimental.pallas import tpu as pltpu

D_MODEL = 1024
DEPTH = 4
CHUNK = 64
N_A_LAYERS = DEPTH // 2
DIFF_HEAD_DIM = 64
DIFF_V_DIM = 2 * DIFF_HEAD_DIM
DIFF_HEADS = D_MODEL // DIFF_V_DIM
MLA_NOPE_DIM = 128
MLA_ROPE_DIM = 64
MLA_V_DIM = 128
MLA_HEADS = D_MODEL // MLA_V_DIM
MLA_Q_RANK = 384
MLA_KV_RANK = 256
MLA_QK_PAD = 256
ROPE_BASE = 10000.0
N_EXPERTS = 32
N_GROUPS = 8
GROUP_SIZE = N_EXPERTS // N_GROUPS
EXPERT_FF = 256
NORM_EPS = 1e-5
DEEPNORM_ALPHA = (2 * DEPTH) ** 0.25
LOG2E = math.log2(math.e)

LANES = 128
T_TILE = 256
GROUP_TILES = 2
Q_TILE = GROUP_TILES * T_TILE
MASKED = 1e30
V_ROWS = DIFF_V_DIM + 16
DIFF_HEAD_PAIR = 4
MLA_HEAD_PAIR = 4
PROJ_TILES = 2
PROJ_ROWS = PROJ_TILES * T_TILE
ROUTE_TILE = 1024
ROUTE_PARTS = 2
MOE_TILE = 256
PLAN_CHUNK = 2048
ROW_WIDE = D_MODEL + LANES
VMEM_LIMIT = 48 * 1024 * 1024

F32 = jnp.float32
BF16 = jnp.bfloat16


def _lambda_init(layer):
    return 0.8 - 0.6 * math.exp(-0.3 * layer)


def _cparams(sem):
    return pltpu.CompilerParams(dimension_semantics=sem, vmem_limit_bytes=VMEM_LIMIT)


def _dot(a, b):
    return jnp.dot(a, b, preferred_element_type=F32)


def _dot_nt(a, b):
    return lax.dot_general(a, b, (((1,), (1,)), ((), ())), preferred_element_type=F32)


def _rms(x, g):
    return x * lax.rsqrt(jnp.mean(x * x, axis=-1, keepdims=True) + NORM_EPS) * g


def _ln(x, g, b):
    mu = jnp.mean(x, axis=-1, keepdims=True)
    xc = x - mu
    var = jnp.mean(xc * xc, axis=-1, keepdims=True)
    return xc * lax.rsqrt(var + NORM_EPS) * g + b


def _rope_halves(a, cs, sn):
    return a * cs + pltpu.roll(a, MLA_ROPE_DIM, axis=1) * sn


def _qkv_kernel(x_ref, wk_ref, wqv_ref, k_ref, qt_ref, vt_ref):
    xb = x_ref[...].astype(BF16)
    k_ref[...] = _dot(xb, wk_ref[...]).astype(BF16)
    qv = _dot_nt(wqv_ref[...], xb)
    for h in range(DIFF_HEADS):
        qt_ref[0, h] = qv[h * LANES:(h + 1) * LANES].astype(BF16)
        vt_ref[0, h, :DIFF_V_DIM] = qv[D_MODEL + h * LANES:D_MODEL + (h + 1) * LANES].astype(BF16)
        vt_ref[0, h, DIFF_V_DIM:] = jnp.ones((V_ROWS - DIFF_V_DIM, T_TILE), BF16)


def _qkv_proj(x, wk, wqv):
    t, d = x.shape
    nt = t // T_TILE

    def head_major(rows):
        return jax.ShapeDtypeStruct((nt, DIFF_HEADS, rows, T_TILE), BF16)

    def head_block(rows):
        return pl.BlockSpec((1, DIFF_HEADS, rows, T_TILE), lambda i: (i, 0, 0, 0))

    return pl.pallas_call(
        _qkv_kernel,
        out_shape=(jax.ShapeDtypeStruct((t, d), BF16), head_major(LANES), head_major(V_ROWS)),
        grid=(nt,),
        in_specs=[pl.BlockSpec((T_TILE, d), lambda i: (i, 0)),
                  pl.BlockSpec(wk.shape, lambda i: (0, 0)),
                  pl.BlockSpec(wqv.shape, lambda i: (0, 0))],
        out_specs=(pl.BlockSpec((T_TILE, d), lambda i: (i, 0)), head_block(LANES), head_block(V_ROWS)),
        compiler_params=_cparams(("parallel",)),
        name="qkv_proj",
    )(x, wk, wqv)


def _mla_kv_kernel(x_ref, wa_ref, g_ref, wk_ref, wv_ref, cs_ref, sn_ref, kcat_ref, vt_ref):
    kva = _dot(x_ref[...].astype(BF16), wa_ref[...])
    c_kv = _rms(kva[:, :MLA_KV_RANK], g_ref[...]).astype(BF16)
    kr = _rope_halves(kva[:, MLA_KV_RANK:], cs_ref[...], sn_ref[...]).astype(BF16)
    kn = _dot(c_kv, wk_ref[...])
    vt = _dot_nt(wv_ref[...], c_kv)
    for h in range(MLA_HEADS):
        base = h * MLA_QK_PAD
        kcat_ref[:, base:base + MLA_NOPE_DIM] = kn[:, h * MLA_NOPE_DIM:(h + 1) * MLA_NOPE_DIM].astype(BF16)
        kcat_ref[:, base + MLA_NOPE_DIM:base + MLA_QK_PAD] = kr
        for u in range(PROJ_TILES):
            cols = slice(u * T_TILE, (u + 1) * T_TILE)
            vt_ref[u, h, :MLA_V_DIM] = vt[h * MLA_V_DIM:(h + 1) * MLA_V_DIM, cols].astype(BF16)
            vt_ref[u, h, MLA_V_DIM:] = jnp.ones((V_ROWS - MLA_V_DIM, T_TILE), BF16)


def _mla_kv_proj(x, wa, g, wk, wv, cs, sn, seq):
    t, d = x.shape
    ns = seq // PROJ_ROWS
    return pl.pallas_call(
        _mla_kv_kernel,
        out_shape=(jax.ShapeDtypeStruct((t, MLA_HEADS * MLA_QK_PAD), BF16),
                   jax.ShapeDtypeStruct((t // T_TILE, MLA_HEADS, V_ROWS, T_TILE), BF16)),
        grid=(t // PROJ_ROWS,),
        in_specs=[pl.BlockSpec((PROJ_ROWS, d), lambda i: (i, 0)),
                  pl.BlockSpec(wa.shape, lambda i: (0, 0)),
                  pl.BlockSpec(g.shape, lambda i: (0, 0)),
                  pl.BlockSpec(wk.shape, lambda i: (0, 0)),
                  pl.BlockSpec(wv.shape, lambda i: (0, 0)),
                  pl.BlockSpec((PROJ_ROWS, LANES), lambda i: (i % ns, 0)),
                  pl.BlockSpec((PROJ_ROWS, LANES), lambda i: (i % ns, 0))],
        out_specs=(pl.BlockSpec((PROJ_ROWS, MLA_HEADS * MLA_QK_PAD), lambda i: (i, 0)),
                   pl.BlockSpec((PROJ_TILES, MLA_HEADS, V_ROWS, T_TILE), lambda i: (i, 0, 0, 0))),
        compiler_params=_cparams(("parallel",)),
        name="mla_kv_proj",
    )(x, wa, g, wk, wv, cs, sn)


def _mla_q_kernel(x_ref, wa_ref, g_ref, wb_ref, cst_ref, snt_ref, qt_ref, *, scale):
    qa = _dot(x_ref[...].astype(BF16), wa_ref[...])
    qn = _rms(qa, g_ref[...]).astype(BF16)
    qt = _dot_nt(wb_ref[...], qn) * scale
    cst = cst_ref[...]
    snt = snt_ref[...]
    zero = jnp.zeros((MLA_ROPE_DIM, T_TILE), BF16)
    for h in range(MLA_HEADS):
        base = h * MLA_QK_PAD
        r0 = base + MLA_NOPE_DIM
        rope = qt[r0:r0 + MLA_ROPE_DIM] * cst + qt[r0 + MLA_ROPE_DIM:base + MLA_QK_PAD] * snt
        for u in range(PROJ_TILES):
            cols = slice(u * T_TILE, (u + 1) * T_TILE)
            qt_ref[u, h, 0:MLA_NOPE_DIM] = qt[base:r0, cols].astype(BF16)
            qt_ref[u, h, MLA_NOPE_DIM:MLA_NOPE_DIM + MLA_ROPE_DIM] = rope[:, cols].astype(BF16)
            qt_ref[u, h, MLA_NOPE_DIM + MLA_ROPE_DIM:MLA_QK_PAD] = zero


def _mla_q_proj(x, wa, g, wb_t, cst, snt, seq):
    t, d = x.shape
    ns = seq // PROJ_ROWS
    scale = LOG2E * (MLA_NOPE_DIM + MLA_ROPE_DIM) ** -0.5
    return pl.pallas_call(
        functools.partial(_mla_q_kernel, scale=scale),
        out_shape=jax.ShapeDtypeStruct((t // T_TILE, MLA_HEADS, MLA_QK_PAD, T_TILE), BF16),
        grid=(t // PROJ_ROWS,),
        in_specs=[pl.BlockSpec((PROJ_ROWS, d), lambda i: (i, 0)),
                  pl.BlockSpec(wa.shape, lambda i: (0, 0)),
                  pl.BlockSpec(g.shape, lambda i: (0, 0)),
                  pl.BlockSpec(wb_t.shape, lambda i: (0, 0)),
                  pl.BlockSpec((MLA_ROPE_DIM, PROJ_ROWS), lambda i: (0, i % ns)),
                  pl.BlockSpec((MLA_ROPE_DIM, PROJ_ROWS), lambda i: (0, i % ns))],
        out_specs=pl.BlockSpec((PROJ_TILES, MLA_HEADS, MLA_QK_PAD, T_TILE), lambda i: (i, 0, 0, 0)),
        compiler_params=_cparams(("parallel",)),
        name="mla_q_proj",
    )(x, wa, g, wb_t, cst, snt)


def _consume_group(tiles, m_ref, acc_ref):
    for c, comp_tiles in enumerate(tiles):
        m_group = None
        for _, m_t, _, _ in comp_tiles:
            m_group = m_t if m_group is None else jnp.maximum(m_group, m_t)
        pv = None
        for s_t, _, shift, vt in comp_tiles:
            part = _dot(vt, jnp.exp2(s_t - (m_group - shift)).astype(BF16))
            pv = part if pv is None else pv + part
        m_prev = m_ref[c]
        m_new = jnp.maximum(m_prev, m_group)
        acc_ref[c] = jnp.exp2(m_prev - m_new) * acc_ref[c] + jnp.exp2(m_group - m_new) * pv
        m_ref[c] = m_new


def _init_online_softmax(m_ref, acc_ref):
    m_ref[...] = jnp.full(m_ref.shape, -jnp.inf, F32)
    acc_ref[...] = jnp.zeros(acc_ref.shape, F32)


def _normalised(acc):
    return acc[:DIFF_V_DIM] / acc[DIFF_V_DIM:DIFF_V_DIM + 1]


def _attention_pipeline(n, issue_scores, consume):
    @pl.when(n == 0)
    def _():
        issue_scores(0, 0, True)
        consume(0, 0)

    @pl.when(n > 0)
    def _():
        issue_scores(0, 0, False)

        def pair(p, carry):
            issue_scores(2 * p + 1, 1, False)
            consume(2 * p, 0)
            issue_scores(2 * p + 2, 0, False)
            consume(2 * p + 1, 1)
            return carry

        lax.fori_loop(0, (n - 1) // 2, pair, 0)

        @pl.when(n % 2 == 1)
        def _():
            issue_scores(n, 1, True)
            consume(n - 1, 0)
            consume(n, 1)

        @pl.when(n % 2 == 0)
        def _():
            issue_scores(n - 1, 1, False)
            consume(n - 2, 0)
            issue_scores(n, 0, True)
            consume(n - 1, 1)
            consume(n, 0)


def _diff_attn_kernel(slh_ref, sll_ref, lam_ref, g_ref, dm_ref, qt_ref, k_ref, vt_ref, o_ref,
                      s_buf, max_buf, m_ref, acc_ref, *, lambda_init):
    i = pl.program_id(2)
    heads = range(DIFF_HEAD_PAIR)
    row = lax.broadcasted_iota(jnp.int32, (LANES, Q_TILE), 0)
    slopes, qs_aug = [], []
    for hh in heads:
        head = pl.program_id(1) * DIFF_HEAD_PAIR + hh
        sl_hi = slh_ref[head]
        sl_lo = sll_ref[head]
        slopes.append(sl_hi + sl_lo)
        qf = jnp.concatenate([qt_ref[u, hh] for u in range(GROUP_TILES)], axis=1).astype(F32)
        slope_rows = jnp.where(row == 0, sl_hi, jnp.where(row == 1, sl_lo, 0.0)).astype(BF16)
        qs_aug += [jnp.concatenate([jnp.where(keep, qf, 0.0).astype(BF16), slope_rows], axis=0)
                   for keep in (row < DIFF_HEAD_DIM, row >= DIFF_HEAD_DIM)]
    n_chains = 2 * DIFF_HEAD_PAIR
    key_off = lax.broadcasted_iota(jnp.int32, (T_TILE, LANES), 0).astype(F32)
    lane = lax.broadcasted_iota(jnp.int32, (T_TILE, LANES), 1)
    key_cols = jnp.where(lane < 2, key_off, 0.0).astype(BF16)
    _init_online_softmax(m_ref, acc_ref)

    def tile_shift(j, hh):
        return slopes[hh] * (j * T_TILE - i * Q_TILE).astype(F32)

    def issue_scores(g, slot, diagonal):
        for u in range(GROUP_TILES):
            j = g * GROUP_TILES + u
            off = pl.multiple_of(j * T_TILE, T_TILE)
            for hh in heads:
                k_aug = jnp.concatenate([k_ref[pl.ds(off, T_TILE), hh * LANES:(hh + 1) * LANES], key_cols],
                                        axis=1)
                for c in range(2):
                    n = 2 * hh + c
                    s_t = _dot(k_aug, qs_aug[n])
                    if diagonal:
                        s_t = s_t + slopes[hh] * dm_ref[u * T_TILE:(u + 1) * T_TILE, :]
                    s_buf[slot, n_chains * u + n] = s_t
                    max_buf[slot, n_chains * u + n] = jnp.max(s_t, axis=0, keepdims=True) + tile_shift(j, hh)

    def consume(g, slot):
        tiles = [[] for _ in range(n_chains)]
        for u in range(GROUP_TILES):
            j = g * GROUP_TILES + u
            for hh in heads:
                for c in range(2):
                    n = 2 * hh + c
                    tiles[n].append((s_buf[slot, n_chains * u + n], max_buf[slot, n_chains * u + n],
                                     tile_shift(j, hh), vt_ref[j, hh]))
        _consume_group(tiles, m_ref, acc_ref)

    _attention_pipeline(i, issue_scores, consume)

    lam = lam_ref[...]
    lam_full = (jnp.exp(jnp.sum(lam[0:1] * lam[1:2], axis=1, keepdims=True))
                - jnp.exp(jnp.sum(lam[2:3] * lam[3:4], axis=1, keepdims=True)) + lambda_init)
    for hh in heads:
        o_t = _normalised(acc_ref[2 * hh]) - lam_full * _normalised(acc_ref[2 * hh + 1])
        inv = lax.rsqrt(jnp.mean(o_t * o_t, axis=0, keepdims=True) + NORM_EPS)
        o_t = o_t * inv * (g_ref[...] * (1.0 - lambda_init))
        o_ref[:, hh * DIFF_V_DIM:(hh + 1) * DIFF_V_DIM] = o_t.T.astype(o_ref.dtype)


def _diagonal_tables():
    rk = jnp.arange(Q_TILE, dtype=jnp.int32)[:, None]
    rq = jnp.arange(Q_TILE, dtype=jnp.int32)[None, :]
    visible = (rk // CHUNK) <= (rq // CHUNK)
    alibi_fix = jnp.where(visible, -2.0 * jnp.maximum(rk - rq, 0).astype(F32), -MASKED)
    mask = jnp.where(visible, 0.0, -MASKED).astype(F32)
    return alibi_fix, mask


def _diff_attention(qt, k, vt, slope_hi, slope_lo, lam, g_col, alibi_fix, batch, seq, lambda_init):
    nq = seq // Q_TILE
    nk = seq // T_TILE
    hp = DIFF_HEAD_PAIR
    n_chains = 2 * hp
    return pl.pallas_call(
        functools.partial(_diff_attn_kernel, lambda_init=lambda_init),
        out_shape=jax.ShapeDtypeStruct((batch * seq, DIFF_HEADS * DIFF_V_DIM), BF16),
        grid=(batch, DIFF_HEADS // hp, nq),
        in_specs=[pl.BlockSpec(memory_space=pltpu.SMEM),
                  pl.BlockSpec(memory_space=pltpu.SMEM),
                  pl.BlockSpec(lam.shape, lambda b, h, i: (0, 0)),
                  pl.BlockSpec(g_col.shape, lambda b, h, i: (0, 0)),
                  pl.BlockSpec(alibi_fix.shape, lambda b, h, i: (0, 0)),
                  pl.BlockSpec((GROUP_TILES, hp, LANES, T_TILE), lambda b, h, i: (b * nq + i, h, 0, 0)),
                  pl.BlockSpec((seq, hp * LANES), lambda b, h, i: (b, h)),
                  pl.BlockSpec((nk, hp, V_ROWS, T_TILE), lambda b, h, i: (b, h, 0, 0))],
        out_specs=pl.BlockSpec((Q_TILE, hp * LANES), lambda b, h, i: (b * nq + i, h)),
        scratch_shapes=[pltpu.VMEM((2, n_chains * GROUP_TILES, T_TILE, Q_TILE), F32),
                        pltpu.VMEM((2, n_chains * GROUP_TILES, 1, Q_TILE), F32),
                        pltpu.VMEM((n_chains, 1, Q_TILE), F32),
                        pltpu.VMEM((n_chains, V_ROWS, Q_TILE), F32)],
        compiler_params=_cparams(("parallel", "parallel", "arbitrary")),
        name="diff_attention",
    )(slope_hi, slope_lo, lam, g_col, alibi_fix, qt, k, vt)


def _mla_attn_kernel(mask_ref, qt_ref, k_ref, vt_ref, o_ref, s_buf, max_buf, m_ref, acc_ref):
    i = pl.program_id(2)
    heads = range(MLA_HEAD_PAIR)
    qts = [jnp.concatenate([qt_ref[u, hh] for u in range(GROUP_TILES)], axis=1) for hh in heads]
    _init_online_softmax(m_ref, acc_ref)

    def issue_scores(g, slot, diagonal):
        for u in range(GROUP_TILES):
            off = pl.multiple_of((g * GROUP_TILES + u) * T_TILE, T_TILE)
            for hh in heads:
                s_t = _dot(k_ref[pl.ds(off, T_TILE), hh * MLA_QK_PAD:(hh + 1) * MLA_QK_PAD], qts[hh])
                if diagonal:
                    s_t = s_t + mask_ref[u * T_TILE:(u + 1) * T_TILE, :]
                s_buf[slot, MLA_HEAD_PAIR * u + hh] = s_t
                max_buf[slot, MLA_HEAD_PAIR * u + hh] = jnp.max(s_t, axis=0, keepdims=True)

    def consume(g, slot):
        tiles = [[(s_buf[slot, MLA_HEAD_PAIR * u + hh], max_buf[slot, MLA_HEAD_PAIR * u + hh], 0.0,
                   vt_ref[g * GROUP_TILES + u, hh]) for u in range(GROUP_TILES)] for hh in heads]
        _consume_group(tiles, m_ref, acc_ref)

    _attention_pipeline(i, issue_scores, consume)
    for hh in heads:
        o_ref[:, hh * MLA_V_DIM:(hh + 1) * MLA_V_DIM] = _normalised(acc_ref[hh]).T.astype(o_ref.dtype)


def _mla_attention(qt, kcat, vt, mask, batch, seq):
    nq = seq // Q_TILE
    nk = seq // T_TILE
    hp = MLA_HEAD_PAIR
    return pl.pallas_call(
        _mla_attn_kernel,
        out_shape=jax.ShapeDtypeStruct((batch * seq, MLA_HEADS * MLA_V_DIM), BF16),
        grid=(batch, MLA_HEADS // hp, nq),
        in_specs=[pl.BlockSpec(mask.shape, lambda b, h, i: (0, 0)),
                  pl.BlockSpec((GROUP_TILES, hp, MLA_QK_PAD, T_TILE), lambda b, h, i: (b * nq + i, h, 0, 0)),
                  pl.BlockSpec((seq, hp * MLA_QK_PAD), lambda b, h, i: (b, h)),
                  pl.BlockSpec((nk, hp, V_ROWS, T_TILE), lambda b, h, i: (b, h, 0, 0))],
        out_specs=pl.BlockSpec((Q_TILE, hp * MLA_V_DIM), lambda b, h, i: (b * nq + i, h)),
        scratch_shapes=[pltpu.VMEM((2, hp * GROUP_TILES, T_TILE, Q_TILE), F32),
                        pltpu.VMEM((2, hp * GROUP_TILES, 1, Q_TILE), F32),
                        pltpu.VMEM((hp, 1, Q_TILE), F32),
                        pltpu.VMEM((hp, V_ROWS, Q_TILE), F32)],
        compiler_params=_cparams(("parallel", "parallel", "arbitrary")),
        name="mla_attention",
    )(mask, qt, kcat, vt)


def _split_bf16(x):
    hi = x.astype(BF16)
    lo = (x - hi.astype(F32)).astype(BF16)
    return hi, lo


def _route(logits_t, bias_t):
    score = 1.0 / (1.0 + jnp.exp(-logits_t))
    sel = score + bias_t
    a = [sel[N_GROUPS * j:N_GROUPS * (j + 1)] for j in range(GROUP_SIZE)]
    sc = [score[N_GROUPS * j:N_GROUPS * (j + 1)] for j in range(GROUP_SIZE)]
    hi1, lo1 = jnp.maximum(a[0], a[1]), jnp.minimum(a[0], a[1])
    hi2, lo2 = jnp.maximum(a[2], a[3]), jnp.minimum(a[2], a[3])
    group_score = jnp.maximum(hi1, hi2) + jnp.maximum(jnp.minimum(hi1, hi2), jnp.maximum(lo1, lo2))
    gidx = lax.broadcasted_iota(jnp.int32, group_score.shape, 0)
    best = jnp.max(group_score, axis=0, keepdims=True)
    first_best = jnp.min(jnp.where(group_score == best, gidx, N_GROUPS), axis=0, keepdims=True)
    in_group = gidx == first_best
    picked = []
    for j in range(GROUP_SIZE):
        beaten_by = jnp.zeros(a[j].shape, jnp.int32)
        for i in range(GROUP_SIZE):
            if i == j:
                continue
            ahead = (a[i] > a[j]) | ((a[i] == a[j]) if i < j else False)
            beaten_by = beaten_by + ahead.astype(jnp.int32)
        picked.append(jnp.where(beaten_by < 2, sc[j], 0.0))
    denom = (picked[0] + picked[1]) + (picked[2] + picked[3])
    member_w = [jnp.sum(jnp.where(in_group, p / denom, 0.0), axis=0, keepdims=True) for p in picked]
    return in_group, member_w


def _out_norm_route_kernel(a_ref, wo_ref, x_ref, g_ref, b_ref, rw_ref, rb_ref, tri_ref,
                           x1_ref, gid_ref, rank_ref, cnt_ref, seen_ref):
    @pl.when(pl.program_id(0) == 0)
    def _():
        seen_ref[...] = jnp.zeros(seen_ref.shape, F32)

    tm = ROUTE_TILE // ROUTE_PARTS
    parts = [slice(p * tm, (p + 1) * tm) for p in range(ROUTE_PARTS)]
    ys = [_dot(a_ref[rows, :], wo_ref[...]) for rows in parts]
    wh, wl = _split_bf16(rw_ref[...])
    w_hi_lo = jnp.concatenate([wh, wl], axis=0)
    row = lax.broadcasted_iota(jnp.int32, (N_GROUPS, tm), 0)
    for rows, y in zip(parts, ys):
        x1 = _ln(DEEPNORM_ALPHA * x_ref[rows, :] + y, g_ref[...], b_ref[...])
        x1_ref[rows, :D_MODEL] = x1
        xh, xl = _split_bf16(x1)
        by_xh = _dot_nt(w_hi_lo, xh)
        logits_t = by_xh[:N_EXPERTS] + (by_xh[N_EXPERTS:] + _dot_nt(wh, xl))
        in_group, member_w = _route(logits_t, rb_ref[...])

        w_rows = jnp.zeros((N_GROUPS, tm), F32)
        for j in range(GROUP_SIZE):
            w_rows = jnp.where(row == j, member_w[j], w_rows)
        w_rows = jnp.concatenate([w_rows, jnp.zeros((LANES - N_GROUPS, tm), F32)], axis=0)
        x1_ref[rows, D_MODEL:] = w_rows.T

        onehot = in_group.astype(F32)
        earlier = _dot(onehot.astype(BF16), tri_ref[...])
        seen = seen_ref[:, 0:1]
        gid_ref[:, rows] = jnp.sum(jnp.where(in_group, row, 0), axis=0, keepdims=True)
        rank_ref[:, rows] = jnp.sum(onehot * (earlier + seen), axis=0, keepdims=True).astype(jnp.int32)
        seen_ref[...] = seen_ref[...] + jnp.sum(onehot, axis=1, keepdims=True)
    cnt_ref[...] = seen_ref[...].astype(jnp.int32)


def _out_norm_route(a, wo, x, g, b, rw_t, rb_t, tri, tm=ROUTE_TILE):
    t, d = x.shape
    return pl.pallas_call(
        _out_norm_route_kernel,
        out_shape=(jax.ShapeDtypeStruct((t, ROW_WIDE), F32),
                   jax.ShapeDtypeStruct((1, t), jnp.int32),
                   jax.ShapeDtypeStruct((1, t), jnp.int32),
                   jax.ShapeDtypeStruct((N_GROUPS, LANES), jnp.int32)),
        grid=(t // tm,),
        in_specs=[pl.BlockSpec((tm, d), lambda i: (i, 0)),
                  pl.BlockSpec((d, d), lambda i: (0, 0)),
                  pl.BlockSpec((tm, d), lambda i: (i, 0)),
                  pl.BlockSpec((1, d), lambda i: (0, 0)),
                  pl.BlockSpec((1, d), lambda i: (0, 0)),
                  pl.BlockSpec((N_EXPERTS, d), lambda i: (0, 0)),
                  pl.BlockSpec((N_EXPERTS, 1), lambda i: (0, 0)),
                  pl.BlockSpec(tri.shape, lambda i: (0, 0))],
        out_specs=(pl.BlockSpec((tm, ROW_WIDE), lambda i: (i, 0)),
                   pl.BlockSpec((1, tm), lambda i: (0, i)),
                   pl.BlockSpec((1, tm), lambda i: (0, i)),
                   pl.BlockSpec((N_GROUPS, LANES), lambda i: (0, 0))),
        scratch_shapes=[pltpu.VMEM((N_GROUPS, LANES), F32)],
        compiler_params=_cparams(("arbitrary",)),
        name="out_norm_route",
    )(a, wo, x, g, b, rw_t, rb_t, tri)


def _tile_plan(gid, rank, counts, n_tiles_max):
    cnt = counts[:, 0]
    padded = ((cnt + MOE_TILE - 1) // MOE_TILE) * MOE_TILE
    ends = jnp.cumsum(padded)
    starts = ends - padded
    n_tiles = ends[-1] // MOE_TILE
    tile_row = jnp.arange(n_tiles_max, dtype=jnp.int32) * MOE_TILE
    tile_group = jnp.sum((tile_row[:, None] >= ends[None, :]).astype(jnp.int32), axis=1)
    last_group = jnp.sum((ends[-1] - 1 >= ends).astype(jnp.int32))
    tile_group = jnp.minimum(tile_group, last_group)
    rows_left = (starts + cnt)[tile_group] - tile_row
    tile_rows = jnp.clip(rows_left, 0, MOE_TILE)
    dest = starts[gid[0]] + rank[0]
    pad_lo = jnp.concatenate([starts + cnt, ends[-1:]])
    pad_hi = jnp.concatenate([ends, jnp.full((1,), n_tiles_max * MOE_TILE, ends.dtype)])
    return (dest.astype(jnp.int32), pad_lo.astype(jnp.int32), pad_hi.astype(jnp.int32),
            tile_group.astype(jnp.int32), tile_rows.astype(jnp.int32), n_tiles.reshape(1).astype(jnp.int32))


def _invert_plan_kernel(pad_lo_ref, pad_hi_ref, dest_ref, src_ref):
    step = pl.program_id(0)

    @pl.when(step == 0)
    def _():
        def clear(p, carry):
            src_ref[p] = 0
            return carry
        for g in range(N_GROUPS + 1):
            lax.fori_loop(pad_lo_ref[g], pad_hi_ref[g], clear, 0)

    def place(r, carry):
        src_ref[dest_ref[0, r]] = step * PLAN_CHUNK + r
        return carry

    lax.fori_loop(0, PLAN_CHUNK, place, 0, unroll=8)


def _invert_plan(dest, pad_lo, pad_hi, n_rows):
    t = dest.shape[0]
    chunks = t // PLAN_CHUNK
    return pl.pallas_call(
        _invert_plan_kernel,
        out_shape=jax.ShapeDtypeStruct((n_rows,), jnp.int32),
        grid=(chunks,),
        in_specs=[pl.BlockSpec(memory_space=pltpu.SMEM),
                  pl.BlockSpec(memory_space=pltpu.SMEM),
                  pl.BlockSpec((None, 1, PLAN_CHUNK), lambda i: (i, 0, 0), memory_space=pltpu.SMEM)],
        out_specs=pl.BlockSpec((n_rows,), lambda i: (0,), memory_space=pltpu.SMEM),
        compiler_params=_cparams(("arbitrary",)),
        name="invert_plan",
    )(pad_lo, pad_hi, dest.reshape(chunks, 1, PLAN_CHUNK))


def _row_copy(src, dst, src_row, dst_row, sem):
    return pltpu.make_async_copy(src.at[pl.ds(src_row, 1)], dst.at[pl.ds(dst_row, 1)], sem)


def _for_rows(n_rows, fn):
    def chunk(q, carry):
        for u in range(8):
            fn(q * 8 + u)
        return carry

    def single(r, carry):
        fn(r)
        return carry

    full = n_rows // 8
    lax.fori_loop(0, full, chunk, 0)
    lax.fori_loop(full * 8, n_rows, single, 0)


def _moe_rows_kernel(tile_group_ref, tile_rows_ref, n_tiles_ref, src_ref, src_next_ref, x_hbm,
                     wg_ref, wu_ref, wd_ref, g_ref, b_ref, out_hbm, xbuf, obuf, wg_bf, wu_bf, wd_bf, gsem, ssem):
    k = pl.program_id(0)
    n_tiles = n_tiles_ref[0]
    slot = k % 2

    def row_copies(tile, one_start, one_wait, wait_all, start):
        n_rows = tile_rows_ref[tile]

        @pl.when(n_rows == MOE_TILE)
        def _():
            if start:
                for r in range(MOE_TILE):
                    one_start(r)
            else:
                wait_all()

        @pl.when(n_rows != MOE_TILE)
        def _():
            _for_rows(n_rows, one_start if start else one_wait)

    def gather(tile, rows_ref, buf_slot, start):
        dst = xbuf.at[buf_slot]
        sem = gsem.at[buf_slot]
        row_copies(tile,
                   lambda r: _row_copy(x_hbm, dst, rows_ref[0, r], r, sem).start(),
                   lambda r: _row_copy(x_hbm, dst, 0, 0, sem).wait(),
                   lambda: pltpu.make_async_copy(x_hbm.at[pl.ds(0, MOE_TILE)], dst, sem).wait(),
                   start)

    def scatter(tile, rows_ref, buf_slot, start):
        src = obuf.at[buf_slot]
        sem = ssem.at[buf_slot]
        row_copies(tile,
                   lambda r: _row_copy(src, out_hbm, r, rows_ref[0, r], sem).start(),
                   lambda r: _row_copy(src, out_hbm, 0, 0, sem).wait(),
                   lambda: pltpu.make_async_copy(src, out_hbm.at[pl.ds(0, MOE_TILE)], sem).wait(),
                   start)

    @pl.when(k == 0)
    def _():
        xbuf[...] = jnp.zeros(xbuf.shape, F32)
        gather(0, src_ref, 0, True)

    @pl.when(k + 1 < n_tiles)
    def _():
        gather(k + 1, src_next_ref, 1 - slot, True)

    @pl.when(k < n_tiles)
    def _():
        gather(k, src_ref, slot, False)

        @pl.when(k >= 2)
        def _():
            scatter(k - 2, src_ref, slot, False)

        @pl.when((k == 0) | (tile_group_ref[k] != tile_group_ref[jnp.maximum(k - 1, 0)]))
        def _():
            for j in range(GROUP_SIZE):
                cols = slice(j * EXPERT_FF, (j + 1) * EXPERT_FF)
                wg_bf[:, cols] = wg_ref[j].astype(BF16)
                wu_bf[:, cols] = wu_ref[j].astype(BF16)
                wd_bf[cols, :] = wd_ref[j].astype(BF16)

        x = xbuf[slot, :, :D_MODEL]
        cw = xbuf[slot, :, D_MODEL:]
        xb = x.astype(BF16)
        hg = _dot(xb, wg_bf[...])
        hu = _dot(xb, wu_bf[...])
        cw_wide = jnp.concatenate([jnp.broadcast_to(cw[:, j:j + 1], (MOE_TILE, EXPERT_FF))
                                   for j in range(GROUP_SIZE)], axis=1)
        hid = (hg / (1.0 + jnp.exp(-hg))) * hu * cw_wide
        y = _dot(hid.astype(BF16), wd_bf[...])
        obuf[slot] = _ln(DEEPNORM_ALPHA * x + y, g_ref[...], b_ref[...])
        scatter(k, src_ref, slot, True)

    @pl.when(k == n_tiles - 1)
    def _():
        @pl.when(k >= 1)
        def _():
            scatter(k - 1, src_ref, 1 - slot, False)
        scatter(k, src_ref, slot, False)


def _moe_rows(x_rows, src, tile_group, tile_rows, n_tiles, wg, wu, wd, g, b, layer):
    t = x_rows.shape[0]
    n_tiles_max = src.shape[0] // MOE_TILE
    d, gs, ff = D_MODEL, GROUP_SIZE, EXPERT_FF
    src3 = src.reshape(n_tiles_max, 1, MOE_TILE)
    grid_spec = pltpu.PrefetchScalarGridSpec(
        num_scalar_prefetch=3,
        grid=(n_tiles_max,),
        in_specs=[pl.BlockSpec((None, 1, MOE_TILE), lambda k, tg, tr, nt: (k, 0, 0), memory_space=pltpu.SMEM),
                  pl.BlockSpec((None, 1, MOE_TILE), lambda k, tg, tr, nt: (jnp.minimum(k + 1, n_tiles_max - 1), 0, 0),
                               memory_space=pltpu.SMEM),
                  pl.BlockSpec(memory_space=pl.ANY),
                  pl.BlockSpec((None, gs, d, ff), lambda k, tg, tr, nt: (layer, tg[k], 0, 0)),
                  pl.BlockSpec((None, gs, d, ff), lambda k, tg, tr, nt: (layer, tg[k], 0, 0)),
                  pl.BlockSpec((None, gs, ff, d), lambda k, tg, tr, nt: (layer, tg[k], 0, 0)),
                  pl.BlockSpec((1, d), lambda k, tg, tr, nt: (0, 0)),
                  pl.BlockSpec((1, d), lambda k, tg, tr, nt: (0, 0))],
        out_specs=pl.BlockSpec(memory_space=pl.ANY),
        scratch_shapes=[pltpu.VMEM((2, MOE_TILE, ROW_WIDE), F32), pltpu.VMEM((2, MOE_TILE, d), F32),
                        pltpu.VMEM((d, gs * ff), BF16), pltpu.VMEM((d, gs * ff), BF16), pltpu.VMEM((gs * ff, d), BF16),
                        pltpu.SemaphoreType.DMA((2,)), pltpu.SemaphoreType.DMA((2,))])
    return pl.pallas_call(
        _moe_rows_kernel,
        out_shape=jax.ShapeDtypeStruct((t, d), F32),
        grid_spec=grid_spec,
        compiler_params=_cparams(("arbitrary",)),
        name="grouped_moe",
    )(tile_group, tile_rows, n_tiles, src3, src3, x_rows, wg, wu, wd, g, b)


def _rope_tables(seq):
    inv_freq = ROPE_BASE ** (-jnp.arange(0, MLA_ROPE_DIM, 2, dtype=F32) / MLA_ROPE_DIM)
    ang = jnp.arange(seq, dtype=F32)[:, None] * inv_freq[None, :]
    cos, sin = jnp.cos(ang), jnp.sin(ang)
    cos2 = jnp.concatenate([cos, cos], axis=1)
    sin2 = jnp.concatenate([-sin, sin], axis=1)
    zero = jnp.zeros((seq, LANES - MLA_ROPE_DIM), F32)
    return (jnp.concatenate([cos2, zero], axis=1), jnp.concatenate([sin2, zero], axis=1), cos2.T, sin2.T)


def _swap_halves(w):
    half = MLA_ROPE_DIM // 2
    return jnp.concatenate([w[..., half:], w[..., :half]], axis=-1)


def kernel(x, diff_w_qkv, diff_lambda, diff_subln_g, diff_w_o, mla_w_kv_a, mla_kv_norm_g, mla_w_kv_b,
           mla_w_q_a, mla_q_norm_g, mla_w_q_b, mla_w_o, router_w, router_b, moe_w_gate, moe_w_up,
           moe_w_down, ln_g, ln_b):
    batch, seq, d = x.shape
    t = batch * seq
    xs = x.reshape(t, d)

    wq, wk, wv = jnp.split(diff_w_qkv, 3, axis=-1)
    w_k = wk.astype(BF16)
    w_qv_t = jnp.concatenate([wq * (LOG2E * DIFF_HEAD_DIM ** -0.5), wv], axis=-1).transpose(0, 2, 1).astype(BF16)
    w_o_a = diff_w_o.astype(BF16)
    w_o_b = mla_w_o.astype(BF16)
    slopes = LOG2E * 2.0 ** (-8.0 * jnp.arange(1, DIFF_HEADS + 1, dtype=F32) / DIFF_HEADS)
    slope_hi = slopes.astype(BF16).astype(F32)
    slope_lo = (slopes - slope_hi).astype(BF16).astype(F32)
    rw_t = router_w.T.reshape(N_GROUPS, GROUP_SIZE, d).transpose(1, 0, 2).reshape(N_EXPERTS, d)
    rb_t = router_b.reshape(N_GROUPS, GROUP_SIZE).T.reshape(N_EXPERTS, 1).astype(F32)
    cs, sn, cs_t, sn_t = _rope_tables(seq)
    kv_rope_w = mla_w_kv_a[:, MLA_KV_RANK:]
    w_kv_a = jnp.concatenate([mla_w_kv_a, _swap_halves(kv_rope_w)], axis=1).astype(BF16)
    w_kv_b = mla_w_kv_b.reshape(MLA_KV_RANK, MLA_HEADS, MLA_NOPE_DIM + MLA_V_DIM)
    w_kn = w_kv_b[..., :MLA_NOPE_DIM].reshape(MLA_KV_RANK, -1).astype(BF16)
    w_v_t = w_kv_b[..., MLA_NOPE_DIM:].reshape(MLA_KV_RANK, -1).T.astype(BF16)
    wqb = mla_w_q_b.reshape(-1, MLA_Q_RANK, MLA_HEADS, MLA_NOPE_DIM + MLA_ROPE_DIM)
    w_q_b_t = jnp.concatenate([wqb, _swap_halves(wqb[..., MLA_NOPE_DIM:])], axis=-1).reshape(
        -1, MLA_Q_RANK, MLA_HEADS * MLA_QK_PAD).transpose(0, 2, 1).astype(BF16)
    w_q_a = mla_w_q_a.astype(BF16)

    alibi_fix, diag_mask = _diagonal_tables()
    part = ROUTE_TILE // ROUTE_PARTS
    tri = jnp.triu(jnp.ones((part, part), BF16), 1)
    n_tiles_max = t // MOE_TILE + N_GROUPS

    kcat = vt_shared = None
    for layer in range(DEPTH):
        if layer < N_A_LAYERS:
            k, qt, vt = _qkv_proj(xs, w_k[layer], w_qv_t[layer])
            attn = _diff_attention(qt, k, vt, slope_hi, slope_lo, diff_lambda[layer],
                                   diff_subln_g[layer].reshape(-1, 1), alibi_fix, batch, seq, _lambda_init(layer))
            w_o = w_o_a[layer]
        else:
            j = layer - N_A_LAYERS
            if j == 0:
                kcat, vt_shared = _mla_kv_proj(xs, w_kv_a, mla_kv_norm_g.reshape(1, -1), w_kn, w_v_t, cs, sn, seq)
            qt = _mla_q_proj(xs, w_q_a[j], mla_q_norm_g[j].reshape(1, -1), w_q_b_t[j], cs_t, sn_t, seq)
            attn = _mla_attention(qt, kcat, vt_shared, diag_mask, batch, seq)
            w_o = w_o_b[j]
        x1, gid, rank, counts = _out_norm_route(attn, w_o, xs, ln_g[layer, 0].reshape(1, -1),
                                                ln_b[layer, 0].reshape(1, -1), rw_t, rb_t, tri)
        dest, pad_lo, pad_hi, tile_group, tile_rows, n_tiles = _tile_plan(gid, rank, counts, n_tiles_max)
        src = _invert_plan(dest, pad_lo, pad_hi, n_tiles_max * MOE_TILE)
        xs = _moe_rows(x1, src, tile_group, tile_rows, n_tiles, moe_w_gate, moe_w_up, moe_w_down,
                       ln_g[layer, 1].reshape(1, -1), ln_b[layer, 1].reshape(1, -1), layer)
    return xs.reshape(batch, seq, d)
```

```python
import functools
import math

import jax
import jax.numpy as jnp
from jax import lax
from jax.experimental import pallas as pl
from jax.experimental.pallas import tpu as pltpu

D_MODEL = 1024
DEPTH = 4
CHUNK = 64
N_A_LAYERS = DEPTH // 2
DIFF_HEAD_DIM = 64
DIFF_V_DIM = 2 * DIFF_HEAD_DIM
DIFF_HEADS = D_MODEL // DIFF_V_DIM
MLA_NOPE_DIM = 128
MLA_ROPE_DIM = 64
MLA_V_DIM = 128
MLA_HEADS = D_MODEL // MLA_V_DIM
MLA_Q_RANK = 384
MLA_KV_RANK = 256
MLA_QK_PAD = 256
ROPE_BASE = 10000.0
N_EXPERTS = 32
N_GROUPS = 8
GROUP_SIZE = N_EXPERTS // N_GROUPS
EXPERT_FF = 256
NORM_EPS = 1e-5
DEEPNORM_ALPHA = (2 * DEPTH) ** 0.25
LOG2E = math.log2(math.e)

LANES = 128
T_TILE = 256
GROUP_TILES = 2
Q_TILE = GROUP_TILES * T_TILE
MASKED = 1e30
V_ROWS = DIFF_V_DIM + 16
DIFF_HEAD_PAIR = 4
MLA_HEAD_PAIR = 4
PROJ_TILES = 2
PROJ_ROWS = PROJ_TILES * T_TILE
ROUTE_TILE = 1024
ROUTE_PARTS = 2
MOE_TILE = 256
PLAN_CHUNK = 2048
ROW_WIDE = D_MODEL + LANES
VMEM_LIMIT = 48 * 1024 * 1024

F32 = jnp.float32
BF16 = jnp.bfloat16


def _lambda_init(layer):
    return 0.8 - 0.6 * math.exp(-0.3 * layer)


def _cparams(sem):
    return pltpu.CompilerParams(dimension_semantics=sem, vmem_limit_bytes=VMEM_LIMIT)


def _dot(a, b):
    return jnp.dot(a, b, preferred_element_type=F32)


def _dot_nt(a, b):
    return lax.dot_general(a, b, (((1,), (1,)), ((), ())), preferred_element_type=F32)


def _rms(x, g):
    return x * lax.rsqrt(jnp.mean(x * x, axis=-1, keepdims=True) + NORM_EPS) * g


def _ln(x, g, b):
    mu = jnp.mean(x, axis=-1, keepdims=True)
    xc = x - mu
    var = jnp.mean(xc * xc, axis=-1, keepdims=True)
    return xc * lax.rsqrt(var + NORM_EPS) * g + b


def _rope_halves(a, cs, sn):
    return a * cs + pltpu.roll(a, MLA_ROPE_DIM, axis=1) * sn


def _qkv_kernel(x_ref, wk_ref, wqv_ref, k_ref, qt_ref, vt_ref):
    xb = x_ref[...].astype(BF16)
    k_ref[...] = _dot(xb, wk_ref[...]).astype(BF16)
    qv = _dot_nt(wqv_ref[...], xb)
    for h in range(DIFF_HEADS):
        qt_ref[0, h] = qv[h * LANES:(h + 1) * LANES].astype(BF16)
        vt_ref[0, h, :DIFF_V_DIM] = qv[D_MODEL + h * LANES:D_MODEL + (h + 1) * LANES].astype(BF16)
        vt_ref[0, h, DIFF_V_DIM:] = jnp.ones((V_ROWS - DIFF_V_DIM, T_TILE), BF16)


def _qkv_proj(x, wk, wqv):
    t, d = x.shape
    nt = t // T_TILE

    def head_major(rows):
        return jax.ShapeDtypeStruct((nt, DIFF_HEADS, rows, T_TILE), BF16)

    def head_block(rows):
        return pl.BlockSpec((1, DIFF_HEADS, rows, T_TILE), lambda i: (i, 0, 0, 0))

    return pl.pallas_call(
        _qkv_kernel,
        out_shape=(jax.ShapeDtypeStruct((t, d), BF16), head_major(LANES), head_major(V_ROWS)),
        grid=(nt,),
        in_specs=[pl.BlockSpec((T_TILE, d), lambda i: (i, 0)),
                  pl.BlockSpec(wk.shape, lambda i: (0, 0)),
                  pl.BlockSpec(wqv.shape, lambda i: (0, 0))],
        out_specs=(pl.BlockSpec((T_TILE, d), lambda i: (i, 0)), head_block(LANES), head_block(V_ROWS)),
        compiler_params=_cparams(("parallel",)),
        name="qkv_proj",
    )(x, wk, wqv)


def _mla_kv_kernel(x_ref, wa_ref, g_ref, wk_ref, wv_ref, cs_ref, sn_ref, kcat_ref, vt_ref):
    kva = _dot(x_ref[...].astype(BF16), wa_ref[...])
    c_kv = _rms(kva[:, :MLA_KV_RANK], g_ref[...]).astype(BF16)
    kr = _rope_halves(kva[:, MLA_KV_RANK:], cs_ref[...], sn_ref[...]).astype(BF16)
    kn = _dot(c_kv, wk_ref[...])
    vt = _dot_nt(wv_ref[...], c_kv)
    for h in range(MLA_HEADS):
        base = h * MLA_QK_PAD
        kcat_ref[:, base:base + MLA_NOPE_DIM] = kn[:, h * MLA_NOPE_DIM:(h + 1) * MLA_NOPE_DIM].astype(BF16)
        kcat_ref[:, base + MLA_NOPE_DIM:base + MLA_QK_PAD] = kr
        for u in range(PROJ_TILES):
            cols = slice(u * T_TILE, (u + 1) * T_TILE)
            vt_ref[u, h, :MLA_V_DIM] = vt[h * MLA_V_DIM:(h + 1) * MLA_V_DIM, cols].astype(BF16)
            vt_ref[u, h, MLA_V_DIM:] = jnp.ones((V_ROWS - MLA_V_DIM, T_TILE), BF16)


def _mla_kv_proj(x, wa, g, wk, wv, cs, sn, seq):
    t, d = x.shape
    ns = seq // PROJ_ROWS
    return pl.pallas_call(
        _mla_kv_kernel,
        out_shape=(jax.ShapeDtypeStruct((t, MLA_HEADS * MLA_QK_PAD), BF16),
                   jax.ShapeDtypeStruct((t // T_TILE, MLA_HEADS, V_ROWS, T_TILE), BF16)),
        grid=(t // PROJ_ROWS,),
        in_specs=[pl.BlockSpec((PROJ_ROWS, d), lambda i: (i, 0)),
                  pl.BlockSpec(wa.shape, lambda i: (0, 0)),
                  pl.BlockSpec(g.shape, lambda i: (0, 0)),
                  pl.BlockSpec(wk.shape, lambda i: (0, 0)),
                  pl.BlockSpec(wv.shape, lambda i: (0, 0)),
                  pl.BlockSpec((PROJ_ROWS, LANES), lambda i: (i % ns, 0)),
                  pl.BlockSpec((PROJ_ROWS, LANES), lambda i: (i % ns, 0))],
        out_specs=(pl.BlockSpec((PROJ_ROWS, MLA_HEADS * MLA_QK_PAD), lambda i: (i, 0)),
                   pl.BlockSpec((PROJ_TILES, MLA_HEADS, V_ROWS, T_TILE), lambda i: (i, 0, 0, 0))),
        compiler_params=_cparams(("parallel",)),
        name="mla_kv_proj",
    )(x, wa, g, wk, wv, cs, sn)


def _mla_q_kernel(x_ref, wa_ref, g_ref, wb_ref, cst_ref, snt_ref, qt_ref, *, scale):
    qa = _dot(x_ref[...].astype(BF16), wa_ref[...])
    qn = _rms(qa, g_ref[...]).astype(BF16)
    qt = _dot_nt(wb_ref[...], qn) * scale
    cst = cst_ref[...]
    snt = snt_ref[...]
    zero = jnp.zeros((MLA_ROPE_DIM, T_TILE), BF16)
    for h in range(MLA_HEADS):
        base = h * MLA_QK_PAD
        r0 = base + MLA_NOPE_DIM
        rope = qt[r0:r0 + MLA_ROPE_DIM] * cst + qt[r0 + MLA_ROPE_DIM:base + MLA_QK_PAD] * snt
        for u in range(PROJ_TILES):
            cols = slice(u * T_TILE, (u + 1) * T_TILE)
            qt_ref[u, h, 0:MLA_NOPE_DIM] = qt[base:r0, cols].astype(BF16)
            qt_ref[u, h, MLA_NOPE_DIM:MLA_NOPE_DIM + MLA_ROPE_DIM] = rope[:, cols].astype(BF16)
            qt_ref[u, h, MLA_NOPE_DIM + MLA_ROPE_DIM:MLA_QK_PAD] = zero


def _mla_q_proj(x, wa, g, wb_t, cst, snt, seq):
    t, d = x.shape
    ns = seq // PROJ_ROWS
    scale = LOG2E * (MLA_NOPE_DIM + MLA_ROPE_DIM) ** -0.5
    return pl.pallas_call(
        functools.partial(_mla_q_kernel, scale=scale),
        out_shape=jax.ShapeDtypeStruct((t // T_TILE, MLA_HEADS, MLA_QK_PAD, T_TILE), BF16),
        grid=(t // PROJ_ROWS,),
        in_specs=[pl.BlockSpec((PROJ_ROWS, d), lambda i: (i, 0)),
                  pl.BlockSpec(wa.shape, lambda i: (0, 0)),
                  pl.BlockSpec(g.shape, lambda i: (0, 0)),
                  pl.BlockSpec(wb_t.shape, lambda i: (0, 0)),
                  pl.BlockSpec((MLA_ROPE_DIM, PROJ_ROWS), lambda i: (0, i % ns)),
                  pl.BlockSpec((MLA_ROPE_DIM, PROJ_ROWS), lambda i: (0, i % ns))],
        out_specs=pl.BlockSpec((PROJ_TILES, MLA_HEADS, MLA_QK_PAD, T_TILE), lambda i: (i, 0, 0, 0)),
        compiler_params=_cparams(("parallel",)),
        name="mla_q_proj",
    )(x, wa, g, wb_t, cst, snt)


def _consume_group(tiles, m_ref, acc_ref):
    for c, comp_tiles in enumerate(tiles):
        m_group = None
        for _, m_t, _, _ in comp_tiles:
            m_group = m_t if m_group is None else jnp.maximum(m_group, m_t)
        m_prev = m_ref[c]
        m_new = jnp.maximum(m_prev, m_group)
        pv = None
        for s_t, _, shift, vt in comp_tiles:
            part = _dot(vt, jnp.exp2(s_t - (m_new - shift)).astype(BF16))
            pv = part if pv is None else pv + part
        acc_ref[c] = jnp.exp2(m_prev - m_new) * acc_ref[c] + pv
        m_ref[c] = m_new


def _init_online_softmax(m_ref, acc_ref):
    m_ref[...] = jnp.full(m_ref.shape, -jnp.inf, F32)
    acc_ref[...] = jnp.zeros(acc_ref.shape, F32)


def _normalised(acc):
    return acc[:DIFF_V_DIM] / acc[DIFF_V_DIM:DIFF_V_DIM + 1]


def _attention_pipeline(n, issue_scores, consume):
    @pl.when(n == 0)
    def _():
        issue_scores(0, 0, True)
        consume(0, 0)

    @pl.when(n > 0)
    def _():
        issue_scores(0, 0, False)

        def pair(p, carry):
            issue_scores(2 * p + 1, 1, False)
            consume(2 * p, 0)
            issue_scores(2 * p + 2, 0, False)
            consume(2 * p + 1, 1)
            return carry

        lax.fori_loop(0, (n - 1) // 2, pair, 0)

        @pl.when(n % 2 == 1)
        def _():
            issue_scores(n, 1, True)
            consume(n - 1, 0)
            consume(n, 1)

        @pl.when(n % 2 == 0)
        def _():
            issue_scores(n - 1, 1, False)
            consume(n - 2, 0)
            issue_scores(n, 0, True)
            consume(n - 1, 1)
            consume(n, 0)


def _diff_attn_kernel(slh_ref, sll_ref, lam_ref, g_ref, dm_ref, qt_ref, k_ref, vt_ref, o_ref,
                      s_buf, max_buf, m_ref, acc_ref, *, lambda_init):
    i = pl.program_id(2)
    heads = range(DIFF_HEAD_PAIR)
    row = lax.broadcasted_iota(jnp.int32, (LANES, Q_TILE), 0)
    slopes, qs_aug = [], []
    for hh in heads:
        head = pl.program_id(1) * DIFF_HEAD_PAIR + hh
        sl_hi = slh_ref[head]
        sl_lo = sll_ref[head]
        slopes.append(sl_hi + sl_lo)
        qf = jnp.concatenate([qt_ref[u, hh] for u in range(GROUP_TILES)], axis=1).astype(F32)
        slope_rows = jnp.where(row == 0, sl_hi, jnp.where(row == 1, sl_lo, 0.0)).astype(BF16)
        qs_aug += [jnp.concatenate([jnp.where(keep, qf, 0.0).astype(BF16), slope_rows], axis=0)
                   for keep in (row < DIFF_HEAD_DIM, row >= DIFF_HEAD_DIM)]
    n_chains = 2 * DIFF_HEAD_PAIR
    key_off = lax.broadcasted_iota(jnp.int32, (T_TILE, LANES), 0).astype(F32)
    lane = lax.broadcasted_iota(jnp.int32, (T_TILE, LANES), 1)
    key_cols = jnp.where(lane < 2, key_off, 0.0).astype(BF16)
    _init_online_softmax(m_ref, acc_ref)

    def tile_shift(j, hh):
        return slopes[hh] * (j * T_TILE - i * Q_TILE).astype(F32)

    def issue_scores(g, slot, diagonal):
        for u in range(GROUP_TILES):
            j = g * GROUP_TILES + u
            off = pl.multiple_of(j * T_TILE, T_TILE)
            for hh in heads:
                k_aug = jnp.concatenate([k_ref[pl.ds(off, T_TILE), hh * LANES:(hh + 1) * LANES], key_cols],
                                        axis=1)
                for c in range(2):
                    n = 2 * hh + c
                    s_t = _dot(k_aug, qs_aug[n])
                    if diagonal:
                        s_t = s_t + slopes[hh] * dm_ref[u * T_TILE:(u + 1) * T_TILE, :]
                    s_buf[slot, n_chains * u + n] = s_t
                    max_buf[slot, n_chains * u + n] = jnp.max(s_t, axis=0, keepdims=True) + tile_shift(j, hh)

    def consume(g, slot):
        tiles = [[] for _ in range(n_chains)]
        for u in range(GROUP_TILES):
            j = g * GROUP_TILES + u
            for hh in heads:
                for c in range(2):
                    n = 2 * hh + c
                    tiles[n].append((s_buf[slot, n_chains * u + n], max_buf[slot, n_chains * u + n],
                                     tile_shift(j, hh), vt_ref[j, hh]))
        _consume_group(tiles, m_ref, acc_ref)

    _attention_pipeline(i, issue_scores, consume)

    lam = lam_ref[...]
    lam_full = (jnp.exp(jnp.sum(lam[0:1] * lam[1:2], axis=1, keepdims=True))
                - jnp.exp(jnp.sum(lam[2:3] * lam[3:4], axis=1, keepdims=True)) + lambda_init)
    for hh in heads:
        o_t = _normalised(acc_ref[2 * hh]) - lam_full * _normalised(acc_ref[2 * hh + 1])
        inv = lax.rsqrt(jnp.mean(o_t * o_t, axis=0, keepdims=True) + NORM_EPS)
        o_t = o_t * inv * (g_ref[...] * (1.0 - lambda_init))
        o_ref[:, hh * DIFF_V_DIM:(hh + 1) * DIFF_V_DIM] = o_t.T.astype(o_ref.dtype)


def _diagonal_tables():
    rk = jnp.arange(Q_TILE, dtype=jnp.int32)[:, None]
    rq = jnp.arange(Q_TILE, dtype=jnp.int32)[None, :]
    visible = (rk // CHUNK) <= (rq // CHUNK)
    alibi_fix = jnp.where(visible, -2.0 * jnp.maximum(rk - rq, 0).astype(F32), -MASKED)
    mask = jnp.where(visible, 0.0, -MASKED).astype(F32)
    return alibi_fix, mask


def _diff_attention(qt, k, vt, slope_hi, slope_lo, lam, g_col, alibi_fix, batch, seq, lambda_init):
    nq = seq // Q_TILE
    nk = seq // T_TILE
    hp = DIFF_HEAD_PAIR
    n_chains = 2 * hp
    return pl.pallas_call(
        functools.partial(_diff_attn_kernel, lambda_init=lambda_init),
        out_shape=jax.ShapeDtypeStruct((batch * seq, DIFF_HEADS * DIFF_V_DIM), BF16),
        grid=(batch, DIFF_HEADS // hp, nq),
        in_specs=[pl.BlockSpec(memory_space=pltpu.SMEM),
                  pl.BlockSpec(memory_space=pltpu.SMEM),
                  pl.BlockSpec(lam.shape, lambda b, h, i: (0, 0)),
                  pl.BlockSpec(g_col.shape, lambda b, h, i: (0, 0)),
                  pl.BlockSpec(alibi_fix.shape, lambda b, h, i: (0, 0)),
                  pl.BlockSpec((GROUP_TILES, hp, LANES, T_TILE), lambda b, h, i: (b * nq + i, h, 0, 0)),
                  pl.BlockSpec((seq, hp * LANES), lambda b, h, i: (b, h)),
                  pl.BlockSpec((nk, hp, V_ROWS, T_TILE), lambda b, h, i: (b, h, 0, 0))],
        out_specs=pl.BlockSpec((Q_TILE, hp * LANES), lambda b, h, i: (b * nq + i, h)),
        scratch_shapes=[pltpu.VMEM((2, n_chains * GROUP_TILES, T_TILE, Q_TILE), F32),
                        pltpu.VMEM((2, n_chains * GROUP_TILES, 1, Q_TILE), F32),
                        pltpu.VMEM((n_chains, 1, Q_TILE), F32),
                        pltpu.VMEM((n_chains, V_ROWS, Q_TILE), F32)],
        compiler_params=_cparams(("parallel", "parallel", "arbitrary")),
        name="diff_attention",
    )(slope_hi, slope_lo, lam, g_col, alibi_fix, qt, k, vt)


def _mla_attn_kernel(mask_ref, qt_ref, k_ref, vt_ref, o_ref, s_buf, max_buf, m_ref, acc_ref):
    i = pl.program_id(2)
    heads = range(MLA_HEAD_PAIR)
    qts = [jnp.concatenate([qt_ref[u, hh] for u in range(GROUP_TILES)], axis=1) for hh in heads]
    _init_online_softmax(m_ref, acc_ref)

    def issue_scores(g, slot, diagonal):
        for u in range(GROUP_TILES):
            off = pl.multiple_of((g * GROUP_TILES + u) * T_TILE, T_TILE)
            for hh in heads:
                s_t = _dot(k_ref[pl.ds(off, T_TILE), hh * MLA_QK_PAD:(hh + 1) * MLA_QK_PAD], qts[hh])
                if diagonal:
                    s_t = s_t + mask_ref[u * T_TILE:(u + 1) * T_TILE, :]
                s_buf[slot, MLA_HEAD_PAIR * u + hh] = s_t
                max_buf[slot, MLA_HEAD_PAIR * u + hh] = jnp.max(s_t, axis=0, keepdims=True)

    def consume(g, slot):
        tiles = [[(s_buf[slot, MLA_HEAD_PAIR * u + hh], max_buf[slot, MLA_HEAD_PAIR * u + hh], 0.0,
                   vt_ref[g * GROUP_TILES + u, hh]) for u in range(GROUP_TILES)] for hh in heads]
        _consume_group(tiles, m_ref, acc_ref)

    _attention_pipeline(i, issue_scores, consume)
    for hh in heads:
        o_ref[:, hh * MLA_V_DIM:(hh + 1) * MLA_V_DIM] = _normalised(acc_ref[hh]).T.astype(o_ref.dtype)


def _mla_attention(qt, kcat, vt, mask, batch, seq):
    nq = seq // Q_TILE
    nk = seq // T_TILE
    hp = MLA_HEAD_PAIR
    return pl.pallas_call(
        _mla_attn_kernel,
        out_shape=jax.ShapeDtypeStruct((batch * seq, MLA_HEADS * MLA_V_DIM), BF16),
        grid=(batch, MLA_HEADS // hp, nq),
        in_specs=[pl.BlockSpec(mask.shape, lambda b, h, i: (0, 0)),
                  pl.BlockSpec((GROUP_TILES, hp, MLA_QK_PAD, T_TILE), lambda b, h, i: (b * nq + i, h, 0, 0)),
                  pl.BlockSpec((seq, hp * MLA_QK_PAD), lambda b, h, i: (b, h)),
                  pl.BlockSpec((nk, hp, V_ROWS, T_TILE), lambda b, h, i: (b, h, 0, 0))],
        out_specs=pl.BlockSpec((Q_TILE, hp * MLA_V_DIM), lambda b, h, i: (b * nq + i, h)),
        scratch_shapes=[pltpu.VMEM((2, hp * GROUP_TILES, T_TILE, Q_TILE), F32),
                        pltpu.VMEM((2, hp * GROUP_TILES, 1, Q_TILE), F32),
                        pltpu.VMEM((hp, 1, Q_TILE), F32),
                        pltpu.VMEM((hp, V_ROWS, Q_TILE), F32)],
        compiler_params=_cparams(("parallel", "parallel", "arbitrary")),
        name="mla_attention",
    )(mask, qt, kcat, vt)


def _split_bf16(x):
    hi = x.astype(BF16)
    lo = (x - hi.astype(F32)).astype(BF16)
    return hi, lo


def _route(logits_t, bias_t):
    score = 1.0 / (1.0 + jnp.exp(-logits_t))
    sel = score + bias_t
    a = [sel[N_GROUPS * j:N_GROUPS * (j + 1)] for j in range(GROUP_SIZE)]
    sc = [score[N_GROUPS * j:N_GROUPS * (j + 1)] for j in range(GROUP_SIZE)]
    hi1, lo1 = jnp.maximum(a[0], a[1]), jnp.minimum(a[0], a[1])
    hi2, lo2 = jnp.maximum(a[2], a[3]), jnp.minimum(a[2], a[3])
    group_score = jnp.maximum(hi1, hi2) + jnp.maximum(jnp.minimum(hi1, hi2), jnp.maximum(lo1, lo2))
    gidx = lax.broadcasted_iota(jnp.int32, group_score.shape, 0)
    best = jnp.max(group_score, axis=0, keepdims=True)
    first_best = jnp.min(jnp.where(group_score == best, gidx, N_GROUPS), axis=0, keepdims=True)
    in_group = gidx == first_best
    picked = []
    for j in range(GROUP_SIZE):
        beaten_by = jnp.zeros(a[j].shape, jnp.int32)
        for i in range(GROUP_SIZE):
            if i == j:
                continue
            ahead = (a[i] > a[j]) | ((a[i] == a[j]) if i < j else False)
            beaten_by = beaten_by + ahead.astype(jnp.int32)
        picked.append(jnp.where(beaten_by < 2, sc[j], 0.0))
    denom = (picked[0] + picked[1]) + (picked[2] + picked[3])
    member_w = [jnp.sum(jnp.where(in_group, p / denom, 0.0), axis=0, keepdims=True) for p in picked]
    return in_group, member_w


def _out_norm_route_kernel(a_ref, wo_ref, x_ref, g_ref, b_ref, rw_ref, rb_ref, tri_ref,
                           x1_ref, gid_ref, rank_ref, cnt_ref, seen_ref):
    @pl.when(pl.program_id(0) == 0)
    def _():
        seen_ref[...] = jnp.zeros(seen_ref.shape, F32)

    tm = ROUTE_TILE // ROUTE_PARTS
    parts = [slice(p * tm, (p + 1) * tm) for p in range(ROUTE_PARTS)]
    ys = [_dot(a_ref[rows, :], wo_ref[...]) for rows in parts]
    wh, wl = _split_bf16(rw_ref[...])
    w_hi_lo = jnp.concatenate([wh, wl], axis=0)
    row = lax.broadcasted_iota(jnp.int32, (N_GROUPS, tm), 0)
    for rows, y in zip(parts, ys):
        x1 = _ln(DEEPNORM_ALPHA * x_ref[rows, :] + y, g_ref[...], b_ref[...])
        x1_ref[rows, :D_MODEL] = x1
        xh, xl = _split_bf16(x1)
        by_xh = _dot_nt(w_hi_lo, xh)
        logits_t = by_xh[:N_EXPERTS] + (by_xh[N_EXPERTS:] + _dot_nt(wh, xl))
        in_group, member_w = _route(logits_t, rb_ref[...])

        w_rows = jnp.zeros((N_GROUPS, tm), F32)
        for j in range(GROUP_SIZE):
            w_rows = jnp.where(row == j, member_w[j], w_rows)
        w_rows = jnp.concatenate([w_rows, jnp.zeros((LANES - N_GROUPS, tm), F32)], axis=0)
        x1_ref[rows, D_MODEL:] = w_rows.T

        onehot = in_group.astype(F32)
        earlier = _dot(onehot.astype(BF16), tri_ref[...])
        seen = seen_ref[:, 0:1]
        gid_ref[:, rows] = jnp.sum(jnp.where(in_group, row, 0), axis=0, keepdims=True)
        rank_ref[:, rows] = jnp.sum(onehot * (earlier + seen), axis=0, keepdims=True).astype(jnp.int32)
        seen_ref[...] = seen_ref[...] + jnp.sum(onehot, axis=1, keepdims=True)
    cnt_ref[...] = seen_ref[...].astype(jnp.int32)


def _out_norm_route(a, wo, x, g, b, rw_t, rb_t, tri, tm=ROUTE_TILE):
    t, d = x.shape
    return pl.pallas_call(
        _out_norm_route_kernel,
        out_shape=(jax.ShapeDtypeStruct((t, ROW_WIDE), F32),
                   jax.ShapeDtypeStruct((1, t), jnp.int32),
                   jax.ShapeDtypeStruct((1, t), jnp.int32),
                   jax.ShapeDtypeStruct((N_GROUPS, LANES), jnp.int32)),
        grid=(t // tm,),
        in_specs=[pl.BlockSpec((tm, d), lambda i: (i, 0)),
                  pl.BlockSpec((d, d), lambda i: (0, 0)),
                  pl.BlockSpec((tm, d), lambda i: (i, 0)),
                  pl.BlockSpec((1, d), lambda i: (0, 0)),
                  pl.BlockSpec((1, d), lambda i: (0, 0)),
                  pl.BlockSpec((N_EXPERTS, d), lambda i: (0, 0)),
                  pl.BlockSpec((N_EXPERTS, 1), lambda i: (0, 0)),
                  pl.BlockSpec(tri.shape, lambda i: (0, 0))],
        out_specs=(pl.BlockSpec((tm, ROW_WIDE), lambda i: (i, 0)),
                   pl.BlockSpec((1, tm), lambda i: (0, i)),
                   pl.BlockSpec((1, tm), lambda i: (0, i)),
                   pl.BlockSpec((N_GROUPS, LANES), lambda i: (0, 0))),
        scratch_shapes=[pltpu.VMEM((N_GROUPS, LANES), F32)],
        compiler_params=_cparams(("arbitrary",)),
        name="out_norm_route",
    )(a, wo, x, g, b, rw_t, rb_t, tri)


def _tile_plan(gid, rank, counts, n_tiles_max):
    cnt = counts[:, 0]
    padded = ((cnt + MOE_TILE - 1) // MOE_TILE) * MOE_TILE
    ends = jnp.cumsum(padded)
    starts = ends - padded
    n_tiles = ends[-1] // MOE_TILE
    tile_row = jnp.arange(n_tiles_max, dtype=jnp.int32) * MOE_TILE
    tile_group = jnp.sum((tile_row[:, None] >= ends[None, :]).astype(jnp.int32), axis=1)
    last_group = jnp.sum((ends[-1] - 1 >= ends).astype(jnp.int32))
    tile_group = jnp.minimum(tile_group, last_group)
    rows_left = (starts + cnt)[tile_group] - tile_row
    tile_rows = jnp.clip(rows_left, 0, MOE_TILE)
    dest = starts[gid[0]] + rank[0]
    pad_lo = jnp.concatenate([starts + cnt, ends[-1:]])
    pad_hi = jnp.concatenate([ends, jnp.full((1,), n_tiles_max * MOE_TILE, ends.dtype)])
    return (dest.astype(jnp.int32), pad_lo.astype(jnp.int32), pad_hi.astype(jnp.int32),
            tile_group.astype(jnp.int32), tile_rows.astype(jnp.int32), n_tiles.reshape(1).astype(jnp.int32))


def _invert_plan_kernel(pad_lo_ref, pad_hi_ref, dest_ref, src_ref):
    step = pl.program_id(0)

    @pl.when(step == 0)
    def _():
        def clear(p, carry):
            src_ref[p] = 0
            return carry
        for g in range(N_GROUPS + 1):
            lax.fori_loop(pad_lo_ref[g], pad_hi_ref[g], clear, 0)

    def place(r, carry):
        src_ref[dest_ref[0, r]] = step * PLAN_CHUNK + r
        return carry

    lax.fori_loop(0, PLAN_CHUNK, place, 0, unroll=8)


def _invert_plan(dest, pad_lo, pad_hi, n_rows):
    t = dest.shape[0]
    chunks = t // PLAN_CHUNK
    return pl.pallas_call(
        _invert_plan_kernel,
        out_shape=jax.ShapeDtypeStruct((n_rows,), jnp.int32),
        grid=(chunks,),
        in_specs=[pl.BlockSpec(memory_space=pltpu.SMEM),
                  pl.BlockSpec(memory_space=pltpu.SMEM),
                  pl.BlockSpec((None, 1, PLAN_CHUNK), lambda i: (i, 0, 0), memory_space=pltpu.SMEM)],
        out_specs=pl.BlockSpec((n_rows,), lambda i: (0,), memory_space=pltpu.SMEM),
        compiler_params=_cparams(("arbitrary",)),
        name="invert_plan",
    )(pad_lo, pad_hi, dest.reshape(chunks, 1, PLAN_CHUNK))


def _row_copy(src, dst, src_row, dst_row, sem):
    return pltpu.make_async_copy(src.at[pl.ds(src_row, 1)], dst.at[pl.ds(dst_row, 1)], sem)


def _for_rows(n_rows, fn):
    def chunk(q, carry):
        for u in range(8):
            fn(q * 8 + u)
        return carry

    def single(r, carry):
        fn(r)
        return carry

    full = n_rows // 8
    lax.fori_loop(0, full, chunk, 0)
    lax.fori_loop(full * 8, n_rows, single, 0)


def _moe_rows_kernel(tile_group_ref, tile_rows_ref, n_tiles_ref, src_ref, src_next_ref, x_hbm,
                     wg_ref, wu_ref, wd_ref, g_ref, b_ref, out_hbm, xbuf, obuf, wg_bf, wu_bf, wd_bf, gsem, ssem):
    k = pl.program_id(0)
    n_tiles = n_tiles_ref[0]
    slot = k % 2

    def row_copies(tile, one_start, one_wait, wait_all, start):
        n_rows = tile_rows_ref[tile]

        @pl.when(n_rows == MOE_TILE)
        def _():
            if start:
                for r in range(MOE_TILE):
                    one_start(r)
            else:
                wait_all()

        @pl.when(n_rows != MOE_TILE)
        def _():
            _for_rows(n_rows, one_start if start else one_wait)

    def gather(tile, rows_ref, buf_slot, start):
        dst = xbuf.at[buf_slot]
        sem = gsem.at[buf_slot]
        row_copies(tile,
                   lambda r: _row_copy(x_hbm, dst, rows_ref[0, r], r, sem).start(),
                   lambda r: _row_copy(x_hbm, dst, 0, 0, sem).wait(),
                   lambda: pltpu.make_async_copy(x_hbm.at[pl.ds(0, MOE_TILE)], dst, sem).wait(),
                   start)

    def scatter(tile, rows_ref, buf_slot, start):
        src = obuf.at[buf_slot]
        sem = ssem.at[buf_slot]
        row_copies(tile,
                   lambda r: _row_copy(src, out_hbm, r, rows_ref[0, r], sem).start(),
                   lambda r: _row_copy(src, out_hbm, 0, 0, sem).wait(),
                   lambda: pltpu.make_async_copy(src, out_hbm.at[pl.ds(0, MOE_TILE)], sem).wait(),
                   start)

    @pl.when(k == 0)
    def _():
        xbuf[...] = jnp.zeros(xbuf.shape, F32)
        gather(0, src_ref, 0, True)

    @pl.when(k + 1 < n_tiles)
    def _():
        gather(k + 1, src_next_ref, 1 - slot, True)

    @pl.when(k < n_tiles)
    def _():
        gather(k, src_ref, slot, False)

        @pl.when(k >= 2)
        def _():
            scatter(k - 2, src_ref, slot, False)

        @pl.when((k == 0) | (tile_group_ref[k] != tile_group_ref[jnp.maximum(k - 1, 0)]))
        def _():
            for j in range(GROUP_SIZE):
                cols = slice(j * EXPERT_FF, (j + 1) * EXPERT_FF)
                wg_bf[:, cols] = wg_ref[j].astype(BF16)
                wu_bf[:, cols] = wu_ref[j].astype(BF16)
                wd_bf[cols, :] = wd_ref[j].astype(BF16)

        x = xbuf[slot, :, :D_MODEL]
        cw = xbuf[slot, :, D_MODEL:]
        xb = x.astype(BF16)
        hg = _dot(xb, wg_bf[...])
        hu = _dot(xb, wu_bf[...])
        cw_wide = jnp.concatenate([jnp.broadcast_to(cw[:, j:j + 1], (MOE_TILE, EXPERT_FF))
                                   for j in range(GROUP_SIZE)], axis=1)
        hid = (hg / (1.0 + jnp.exp(-hg))) * hu * cw_wide
        y = _dot(hid.astype(BF16), wd_bf[...])
        obuf[slot] = _ln(DEEPNORM_ALPHA * x + y, g_ref[...], b_ref[...])
        scatter(k, src_ref, slot, True)

    @pl.when(k == n_tiles - 1)
    def _():
        @pl.when(k >= 1)
        def _():
            scatter(k - 1, src_ref, 1 - slot, False)
        scatter(k, src_ref, slot, False)


def _moe_rows(x_rows, src, tile_group, tile_rows, n_tiles, wg, wu, wd, g, b, layer):
    t = x_rows.shape[0]
    n_tiles_max = src.shape[0] // MOE_TILE
    d, gs, ff = D_MODEL, GROUP_SIZE, EXPERT_FF
    src3 = src.reshape(n_tiles_max, 1, MOE_TILE)
    grid_spec = pltpu.PrefetchScalarGridSpec(
        num_scalar_prefetch=3,
        grid=(n_tiles_max,),
        in_specs=[pl.BlockSpec((None, 1, MOE_TILE), lambda k, tg, tr, nt: (k, 0, 0), memory_space=pltpu.SMEM),
                  pl.BlockSpec((None, 1, MOE_TILE), lambda k, tg, tr, nt: (jnp.minimum(k + 1, n_tiles_max - 1), 0, 0),
                               memory_space=pltpu.SMEM),
                  pl.BlockSpec(memory_space=pl.ANY),
                  pl.BlockSpec((None, gs, d, ff), lambda k, tg, tr, nt: (layer, tg[k], 0, 0)),
                  pl.BlockSpec((None, gs, d, ff), lambda k, tg, tr, nt: (layer, tg[k], 0, 0)),
                  pl.BlockSpec((None, gs, ff, d), lambda k, tg, tr, nt: (layer, tg[k], 0, 0)),
                  pl.BlockSpec((1, d), lambda k, tg, tr, nt: (0, 0)),
                  pl.BlockSpec((1, d), lambda k, tg, tr, nt: (0, 0))],
        out_specs=pl.BlockSpec(memory_space=pl.ANY),
        scratch_shapes=[pltpu.VMEM((2, MOE_TILE, ROW_WIDE), F32), pltpu.VMEM((2, MOE_TILE, d), F32),
                        pltpu.VMEM((d, gs * ff), BF16), pltpu.VMEM((d, gs * ff), BF16), pltpu.VMEM((gs * ff, d), BF16),
                        pltpu.SemaphoreType.DMA((2,)), pltpu.SemaphoreType.DMA((2,))])
    return pl.pallas_call(
        _moe_rows_kernel,
        out_shape=jax.ShapeDtypeStruct((t, d), F32),
        grid_spec=grid_spec,
        compiler_params=_cparams(("arbitrary",)),
        name="grouped_moe",
    )(tile_group, tile_rows, n_tiles, src3, src3, x_rows, wg, wu, wd, g, b)


def _rope_tables(seq):
    inv_freq = ROPE_BASE ** (-jnp.arange(0, MLA_ROPE_DIM, 2, dtype=F32) / MLA_ROPE_DIM)
    ang = jnp.arange(seq, dtype=F32)[:, None] * inv_freq[None, :]
    cos, sin = jnp.cos(ang), jnp.sin(ang)
    cos2 = jnp.concatenate([cos, cos], axis=1)
    sin2 = jnp.concatenate([-sin, sin], axis=1)
    zero = jnp.zeros((seq, LANES - MLA_ROPE_DIM), F32)
    return (jnp.concatenate([cos2, zero], axis=1), jnp.concatenate([sin2, zero], axis=1), cos2.T, sin2.T)


def _swap_halves(w):
    half = MLA_ROPE_DIM // 2
    return jnp.concatenate([w[..., half:], w[..., :half]], axis=-1)


def kernel(x, diff_w_qkv, diff_lambda, diff_subln_g, diff_w_o, mla_w_kv_a, mla_kv_norm_g, mla_w_kv_b,
           mla_w_q_a, mla_q_norm_g, mla_w_q_b, mla_w_o, router_w, router_b, moe_w_gate, moe_w_up,
           moe_w_down, ln_g, ln_b):
    batch, seq, d = x.shape
    t = batch * seq
    xs = x.reshape(t, d)

    wq, wk, wv = jnp.split(diff_w_qkv, 3, axis=-1)
    w_k = wk.astype(BF16)
    w_qv_t = jnp.concatenate([wq * (LOG2E * DIFF_HEAD_DIM ** -0.5), wv], axis=-1).transpose(0, 2, 1).astype(BF16)
    w_o_a = diff_w_o.astype(BF16)
    w_o_b = mla_w_o.astype(BF16)
    slopes = LOG2E * 2.0 ** (-8.0 * jnp.arange(1, DIFF_HEADS + 1, dtype=F32) / DIFF_HEADS)
    slope_hi = slopes.astype(BF16).astype(F32)
    slope_lo = (slopes - slope_hi).astype(BF16).astype(F32)
    rw_t = router_w.T.reshape(N_GROUPS, GROUP_SIZE, d).transpose(1, 0, 2).reshape(N_EXPERTS, d)
    rb_t = router_b.reshape(N_GROUPS, GROUP_SIZE).T.reshape(N_EXPERTS, 1).astype(F32)
    cs, sn, cs_t, sn_t = _rope_tables(seq)
    kv_rope_w = mla_w_kv_a[:, MLA_KV_RANK:]
    w_kv_a = jnp.concatenate([mla_w_kv_a, _swap_halves(kv_rope_w)], axis=1).astype(BF16)
    w_kv_b = mla_w_kv_b.reshape(MLA_KV_RANK, MLA_HEADS, MLA_NOPE_DIM + MLA_V_DIM)
    w_kn = w_kv_b[..., :MLA_NOPE_DIM].reshape(MLA_KV_RANK, -1).astype(BF16)
    w_v_t = w_kv_b[..., MLA_NOPE_DIM:].reshape(MLA_KV_RANK, -1).T.astype(BF16)
    wqb = mla_w_q_b.reshape(-1, MLA_Q_RANK, MLA_HEADS, MLA_NOPE_DIM + MLA_ROPE_DIM)
    w_q_b_t = jnp.concatenate([wqb, _swap_halves(wqb[..., MLA_NOPE_DIM:])], axis=-1).reshape(
        -1, MLA_Q_RANK, MLA_HEADS * MLA_QK_PAD).transpose(0, 2, 1).astype(BF16)
    w_q_a = mla_w_q_a.astype(BF16)

    alibi_fix, diag_mask = _diagonal_tables()
    part = ROUTE_TILE // ROUTE_PARTS
    tri = jnp.triu(jnp.ones((part, part), BF16), 1)
    n_tiles_max = t // MOE_TILE + N_GROUPS

    kcat = vt_shared = None
    for layer in range(DEPTH):
        if layer < N_A_LAYERS:
            k, qt, vt = _qkv_proj(xs, w_k[layer], w_qv_t[layer])
            attn = _diff_attention(qt, k, vt, slope_hi, slope_lo, diff_lambda[layer],
                                   diff_subln_g[layer].reshape(-1, 1), alibi_fix, batch, seq, _lambda_init(layer))
            w_o = w_o_a[layer]
        else:
            j = layer - N_A_LAYERS
            if j == 0:
                kcat, vt_shared = _mla_kv_proj(xs, w_kv_a, mla_kv_norm_g.reshape(1, -1), w_kn, w_v_t, cs, sn, seq)
            qt = _mla_q_proj(xs, w_q_a[j], mla_q_norm_g[j].reshape(1, -1), w_q_b_t[j], cs_t, sn_t, seq)
            attn = _mla_attention(qt, kcat, vt_shared, diag_mask, batch, seq)
            w_o = w_o_b[j]
        x1, gid, rank, counts = _out_norm_route(attn, w_o, xs, ln_g[layer, 0].reshape(1, -1),
                                                ln_b[layer, 0].reshape(1, -1), rw_t, rb_t, tri)
        dest, pad_lo, pad_hi, tile_group, tile_rows, n_tiles = _tile_plan(gid, rank, counts, n_tiles_max)
        src = _invert_plan(dest, pad_lo, pad_hi, n_tiles_max * MOE_TILE)
        xs = _moe_rows(x1, src, tile_group, tile_rows, n_tiles, moe_w_gate, moe_w_up, moe_w_down,
                       ln_g[layer, 1].reshape(1, -1), ln_b[layer, 1].reshape(1, -1), layer)
    return xs.reshape(batch, seq, d)
```

```python
import functools
import math

import jax
import jax.numpy as jnp
from jax import lax
from jax.experimental import pallas as pl
from jax.experimental.pallas import tpu as pltpu

D_MODEL = 1024
DEPTH = 4
CHUNK = 64
N_A_LAYERS = DEPTH // 2
DIFF_HEAD_DIM = 64
DIFF_V_DIM = 2 * DIFF_HEAD_DIM
DIFF_HEADS = D_MODEL // DIFF_V_DIM
MLA_NOPE_DIM = 128
MLA_ROPE_DIM = 64
MLA_V_DIM = 128
MLA_HEADS = D_MODEL // MLA_V_DIM
MLA_Q_RANK = 384
MLA_KV_RANK = 256
MLA_QK_PAD = 256
ROPE_BASE = 10000.0
N_EXPERTS = 32
N_GROUPS = 8
GROUP_SIZE = N_EXPERTS // N_GROUPS
EXPERT_FF = 256
NORM_EPS = 1e-5
DEEPNORM_ALPHA = (2 * DEPTH) ** 0.25
LOG2E = math.log2(math.e)

LANES = 128
T_TILE = 256
GROUP_TILES = 2
Q_TILE = GROUP_TILES * T_TILE
MASKED = 1e30
V_ROWS = DIFF_V_DIM + 16
DIFF_HEAD_PAIR = 4
MLA_HEAD_PAIR = 4
PROJ_TILES = 2
PROJ_ROWS = PROJ_TILES * T_TILE
ROUTE_TILE = 1024
ROUTE_PARTS = 2
MOE_TILE = 256
PLAN_CHUNK = 2048
ROW_WIDE = D_MODEL + LANES
VMEM_LIMIT = 48 * 1024 * 1024

F32 = jnp.float32
BF16 = jnp.bfloat16


def _lambda_init(layer):
    return 0.8 - 0.6 * math.exp(-0.3 * layer)


def _cparams(sem):
    return pltpu.CompilerParams(dimension_semantics=sem, vmem_limit_bytes=VMEM_LIMIT)


def _dot(a, b):
    return jnp.dot(a, b, preferred_element_type=F32)


def _dot_nt(a, b):
    return lax.dot_general(a, b, (((1,), (1,)), ((), ())), preferred_element_type=F32)


def _rms(x, g):
    return x * lax.rsqrt(jnp.mean(x * x, axis=-1, keepdims=True) + NORM_EPS) * g


def _ln(x, g, b):
    mu = jnp.mean(x, axis=-1, keepdims=True)
    xc = x - mu
    var = jnp.mean(xc * xc, axis=-1, keepdims=True)
    return xc * lax.rsqrt(var + NORM_EPS) * g + b


def _rope_halves(a, cs, sn):
    return a * cs + pltpu.roll(a, MLA_ROPE_DIM, axis=1) * sn


def _qkv_kernel(x_ref, wk_ref, wqv_ref, k_ref, qt_ref, vt_ref):
    xb = x_ref[...].astype(BF16)
    k_ref[...] = _dot(xb, wk_ref[...]).astype(BF16)
    qv = _dot_nt(wqv_ref[...], xb)
    for h in range(DIFF_HEADS):
        qt_ref[0, h] = qv[h * LANES:(h + 1) * LANES].astype(BF16)
        vt_ref[0, h, :DIFF_V_DIM] = qv[D_MODEL + h * LANES:D_MODEL + (h + 1) * LANES].astype(BF16)
        vt_ref[0, h, DIFF_V_DIM:] = jnp.ones((V_ROWS - DIFF_V_DIM, T_TILE), BF16)


def _qkv_proj(x, wk, wqv):
    t, d = x.shape
    nt = t // T_TILE

    def head_major(rows):
        return jax.ShapeDtypeStruct((nt, DIFF_HEADS, rows, T_TILE), BF16)

    def head_block(rows):
        return pl.BlockSpec((1, DIFF_HEADS, rows, T_TILE), lambda i: (i, 0, 0, 0))

    return pl.pallas_call(
        _qkv_kernel,
        out_shape=(jax.ShapeDtypeStruct((t, d), BF16), head_major(LANES), head_major(V_ROWS)),
        grid=(nt,),
        in_specs=[pl.BlockSpec((T_TILE, d), lambda i: (i, 0)),
                  pl.BlockSpec(wk.shape, lambda i: (0, 0)),
                  pl.BlockSpec(wqv.shape, lambda i: (0, 0))],
        out_specs=(pl.BlockSpec((T_TILE, d), lambda i: (i, 0)), head_block(LANES), head_block(V_ROWS)),
        compiler_params=_cparams(("parallel",)),
        name="qkv_proj",
    )(x, wk, wqv)


def _mla_kv_kernel(x_ref, wa_ref, g_ref, wk_ref, wv_ref, cs_ref, sn_ref, kcat_ref, vt_ref):
    kva = _dot(x_ref[...].astype(BF16), wa_ref[...])
    c_kv = _rms(kva[:, :MLA_KV_RANK], g_ref[...]).astype(BF16)
    kr = _rope_halves(kva[:, MLA_KV_RANK:], cs_ref[...], sn_ref[...]).astype(BF16)
    kn = _dot(c_kv, wk_ref[...])
    vt = _dot_nt(wv_ref[...], c_kv)
    for h in range(MLA_HEADS):
        base = h * MLA_QK_PAD
        kcat_ref[:, base:base + MLA_NOPE_DIM] = kn[:, h * MLA_NOPE_DIM:(h + 1) * MLA_NOPE_DIM].astype(BF16)
        kcat_ref[:, base + MLA_NOPE_DIM:base + MLA_QK_PAD] = kr
        for u in range(PROJ_TILES):
            cols = slice(u * T_TILE, (u + 1) * T_TILE)
            vt_ref[u, h, :MLA_V_DIM] = vt[h * MLA_V_DIM:(h + 1) * MLA_V_DIM, cols].astype(BF16)
            vt_ref[u, h, MLA_V_DIM:] = jnp.ones((V_ROWS - MLA_V_DIM, T_TILE), BF16)


def _mla_kv_proj(x, wa, g, wk, wv, cs, sn, seq):
    t, d = x.shape
    ns = seq // PROJ_ROWS
    return pl.pallas_call(
        _mla_kv_kernel,
        out_shape=(jax.ShapeDtypeStruct((t, MLA_HEADS * MLA_QK_PAD), BF16),
                   jax.ShapeDtypeStruct((t // T_TILE, MLA_HEADS, V_ROWS, T_TILE), BF16)),
        grid=(t // PROJ_ROWS,),
        in_specs=[pl.BlockSpec((PROJ_ROWS, d), lambda i: (i, 0)),
                  pl.BlockSpec(wa.shape, lambda i: (0, 0)),
                  pl.BlockSpec(g.shape, lambda i: (0, 0)),
                  pl.BlockSpec(wk.shape, lambda i: (0, 0)),
                  pl.BlockSpec(wv.shape, lambda i: (0, 0)),
                  pl.BlockSpec((PROJ_ROWS, LANES), lambda i: (i % ns, 0)),
                  pl.BlockSpec((PROJ_ROWS, LANES), lambda i: (i % ns, 0))],
        out_specs=(pl.BlockSpec((PROJ_ROWS, MLA_HEADS * MLA_QK_PAD), lambda i: (i, 0)),
                   pl.BlockSpec((PROJ_TILES, MLA_HEADS, V_ROWS, T_TILE), lambda i: (i, 0, 0, 0))),
        compiler_params=_cparams(("parallel",)),
        name="mla_kv_proj",
    )(x, wa, g, wk, wv, cs, sn)


def _mla_q_kernel(x_ref, wa_ref, g_ref, wb_ref, cst_ref, snt_ref, qt_ref, *, scale):
    qa = _dot(x_ref[...].astype(BF16), wa_ref[...])
    qn = _rms(qa, g_ref[...]).astype(BF16)
    qt = _dot_nt(wb_ref[...], qn) * scale
    cst = cst_ref[...]
    snt = snt_ref[...]
    zero = jnp.zeros((MLA_ROPE_DIM, T_TILE), BF16)
    for h in range(MLA_HEADS):
        base = h * MLA_QK_PAD
        r0 = base + MLA_NOPE_DIM
        rope = qt[r0:r0 + MLA_ROPE_DIM] * cst + qt[r0 + MLA_ROPE_DIM:base + MLA_QK_PAD] * snt
        for u in range(PROJ_TILES):
            cols = slice(u * T_TILE, (u + 1) * T_TILE)
            qt_ref[u, h, 0:MLA_NOPE_DIM] = qt[base:r0, cols].astype(BF16)
            qt_ref[u, h, MLA_NOPE_DIM:MLA_NOPE_DIM + MLA_ROPE_DIM] = rope[:, cols].astype(BF16)
            qt_ref[u, h, MLA_NOPE_DIM + MLA_ROPE_DIM:MLA_QK_PAD] = zero


def _mla_q_proj(x, wa, g, wb_t, cst, snt, seq):
    t, d = x.shape
    ns = seq // PROJ_ROWS
    scale = LOG2E * (MLA_NOPE_DIM + MLA_ROPE_DIM) ** -0.5
    return pl.pallas_call(
        functools.partial(_mla_q_kernel, scale=scale),
        out_shape=jax.ShapeDtypeStruct((t // T_TILE, MLA_HEADS, MLA_QK_PAD, T_TILE), BF16),
        grid=(t // PROJ_ROWS,),
        in_specs=[pl.BlockSpec((PROJ_ROWS, d), lambda i: (i, 0)),
                  pl.BlockSpec(wa.shape, lambda i: (0, 0)),
                  pl.BlockSpec(g.shape, lambda i: (0, 0)),
                  pl.BlockSpec(wb_t.shape, lambda i: (0, 0)),
                  pl.BlockSpec((MLA_ROPE_DIM, PROJ_ROWS), lambda i: (0, i % ns)),
                  pl.BlockSpec((MLA_ROPE_DIM, PROJ_ROWS), lambda i: (0, i % ns))],
        out_specs=pl.BlockSpec((PROJ_TILES, MLA_HEADS, MLA_QK_PAD, T_TILE), lambda i: (i, 0, 0, 0)),
        compiler_params=_cparams(("parallel",)),
        name="mla_q_proj",
    )(x, wa, g, wb_t, cst, snt)


def _consume_group(tiles, m_ref, acc_ref):
    for c, comp_tiles in enumerate(tiles):
        m_group = None
        for _, m_t, _, _ in comp_tiles:
            m_group = m_t if m_group is None else jnp.maximum(m_group, m_t)
        m_prev = m_ref[c]
        m_new = jnp.maximum(m_prev, m_group)
        pv = None
        for s_t, _, shift, vt in comp_tiles:
            part = _dot(vt, jnp.exp2(s_t - (m_new - shift)).astype(BF16))
            pv = part if pv is None else pv + part
        acc_ref[c] = jnp.exp2(m_prev - m_new) * acc_ref[c] + pv
        m_ref[c] = m_new


def _init_online_softmax(m_ref, acc_ref):
    m_ref[...] = jnp.full(m_ref.shape, -jnp.inf, F32)
    acc_ref[...] = jnp.zeros(acc_ref.shape, F32)


def _normalised(acc):
    return acc[:DIFF_V_DIM] / acc[DIFF_V_DIM:DIFF_V_DIM + 1]


def _attention_pipeline(n, issue_scores, consume):
    @pl.when(n == 0)
    def _():
        issue_scores(0, 0, True)
        consume(0, 0)

    @pl.when(n > 0)
    def _():
        issue_scores(0, 0, False)

        def pair(p, carry):
            issue_scores(2 * p + 1, 1, False)
            consume(2 * p, 0)
            issue_scores(2 * p + 2, 0, False)
            consume(2 * p + 1, 1)
            return carry

        lax.fori_loop(0, (n - 1) // 2, pair, 0)

        @pl.when(n % 2 == 1)
        def _():
            issue_scores(n, 1, True)
            consume(n - 1, 0)
            consume(n, 1)

        @pl.when(n % 2 == 0)
        def _():
            issue_scores(n - 1, 1, False)
            consume(n - 2, 0)
            issue_scores(n, 0, True)
            consume(n - 1, 1)
            consume(n, 0)


def _diff_attn_kernel(slh_ref, sll_ref, lam_ref, g_ref, dm_ref, qt_ref, k_ref, vt_ref, o_ref,
                      s_buf, max_buf, m_ref, acc_ref, *, lambda_init):
    i = pl.program_id(2)
    heads = range(DIFF_HEAD_PAIR)
    row = lax.broadcasted_iota(jnp.int32, (LANES, Q_TILE), 0)
    slopes, qs_aug = [], []
    for hh in heads:
        head = pl.program_id(1) * DIFF_HEAD_PAIR + hh
        sl_hi = slh_ref[head]
        sl_lo = sll_ref[head]
        slopes.append(sl_hi + sl_lo)
        qf = jnp.concatenate([qt_ref[u, hh] for u in range(GROUP_TILES)], axis=1).astype(F32)
        slope_rows = jnp.where(row == 0, sl_hi, jnp.where(row == 1, sl_lo, 0.0)).astype(BF16)
        qs_aug += [jnp.concatenate([jnp.where(keep, qf, 0.0).astype(BF16), slope_rows], axis=0)
                   for keep in (row < DIFF_HEAD_DIM, row >= DIFF_HEAD_DIM)]
    n_chains = 2 * DIFF_HEAD_PAIR
    key_off = lax.broadcasted_iota(jnp.int32, (T_TILE, LANES), 0).astype(F32)
    lane = lax.broadcasted_iota(jnp.int32, (T_TILE, LANES), 1)
    key_cols = jnp.where(lane < 2, key_off, 0.0).astype(BF16)
    _init_online_softmax(m_ref, acc_ref)

    def tile_shift(j, hh):
        return slopes[hh] * (j * T_TILE - i * Q_TILE).astype(F32)

    def issue_scores(g, slot, diagonal):
        for u in range(GROUP_TILES):
            j = g * GROUP_TILES + u
            off = pl.multiple_of(j * T_TILE, T_TILE)
            for hh in heads:
                k_aug = jnp.concatenate([k_ref[pl.ds(off, T_TILE), hh * LANES:(hh + 1) * LANES], key_cols],
                                        axis=1)
                for c in range(2):
                    n = 2 * hh + c
                    s_t = _dot(k_aug, qs_aug[n])
                    if diagonal:
                        s_t = s_t + slopes[hh] * dm_ref[u * T_TILE:(u + 1) * T_TILE, :]
                    s_buf[slot, n_chains * u + n] = s_t
                    max_buf[slot, n_chains * u + n] = jnp.max(s_t, axis=0, keepdims=True) + tile_shift(j, hh)

    def consume(g, slot):
        tiles = [[] for _ in range(n_chains)]
        for u in range(GROUP_TILES):
            j = g * GROUP_TILES + u
            for hh in heads:
                for c in range(2):
                    n = 2 * hh + c
                    tiles[n].append((s_buf[slot, n_chains * u + n], max_buf[slot, n_chains * u + n],
                                     tile_shift(j, hh), vt_ref[j, hh]))
        _consume_group(tiles, m_ref, acc_ref)

    _attention_pipeline(i, issue_scores, consume)

    lam = lam_ref[...]
    lam_full = (jnp.exp(jnp.sum(lam[0:1] * lam[1:2], axis=1, keepdims=True))
                - jnp.exp(jnp.sum(lam[2:3] * lam[3:4], axis=1, keepdims=True)) + lambda_init)
    for hh in heads:
        o_t = _normalised(acc_ref[2 * hh]) - lam_full * _normalised(acc_ref[2 * hh + 1])
        inv = lax.rsqrt(jnp.mean(o_t * o_t, axis=0, keepdims=True) + NORM_EPS)
        o_t = o_t * inv * (g_ref[...] * (1.0 - lambda_init))
        o_ref[:, hh * DIFF_V_DIM:(hh + 1) * DIFF_V_DIM] = o_t.T.astype(o_ref.dtype)


def _diagonal_tables():
    rk = jnp.arange(Q_TILE, dtype=jnp.int32)[:, None]
    rq = jnp.arange(Q_TILE, dtype=jnp.int32)[None, :]
    visible = (rk // CHUNK) <= (rq // CHUNK)
    alibi_fix = jnp.where(visible, -2.0 * jnp.maximum(rk - rq, 0).astype(F32), -MASKED)
    mask = jnp.where(visible, 0.0, -MASKED).astype(F32)
    return alibi_fix, mask


def _diff_attention(qt, k, vt, slope_hi, slope_lo, lam, g_col, alibi_fix, batch, seq, lambda_init):
    nq = seq // Q_TILE
    nk = seq // T_TILE
    hp = DIFF_HEAD_PAIR
    n_chains = 2 * hp
    return pl.pallas_call(
        functools.partial(_diff_attn_kernel, lambda_init=lambda_init),
        out_shape=jax.ShapeDtypeStruct((batch * seq, DIFF_HEADS * DIFF_V_DIM), BF16),
        grid=(batch, DIFF_HEADS // hp, nq),
        in_specs=[pl.BlockSpec(memory_space=pltpu.SMEM),
                  pl.BlockSpec(memory_space=pltpu.SMEM),
                  pl.BlockSpec(lam.shape, lambda b, h, i: (0, 0)),
                  pl.BlockSpec(g_col.shape, lambda b, h, i: (0, 0)),
                  pl.BlockSpec(alibi_fix.shape, lambda b, h, i: (0, 0)),
                  pl.BlockSpec((GROUP_TILES, hp, LANES, T_TILE), lambda b, h, i: (b * nq + i, h, 0, 0)),
                  pl.BlockSpec((seq, hp * LANES), lambda b, h, i: (b, h)),
                  pl.BlockSpec((nk, hp, V_ROWS, T_TILE), lambda b, h, i: (b, h, 0, 0))],
        out_specs=pl.BlockSpec((Q_TILE, hp * LANES), lambda b, h, i: (b * nq + i, h)),
        scratch_shapes=[pltpu.VMEM((2, n_chains * GROUP_TILES, T_TILE, Q_TILE), F32),
                        pltpu.VMEM((2, n_chains * GROUP_TILES, 1, Q_TILE), F32),
                        pltpu.VMEM((n_chains, 1, Q_TILE), F32),
                        pltpu.VMEM((n_chains, V_ROWS, Q_TILE), F32)],
        compiler_params=_cparams(("parallel", "parallel", "arbitrary")),
        name="diff_attention",
    )(slope_hi, slope_lo, lam, g_col, alibi_fix, qt, k, vt)


def _mla_attn_kernel(mask_ref, qt_ref, k_ref, vt_ref, o_ref, s_buf, max_buf, m_ref, acc_ref):
    i = pl.program_id(2)
    heads = range(MLA_HEAD_PAIR)
    qts = [jnp.concatenate([qt_ref[u, hh] for u in range(GROUP_TILES)], axis=1) for hh in heads]
    _init_online_softmax(m_ref, acc_ref)

    def issue_scores(g, slot, diagonal):
        for u in range(GROUP_TILES):
            off = pl.multiple_of((g * GROUP_TILES + u) * T_TILE, T_TILE)
            for hh in heads:
                s_t = _dot(k_ref[pl.ds(off, T_TILE), hh * MLA_QK_PAD:(hh + 1) * MLA_QK_PAD], qts[hh])
                if diagonal:
                    s_t = s_t + mask_ref[u * T_TILE:(u + 1) * T_TILE, :]
                s_buf[slot, MLA_HEAD_PAIR * u + hh] = s_t
                max_buf[slot, MLA_HEAD_PAIR * u + hh] = jnp.max(s_t, axis=0, keepdims=True)

    def consume(g, slot):
        tiles = [[(s_buf[slot, MLA_HEAD_PAIR * u + hh], max_buf[slot, MLA_HEAD_PAIR * u + hh], 0.0,
                   vt_ref[g * GROUP_TILES + u, hh]) for u in range(GROUP_TILES)] for hh in heads]
        _consume_group(tiles, m_ref, acc_ref)

    _attention_pipeline(i, issue_scores, consume)
    for hh in heads:
        o_ref[:, hh * MLA_V_DIM:(hh + 1) * MLA_V_DIM] = _normalised(acc_ref[hh]).T.astype(o_ref.dtype)


def _mla_attention(qt, kcat, vt, mask, batch, seq):
    nq = seq // Q_TILE
    nk = seq // T_TILE
    hp = MLA_HEAD_PAIR
    return pl.pallas_call(
        _mla_attn_kernel,
        out_shape=jax.ShapeDtypeStruct((batch * seq, MLA_HEADS * MLA_V_DIM), BF16),
        grid=(batch, MLA_HEADS // hp, nq),
        in_specs=[pl.BlockSpec(mask.shape, lambda b, h, i: (0, 0)),
                  pl.BlockSpec((GROUP_TILES, hp, MLA_QK_PAD, T_TILE), lambda b, h, i: (b * nq + i, h, 0, 0)),
                  pl.BlockSpec((seq, hp * MLA_QK_PAD), lambda b, h, i: (b, h)),
                  pl.BlockSpec((nk, hp, V_ROWS, T_TILE), lambda b, h, i: (b, h, 0, 0))],
        out_specs=pl.BlockSpec((Q_TILE, hp * MLA_V_DIM), lambda b, h, i: (b * nq + i, h)),
        scratch_shapes=[pltpu.VMEM((2, hp * GROUP_TILES, T_TILE, Q_TILE), F32),
                        pltpu.VMEM((2, hp * GROUP_TILES, 1, Q_TILE), F32),
                        pltpu.VMEM((hp, 1, Q_TILE), F32),
                        pltpu.VMEM((hp, V_ROWS, Q_TILE), F32)],
        compiler_params=_cparams(("parallel", "parallel", "arbitrary")),
        name="mla_attention",
    )(mask, qt, kcat, vt)


def _split_bf16(x):
    hi = x.astype(BF16)
    lo = (x - hi.astype(F32)).astype(BF16)
    return hi, lo


def _route(logits_t, bias_t):
    score = 1.0 / (1.0 + jnp.exp(-logits_t))
    sel = score + bias_t
    a = [sel[N_GROUPS * j:N_GROUPS * (j + 1)] for j in range(GROUP_SIZE)]
    sc = [score[N_GROUPS * j:N_GROUPS * (j + 1)] for j in range(GROUP_SIZE)]
    hi1, lo1 = jnp.maximum(a[0], a[1]), jnp.minimum(a[0], a[1])
    hi2, lo2 = jnp.maximum(a[2], a[3]), jnp.minimum(a[2], a[3])
    group_score = jnp.maximum(hi1, hi2) + jnp.maximum(jnp.minimum(hi1, hi2), jnp.maximum(lo1, lo2))
    gidx = lax.broadcasted_iota(jnp.int32, group_score.shape, 0)
    best = jnp.max(group_score, axis=0, keepdims=True)
    first_best = jnp.min(jnp.where(group_score == best, gidx, N_GROUPS), axis=0, keepdims=True)
    in_group = gidx == first_best
    picked = []
    for j in range(GROUP_SIZE):
        beaten_by = jnp.zeros(a[j].shape, jnp.int32)
        for i in range(GROUP_SIZE):
            if i == j:
                continue
            ahead = (a[i] > a[j]) | ((a[i] == a[j]) if i < j else False)
            beaten_by = beaten_by + ahead.astype(jnp.int32)
        picked.append(jnp.where(beaten_by < 2, sc[j], 0.0))
    denom = (picked[0] + picked[1]) + (picked[2] + picked[3])
    member_w = [jnp.sum(jnp.where(in_group, p / denom, 0.0), axis=0, keepdims=True) for p in picked]
    return in_group, member_w


def _out_norm_route_kernel(a_ref, wo_ref, x_ref, g_ref, b_ref, rw_ref, rb_ref, tri_ref,
                           x1_ref, gid_ref, rank_ref, cnt_ref, seen_ref):
    @pl.when(pl.program_id(0) == 0)
    def _():
        seen_ref[...] = jnp.zeros(seen_ref.shape, F32)

    tm = ROUTE_TILE // ROUTE_PARTS
    parts = [slice(p * tm, (p + 1) * tm) for p in range(ROUTE_PARTS)]
    ys = [_dot(a_ref[rows, :], wo_ref[...]) for rows in parts]
    wh, wl = _split_bf16(rw_ref[...])
    w_hi_lo = jnp.concatenate([wh, wl], axis=0)
    row = lax.broadcasted_iota(jnp.int32, (N_GROUPS, tm), 0)
    for rows, y in zip(parts, ys):
        x1 = _ln(DEEPNORM_ALPHA * x_ref[rows, :] + y, g_ref[...], b_ref[...])
        x1_ref[rows, :D_MODEL] = x1
        xh, xl = _split_bf16(x1)
        by_xh = _dot_nt(w_hi_lo, xh)
        logits_t = by_xh[:N_EXPERTS] + (by_xh[N_EXPERTS:] + _dot_nt(wh, xl))
        in_group, member_w = _route(logits_t, rb_ref[...])

        w_rows = jnp.zeros((N_GROUPS, tm), F32)
        for j in range(GROUP_SIZE):
            w_rows = jnp.where(row == j, member_w[j], w_rows)
        w_rows = jnp.concatenate([w_rows, jnp.zeros((LANES - N_GROUPS, tm), F32)], axis=0)
        x1_ref[rows, D_MODEL:] = w_rows.T

        onehot = in_group.astype(F32)
        earlier = _dot(onehot.astype(BF16), tri_ref[...])
        seen = seen_ref[:, 0:1]
        gid_ref[:, rows] = jnp.sum(jnp.where(in_group, row, 0), axis=0, keepdims=True)
        rank_ref[:, rows] = jnp.sum(onehot * (earlier + seen), axis=0, keepdims=True).astype(jnp.int32)
        seen_ref[...] = seen_ref[...] + jnp.sum(onehot, axis=1, keepdims=True)
    cnt_ref[...] = seen_ref[...].astype(jnp.int32)


def _out_norm_route(a, wo, x, g, b, rw_t, rb_t, tri, tm=ROUTE_TILE):
    t, d = x.shape
    return pl.pallas_call(
        _out_norm_route_kernel,
        out_shape=(jax.ShapeDtypeStruct((t, ROW_WIDE), F32),
                   jax.ShapeDtypeStruct((1, t), jnp.int32),
                   jax.ShapeDtypeStruct((1, t), jnp.int32),
                   jax.ShapeDtypeStruct((N_GROUPS, LANES), jnp.int32)),
        grid=(t // tm,),
        in_specs=[pl.BlockSpec((tm, d), lambda i: (i, 0)),
                  pl.BlockSpec((d, d), lambda i: (0, 0)),
                  pl.BlockSpec((tm, d), lambda i: (i, 0)),
                  pl.BlockSpec((1, d), lambda i: (0, 0)),
                  pl.BlockSpec((1, d), lambda i: (0, 0)),
                  pl.BlockSpec((N_EXPERTS, d), lambda i: (0, 0)),
                  pl.BlockSpec((N_EXPERTS, 1), lambda i: (0, 0)),
                  pl.BlockSpec(tri.shape, lambda i: (0, 0))],
        out_specs=(pl.BlockSpec((tm, ROW_WIDE), lambda i: (i, 0)),
                   pl.BlockSpec((1, tm), lambda i: (0, i)),
                   pl.BlockSpec((1, tm), lambda i: (0, i)),
                   pl.BlockSpec((N_GROUPS, LANES), lambda i: (0, 0))),
        scratch_shapes=[pltpu.VMEM((N_GROUPS, LANES), F32)],
        compiler_params=_cparams(("arbitrary",)),
        name="out_norm_route",
    )(a, wo, x, g, b, rw_t, rb_t, tri)


def _tile_plan(gid, rank, counts, n_tiles_max):
    cnt = counts[:, 0]
    padded = ((cnt + MOE_TILE - 1) // MOE_TILE) * MOE_TILE
    ends = jnp.cumsum(padded)
    starts = ends - padded
    n_tiles = ends[-1] // MOE_TILE
    tile_row = jnp.arange(n_tiles_max, dtype=jnp.int32) * MOE_TILE
    tile_group = jnp.sum((tile_row[:, None] >= ends[None, :]).astype(jnp.int32), axis=1)
    last_group = jnp.sum((ends[-1] - 1 >= ends).astype(jnp.int32))
    tile_group = jnp.minimum(tile_group, last_group)
    rows_left = (starts + cnt)[tile_group] - tile_row
    tile_rows = jnp.clip(rows_left, 0, MOE_TILE)
    dest = starts[gid[0]] + rank[0]
    pad_lo = jnp.concatenate([starts + cnt, ends[-1:]])
    pad_hi = jnp.concatenate([ends, jnp.full((1,), n_tiles_max * MOE_TILE, ends.dtype)])
    return (dest.astype(jnp.int32), pad_lo.astype(jnp.int32), pad_hi.astype(jnp.int32),
            tile_group.astype(jnp.int32), tile_rows.astype(jnp.int32), n_tiles.reshape(1).astype(jnp.int32))


def _invert_plan_kernel(pad_lo_ref, pad_hi_ref, dest_ref, src_ref):
    step = pl.program_id(0)

    @pl.when(step == 0)
    def _():
        def clear(p, carry):
            src_ref[p] = 0
            return carry
        for g in range(N_GROUPS + 1):
            lax.fori_loop(pad_lo_ref[g], pad_hi_ref[g], clear, 0)

    def place(r, carry):
        src_ref[dest_ref[0, r]] = step * PLAN_CHUNK + r
        return carry

    lax.fori_loop(0, PLAN_CHUNK, place, 0, unroll=8)


def _invert_plan(dest, pad_lo, pad_hi, n_rows):
    t = dest.shape[0]
    chunks = t // PLAN_CHUNK
    return pl.pallas_call(
        _invert_plan_kernel,
        out_shape=jax.ShapeDtypeStruct((n_rows,), jnp.int32),
        grid=(chunks,),
        in_specs=[pl.BlockSpec(memory_space=pltpu.SMEM),
                  pl.BlockSpec(memory_space=pltpu.SMEM),
                  pl.BlockSpec((None, 1, PLAN_CHUNK), lambda i: (i, 0, 0), memory_space=pltpu.SMEM)],
        out_specs=pl.BlockSpec((n_rows,), lambda i: (0,), memory_space=pltpu.SMEM),
        compiler_params=_cparams(("arbitrary",)),
        name="invert_plan",
    )(pad_lo, pad_hi, dest.reshape(chunks, 1, PLAN_CHUNK))


def _row_copy(src, dst, src_row, dst_row, sem):
    return pltpu.make_async_copy(src.at[pl.ds(src_row, 1)], dst.at[pl.ds(dst_row, 1)], sem)


def _for_rows(n_rows, fn):
    def chunk(q, carry):
        for u in range(8):
            fn(q * 8 + u)
        return carry

    def single(r, carry):
        fn(r)
        return carry

    full = n_rows // 8
    lax.fori_loop(0, full, chunk, 0)
    lax.fori_loop(full * 8, n_rows, single, 0)


def _moe_rows_kernel(tile_group_ref, tile_rows_ref, n_tiles_ref, src_ref, src_next_ref, x_hbm,
                     wg_ref, wu_ref, wd_ref, g_ref, b_ref, out_hbm, xbuf, obuf, wg_bf, wu_bf, wd_bf, gsem, ssem):
    k = pl.program_id(0)
    n_tiles = n_tiles_ref[0]
    slot = k % 2

    def row_copies(tile, one_start, one_wait, wait_all, start):
        n_rows = tile_rows_ref[tile]

        @pl.when(n_rows == MOE_TILE)
        def _():
            if start:
                for r in range(MOE_TILE):
                    one_start(r, r % 2)
            else:
                wait_all()

        @pl.when(n_rows != MOE_TILE)
        def _():
            _for_rows(n_rows, (lambda r: one_start(r, 0)) if start else one_wait)

    def gather(tile, rows_ref, buf_slot, start):
        dst = xbuf.at[buf_slot]
        sem = gsem.at[buf_slot]
        row_copies(tile,
                   lambda r, prio: _row_copy(x_hbm, dst, rows_ref[0, r], r, sem).start(priority=prio),
                   lambda r: _row_copy(x_hbm, dst, 0, 0, sem).wait(),
                   lambda: pltpu.make_async_copy(x_hbm.at[pl.ds(0, MOE_TILE)], dst, sem).wait(),
                   start)

    def scatter(tile, rows_ref, buf_slot, start):
        src = obuf.at[buf_slot]
        sem = ssem.at[buf_slot]
        row_copies(tile,
                   lambda r, prio: _row_copy(src, out_hbm, r, rows_ref[0, r], sem).start(priority=prio),
                   lambda r: _row_copy(src, out_hbm, 0, 0, sem).wait(),
                   lambda: pltpu.make_async_copy(src, out_hbm.at[pl.ds(0, MOE_TILE)], sem).wait(),
                   start)

    @pl.when(k == 0)
    def _():
        xbuf[...] = jnp.zeros(xbuf.shape, F32)
        gather(0, src_ref, 0, True)

    @pl.when(k + 1 < n_tiles)
    def _():
        gather(k + 1, src_next_ref, 1 - slot, True)

    @pl.when(k < n_tiles)
    def _():
        gather(k, src_ref, slot, False)

        @pl.when(k >= 2)
        def _():
            scatter(k - 2, src_ref, slot, False)

        @pl.when((k == 0) | (tile_group_ref[k] != tile_group_ref[jnp.maximum(k - 1, 0)]))
        def _():
            for j in range(GROUP_SIZE):
                cols = slice(j * EXPERT_FF, (j + 1) * EXPERT_FF)
                wg_bf[:, cols] = wg_ref[j].astype(BF16)
                wu_bf[:, cols] = wu_ref[j].astype(BF16)
                wd_bf[cols, :] = wd_ref[j].astype(BF16)

        x = xbuf[slot, :, :D_MODEL]
        cw = xbuf[slot, :, D_MODEL:]
        xb = x.astype(BF16)
        hg = _dot(xb, wg_bf[...])
        hu = _dot(xb, wu_bf[...])
        cw_wide = jnp.concatenate([jnp.broadcast_to(cw[:, j:j + 1], (MOE_TILE, EXPERT_FF))
                                   for j in range(GROUP_SIZE)], axis=1)
        hid = (hg / (1.0 + jnp.exp(-hg))) * hu * cw_wide
        y = _dot(hid.astype(BF16), wd_bf[...])
        obuf[slot] = _ln(DEEPNORM_ALPHA * x + y, g_ref[...], b_ref[...])
        scatter(k, src_ref, slot, True)

    @pl.when(k == n_tiles - 1)
    def _():
        @pl.when(k >= 1)
        def _():
            scatter(k - 1, src_ref, 1 - slot, False)
        scatter(k, src_ref, slot, False)


def _moe_rows(x_rows, src, tile_group, tile_rows, n_tiles, wg, wu, wd, g, b, layer):
    t = x_rows.shape[0]
    n_tiles_max = src.shape[0] // MOE_TILE
    d, gs, ff = D_MODEL, GROUP_SIZE, EXPERT_FF
    src3 = src.reshape(n_tiles_max, 1, MOE_TILE)
    grid_spec = pltpu.PrefetchScalarGridSpec(
        num_scalar_prefetch=3,
        grid=(n_tiles_max,),
        in_specs=[pl.BlockSpec((None, 1, MOE_TILE), lambda k, tg, tr, nt: (k, 0, 0), memory_space=pltpu.SMEM),
                  pl.BlockSpec((None, 1, MOE_TILE), lambda k, tg, tr, nt: (jnp.minimum(k + 1, n_tiles_max - 1), 0, 0),
                               memory_space=pltpu.SMEM),
                  pl.BlockSpec(memory_space=pl.ANY),
                  pl.BlockSpec((None, gs, d, ff), lambda k, tg, tr, nt: (layer, tg[k], 0, 0)),
                  pl.BlockSpec((None, gs, d, ff), lambda k, tg, tr, nt: (layer, tg[k], 0, 0)),
                  pl.BlockSpec((None, gs, ff, d), lambda k, tg, tr, nt: (layer, tg[k], 0, 0)),
                  pl.BlockSpec((1, d), lambda k, tg, tr, nt: (0, 0)),
                  pl.BlockSpec((1, d), lambda k, tg, tr, nt: (0, 0))],
        out_specs=pl.BlockSpec(memory_space=pl.ANY),
        scratch_shapes=[pltpu.VMEM((2, MOE_TILE, ROW_WIDE), F32), pltpu.VMEM((2, MOE_TILE, d), F32),
                        pltpu.VMEM((d, gs * ff), BF16), pltpu.VMEM((d, gs * ff), BF16), pltpu.VMEM((gs * ff, d), BF16),
                        pltpu.SemaphoreType.DMA((2,)), pltpu.SemaphoreType.DMA((2,))])
    return pl.pallas_call(
        _moe_rows_kernel,
        out_shape=jax.ShapeDtypeStruct((t, d), F32),
        grid_spec=grid_spec,
        compiler_params=_cparams(("arbitrary",)),
        name="grouped_moe",
    )(tile_group, tile_rows, n_tiles, src3, src3, x_rows, wg, wu, wd, g, b)


def _rope_tables(seq):
    inv_freq = ROPE_BASE ** (-jnp.arange(0, MLA_ROPE_DIM, 2, dtype=F32) / MLA_ROPE_DIM)
    ang = jnp.arange(seq, dtype=F32)[:, None] * inv_freq[None, :]
    cos, sin = jnp.cos(ang), jnp.sin(ang)
    cos2 = jnp.concatenate([cos, cos], axis=1)
    sin2 = jnp.concatenate([-sin, sin], axis=1)
    zero = jnp.zeros((seq, LANES - MLA_ROPE_DIM), F32)
    return (jnp.concatenate([cos2, zero], axis=1), jnp.concatenate([sin2, zero], axis=1), cos2.T, sin2.T)


def _swap_halves(w):
    half = MLA_ROPE_DIM // 2
    return jnp.concatenate([w[..., half:], w[..., :half]], axis=-1)


def kernel(x, diff_w_qkv, diff_lambda, diff_subln_g, diff_w_o, mla_w_kv_a, mla_kv_norm_g, mla_w_kv_b,
           mla_w_q_a, mla_q_norm_g, mla_w_q_b, mla_w_o, router_w, router_b, moe_w_gate, moe_w_up,
           moe_w_down, ln_g, ln_b):
    batch, seq, d = x.shape
    t = batch * seq
    xs = x.reshape(t, d)

    wq, wk, wv = jnp.split(diff_w_qkv, 3, axis=-1)
    w_k = wk.astype(BF16)
    w_qv_t = jnp.concatenate([wq * (LOG2E * DIFF_HEAD_DIM ** -0.5), wv], axis=-1).transpose(0, 2, 1).astype(BF16)
    w_o_a = diff_w_o.astype(BF16)
    w_o_b = mla_w_o.astype(BF16)
    slopes = LOG2E * 2.0 ** (-8.0 * jnp.arange(1, DIFF_HEADS + 1, dtype=F32) / DIFF_HEADS)
    slope_hi = slopes.astype(BF16).astype(F32)
    slope_lo = (slopes - slope_hi).astype(BF16).astype(F32)
    rw_t = router_w.T.reshape(N_GROUPS, GROUP_SIZE, d).transpose(1, 0, 2).reshape(N_EXPERTS, d)
    rb_t = router_b.reshape(N_GROUPS, GROUP_SIZE).T.reshape(N_EXPERTS, 1).astype(F32)
    cs, sn, cs_t, sn_t = _rope_tables(seq)
    kv_rope_w = mla_w_kv_a[:, MLA_KV_RANK:]
    w_kv_a = jnp.concatenate([mla_w_kv_a, _swap_halves(kv_rope_w)], axis=1).astype(BF16)
    w_kv_b = mla_w_kv_b.reshape(MLA_KV_RANK, MLA_HEADS, MLA_NOPE_DIM + MLA_V_DIM)
    w_kn = w_kv_b[..., :MLA_NOPE_DIM].reshape(MLA_KV_RANK, -1).astype(BF16)
    w_v_t = w_kv_b[..., MLA_NOPE_DIM:].reshape(MLA_KV_RANK, -1).T.astype(BF16)
    wqb = mla_w_q_b.reshape(-1, MLA_Q_RANK, MLA_HEADS, MLA_NOPE_DIM + MLA_ROPE_DIM)
    w_q_b_t = jnp.concatenate([wqb, _swap_halves(wqb[..., MLA_NOPE_DIM:])], axis=-1).reshape(
        -1, MLA_Q_RANK, MLA_HEADS * MLA_QK_PAD).transpose(0, 2, 1).astype(BF16)
    w_q_a = mla_w_q_a.astype(BF16)

    alibi_fix, diag_mask = _diagonal_tables()
    part = ROUTE_TILE // ROUTE_PARTS
    tri = jnp.triu(jnp.ones((part, part), BF16), 1)
    n_tiles_max = t // MOE_TILE + N_GROUPS

    kcat = vt_shared = None
    for layer in range(DEPTH):
        if layer < N_A_LAYERS:
            k, qt, vt = _qkv_proj(xs, w_k[layer], w_qv_t[layer])
            attn = _diff_attention(qt, k, vt, slope_hi, slope_lo, diff_lambda[layer],
                                   diff_subln_g[layer].reshape(-1, 1), alibi_fix, batch, seq, _lambda_init(layer))
            w_o = w_o_a[layer]
        else:
            j = layer - N_A_LAYERS
            if j == 0:
                kcat, vt_shared = _mla_kv_proj(xs, w_kv_a, mla_kv_norm_g.reshape(1, -1), w_kn, w_v_t, cs, sn, seq)
            qt = _mla_q_proj(xs, w_q_a[j], mla_q_norm_g[j].reshape(1, -1), w_q_b_t[j], cs_t, sn_t, seq)
            attn = _mla_attention(qt, kcat, vt_shared, diag_mask, batch, seq)
            w_o = w_o_b[j]
        x1, gid, rank, counts = _out_norm_route(attn, w_o, xs, ln_g[layer, 0].reshape(1, -1),
                                                ln_b[layer, 0].reshape(1, -1), rw_t, rb_t, tri)
        dest, pad_lo, pad_hi, tile_group, tile_rows, n_tiles = _tile_plan(gid, rank, counts, n_tiles_max)
        src = _invert_plan(dest, pad_lo, pad_hi, n_tiles_max * MOE_TILE)
        xs = _moe_rows(x1, src, tile_group, tile_rows, n_tiles, moe_w_gate, moe_w_up, moe_w_down,
                       ln_g[layer, 1].reshape(1, -1), ln_b[layer, 1].reshape(1, -1), layer)
    return xs.reshape(batch, seq, d)
```
